```python
import math
import jax, jax.numpy as jnp
from jax import lax
import numpy as np

D_MODEL = 1024
BATCH = 8
SEQ = 16384
DEPTH = 4

D_PLE = 256
SSD_WIDTH = 512
SSD_HEAD_DIM = 64
SSD_HEADS = SSD_WIDTH // SSD_HEAD_DIM
SSD_GROUPS = 2
SSD_STATE = 128
SSD_CONV = 4
SSD_CHUNK = 128
SSD_XBC = SSD_WIDTH + 2 * SSD_GROUPS * SSD_STATE
POOL_WINDOWS = (2, 4, 8, 16)
POOL_WIDTH = D_MODEL - SSD_WIDTH
POOL_GROUP = POOL_WIDTH // len(POOL_WINDOWS)
D_MIX = SSD_WIDTH + POOL_WIDTH
D_IN_PROJ = SSD_WIDTH + SSD_XBC + SSD_HEADS + POOL_WIDTH
D_FF = 2816
FFN_CONV = 3
EPS = 1e-6

kernel_name = 'hymba_ssd_pool_convffn_ple'


def rmsnorm(x, g):
    xf = x.astype(jnp.float32)
    y = xf * lax.rsqrt(jnp.mean(xf * xf, axis=-1, keepdims=True) + EPS)
    return (y * g.astype(jnp.float32)).astype(x.dtype)


def causal_dwconv(x, w, b):
    k_taps = w.shape[0]
    s = x.shape[1]
    xp = jnp.pad(x, ((0, 0), (k_taps - 1, 0), (0, 0)))
    out = b + xp[:, 0:s] * w[0]
    for k in range(1, k_taps):
        out = out + xp[:, k:k + s] * w[k]
    return out


def segsum_exp(a):
    q = a.shape[-1]
    cs = jnp.cumsum(a, axis=-1)
    diff = cs[..., :, None] - cs[..., None, :]
    mask = jnp.tril(jnp.ones((q, q), dtype=bool))
    return jnp.exp(jnp.where(mask, diff, -jnp.inf))


def ssd_scan(x, dt, A, B, C):
    b, l, h, p = x.shape
    g, n = B.shape[-2:]
    e = h // g
    q = SSD_CHUNK
    c = l // q
    X = (x * dt[..., None]).reshape(b, c, q, g, e, p)
    a = (dt * A).reshape(b, c, q, g, e).transpose(0, 3, 4, 1, 2)
    Bc = B.reshape(b, c, q, g, n)
    Cc = C.reshape(b, c, q, g, n)
    a_cs = jnp.cumsum(a, axis=-1)
    CB = jnp.einsum('bclgn,bcsgn->bgcls', Cc, Bc)
    M = CB[:, :, None] * segsum_exp(a)
    y_diag = jnp.einsum('bgecls,bcsgep->bclgep', M, X)
    decay_states = jnp.exp(a_cs[..., -1:] - a_cs).transpose(0, 3, 4, 1, 2)
    states = jnp.einsum('bclgn,bclgep->bcgepn', Bc, X * decay_states[..., None])
    chunk_decay = jnp.exp(a_cs[..., -1])

    def step(s, inp):
        dec, st = inp
        return s * dec[..., None, None] + st, s

    init = jnp.zeros((b, g, e, p, n), jnp.float32)
    _, prev = lax.scan(step, init, (jnp.moveaxis(chunk_decay, -1, 0), jnp.moveaxis(states, 1, 0)))
    y_off = jnp.einsum('bclgn,cbgepn->bclgep', Cc, prev) * jnp.exp(a_cs).transpose(0, 3, 4, 1, 2)[..., None]
    return (y_diag + y_off).reshape(b, l, h, p)


def ssd_branch(z, xbc, dt_raw, conv_w, conv_b, dt_bias, a_log, d_skip, norm_g):
    b, s, _ = z.shape
    f32 = jnp.float32
    xbc = jax.nn.silu(causal_dwconv(xbc, conv_w, conv_b))
    xs, Bm, Cm = jnp.split(xbc, [SSD_WIDTH, SSD_WIDTH + SSD_GROUPS * SSD_STATE], axis=-1)
    dt = jax.nn.softplus(dt_raw.astype(f32) + dt_bias.astype(f32))
    A = -jnp.exp(a_log.astype(f32))
    xh = xs.astype(f32).reshape(b, s, SSD_HEADS, SSD_HEAD_DIM)
    y = ssd_scan(xh, dt, A,
                 Bm.astype(f32).reshape(b, s, SSD_GROUPS, SSD_STATE),
                 Cm.astype(f32).reshape(b, s, SSD_GROUPS, SSD_STATE))
    y = y + xh * d_skip.astype(f32)[:, None]
    y = y.reshape(b, s, SSD_WIDTH) * jax.nn.silu(z.astype(f32))
    gs = SSD_WIDTH // SSD_GROUPS
    y = rmsnorm(y.reshape(b, s, SSD_GROUPS, gs), norm_g.reshape(SSD_GROUPS, gs))
    return y.reshape(b, s, SSD_WIDTH).astype(z.dtype)


def pool_branch(u, pool_w, pool_scale):
    b, s, _ = u.shape
    f32 = jnp.float32
    uf = u.astype(f32).reshape(b, s, len(POOL_WINDOWS), POOL_GROUP)
    cs = jnp.cumsum(uf, axis=1)
    pos = jnp.arange(1, s + 1, dtype=f32)
    outs = []
    for gi, w in enumerate(POOL_WINDOWS):
        c = cs[:, :, gi]
        lag = jnp.pad(c, ((0, 0), (w, 0), (0, 0)))[:, :s]
        mean = (c - lag) / jnp.minimum(pos, float(w))[:, None]
        outs.append(mean - uf[:, :, gi])
    pooled = jnp.stack(outs, axis=2)
    mixed = jnp.einsum('bsgc,gcd->bsgd', pooled, pool_w.astype(f32))
    return (mixed.reshape(b, s, POOL_WIDTH) * pool_scale.astype(f32)).astype(u.dtype)


def conv_ffn(h, w_up, conv_w, conv_b, w_down):
    up = causal_dwconv(h @ w_up, conv_w, conv_b)
    gate, val = jnp.split(up, 2, axis=-1)
    return (jax.nn.gelu(gate) * val) @ w_down


def _fwd_setup_inputs(seed: int = 0) -> dict:
    key = jax.random.key(seed)
    ks = jax.random.split(key, 24)
    f32 = jnp.float32
    L = DEPTH

    def nrm(k, shape, scale):
        return jax.random.normal(k, shape, f32) * scale

    def gain(k, shape):
        return 1.0 + 0.02 * jax.random.normal(k, shape, f32)

    dt0 = jnp.exp(jax.random.uniform(ks[5], (L, SSD_HEADS), f32, math.log(1e-3), math.log(1e-1)))
    return {
        'x': jax.random.normal(ks[0], (BATCH, SEQ, D_MODEL), f32),
        'p': jax.random.normal(ks[1], (DEPTH, BATCH, SEQ, D_PLE), f32),
        'mix_norm_g': gain(ks[2], (L, D_MODEL)),
        'w_in': nrm(ks[3], (L, D_MODEL, D_IN_PROJ), D_MODEL ** -0.5),
        'ssd_conv_w': nrm(ks[4], (L, SSD_CONV, SSD_XBC), SSD_CONV ** -0.5),
        'ssd_conv_b': nrm(ks[6], (L, SSD_XBC), 0.02),
        'ssd_dt_bias': dt0 + jnp.log(-jnp.expm1(-dt0)),
        'ssd_a_log': jnp.log(jax.random.uniform(ks[7], (L, SSD_HEADS), f32, 1.0, 16.0)),
        'ssd_d': gain(ks[8], (L, SSD_HEADS)),
        'ssd_norm_g': gain(ks[9], (L, SSD_WIDTH)),
        'pool_w': nrm(ks[10], (L, len(POOL_WINDOWS), POOL_GROUP, POOL_GROUP), POOL_GROUP ** -0.5),
        'pool_scale': gain(ks[11], (L, POOL_WIDTH)),
        'w_out': nrm(ks[12], (L, D_MIX, D_MODEL), D_MIX ** -0.5),
        'ffn_norm_g': gain(ks[13], (L, D_MODEL)),
        'ffn_w_up': nrm(ks[14], (L, D_MODEL, 2 * D_FF), D_MODEL ** -0.5),
        'ffn_conv_w': nrm(ks[15], (L, FFN_CONV, 2 * D_FF), FFN_CONV ** -0.5),
        'ffn_conv_b': nrm(ks[16], (L, 2 * D_FF), 0.02),
        'ffn_w_down': nrm(ks[17], (L, D_FF, D_MODEL), D_FF ** -0.5),
        'ple_norm_g': gain(ks[18], (L, D_MODEL)),
        'ple_w_gate': nrm(ks[19], (L, D_MODEL, D_MODEL), D_MODEL ** -0.5),
        'ple_w_proj': nrm(ks[20], (L, D_PLE, D_MODEL), D_PLE ** -0.5),
        'final_norm_g': gain(ks[21], (D_MODEL,)),
    }


def _fwd_reference(x, p, mix_norm_g, w_in, ssd_conv_w, ssd_conv_b, ssd_dt_bias, ssd_a_log, ssd_d, ssd_norm_g,
              pool_w, pool_scale, w_out, ffn_norm_g, ffn_w_up, ffn_conv_w, ffn_conv_b, ffn_w_down,
              ple_norm_g, ple_w_gate, ple_w_proj, final_norm_g):
    h = x
    splits = [SSD_WIDTH, SSD_WIDTH + SSD_XBC, SSD_WIDTH + SSD_XBC + SSD_HEADS]
    for i in range(DEPTH):
        proj = rmsnorm(h, mix_norm_g[i]) @ w_in[i]
        z, xbc, dt_raw, u = jnp.split(proj, splits, axis=-1)
        y_ssd = ssd_branch(z, xbc, dt_raw, ssd_conv_w[i], ssd_conv_b[i], ssd_dt_bias[i],
                           ssd_a_log[i], ssd_d[i], ssd_norm_g[i])
        y_pool = pool_branch(u, pool_w[i], pool_scale[i])
        h = h + jnp.concatenate([y_ssd, y_pool], axis=-1) @ w_out[i]
        h = h + conv_ffn(rmsnorm(h, ffn_norm_g[i]), ffn_w_up[i], ffn_conv_w[i], ffn_conv_b[i], ffn_w_down[i])
        h = h + (p[i] @ ple_w_proj[i]) * jax.nn.sigmoid(rmsnorm(h, ple_norm_g[i]) @ ple_w_gate[i])
    return rmsnorm(h, final_norm_g)


import jax as _jax
import jax.numpy as _jnp

TWIN_FORMAT = 'train_step'
FWD_PARAMS = ['x', 'p', 'mix_norm_g', 'w_in', 'ssd_conv_w', 'ssd_conv_b', 'ssd_dt_bias', 'ssd_a_log', 'ssd_d', 'ssd_norm_g', 'pool_w', 'pool_scale', 'w_out', 'ffn_norm_g', 'ffn_w_up', 'ffn_conv_w', 'ffn_conv_b', 'ffn_w_down', 'ple_norm_g', 'ple_w_gate', 'ple_w_proj', 'final_norm_g']
TWIN_WEIGHTS = ['mix_norm_g', 'w_in', 'ssd_conv_w', 'ssd_conv_b', 'ssd_dt_bias', 'ssd_a_log', 'ssd_d', 'ssd_norm_g', 'pool_w', 'pool_scale', 'w_out', 'ffn_norm_g', 'ffn_w_up', 'ffn_conv_w', 'ffn_conv_b', 'ffn_w_down', 'ple_norm_g', 'ple_w_gate', 'ple_w_proj', 'final_norm_g']
TWIN_DIFF_INPUT = 'x'
TWIN_INPUTS = ['x', 'p', 'mix_norm_g', 'w_in', 'ssd_conv_w', 'ssd_conv_b', 'ssd_dt_bias', 'ssd_a_log', 'ssd_d', 'ssd_norm_g', 'pool_w', 'pool_scale', 'w_out', 'ffn_norm_g', 'ffn_w_up', 'ffn_conv_w', 'ffn_conv_b', 'ffn_w_down', 'ple_norm_g', 'ple_w_gate', 'ple_w_proj', 'final_norm_g', 'loss_target', 'm_mix_norm_g', 'm_w_in', 'm_ssd_conv_w', 'm_ssd_conv_b', 'm_ssd_dt_bias', 'm_ssd_a_log', 'm_ssd_d', 'm_ssd_norm_g', 'm_pool_w', 'm_pool_scale', 'm_w_out', 'm_ffn_norm_g', 'm_ffn_w_up', 'm_ffn_conv_w', 'm_ffn_conv_b', 'm_ffn_w_down', 'm_ple_norm_g', 'm_ple_w_gate', 'm_ple_w_proj', 'm_final_norm_g', 'v_mix_norm_g', 'v_w_in', 'v_ssd_conv_w', 'v_ssd_conv_b', 'v_ssd_dt_bias', 'v_ssd_a_log', 'v_ssd_d', 'v_ssd_norm_g', 'v_pool_w', 'v_pool_scale', 'v_w_out', 'v_ffn_norm_g', 'v_ffn_w_up', 'v_ffn_conv_w', 'v_ffn_conv_b', 'v_ffn_w_down', 'v_ple_norm_g', 'v_ple_w_gate', 'v_ple_w_proj', 'v_final_norm_g']
TWIN_OUTPUTS = ['loss', 'grad_x', 'grad_mix_norm_g', 'grad_w_in', 'grad_ssd_conv_w', 'grad_ssd_conv_b', 'grad_ssd_dt_bias', 'grad_ssd_a_log', 'grad_ssd_d', 'grad_ssd_norm_g', 'grad_pool_w', 'grad_pool_scale', 'grad_w_out', 'grad_ffn_norm_g', 'grad_ffn_w_up', 'grad_ffn_conv_w', 'grad_ffn_conv_b', 'grad_ffn_w_down', 'grad_ple_norm_g', 'grad_ple_w_gate', 'grad_ple_w_proj', 'grad_final_norm_g', 'delta_mix_norm_g', 'delta_w_in', 'delta_ssd_conv_w', 'delta_ssd_conv_b', 'delta_ssd_dt_bias', 'delta_ssd_a_log', 'delta_ssd_d', 'delta_ssd_norm_g', 'delta_pool_w', 'delta_pool_scale', 'delta_w_out', 'delta_ffn_norm_g', 'delta_ffn_w_up', 'delta_ffn_conv_w', 'delta_ffn_conv_b', 'delta_ffn_w_down', 'delta_ple_norm_g', 'delta_ple_w_gate', 'delta_ple_w_proj', 'delta_final_norm_g', 'new_m_mix_norm_g', 'new_m_w_in', 'new_m_ssd_conv_w', 'new_m_ssd_conv_b', 'new_m_ssd_dt_bias', 'new_m_ssd_a_log', 'new_m_ssd_d', 'new_m_ssd_norm_g', 'new_m_pool_w', 'new_m_pool_scale', 'new_m_w_out', 'new_m_ffn_norm_g', 'new_m_ffn_w_up', 'new_m_ffn_conv_w', 'new_m_ffn_conv_b', 'new_m_ffn_w_down', 'new_m_ple_norm_g', 'new_m_ple_w_gate', 'new_m_ple_w_proj', 'new_m_final_norm_g', 'new_v_mix_norm_g', 'new_v_w_in', 'new_v_ssd_conv_w', 'new_v_ssd_conv_b', 'new_v_ssd_dt_bias', 'new_v_ssd_a_log', 'new_v_ssd_d', 'new_v_ssd_norm_g', 'new_v_pool_w', 'new_v_pool_scale', 'new_v_w_out', 'new_v_ffn_norm_g', 'new_v_ffn_w_up', 'new_v_ffn_conv_w', 'new_v_ffn_conv_b', 'new_v_ffn_w_down', 'new_v_ple_norm_g', 'new_v_ple_w_gate', 'new_v_ple_w_proj', 'new_v_final_norm_g']
TWIN_LEAF_KINDS = {'loss': 'loss', 'grad_x': 'grad_x', 'grad_mix_norm_g': 'grad_w', 'grad_w_in': 'grad_w', 'grad_ssd_conv_w': 'grad_w', 'grad_ssd_conv_b': 'grad_w', 'grad_ssd_dt_bias': 'grad_w', 'grad_ssd_a_log': 'grad_w', 'grad_ssd_d': 'grad_w', 'grad_ssd_norm_g': 'grad_w', 'grad_pool_w': 'grad_w', 'grad_pool_scale': 'grad_w', 'grad_w_out': 'grad_w', 'grad_ffn_norm_g': 'grad_w', 'grad_ffn_w_up': 'grad_w', 'grad_ffn_conv_w': 'grad_w', 'grad_ffn_conv_b': 'grad_w', 'grad_ffn_w_down': 'grad_w', 'grad_ple_norm_g': 'grad_w', 'grad_ple_w_gate': 'grad_w', 'grad_ple_w_proj': 'grad_w', 'grad_final_norm_g': 'grad_w', 'delta_mix_norm_g': 'delta_w', 'delta_w_in': 'delta_w', 'delta_ssd_conv_w': 'delta_w', 'delta_ssd_conv_b': 'delta_w', 'delta_ssd_dt_bias': 'delta_w', 'delta_ssd_a_log': 'delta_w', 'delta_ssd_d': 'delta_w', 'delta_ssd_norm_g': 'delta_w', 'delta_pool_w': 'delta_w', 'delta_pool_scale': 'delta_w', 'delta_w_out': 'delta_w', 'delta_ffn_norm_g': 'delta_w', 'delta_ffn_w_up': 'delta_w', 'delta_ffn_conv_w': 'delta_w', 'delta_ffn_conv_b': 'delta_w', 'delta_ffn_w_down': 'delta_w', 'delta_ple_norm_g': 'delta_w', 'delta_ple_w_gate': 'delta_w', 'delta_ple_w_proj': 'delta_w', 'delta_final_norm_g': 'delta_w', 'new_m_mix_norm_g': 'new_m', 'new_m_w_in': 'new_m', 'new_m_ssd_conv_w': 'new_m', 'new_m_ssd_conv_b': 'new_m', 'new_m_ssd_dt_bias': 'new_m', 'new_m_ssd_a_log': 'new_m', 'new_m_ssd_d': 'new_m', 'new_m_ssd_norm_g': 'new_m', 'new_m_pool_w': 'new_m', 'new_m_pool_scale': 'new_m', 'new_m_w_out': 'new_m', 'new_m_ffn_norm_g': 'new_m', 'new_m_ffn_w_up': 'new_m', 'new_m_ffn_conv_w': 'new_m', 'new_m_ffn_conv_b': 'new_m', 'new_m_ffn_w_down': 'new_m', 'new_m_ple_norm_g': 'new_m', 'new_m_ple_w_gate': 'new_m', 'new_m_ple_w_proj': 'new_m', 'new_m_final_norm_g': 'new_m', 'new_v_mix_norm_g': 'new_v', 'new_v_w_in': 'new_v', 'new_v_ssd_conv_w': 'new_v', 'new_v_ssd_conv_b': 'new_v', 'new_v_ssd_dt_bias': 'new_v', 'new_v_ssd_a_log': 'new_v', 'new_v_ssd_d': 'new_v', 'new_v_ssd_norm_g': 'new_v', 'new_v_pool_w': 'new_v', 'new_v_pool_scale': 'new_v', 'new_v_w_out': 'new_v', 'new_v_ffn_norm_g': 'new_v', 'new_v_ffn_w_up': 'new_v', 'new_v_ffn_conv_w': 'new_v', 'new_v_ffn_conv_b': 'new_v', 'new_v_ffn_w_down': 'new_v', 'new_v_ple_norm_g': 'new_v', 'new_v_ple_w_gate': 'new_v', 'new_v_ple_w_proj': 'new_v', 'new_v_final_norm_g': 'new_v'}


def _forward(args):
    return _fwd_reference(*[args[k] for k in FWD_PARAMS])


def _output_shape():
    def fwd():
        inp = _fwd_setup_inputs(0)
        return _fwd_reference(*[inp[k] for k in FWD_PARAMS])
    out = _jax.eval_shape(fwd)
    return out.shape, out.dtype

N_MICROBATCH = 1
ADAM_LR = 0.001
ADAM_B1 = 0.9
ADAM_B2 = 0.999
ADAM_EPS = 1e-08
ADAM_WD = 0.01
ADAM_STEP = 10
PER_EXAMPLE_BATCH_AXIS = {'x': 0, 'p': 1, 'loss_target': 0}
SHARED_INPUTS = []
_WEIGHT_DTYPES = {'mix_norm_g': _jnp.float32, 'w_in': _jnp.float32, 'ssd_conv_w': _jnp.float32, 'ssd_conv_b': _jnp.float32, 'ssd_dt_bias': _jnp.float32, 'ssd_a_log': _jnp.float32, 'ssd_d': _jnp.float32, 'ssd_norm_g': _jnp.float32, 'pool_w': _jnp.float32, 'pool_scale': _jnp.float32, 'w_out': _jnp.float32, 'ffn_norm_g': _jnp.float32, 'ffn_w_up': _jnp.float32, 'ffn_conv_w': _jnp.float32, 'ffn_conv_b': _jnp.float32, 'ffn_w_down': _jnp.float32, 'ple_norm_g': _jnp.float32, 'ple_w_gate': _jnp.float32, 'ple_w_proj': _jnp.float32, 'final_norm_g': _jnp.float32}
MOMENT_SCALE = {'mix_norm_g': 2.958672e-01, 'w_in': 2.054639e-01, 'ssd_conv_w': 1.813981e-01, 'ssd_conv_b': 2.472392e-01, 'ssd_dt_bias': 1.056819e+00, 'ssd_a_log': 8.096135e-01, 'ssd_d': 1.085964e+00, 'ssd_norm_g': 2.479799e-01, 'pool_w': 2.104712e-01, 'pool_scale': 2.069214e-01, 'w_out': 2.232242e-01, 'ffn_norm_g': 1.967125e-01, 'ffn_w_up': 8.260694e-02, 'ffn_conv_w': 8.295980e-02, 'ffn_conv_b': 8.146752e-02, 'ffn_w_down': 1.342416e-01, 'ple_norm_g': 4.395592e-02, 'ple_w_gate': 4.249281e-02, 'ple_w_proj': 1.086836e-01, 'final_norm_g': 1.282914e+02}


def _to_microbatches(a, axis):
    t = _jnp.moveaxis(a, axis, 0)
    t = t.reshape((N_MICROBATCH, t.shape[0] // N_MICROBATCH) + t.shape[1:])
    return _jnp.moveaxis(t, 1, axis + 1)


def setup_inputs(seed: int = 0) -> dict:
    inp = _fwd_setup_inputs(seed)
    key = _jax.random.fold_in(_jax.random.key(seed), 7919)
    shape, _ = _output_shape()
    out = dict(inp)
    out["loss_target"] = _jax.random.normal(_jax.random.fold_in(key, 0), shape, _jnp.float32)
    for i, name in enumerate(TWIN_WEIGHTS):
        w = inp[name].astype(_jnp.float32)
        if MOMENT_SCALE is None:
            s = _jnp.sqrt(_jnp.mean(_jnp.square(w)) + 1e-30)
        else:
            s = MOMENT_SCALE[name]
        km, kv = _jax.random.split(_jax.random.fold_in(key, i + 1))
        out[name] = w
        out["m_" + name] = s * _jax.random.normal(km, w.shape, _jnp.float32)
        out["v_" + name] = (s * s) * _jax.random.uniform(kv, w.shape, _jnp.float32, 0.5, 1.5)
    if N_MICROBATCH > 1:
        for name, axis in PER_EXAMPLE_BATCH_AXIS.items():
            out[name] = _to_microbatches(out[name], axis)
    return {'x': out['x'], 'p': out['p'], 'mix_norm_g': out['mix_norm_g'], 'w_in': out['w_in'], 'ssd_conv_w': out['ssd_conv_w'], 'ssd_conv_b': out['ssd_conv_b'], 'ssd_dt_bias': out['ssd_dt_bias'], 'ssd_a_log': out['ssd_a_log'], 'ssd_d': out['ssd_d'], 'ssd_norm_g': out['ssd_norm_g'], 'pool_w': out['pool_w'], 'pool_scale': out['pool_scale'], 'w_out': out['w_out'], 'ffn_norm_g': out['ffn_norm_g'], 'ffn_w_up': out['ffn_w_up'], 'ffn_conv_w': out['ffn_conv_w'], 'ffn_conv_b': out['ffn_conv_b'], 'ffn_w_down': out['ffn_w_down'], 'ple_norm_g': out['ple_norm_g'], 'ple_w_gate': out['ple_w_gate'], 'ple_w_proj': out['ple_w_proj'], 'final_norm_g': out['final_norm_g'], 'loss_target': out['loss_target'], 'm_mix_norm_g': out['m_mix_norm_g'], 'm_w_in': out['m_w_in'], 'm_ssd_conv_w': out['m_ssd_conv_w'], 'm_ssd_conv_b': out['m_ssd_conv_b'], 'm_ssd_dt_bias': out['m_ssd_dt_bias'], 'm_ssd_a_log': out['m_ssd_a_log'], 'm_ssd_d': out['m_ssd_d'], 'm_ssd_norm_g': out['m_ssd_norm_g'], 'm_pool_w': out['m_pool_w'], 'm_pool_scale': out['m_pool_scale'], 'm_w_out': out['m_w_out'], 'm_ffn_norm_g': out['m_ffn_norm_g'], 'm_ffn_w_up': out['m_ffn_w_up'], 'm_ffn_conv_w': out['m_ffn_conv_w'], 'm_ffn_conv_b': out['m_ffn_conv_b'], 'm_ffn_w_down': out['m_ffn_w_down'], 'm_ple_norm_g': out['m_ple_norm_g'], 'm_ple_w_gate': out['m_ple_w_gate'], 'm_ple_w_proj': out['m_ple_w_proj'], 'm_final_norm_g': out['m_final_norm_g'], 'v_mix_norm_g': out['v_mix_norm_g'], 'v_w_in': out['v_w_in'], 'v_ssd_conv_w': out['v_ssd_conv_w'], 'v_ssd_conv_b': out['v_ssd_conv_b'], 'v_ssd_dt_bias': out['v_ssd_dt_bias'], 'v_ssd_a_log': out['v_ssd_a_log'], 'v_ssd_d': out['v_ssd_d'], 'v_ssd_norm_g': out['v_ssd_norm_g'], 'v_pool_w': out['v_pool_w'], 'v_pool_scale': out['v_pool_scale'], 'v_w_out': out['v_w_out'], 'v_ffn_norm_g': out['v_ffn_norm_g'], 'v_ffn_w_up': out['v_ffn_w_up'], 'v_ffn_conv_w': out['v_ffn_conv_w'], 'v_ffn_conv_b': out['v_ffn_conv_b'], 'v_ffn_w_down': out['v_ffn_w_down'], 'v_ple_norm_g': out['v_ple_norm_g'], 'v_ple_w_gate': out['v_ple_w_gate'], 'v_ple_w_proj': out['v_ple_w_proj'], 'v_final_norm_g': out['v_final_norm_g']}


def _loss(weights, diff, rest, loss_target):
    with _jax.named_scope("forward"):
        args = {**rest, TWIN_DIFF_INPUT: diff, **{k: w.astype(_WEIGHT_DTYPES[k]) for k, w in weights.items()}}
        y = _forward(args)
    with _jax.named_scope("loss_head"):
        err = _jnp.square(y.astype(_jnp.float32) - loss_target)
        return 0.5 * _jnp.sum(_jnp.mean(err, axis=-1)) if err.ndim else 0.5 * err


def _adamw(w, g, m, v):
    m = ADAM_B1 * m + (1.0 - ADAM_B1) * g
    v = ADAM_B2 * v + (1.0 - ADAM_B2) * _jnp.square(g)
    m_hat = m / (1.0 - ADAM_B1 ** ADAM_STEP)
    v_hat = v / (1.0 - ADAM_B2 ** ADAM_STEP)
    delta = -ADAM_LR * (m_hat / (_jnp.sqrt(v_hat) + ADAM_EPS) + ADAM_WD * w)
    return delta, m, v


def reference(x, p, mix_norm_g, w_in, ssd_conv_w, ssd_conv_b, ssd_dt_bias, ssd_a_log, ssd_d, ssd_norm_g, pool_w, pool_scale, w_out, ffn_norm_g, ffn_w_up, ffn_conv_w, ffn_conv_b, ffn_w_down, ple_norm_g, ple_w_gate, ple_w_proj, final_norm_g, loss_target, m_mix_norm_g, m_w_in, m_ssd_conv_w, m_ssd_conv_b, m_ssd_dt_bias, m_ssd_a_log, m_ssd_d, m_ssd_norm_g, m_pool_w, m_pool_scale, m_w_out, m_ffn_norm_g, m_ffn_w_up, m_ffn_conv_w, m_ffn_conv_b, m_ffn_w_down, m_ple_norm_g, m_ple_w_gate, m_ple_w_proj, m_final_norm_g, v_mix_norm_g, v_w_in, v_ssd_conv_w, v_ssd_conv_b, v_ssd_dt_bias, v_ssd_a_log, v_ssd_d, v_ssd_norm_g, v_pool_w, v_pool_scale, v_w_out, v_ffn_norm_g, v_ffn_w_up, v_ffn_conv_w, v_ffn_conv_b, v_ffn_w_down, v_ple_norm_g, v_ple_w_gate, v_ple_w_proj, v_final_norm_g):
    given = dict(x=x, p=p, mix_norm_g=mix_norm_g, w_in=w_in, ssd_conv_w=ssd_conv_w, ssd_conv_b=ssd_conv_b, ssd_dt_bias=ssd_dt_bias, ssd_a_log=ssd_a_log, ssd_d=ssd_d, ssd_norm_g=ssd_norm_g, pool_w=pool_w, pool_scale=pool_scale, w_out=w_out, ffn_norm_g=ffn_norm_g, ffn_w_up=ffn_w_up, ffn_conv_w=ffn_conv_w, ffn_conv_b=ffn_conv_b, ffn_w_down=ffn_w_down, ple_norm_g=ple_norm_g, ple_w_gate=ple_w_gate, ple_w_proj=ple_w_proj, final_norm_g=final_norm_g, loss_target=loss_target, m_mix_norm_g=m_mix_norm_g, m_w_in=m_w_in, m_ssd_conv_w=m_ssd_conv_w, m_ssd_conv_b=m_ssd_conv_b, m_ssd_dt_bias=m_ssd_dt_bias, m_ssd_a_log=m_ssd_a_log, m_ssd_d=m_ssd_d, m_ssd_norm_g=m_ssd_norm_g, m_pool_w=m_pool_w, m_pool_scale=m_pool_scale, m_w_out=m_w_out, m_ffn_norm_g=m_ffn_norm_g, m_ffn_w_up=m_ffn_w_up, m_ffn_conv_w=m_ffn_conv_w, m_ffn_conv_b=m_ffn_conv_b, m_ffn_w_down=m_ffn_w_down, m_ple_norm_g=m_ple_norm_g, m_ple_w_gate=m_ple_w_gate, m_ple_w_proj=m_ple_w_proj, m_final_norm_g=m_final_norm_g, v_mix_norm_g=v_mix_norm_g, v_w_in=v_w_in, v_ssd_conv_w=v_ssd_conv_w, v_ssd_conv_b=v_ssd_conv_b, v_ssd_dt_bias=v_ssd_dt_bias, v_ssd_a_log=v_ssd_a_log, v_ssd_d=v_ssd_d, v_ssd_norm_g=v_ssd_norm_g, v_pool_w=v_pool_w, v_pool_scale=v_pool_scale, v_w_out=v_w_out, v_ffn_norm_g=v_ffn_norm_g, v_ffn_w_up=v_ffn_w_up, v_ffn_conv_w=v_ffn_conv_w, v_ffn_conv_b=v_ffn_conv_b, v_ffn_w_down=v_ffn_w_down, v_ple_norm_g=v_ple_norm_g, v_ple_w_gate=v_ple_w_gate, v_ple_w_proj=v_ple_w_proj, v_final_norm_g=v_final_norm_g)
    weights = {n: given[n] for n in TWIN_WEIGHTS}
    shared = {n: given[n] for n in SHARED_INPUTS}
    per_example = {n: given[n] for n in ['x', 'p']}
    grad_fn = _jax.value_and_grad(_loss, argnums=(0, 1))

    def one_microbatch(ex, loss_target):
        ex = dict(ex)
        diff = ex.pop(TWIN_DIFF_INPUT)
        return grad_fn(weights, diff, {**shared, **ex}, loss_target)

    if N_MICROBATCH == 1:
        loss, (grad_w, grad_x) = one_microbatch(per_example, given["loss_target"])
    else:
        def body(carry, xs):
            loss_sum, grad_sum = carry
            l_k, (gw_k, gx_k) = one_microbatch(xs[0], xs[1])
            with _jax.named_scope("update"):
                return (loss_sum + l_k, _jax.tree.map(_jnp.add, grad_sum, gw_k)), gx_k

        init = (_jnp.zeros((), _jnp.float32), _jax.tree.map(_jnp.zeros_like, weights))
        (loss, grad_w), grad_x = _jax.lax.scan(body, init, (per_example, given["loss_target"]))
    with _jax.named_scope("update"):
        delta_w, new_m, new_v = {}, {}, {}
        for n in TWIN_WEIGHTS:
            delta_w[n], new_m[n], new_v[n] = _adamw(weights[n], grad_w[n], given["m_" + n], given["v_" + n])
    return (loss, grad_x, *[grad_w[n] for n in TWIN_WEIGHTS], *[delta_w[n] for n in TWIN_WEIGHTS],
            *[new_m[n] for n in TWIN_WEIGHTS], *[new_v[n] for n in TWIN_WEIGHTS])
```

```python
import functools
import math

import jax
import jax.numpy as jnp
from jax import lax
from jax.experimental import pallas as pl
from jax.experimental.pallas import tpu as pltpu

F32 = jnp.float32
BF16 = jnp.bfloat16
HI = lax.Precision.HIGHEST

N_DEV = 8
EPS = 1e-6
DEPTH = 4
D_MODEL = 1024
D_PLE = 256
SSD_WIDTH = 512
SSD_HEADS = 8
SSD_HEAD_DIM = 64
SSD_GROUPS = 2
SSD_STATE = 128
SSD_CONV = 4
CHUNK = 128
SSD_XBC = 1024
POOL_WINDOWS = (2, 4, 8, 16)
POOL_WIDTH = 512
POOL_GROUP = 128
POOL_HALO = 16
D_IN_PROJ = 2056
D_FF = 2816
D_UP = 2 * D_FF
FFN_CONV = 3
SUBLANES = 8
LANES = 128
N_PROJ = 2176
COL_XBC, COL_Z, COL_U, COL_DT = 0, 1024, 1536, 2048
N_PAIRS = SSD_HEADS // 2
ADAM_LR, ADAM_B1, ADAM_B2, ADAM_EPS, ADAM_WD, ADAM_STEP = 0.001, 0.9, 0.999, 1e-08, 0.01, 10
GELU_C = math.sqrt(2.0 / math.pi)
GELU_A = 0.044715
VMEM_LIMIT = 56 * 1024 * 1024

NT_DIMS = (((1,), (1,)), ((), ()))
TN_DIMS = (((0,), (0,)), ((), ()))


def _params(*sem):
    return pltpu.CompilerParams(dimension_semantics=sem, vmem_limit_bytes=VMEM_LIMIT)


def _dot(a, b):
    return jnp.dot(a, b, preferred_element_type=F32)


def _dot_nt(a, b):
    return lax.dot_general(a, b, NT_DIMS, preferred_element_type=F32)


def _dot_tn(a, b):
    return lax.dot_general(a, b, TN_DIMS, preferred_element_type=F32)


def _hdot(a, b):
    return jnp.dot(a, b, preferred_element_type=F32, precision=HI)


def _headsum(q, e):
    return lax.dot_general(q, e, NT_DIMS, preferred_element_type=F32, precision=HI)


def _colsum(v):
    return jnp.sum(v, axis=0, keepdims=True)


def _sigmoid(v):
    return 1.0 / (1.0 + jnp.exp(-v))


def _softplus(v):
    e = jnp.exp(-jnp.abs(v))
    return jnp.maximum(v, 0.0) + jnp.where(e < 1e-4, e * (1.0 - 0.5 * e), jnp.log(1.0 + e))


def _rms_r(x):
    return lax.rsqrt(jnp.mean(x * x, axis=-1, keepdims=True) + EPS)


def _rms_bwd(x, r, g, dn):
    xhat = x * r
    gd = dn * g
    dx = r * (gd - xhat * jnp.mean(gd * xhat, axis=-1, keepdims=True))
    return dx, _colsum(dn * xhat)


def _gelu_parts(v):
    th = jnp.tanh(GELU_C * (v + GELU_A * v * v * v))
    gelu = 0.5 * v * (1.0 + th)
    dgelu = 0.5 * (1.0 + th) + 0.5 * v * (1.0 - th * th) * GELU_C * (1.0 + 3.0 * GELU_A * v * v)
    return gelu, dgelu


def _tile(t, want):
    return min(t, want)


def _exchange(src, *, scatter, name):
    blk = src.shape[1:] if scatter else src.shape

    def body(src_ref, out_ref, send_sems, recv_sems, local_sem):
        x, y, c = lax.axis_index("x"), lax.axis_index("y"), lax.axis_index("c")
        me = 4 * x + 2 * y + c

        def mine(idx):
            return src_ref.at[idx] if scatter else src_ref

        local = pltpu.make_async_copy(mine(me), out_ref.at[me], local_sem)
        local.start()
        sends, recvs = [], []
        for k in range(1, N_DEV):
            px = 1 - x if k & 4 else x
            py = 1 - y if k & 2 else y
            pc = 1 - c if k & 1 else c
            peer = 4 * px + 2 * py + pc
            send = pltpu.make_async_remote_copy(
                src_ref=mine(peer), dst_ref=out_ref.at[me], send_sem=send_sems.at[k - 1], recv_sem=recv_sems.at[k - 1],
                device_id=(px, py, pc), device_id_type=pl.DeviceIdType.MESH)
            send.start()
            sends.append(send)
            recvs.append(pltpu.make_async_remote_copy(
                src_ref=mine(peer), dst_ref=out_ref.at[peer], send_sem=send_sems.at[k - 1], recv_sem=recv_sems.at[k - 1],
                device_id=(px, py, pc), device_id_type=pl.DeviceIdType.MESH))
        for send, recv in zip(sends, recvs):
            send.wait_send()
            recv.wait_recv()
        local.wait()

    return pl.pallas_call(
        body, name=name,
        out_shape=jax.ShapeDtypeStruct((N_DEV,) + tuple(blk), src.dtype),
        in_specs=[pl.BlockSpec(memory_space=pl.ANY)],
        out_specs=pl.BlockSpec(memory_space=pl.ANY),
        scratch_shapes=[pltpu.SemaphoreType.DMA((N_DEV - 1,)), pltpu.SemaphoreType.DMA((N_DEV - 1,)),
                        pltpu.SemaphoreType.DMA],
    )(src)


def _norm_matmul(h, w, g=None, *, tt, tn, name):
    t, k = h.shape
    n = w.shape[1]
    tt, tn = _tile(t, tt), _tile(n, tn)
    normed = g is not None

    def body(*refs):
        if normed:
            h_ref, g_ref, w_ref, o_ref = refs
            x = h_ref[...]
            xn = (x * _rms_r(x) * g_ref[...]).astype(BF16)
        else:
            h_ref, w_ref, o_ref = refs
            xn = h_ref[...].astype(BF16)
        o_ref[...] = _dot(xn, w_ref[...])

    in_specs = [pl.BlockSpec((tt, k), lambda j, i: (i, 0))]
    args = [h]
    if normed:
        in_specs.append(pl.BlockSpec((1, k), lambda j, i: (0, 0)))
        args.append(g)
    in_specs.append(pl.BlockSpec((k, tn), lambda j, i: (0, j)))
    args.append(w)
    return pl.pallas_call(
        body, name=name, grid=(n // tn, t // tt), in_specs=in_specs,
        out_specs=pl.BlockSpec((tt, tn), lambda j, i: (i, j)), out_shape=jax.ShapeDtypeStruct((t, n), F32),
        compiler_params=_params("arbitrary", "arbitrary"))(*args)


def _matmul_tn(a, b, *, tm, tn, tk, name):
    t, m = a.shape
    n = b.shape[1]
    tm, tn, tk = _tile(m, tm), _tile(n, tn), _tile(t, tk)

    def body(a_ref, b_ref, o_ref):
        @pl.when(pl.program_id(2) == 0)
        def _():
            o_ref[...] = jnp.zeros_like(o_ref)

        o_ref[...] += _dot_tn(a_ref[...].astype(BF16), b_ref[...].astype(BF16))

    return pl.pallas_call(
        body, name=name, grid=(m // tm, n // tn, t // tk),
        in_specs=[pl.BlockSpec((tk, tm), lambda i, j, kk: (kk, i)), pl.BlockSpec((tk, tn), lambda i, j, kk: (kk, j))],
        out_specs=pl.BlockSpec((tm, tn), lambda i, j, kk: (i, j)),
        out_shape=jax.ShapeDtypeStruct((m, n), F32),
        compiler_params=_params("arbitrary", "arbitrary", "arbitrary"))(a, b)


def _matmul_rmsbwd(a, wt, x, g, dh, *, tt, tk, name):
    t, k = a.shape
    d = wt.shape[1]
    tt, tk = _tile(t, tt), _tile(k, tk)
    nk = k // tk

    def body(a_ref, w_ref, x_ref, g_ref, dh_ref, o_ref, dg_ref, acc):
        i, kk = pl.program_id(0), pl.program_id(1)

        @pl.when(kk == 0)
        def _():
            acc[...] = jnp.zeros_like(acc)

        @pl.when((i == 0) & (kk == 0))
        def _():
            dg_ref[...] = jnp.zeros_like(dg_ref)

        acc[...] += _dot(a_ref[...], w_ref[...])

        @pl.when(kk == nk - 1)
        def _():
            xv = x_ref[...]
            dx, dg = _rms_bwd(xv, _rms_r(xv), g_ref[...], acc[...])
            o_ref[...] = dh_ref[...] + dx
            dg_ref[...] += dg

    return pl.pallas_call(
        body, name=name, grid=(t // tt, nk),
        in_specs=[pl.BlockSpec((tt, tk), lambda i, kk: (i, kk)), pl.BlockSpec((tk, d), lambda i, kk: (kk, 0)),
                  pl.BlockSpec((tt, d), lambda i, kk: (i, 0)), pl.BlockSpec((1, d), lambda i, kk: (0, 0)),
                  pl.BlockSpec((tt, d), lambda i, kk: (i, 0))],
        out_specs=[pl.BlockSpec((tt, d), lambda i, kk: (i, 0)), pl.BlockSpec((1, d), lambda i, kk: (0, 0))],
        out_shape=[jax.ShapeDtypeStruct((t, d), F32), jax.ShapeDtypeStruct((1, d), F32)],
        scratch_shapes=[pltpu.VMEM((tt, d), F32)],
        compiler_params=_params("arbitrary", "arbitrary"))(a, wt, x, g, dh)


def _ssd_tile_prologue(i_is_first, xbc_ref, halo_ref, dt_ref, cw_ref, cb_ref, dtb_ref, alog_ref, e_ref, buf, xc_scr,
                       xa_scr, a_scr, dte_scr, x_scr, ts):
    buf[0:SUBLANES, :] = jnp.where(i_is_first, 0.0, halo_ref[...])
    buf[SUBLANES:SUBLANES + ts, :] = xbc_ref[...]
    cw = cw_ref[...]
    xc = cb_ref[...]
    for k in range(SSD_CONV):
        off = SUBLANES - (SSD_CONV - 1) + k
        xc = xc + cw[k:k + 1, :] * buf[off:off + ts, :]
    if xc_scr is not None:
        xc_scr[...] = xc
    xa_scr[...] = xc * _sigmoid(xc)
    dt = _softplus(dt_ref[...] + dtb_ref[...])
    a_neg = -jnp.exp(alog_ref[...])
    a_scr[...] = dt * a_neg
    dte = _hdot(dt, e_ref[...])
    dte_scr[...] = dte
    x_scr[...] = xa_scr[:, 0:SSD_WIDTH] * dte
    return dt, a_neg


def _chunk_decays(a_c, tril, e):
    cs = _hdot(tril, a_c)
    cs_t = cs.T
    cs_e = _hdot(cs, e)
    last_e = cs_e[CHUNK - 1:CHUNK, :]
    return cs, cs_t, cs_e, last_e


def _ssd_fwd(proj, cw, cb, dtb, alog, dexp, ng, tril, e, *, ts, name):
    t = proj.shape[0]
    ts = _tile(t, ts)
    nch = ts // CHUNK
    hb = ts // SUBLANES

    def body(xbc_ref, halo_ref, z_ref, dt_ref, cw_ref, cb_ref, dtb_ref, alog_ref, dexp_ref, ng_ref, tril_ref, e_ref,
             y_ref, ypre_ref, st_ref, buf, xa_scr, a_scr, dte_scr, x_scr, ys_scr, hstate):
        i = pl.program_id(0)

        @pl.when(i == 0)
        def _():
            hstate[...] = jnp.zeros_like(hstate)

        _ssd_tile_prologue(i == 0, xbc_ref, halo_ref, dt_ref, cw_ref, cb_ref, dtb_ref, alog_ref, e_ref, buf, None,
                           xa_scr, a_scr, dte_scr, x_scr, ts)
        tril = tril_ref[...]
        e_mat = e_ref[...]
        causal = tril > 0.5
        lane = lax.broadcasted_iota(jnp.int32, (CHUNK, LANES), 1)

        def chunk(c, carry):
            r0 = pl.multiple_of(c * CHUNK, CHUNK)
            rows = pl.ds(r0, CHUNK)
            cs, cs_t, cs_e, last_e = _chunk_decays(a_scr[rows, :], tril, e_mat)
            decay_e = jnp.exp(last_e - cs_e)
            ecs_e = jnp.exp(cs_e)
            xc = x_scr[rows, :]
            xb = xc.astype(BF16)
            xd = (xc * decay_e).astype(BF16)
            for g in range(SSD_GROUPS):
                bg = xa_scr[rows, SSD_WIDTH + g * SSD_STATE:SSD_WIDTH + (g + 1) * SSD_STATE].astype(BF16)
                cg = xa_scr[rows, SSD_WIDTH + (SSD_GROUPS + g) * SSD_STATE:
                            SSD_WIDTH + (SSD_GROUPS + g + 1) * SSD_STATE].astype(BF16)
                cbm = _dot_nt(cg, bg)
                for jj in range(2):
                    j = 2 * g + jj
                    cols = slice(j * LANES, (j + 1) * LANES)
                    xp = xb[:, cols]
                    ypair = jnp.zeros((CHUNK, LANES), F32)
                    for hh in range(2):
                        h = 2 * j + hh
                        seg = jnp.exp(jnp.where(causal, cs[:, h:h + 1] - cs_t[h:h + 1, :], -jnp.inf))
                        m = (cbm * seg).astype(BF16)
                        half = (lane < SSD_HEAD_DIM) if hh == 0 else (lane >= SSD_HEAD_DIM)
                        ypair = ypair + _dot(m, jnp.where(half, xp, jnp.zeros_like(xp)))
                    hp = hstate[j]
                    st_ref[c, j] = hp
                    ypair = ypair + _dot(cg, hp.astype(BF16)) * ecs_e[:, cols]
                    ys_scr[rows, cols] = ypair
                    hstate[j] = hp * jnp.exp(last_e[:, cols]) + _dot_tn(bg, xd[:, cols])
            return carry

        lax.fori_loop(0, nch, chunk, 0)
        ypre = ys_scr[...] + xa_scr[:, 0:SSD_WIDTH] * dexp_ref[...]
        ypre_ref[...] = ypre
        z = z_ref[...]
        yg = ypre * (z * _sigmoid(z))
        gw = SSD_WIDTH // SSD_GROUPS
        outs = []
        for g in range(SSD_GROUPS):
            v = yg[:, g * gw:(g + 1) * gw]
            outs.append(v * _rms_r(v))
        y_ref[...] = jnp.concatenate(outs, axis=1) * ng_ref[...]

    full = lambda shape: pl.BlockSpec(shape, lambda i: tuple(0 for _ in shape))
    return pl.pallas_call(
        body, name=name, grid=(t // ts,),
        in_specs=[pl.BlockSpec((ts, SSD_XBC), lambda i: (i, COL_XBC // SSD_XBC)),
                  pl.BlockSpec((SUBLANES, SSD_XBC), lambda i: (jnp.maximum(i * hb - 1, 0), COL_XBC // SSD_XBC)),
                  pl.BlockSpec((ts, SSD_WIDTH), lambda i: (i, COL_Z // SSD_WIDTH)),
                  pl.BlockSpec((ts, LANES), lambda i: (i, COL_DT // LANES)),
                  full((SSD_CONV, SSD_XBC)), full((1, SSD_XBC)), full((1, LANES)), full((1, LANES)),
                  full((1, SSD_WIDTH)), full((1, SSD_WIDTH)), full((CHUNK, CHUNK)), full((LANES, SSD_WIDTH))],
        out_specs=[pl.BlockSpec((ts, SSD_WIDTH), lambda i: (i, 0)), pl.BlockSpec((ts, SSD_WIDTH), lambda i: (i, 0)),
                   pl.BlockSpec((nch, N_PAIRS, SSD_STATE, LANES), lambda i: (i, 0, 0, 0))],
        out_shape=[jax.ShapeDtypeStruct((t, SSD_WIDTH), F32), jax.ShapeDtypeStruct((t, SSD_WIDTH), F32),
                   jax.ShapeDtypeStruct((t // CHUNK, N_PAIRS, SSD_STATE, LANES), F32)],
        scratch_shapes=[pltpu.VMEM((SUBLANES + ts, SSD_XBC), F32), pltpu.VMEM((ts, SSD_XBC), F32),
                        pltpu.VMEM((ts, LANES), F32), pltpu.VMEM((ts, SSD_WIDTH), F32),
                        pltpu.VMEM((ts, SSD_WIDTH), F32), pltpu.VMEM((ts, SSD_WIDTH), F32),
                        pltpu.VMEM((N_PAIRS, SSD_STATE, LANES), F32)],
        compiler_params=_params("arbitrary"))(proj, proj, proj, proj, cw, cb, dtb, alog, dexp, ng, tril, e)


def _ssd_bwd(dymix, proj, ypre, states, du, cw, cb, dtb, alog, dexp, ng, tril, triu, e, *, ts, name):
    t = proj.shape[0]
    ts = _tile(t, ts)
    nch = ts // CHUNK
    hb = ts // SUBLANES
    nt = t // ts

    def body(dy_ref, xbc_ref, halo_ref, z_ref, dt_ref, ypre_ref, st_ref, du_ref, cw_ref, cb_ref, dtb_ref, alog_ref,
             dexp_ref, ng_ref, tril_ref, triu_ref, e_ref,
             dproj_ref, dcw_ref, dcb_ref, ddtb_ref, dalog_ref, dd_ref, dng_ref,
             buf, xc_scr, xa_scr, a_scr, dte_scr, x_scr, dyp_scr, dxa_scr, dx_scr, dbuf, carry, gstate):
        i = pl.program_id(0)

        @pl.when(i == 0)
        def _():
            gstate[...] = jnp.zeros_like(gstate)
            carry[...] = jnp.zeros_like(carry)
            for ref in (dcw_ref, dcb_ref, ddtb_ref, dalog_ref, dd_ref, dng_ref):
                ref[...] = jnp.zeros_like(ref)

        dt, a_neg = _ssd_tile_prologue(i == nt - 1, xbc_ref, halo_ref, dt_ref, cw_ref, cb_ref, dtb_ref, alog_ref, e_ref,
                                       buf, xc_scr, xa_scr, a_scr, dte_scr, x_scr, ts)
        tril = tril_ref[...]
        triu = triu_ref[...]
        e_mat = e_ref[...]
        causal = tril > 0.5
        lane = lax.broadcasted_iota(jnp.int32, (CHUNK, LANES), 1)
        sub = lax.broadcasted_iota(jnp.int32, (CHUNK, LANES), 0)

        z = z_ref[...]
        sig = _sigmoid(z)
        zs = z * sig
        ypre = ypre_ref[...]
        yg = ypre * zs
        dout = dy_ref[...]
        ngv = ng_ref[...]
        gw = SSD_WIDTH // SSD_GROUPS
        dyg_parts, dng_parts = [], []
        for g in range(SSD_GROUPS):
            cols = slice(g * gw, (g + 1) * gw)
            v = yg[:, cols]
            dx, dg = _rms_bwd(v, _rms_r(v), ngv[:, cols], dout[:, cols])
            dyg_parts.append(dx)
            dng_parts.append(dg)
        dyg = jnp.concatenate(dyg_parts, axis=1)
        dng_ref[...] += jnp.concatenate(dng_parts, axis=1)
        dyp = dyg * zs
        dyp_scr[...] = dyp
        dproj_ref[:, COL_Z:COL_Z + SSD_WIDTH] = dyg * ypre * (sig * (1.0 + z * (1.0 - sig)))
        dproj_ref[:, COL_U:COL_U + POOL_WIDTH] = du_ref[...]
        xs_all = xa_scr[:, 0:SSD_WIDTH]
        dd_ref[...] += _headsum(jnp.broadcast_to(_colsum(dyp * xs_all), (SUBLANES, SSD_WIDTH)), e_mat)[0:1, :]

        def chunk(k, carry_):
            c = nch - 1 - k
            r0 = pl.multiple_of(c * CHUNK, CHUNK)
            rows = pl.ds(r0, CHUNK)
            a_c = a_scr[rows, :]
            cs, cs_t, cs_e, last_e = _chunk_decays(a_c, tril, e_mat)
            decay_e = jnp.exp(last_e - cs_e)
            ecs_e = jnp.exp(cs_e)
            elast_e = jnp.exp(last_e)
            xc = x_scr[rows, :]
            xb = xc.astype(BF16)
            xd = (xc * decay_e).astype(BF16)
            dyc = dyp_scr[rows, :]
            dcs = jnp.zeros((CHUNK, LANES), F32)
            dcs_neg_t = jnp.zeros((LANES, CHUNK), F32)
            qoff, rin, ghrow = [], [], []
            for g in range(SSD_GROUPS):
                b_cols = slice(SSD_WIDTH + g * SSD_STATE, SSD_WIDTH + (g + 1) * SSD_STATE)
                c_cols = slice(SSD_WIDTH + (SSD_GROUPS + g) * SSD_STATE, SSD_WIDTH + (SSD_GROUPS + g + 1) * SSD_STATE)
                bg = xa_scr[rows, b_cols].astype(BF16)
                cg = xa_scr[rows, c_cols].astype(BF16)
                cbm = _dot_nt(cg, bg)
                dcb_m = jnp.zeros((CHUNK, CHUNK), F32)
                dbg = jnp.zeros((CHUNK, SSD_STATE), F32)
                dcg = jnp.zeros((CHUNK, SSD_STATE), F32)
                for jj in range(2):
                    j = 2 * g + jj
                    cols = slice(j * LANES, (j + 1) * LANES)
                    dyp_j = dyc[:, cols]
                    hp = st_ref[c, j]
                    hpb = hp.astype(BF16)
                    gt = gstate[j]
                    gtb = gt.astype(BF16)
                    ecs = ecs_e[:, cols]
                    yoff = _dot(cg, hpb) * ecs
                    dye = (dyp_j * ecs).astype(BF16)
                    dcg = dcg + _dot_nt(dye, hpb)
                    dht = _dot_tn(cg, dye)
                    qoff.append(dyp_j * yoff)
                    xg = _dot(bg, gtb)
                    dxp = xg * decay_e[:, cols]
                    rin.append(xg * xc[:, cols])
                    dbg = dbg + _dot_nt(xd[:, cols], gtb)
                    ghrow.append(_colsum(gt * hp) * elast_e[:, cols])
                    gstate[j] = dht + gt * elast_e[:, cols]
                    for hh in range(2):
                        h = 2 * j + hh
                        seg = jnp.exp(jnp.where(causal, cs[:, h:h + 1] - cs_t[h:h + 1, :], -jnp.inf))
                        m = cbm * seg
                        half = (lane < SSD_HEAD_DIM) if hh == 0 else (lane >= SSD_HEAD_DIM)
                        dym = jnp.where(half, dyp_j, 0.0).astype(BF16)
                        w = _dot_nt(dym, xb[:, cols])
                        pm = w * m
                        dcs = dcs + jnp.where(lane == h, jnp.sum(pm, axis=1, keepdims=True), 0.0)
                        dcs_neg_t = dcs_neg_t + jnp.where(sub == h, _colsum(pm), 0.0)
                        dcb_m = dcb_m + w * seg
                        dxp = dxp + _dot_tn(m.astype(BF16), dym)
                    dx_scr[:, cols] = dxp
                dcbb = dcb_m.astype(BF16)
                dxa_scr[rows, c_cols] = dcg + _dot(dcbb, bg)
                dxa_scr[rows, b_cols] = dbg + _dot_tn(dcbb, cg)
            decay_th = jnp.exp(cs[CHUNK - 1:CHUNK, :] - cs)
            rd = _headsum(jnp.concatenate(rin, axis=1), e_mat) * decay_th
            dcs = dcs - dcs_neg_t.T + _headsum(jnp.concatenate(qoff, axis=1), e_mat) - rd
            gh = _headsum(jnp.broadcast_to(jnp.concatenate(ghrow, axis=1), (SUBLANES, SSD_WIDTH)), e_mat)[0:1, :]
            dcs = dcs + jnp.where(sub == CHUNK - 1, _colsum(rd) + gh, 0.0)
            da = _hdot(triu, dcs)
            dx_all = dx_scr[...]
            xs = xa_scr[rows, 0:SSD_WIDTH]
            dt_c = _softplus(dt_ref[rows, :] + dtb_ref[...])
            ddt = da * a_neg + _headsum(dx_all * xs, e_mat)
            dalog_ref[...] += _colsum(da * dt_c) * a_neg
            ddtraw = ddt * _sigmoid(dt_ref[rows, :] + dtb_ref[...])
            dproj_ref[rows, COL_DT:COL_DT + LANES] = ddtraw
            ddtb_ref[...] += _colsum(ddtraw)
            dxa_scr[rows, 0:SSD_WIDTH] = dx_all * dte_scr[rows, :] + dyc * dexp_ref[...]
            return carry_

        lax.fori_loop(0, nch, chunk, 0)

        xcv = xc_scr[...]
        sgc = _sigmoid(xcv)
        dxc = dxa_scr[...] * (sgc * (1.0 + xcv * (1.0 - sgc)))
        dcb_ref[...] += _colsum(dxc)
        dbuf[0:ts, :] = dxc
        dbuf[ts:ts + SUBLANES, :] = carry[...]
        cwv = cw_ref[...]
        dxbc = jnp.zeros((ts, SSD_XBC), F32)
        dcw_rows = []
        for k in range(SSD_CONV):
            off = SUBLANES - (SSD_CONV - 1) + k
            dcw_rows.append(_colsum(dxc * buf[off:off + ts, :]))
            back = SSD_CONV - 1 - k
            dxbc = dxbc + cwv[k:k + 1, :] * dbuf[back:back + ts, :]
        dcw_ref[...] += jnp.concatenate(dcw_rows, axis=0)
        dproj_ref[:, COL_XBC:COL_XBC + SSD_XBC] = dxbc
        carry[...] = dxc[0:SUBLANES, :]

    rev = lambda i: nt - 1 - i
    full = lambda shape: pl.BlockSpec(shape, lambda i: tuple(0 for _ in shape))
    outs = pl.pallas_call(
        body, name=name, grid=(nt,),
        in_specs=[pl.BlockSpec((ts, SSD_WIDTH), lambda i: (rev(i), 0)),
                  pl.BlockSpec((ts, SSD_XBC), lambda i: (rev(i), COL_XBC // SSD_XBC)),
                  pl.BlockSpec((SUBLANES, SSD_XBC), lambda i: (jnp.maximum(rev(i) * hb - 1, 0), COL_XBC // SSD_XBC)),
                  pl.BlockSpec((ts, SSD_WIDTH), lambda i: (rev(i), COL_Z // SSD_WIDTH)),
                  pl.BlockSpec((ts, LANES), lambda i: (rev(i), COL_DT // LANES)),
                  pl.BlockSpec((ts, SSD_WIDTH), lambda i: (rev(i), 0)),
                  pl.BlockSpec((nch, N_PAIRS, SSD_STATE, LANES), lambda i: (rev(i), 0, 0, 0)),
                  pl.BlockSpec((ts, POOL_WIDTH), lambda i: (rev(i), 0)),
                  full((SSD_CONV, SSD_XBC)), full((1, SSD_XBC)), full((1, LANES)), full((1, LANES)),
                  full((1, SSD_WIDTH)), full((1, SSD_WIDTH)), full((CHUNK, CHUNK)), full((CHUNK, CHUNK)),
                  full((LANES, SSD_WIDTH))],
        out_specs=[pl.BlockSpec((ts, N_PROJ), lambda i: (rev(i), 0)),
                   full((SSD_CONV, SSD_XBC)), full((1, SSD_XBC)), full((1, LANES)), full((1, LANES)),
                   full((1, LANES)), full((1, SSD_WIDTH))],
        out_shape=[jax.ShapeDtypeStruct((t, N_PROJ), F32),
                   jax.ShapeDtypeStruct((SSD_CONV, SSD_XBC), F32), jax.ShapeDtypeStruct((1, SSD_XBC), F32),
                   jax.ShapeDtypeStruct((1, LANES), F32), jax.ShapeDtypeStruct((1, LANES), F32),
                   jax.ShapeDtypeStruct((1, LANES), F32), jax.ShapeDtypeStruct((1, SSD_WIDTH), F32)],
        scratch_shapes=[pltpu.VMEM((SUBLANES + ts, SSD_XBC), F32), pltpu.VMEM((ts, SSD_XBC), F32),
                        pltpu.VMEM((ts, SSD_XBC), F32), pltpu.VMEM((ts, LANES), F32),
                        pltpu.VMEM((ts, SSD_WIDTH), F32), pltpu.VMEM((ts, SSD_WIDTH), F32),
                        pltpu.VMEM((ts, SSD_WIDTH), F32), pltpu.VMEM((ts, SSD_XBC), F32),
                        pltpu.VMEM((CHUNK, SSD_WIDTH), F32), pltpu.VMEM((ts + SUBLANES, SSD_XBC), F32),
                        pltpu.VMEM((SUBLANES, SSD_XBC), F32), pltpu.VMEM((N_PAIRS, SSD_STATE, LANES), F32)],
        compiler_params=_params("arbitrary"))(
            dymix, proj, proj, proj, proj, ypre, states, du, cw, cb, dtb, alog, dexp, ng, tril, triu, e)
    return outs


def _pooled(ubuf, u, pos, tt):
    out = []
    for gi, w in enumerate(POOL_WINDOWS):
        cols = slice(gi * POOL_GROUP, (gi + 1) * POOL_GROUP)
        acc = u[:, cols]
        for j in range(1, w):
            acc = acc + ubuf[POOL_HALO - j:POOL_HALO - j + tt, cols]
        out.append(acc / jnp.minimum(pos, float(w)) - u[:, cols])
    return out


def _mix_out(h, yssd, proj, pool_w, pool_scale, w_out, g_next, *, tt, name):
    t = h.shape[0]
    tt = _tile(t, tt)
    hb = tt // POOL_HALO

    def body(h_ref, ys_ref, u_ref, uh_ref, pw_ref, sc_ref, wo_ref, gn_ref, o_ref, ym_ref, n_ref, ubuf):
        i = pl.program_id(0)
        ubuf[0:POOL_HALO, :] = jnp.where(i == 0, 0.0, uh_ref[...])
        u = u_ref[...]
        ubuf[POOL_HALO:POOL_HALO + tt, :] = u
        pos = (i * tt + 1 + lax.broadcasted_iota(jnp.int32, (tt, 1), 0)).astype(F32)
        sc = sc_ref[...]
        parts = [ys_ref[...]]
        for gi, pooled in enumerate(_pooled(ubuf, u, pos, tt)):
            cols = slice(gi * POOL_GROUP, (gi + 1) * POOL_GROUP)
            parts.append(_dot(pooled.astype(BF16), pw_ref[gi]) * sc[:, cols])
        ymix = jnp.concatenate(parts, axis=1).astype(BF16)
        ym_ref[...] = ymix
        h2 = h_ref[...] + _dot(ymix, wo_ref[...])
        o_ref[...] = h2
        n_ref[...] = (h2 * _rms_r(h2) * gn_ref[...]).astype(BF16)

    full = lambda shape: pl.BlockSpec(shape, lambda i: tuple(0 for _ in shape))
    return pl.pallas_call(
        body, name=name, grid=(t // tt,),
        in_specs=[pl.BlockSpec((tt, D_MODEL), lambda i: (i, 0)), pl.BlockSpec((tt, SSD_WIDTH), lambda i: (i, 0)),
                  pl.BlockSpec((tt, POOL_WIDTH), lambda i: (i, COL_U // POOL_WIDTH)),
                  pl.BlockSpec((POOL_HALO, POOL_WIDTH), lambda i: (jnp.maximum(i * hb - 1, 0), COL_U // POOL_WIDTH)),
                  full((len(POOL_WINDOWS), POOL_GROUP, POOL_GROUP)), full((1, POOL_WIDTH)),
                  full((D_MODEL, D_MODEL)), full((1, D_MODEL))],
        out_specs=[pl.BlockSpec((tt, D_MODEL), lambda i: (i, 0))] * 3,
        out_shape=[jax.ShapeDtypeStruct((t, D_MODEL), F32), jax.ShapeDtypeStruct((t, D_MODEL), BF16),
                   jax.ShapeDtypeStruct((t, D_MODEL), BF16)],
        scratch_shapes=[pltpu.VMEM((POOL_HALO + tt, POOL_WIDTH), F32)],
        compiler_params=_params("arbitrary"))(h, yssd, proj, proj, pool_w, pool_scale, w_out, g_next)


def _out_bwd(dh, ymix, w_out_t, *, tt, name):
    t = dh.shape[0]
    tt = _tile(t, tt)

    def body(dh_ref, ym_ref, wt_ref, dym_ref, dw_ref):
        @pl.when(pl.program_id(0) == 0)
        def _():
            dw_ref[...] = jnp.zeros_like(dw_ref)

        dhb = dh_ref[...].astype(BF16)
        dym_ref[...] = _dot(dhb, wt_ref[...])
        dw_ref[...] += _dot_tn(ym_ref[...], dhb)

    return pl.pallas_call(
        body, name=name, grid=(t // tt,),
        in_specs=[pl.BlockSpec((tt, D_MODEL), lambda i: (i, 0)), pl.BlockSpec((tt, D_MODEL), lambda i: (i, 0)),
                  pl.BlockSpec((D_MODEL, D_MODEL), lambda i: (0, 0))],
        out_specs=[pl.BlockSpec((tt, D_MODEL), lambda i: (i, 0)), pl.BlockSpec((D_MODEL, D_MODEL), lambda i: (0, 0))],
        out_shape=[jax.ShapeDtypeStruct((t, D_MODEL), F32), jax.ShapeDtypeStruct((D_MODEL, D_MODEL), F32)],
        compiler_params=_params("arbitrary"))(dh, ymix, w_out_t)


def _pool_bwd(dymix, proj, pool_w, pool_w_t, pool_scale, *, tt, name):
    t = proj.shape[0]
    tt = _tile(t, tt)
    hb = tt // POOL_HALO
    nt = t // tt
    ng = len(POOL_WINDOWS)

    def body(dy_ref, dyh_ref, u_ref, uh_ref, pw_ref, pwt_ref, sc_ref, du_ref, dpw_ref, dsc_ref, ubuf, dbuf):
        i = pl.program_id(0)

        @pl.when(i == 0)
        def _():
            dpw_ref[...] = jnp.zeros_like(dpw_ref)
            dsc_ref[...] = jnp.zeros_like(dsc_ref)

        ubuf[0:POOL_HALO, :] = jnp.where(i == 0, 0.0, uh_ref[...])
        u = u_ref[...]
        ubuf[POOL_HALO:POOL_HALO + tt, :] = u
        pos = (i * tt + 1 + lax.broadcasted_iota(jnp.int32, (tt, 1), 0)).astype(F32)
        sc = sc_ref[...]
        dy = dy_ref[...]
        dyh = jnp.where(i == nt - 1, 0.0, dyh_ref[...])
        dsc_parts, du_parts = [], []
        for gi, pooled in enumerate(_pooled(ubuf, u, pos, tt)):
            w = POOL_WINDOWS[gi]
            cols = slice(gi * POOL_GROUP, (gi + 1) * POOL_GROUP)
            pb = pooled.astype(BF16)
            dsc_parts.append(_colsum(dy[:, cols] * _dot(pb, pw_ref[gi])))
            dmx = (dy[:, cols] * sc[:, cols]).astype(BF16)
            dpw_ref[gi] += _dot_tn(pb, dmx)
            dpool = _dot(dmx, pwt_ref[gi])
            dpool_h = _dot((dyh[:, cols] * sc[:, cols]).astype(BF16), pwt_ref[gi])
            dbuf[0:tt, cols] = dpool / jnp.minimum(pos, float(w))
            dbuf[tt:tt + POOL_HALO, cols] = dpool_h / float(w)
            acc = -dpool
            for j in range(w):
                acc = acc + dbuf[j:j + tt, cols]
            du_parts.append(acc)
        du_ref[...] = jnp.concatenate(du_parts, axis=1)
        dsc_ref[...] += jnp.concatenate(dsc_parts, axis=1)

    full = lambda shape: pl.BlockSpec(shape, lambda i: tuple(0 for _ in shape))
    ucol = COL_U // POOL_WIDTH
    return pl.pallas_call(
        body, name=name, grid=(nt,),
        in_specs=[pl.BlockSpec((tt, POOL_WIDTH), lambda i: (i, 1)),
                  pl.BlockSpec((POOL_HALO, POOL_WIDTH), lambda i: (jnp.minimum((i + 1) * hb, t // POOL_HALO - 1), 1)),
                  pl.BlockSpec((tt, POOL_WIDTH), lambda i: (i, ucol)),
                  pl.BlockSpec((POOL_HALO, POOL_WIDTH), lambda i: (jnp.maximum(i * hb - 1, 0), ucol)),
                  full((ng, POOL_GROUP, POOL_GROUP)), full((ng, POOL_GROUP, POOL_GROUP)), full((1, POOL_WIDTH))],
        out_specs=[pl.BlockSpec((tt, POOL_WIDTH), lambda i: (i, 0)), full((ng, POOL_GROUP, POOL_GROUP)),
                   full((1, POOL_WIDTH))],
        out_shape=[jax.ShapeDtypeStruct((t, POOL_WIDTH), F32), jax.ShapeDtypeStruct((ng, POOL_GROUP, POOL_GROUP), F32),
                   jax.ShapeDtypeStruct((1, POOL_WIDTH), F32)],
        scratch_shapes=[pltpu.VMEM((POOL_HALO + tt, POOL_WIDTH), F32), pltpu.VMEM((tt + POOL_HALO, POOL_WIDTH), F32)],
        compiler_params=_params("arbitrary"))(dymix, dymix, proj, proj, pool_w, pool_w_t, pool_scale)


def _in_bwd(dproj, h, g, w_in_t, dh, *, tt, name):
    t = h.shape[0]
    tt = _tile(t, tt)

    def body(dp_ref, h_ref, g_ref, wt_ref, dh_ref, o_ref, dw_ref, dg_ref):
        @pl.when(pl.program_id(0) == 0)
        def _():
            dw_ref[...] = jnp.zeros_like(dw_ref)
            dg_ref[...] = jnp.zeros_like(dg_ref)

        x = h_ref[...]
        r = _rms_r(x)
        gv = g_ref[...]
        dpb = dp_ref[...].astype(BF16)
        dw_ref[...] += _dot_tn((x * r * gv).astype(BF16), dpb)
        dx, dg = _rms_bwd(x, r, gv, _dot(dpb, wt_ref[...]))
        o_ref[...] = dh_ref[...] + dx
        dg_ref[...] += dg

    full = lambda shape: pl.BlockSpec(shape, lambda i: tuple(0 for _ in shape))
    row = lambda n: pl.BlockSpec((tt, n), lambda i: (i, 0))
    return pl.pallas_call(
        body, name=name, grid=(t // tt,),
        in_specs=[row(N_PROJ), row(D_MODEL), full((1, D_MODEL)), full((N_PROJ, D_MODEL)), row(D_MODEL)],
        out_specs=[row(D_MODEL), full((D_MODEL, N_PROJ)), full((1, D_MODEL))],
        out_shape=[jax.ShapeDtypeStruct((t, D_MODEL), F32), jax.ShapeDtypeStruct((D_MODEL, N_PROJ), F32),
                   jax.ShapeDtypeStruct((1, D_MODEL), F32)],
        compiler_params=_params("arbitrary"))(dproj, h, g, w_in_t, dh)


FFN_COLS = 256
N_SLABS = D_FF // FFN_COLS


def _ffn_down(h, up, cw, cb, w_down, *, tt, name):
    t = h.shape[0]
    tt = _tile(t, tt)
    hb = tt // SUBLANES

    def body(h_ref, up_ref, uh_ref, cw_ref, cb_ref, wd_ref, o_ref, act_ref, buf):
        i = pl.program_id(0)
        buf[0:SUBLANES, :] = jnp.where(i == 0, 0.0, uh_ref[...])
        buf[SUBLANES:SUBLANES + tt, :] = up_ref[...]

        def conv(cols):
            acc = cb_ref[:, cols]
            for k in range(FFN_CONV):
                off = SUBLANES - (FFN_CONV - 1) + k
                acc = acc + cw_ref[k:k + 1, cols] * buf[off:off + tt, cols]
            return acc

        out = h_ref[...]
        for s in range(N_SLABS):
            gate = conv(slice(s * FFN_COLS, (s + 1) * FFN_COLS))
            val = conv(slice(D_FF + s * FFN_COLS, D_FF + (s + 1) * FFN_COLS))
            act = (_gelu_parts(gate)[0] * val).astype(BF16)
            act_ref[:, s * FFN_COLS:(s + 1) * FFN_COLS] = act
            out = out + _dot(act, wd_ref[s * FFN_COLS:(s + 1) * FFN_COLS, :])
        o_ref[...] = out

    full = lambda shape: pl.BlockSpec(shape, lambda i: tuple(0 for _ in shape))
    return pl.pallas_call(
        body, name=name, grid=(t // tt,),
        in_specs=[pl.BlockSpec((tt, D_MODEL), lambda i: (i, 0)), pl.BlockSpec((tt, D_UP), lambda i: (i, 0)),
                  pl.BlockSpec((SUBLANES, D_UP), lambda i: (jnp.maximum(i * hb - 1, 0), 0)),
                  full((FFN_CONV, D_UP)), full((1, D_UP)), full((D_FF, D_MODEL))],
        out_specs=[pl.BlockSpec((tt, D_MODEL), lambda i: (i, 0)), pl.BlockSpec((tt, D_FF), lambda i: (i, 0))],
        out_shape=[jax.ShapeDtypeStruct((t, D_MODEL), F32), jax.ShapeDtypeStruct((t, D_FF), BF16)],
        scratch_shapes=[pltpu.VMEM((SUBLANES + tt, D_UP), F32)],
        compiler_params=_params("arbitrary"))(h, up, up, cw, cb, w_down)


def _ffn_act_bwd(up, dact, cw, cb, *, tt, name):
    t = up.shape[0]
    tt = _tile(t, tt)
    hb = tt // SUBLANES
    nt = t // tt
    te = tt + SUBLANES

    def body(up_ref, up_prev, up_next, da_ref, da_next, cw_ref, cb_ref, dup_ref, dcw_ref, dcb_ref, buf, dbuf):
        i = pl.program_id(0)

        @pl.when(i == 0)
        def _():
            dcw_ref[...] = jnp.zeros_like(dcw_ref)
            dcb_ref[...] = jnp.zeros_like(dcb_ref)

        buf[0:SUBLANES, :] = jnp.where(i == 0, 0.0, up_prev[...])
        buf[SUBLANES:SUBLANES + tt, :] = up_ref[...]
        buf[SUBLANES + tt:2 * SUBLANES + tt, :] = up_next[...]
        inside = (lax.broadcasted_iota(jnp.int32, (te, 1), 0) < tt) | (i < nt - 1)

        def conv(cols):
            acc = cb_ref[:, cols]
            for k in range(FFN_CONV):
                off = SUBLANES - (FFN_CONV - 1) + k
                acc = acc + cw_ref[k:k + 1, cols] * buf[off:off + te, cols]
            return acc

        def finish(cols, dpre):
            dpre = jnp.where(inside, dpre, 0.0)
            dbuf[:, cols] = dpre
            dmain = dpre[0:tt, :]
            dcb_ref[:, cols] += _colsum(dmain)
            rows_, dup = [], jnp.zeros((tt, FFN_COLS), F32)
            for k in range(FFN_CONV):
                off = SUBLANES - (FFN_CONV - 1) + k
                rows_.append(_colsum(dmain * buf[off:off + tt, cols]))
                back = FFN_CONV - 1 - k
                dup = dup + cw_ref[k:k + 1, cols] * dbuf[back:back + tt, cols]
            dcw_ref[:, cols] += jnp.concatenate(rows_, axis=0)
            dup_ref[:, cols] = dup.astype(BF16)

        for s in range(N_SLABS):
            gcols = slice(s * FFN_COLS, (s + 1) * FFN_COLS)
            vcols = slice(D_FF + s * FFN_COLS, D_FF + (s + 1) * FFN_COLS)
            gate, val = conv(gcols), conv(vcols)
            gelu, dgelu = _gelu_parts(gate)
            da = jnp.concatenate([da_ref[:, gcols], da_next[:, gcols]], axis=0)
            finish(gcols, da * val * dgelu)
            finish(vcols, da * gelu)

    full = lambda shape: pl.BlockSpec(shape, lambda i: tuple(0 for _ in shape))
    prev = lambda i: (jnp.maximum(i * hb - 1, 0), 0)
    nxt = lambda i: (jnp.minimum((i + 1) * hb, t // SUBLANES - 1), 0)
    return pl.pallas_call(
        body, name=name, grid=(nt,),
        in_specs=[pl.BlockSpec((tt, D_UP), lambda i: (i, 0)), pl.BlockSpec((SUBLANES, D_UP), prev),
                  pl.BlockSpec((SUBLANES, D_UP), nxt), pl.BlockSpec((tt, D_FF), lambda i: (i, 0)),
                  pl.BlockSpec((SUBLANES, D_FF), nxt), full((FFN_CONV, D_UP)), full((1, D_UP))],
        out_specs=[pl.BlockSpec((tt, D_UP), lambda i: (i, 0)), full((FFN_CONV, D_UP)), full((1, D_UP))],
        out_shape=[jax.ShapeDtypeStruct((t, D_UP), BF16), jax.ShapeDtypeStruct((FFN_CONV, D_UP), F32),
                   jax.ShapeDtypeStruct((1, D_UP), F32)],
        scratch_shapes=[pltpu.VMEM((2 * SUBLANES + tt, D_UP), F32), pltpu.VMEM((te, D_UP), F32)],
        compiler_params=_params("arbitrary"))(up, up, up, dact, dact, cw, cb)


def _ple_fwd(h, p, g, w_gate, w_proj, *, tt, name):
    t = h.shape[0]
    tt = _tile(t, tt)

    def body(h_ref, p_ref, g_ref, wg_ref, wp_ref, o_ref):
        x = h_ref[...]
        n = (x * _rms_r(x) * g_ref[...]).astype(BF16)
        gate = _sigmoid(_dot(n, wg_ref[...]))
        o_ref[...] = x + _dot(p_ref[...].astype(BF16), wp_ref[...]) * gate

    full = lambda shape: pl.BlockSpec(shape, lambda i: tuple(0 for _ in shape))
    return pl.pallas_call(
        body, name=name, grid=(t // tt,),
        in_specs=[pl.BlockSpec((tt, D_MODEL), lambda i: (i, 0)), pl.BlockSpec((tt, D_PLE), lambda i: (i, 0)),
                  full((1, D_MODEL)), full((D_MODEL, D_MODEL)), full((D_PLE, D_MODEL))],
        out_specs=pl.BlockSpec((tt, D_MODEL), lambda i: (i, 0)),
        out_shape=jax.ShapeDtypeStruct((t, D_MODEL), F32),
        compiler_params=_params("arbitrary"))(h, p, g, w_gate, w_proj)


def _ple_bwd(dh, h, p, g, w_gate, w_gate_t, w_proj, *, tt, name):
    t = h.shape[0]
    tt = _tile(t, tt)

    def body(dh_ref, h_ref, p_ref, g_ref, wg_ref, wgt_ref, wp_ref, o_ref, dwg_ref, dwp_ref, dg_ref):
        @pl.when(pl.program_id(0) == 0)
        def _():
            dwg_ref[...] = jnp.zeros_like(dwg_ref)
            dwp_ref[...] = jnp.zeros_like(dwp_ref)
            dg_ref[...] = jnp.zeros_like(dg_ref)

        x = h_ref[...]
        r = _rms_r(x)
        gv = g_ref[...]
        n = (x * r * gv).astype(BF16)
        gate = _sigmoid(_dot(n, wg_ref[...]))
        pb = p_ref[...].astype(BF16)
        pe = _dot(pb, wp_ref[...])
        dhv = dh_ref[...]
        dwp_ref[...] += _dot_tn(pb, (dhv * gate).astype(BF16))
        ds = (dhv * pe * gate * (1.0 - gate)).astype(BF16)
        dwg_ref[...] += _dot_tn(n, ds)
        dx, dg = _rms_bwd(x, r, gv, _dot(ds, wgt_ref[...]))
        o_ref[...] = dhv + dx
        dg_ref[...] += dg

    full = lambda shape: pl.BlockSpec(shape, lambda i: tuple(0 for _ in shape))
    row = lambda n: pl.BlockSpec((tt, n), lambda i: (i, 0))
    return pl.pallas_call(
        body, name=name, grid=(t // tt,),
        in_specs=[row(D_MODEL), row(D_MODEL), row(D_PLE), full((1, D_MODEL)), full((D_MODEL, D_MODEL)),
                  full((D_MODEL, D_MODEL)), full((D_PLE, D_MODEL))],
        out_specs=[row(D_MODEL), full((D_MODEL, D_MODEL)), full((D_PLE, D_MODEL)), full((1, D_MODEL))],
        out_shape=[jax.ShapeDtypeStruct((t, D_MODEL), F32), jax.ShapeDtypeStruct((D_MODEL, D_MODEL), F32),
                   jax.ShapeDtypeStruct((D_PLE, D_MODEL), F32), jax.ShapeDtypeStruct((1, D_MODEL), F32)],
        compiler_params=_params("arbitrary"))(dh, h, p, g, w_gate, w_gate_t, w_proj)


def _loss_head(h, g, target, *, tt, name):
    t = h.shape[0]
    tt = _tile(t, tt)

    def body(h_ref, g_ref, tg_ref, dh_ref, loss_ref, dg_ref):
        @pl.when(pl.program_id(0) == 0)
        def _():
            loss_ref[...] = jnp.zeros_like(loss_ref)
            dg_ref[...] = jnp.zeros_like(dg_ref)

        x = h_ref[...]
        r = _rms_r(x)
        gv = g_ref[...]
        diff = x * r * gv - tg_ref[...]
        loss_ref[...] += 0.5 * jnp.sum(jnp.mean(diff * diff, axis=-1, keepdims=True), axis=0, keepdims=True)
        dx, dg = _rms_bwd(x, r, gv, diff * (1.0 / D_MODEL))
        dh_ref[...] = dx
        dg_ref[...] += dg

    return pl.pallas_call(
        body, name=name, grid=(t // tt,),
        in_specs=[pl.BlockSpec((tt, D_MODEL), lambda i: (i, 0)), pl.BlockSpec((1, D_MODEL), lambda i: (0, 0)),
                  pl.BlockSpec((tt, D_MODEL), lambda i: (i, 0))],
        out_specs=[pl.BlockSpec((tt, D_MODEL), lambda i: (i, 0)), pl.BlockSpec((SUBLANES, LANES), lambda i: (0, 0)),
                   pl.BlockSpec((1, D_MODEL), lambda i: (0, 0))],
        out_shape=[jax.ShapeDtypeStruct((t, D_MODEL), F32), jax.ShapeDtypeStruct((SUBLANES, LANES), F32),
                   jax.ShapeDtypeStruct((1, D_MODEL), F32)],
        compiler_params=_params("arbitrary"))(h, g, target)


def _sum_adamw(parts, w, m, v, *, tr, name):
    rows = w.shape[0]
    tr = _tile(rows, tr)
    assert rows % tr == 0, (rows, tr)

    def body(p_ref, w_ref, m_ref, v_ref, g_ref, d_ref, nm_ref, nv_ref):
        g = p_ref[0]
        for k in range(1, N_DEV):
            g = g + p_ref[k]
        g_ref[...] = g
        nm = ADAM_B1 * m_ref[...] + (1.0 - ADAM_B1) * g
        nv = ADAM_B2 * v_ref[...] + (1.0 - ADAM_B2) * (g * g)
        m_hat = nm / (1.0 - ADAM_B1 ** ADAM_STEP)
        v_hat = nv / (1.0 - ADAM_B2 ** ADAM_STEP)
        d_ref[...] = -ADAM_LR * (m_hat / (jnp.sqrt(v_hat) + ADAM_EPS) + ADAM_WD * w_ref[...])
        nm_ref[...] = nm
        nv_ref[...] = nv

    row = pl.BlockSpec((tr, LANES), lambda i: (i, 0))
    return pl.pallas_call(
        body, name=name, grid=(rows // tr,),
        in_specs=[pl.BlockSpec((N_DEV, tr, LANES), lambda i: (0, i, 0)), row, row, row],
        out_specs=[row, row, row, row],
        out_shape=[jax.ShapeDtypeStruct((rows, LANES), F32)] * 4,
        compiler_params=_params("arbitrary"))(parts, w, m, v)


PACK_ROWS = 512


def _pack(arrays, dtype):
    flat = jnp.concatenate([a.astype(dtype).reshape(-1) for a in arrays])
    pad = (-flat.shape[0]) % (PACK_ROWS * LANES)
    return jnp.pad(flat, (0, pad)).reshape(-1, LANES)


def _unpack(buf, shapes):
    flat = buf.reshape(-1)
    out, off = [], 0
    for s in shapes:
        n = math.prod(s)
        out.append(flat[off:off + n].reshape(s))
        off += n
    return out


def _unpack_gathered(buf, shapes, axes):
    per_dev = [_unpack(buf[k], shapes) for k in range(N_DEV)]
    return [jnp.concatenate([per_dev[k][a] for k in range(N_DEV)], axis=axes[a]) for a in range(len(shapes))]


def _split_pack(arrays, axes):
    slots = []
    for k in range(N_DEV):
        shards = []
        for a, ax in zip(arrays, axes):
            n = a.shape[ax] // N_DEV
            shards.append(lax.slice_in_dim(a, k * n, (k + 1) * n, axis=ax))
        slots.append(_pack(shards, F32))
    return jnp.stack(slots)


def _to_proj_cols(w):
    z, xbc, dtc, u = jnp.split(w, [SSD_WIDTH, SSD_WIDTH + SSD_XBC, SSD_WIDTH + SSD_XBC + SSD_HEADS], axis=-1)
    pad = jnp.zeros(w.shape[:-1] + (LANES - SSD_HEADS,), w.dtype)
    return jnp.concatenate([xbc, z, u, dtc, pad], axis=-1)


def _from_proj_cols(w):
    xbc, z, u, dtc = (w[..., COL_XBC:COL_Z], w[..., COL_Z:COL_U], w[..., COL_U:COL_DT],
                      w[..., COL_DT:COL_DT + SSD_HEADS])
    return jnp.concatenate([z, xbc, dtc, u], axis=-1)


def _pad_heads(v):
    return jnp.pad(v, (0, LANES - SSD_HEADS)).reshape(1, LANES)


SHARDED = ("w_in", "w_out", "ffn_w_up", "ffn_w_down", "ple_w_gate", "ple_w_proj", "ssd_conv_w", "ffn_conv_w")
SHARD_AXIS = {"w_in": 2, "w_out": 1, "ffn_w_up": 2, "ffn_w_down": 1, "ple_w_gate": 1, "ple_w_proj": 2,
              "ssd_conv_w": 2, "ffn_conv_w": 2}
MATMUL_W = SHARDED[:6]
CONV_W = SHARDED[6:]
REPLICATED = ("mix_norm_g", "ssd_conv_b", "ssd_dt_bias", "ssd_a_log", "ssd_d", "ssd_norm_g", "pool_w", "pool_scale",
              "ffn_norm_g", "ffn_conv_b", "ple_norm_g", "final_norm_g")
WEIGHTS = ("mix_norm_g", "w_in", "ssd_conv_w", "ssd_conv_b", "ssd_dt_bias", "ssd_a_log", "ssd_d", "ssd_norm_g",
           "pool_w", "pool_scale", "w_out", "ffn_norm_g", "ffn_w_up", "ffn_conv_w", "ffn_conv_b", "ffn_w_down",
           "ple_norm_g", "ple_w_gate", "ple_w_proj", "final_norm_g")


def _local_grads(x, p, target, full):
    t = x.shape[0]
    tril = jnp.tril(jnp.ones((CHUNK, CHUNK), F32))
    triu = tril.T
    e_mat = (jnp.arange(SSD_WIDTH)[None, :] // SSD_HEAD_DIM == jnp.arange(LANES)[:, None]).astype(F32)
    row = lambda v: v.reshape(1, -1)

    saved = []
    h = x
    for i in range(DEPTH):
        lw = {k: full[k][i] for k in full if k != "final_norm_g"}
        w_in = _to_proj_cols(lw["w_in"])
        dtb, alog = _pad_heads(lw["ssd_dt_bias"]), _pad_heads(lw["ssd_a_log"])
        dexp = row(jnp.repeat(lw["ssd_d"], SSD_HEAD_DIM))
        pw = lw["pool_w"].astype(BF16)
        h1 = h
        proj = _norm_matmul(h1, w_in, row(lw["mix_norm_g"]), tt=512, tn=N_PROJ, name=f"in_proj_{i}")
        yssd, ypre, states = _ssd_fwd(proj, lw["ssd_conv_w"], row(lw["ssd_conv_b"]), dtb, alog, dexp,
                                      row(lw["ssd_norm_g"]), tril, e_mat, ts=512, name=f"ssd_fwd_{i}")
        h2, ymix, n2 = _mix_out(h1, yssd, proj, pw, row(lw["pool_scale"]), lw["w_out"], row(lw["ffn_norm_g"]), tt=512,
                                name=f"mix_out_{i}")
        up = _norm_matmul(n2, lw["ffn_w_up"], tt=512, tn=D_FF, name=f"ffn_up_{i}")
        h3, act = _ffn_down(h2, up, lw["ffn_conv_w"], row(lw["ffn_conv_b"]), lw["ffn_w_down"], tt=256,
                            name=f"ffn_down_{i}")
        h = _ple_fwd(h3, p[i], row(lw["ple_norm_g"]), lw["ple_w_gate"], lw["ple_w_proj"], tt=512, name=f"ple_fwd_{i}")
        saved.append(dict(h1=h1, proj=proj, ypre=ypre, states=states, ymix=ymix, h2=h2, n2=n2, up=up, act=act, h3=h3,
                          w_in=w_in, dtb=dtb, alog=alog, dexp=dexp, pw=pw))

    dh, loss_blk, dgf = _loss_head(h, row(full["final_norm_g"]), target, tt=512, name="loss_head")
    grads = {k: [None] * DEPTH for k in WEIGHTS if k != "final_norm_g"}
    grads["final_norm_g"] = dgf.reshape(-1)

    for i in reversed(range(DEPTH)):
        lw = {k: full[k][i] for k in full if k != "final_norm_g"}
        s = saved[i]
        dh, dwg, dwp, dg3 = _ple_bwd(dh, s["h3"], p[i], row(lw["ple_norm_g"]), lw["ple_w_gate"], lw["ple_w_gate"].T,
                                     lw["ple_w_proj"], tt=512, name=f"ple_bwd_{i}")
        grads["ple_w_gate"][i], grads["ple_w_proj"][i], grads["ple_norm_g"][i] = dwg, dwp, dg3.reshape(-1)

        grads["ffn_w_down"][i] = _matmul_tn(s["act"], dh, tm=D_FF // 2, tn=D_MODEL, tk=1024, name=f"dw_down_{i}")
        dact = _norm_matmul(dh, lw["ffn_w_down"].T, tt=512, tn=D_FF, name=f"d_act_{i}")
        dup, dcw, dcb = _ffn_act_bwd(s["up"], dact, lw["ffn_conv_w"], row(lw["ffn_conv_b"]), tt=256,
                                     name=f"ffn_act_bwd_{i}")
        grads["ffn_conv_w"][i], grads["ffn_conv_b"][i] = dcw, dcb.reshape(-1)
        grads["ffn_w_up"][i] = _matmul_tn(s["n2"], dup, tm=D_MODEL, tn=D_UP // 4, tk=1024, name=f"dw_up_{i}")
        dh, dg2 = _matmul_rmsbwd(dup, lw["ffn_w_up"].T, s["h2"], row(lw["ffn_norm_g"]), dh, tt=512, tk=D_UP // 4,
                                 name=f"ffn_up_bwd_{i}")
        grads["ffn_norm_g"][i] = dg2.reshape(-1)

        dymix, dwo = _out_bwd(dh, s["ymix"], lw["w_out"].T, tt=512, name=f"out_bwd_{i}")
        grads["w_out"][i] = dwo
        du, dpw, dsc = _pool_bwd(dymix, s["proj"], s["pw"], jnp.swapaxes(s["pw"], 1, 2), row(lw["pool_scale"]),
                                 tt=512, name=f"pool_bwd_{i}")
        grads["pool_w"][i], grads["pool_scale"][i] = dpw, dsc.reshape(-1)
        dproj, dcw, dcb, ddtb, dalog, dd, dng = _ssd_bwd(
            dymix, s["proj"], s["ypre"], s["states"], du, lw["ssd_conv_w"], row(lw["ssd_conv_b"]), s["dtb"],
            s["alog"], s["dexp"], row(lw["ssd_norm_g"]), tril, triu, e_mat, ts=512, name=f"ssd_bwd_{i}")
        grads["ssd_conv_w"][i], grads["ssd_conv_b"][i] = dcw, dcb.reshape(-1)
        grads["ssd_dt_bias"][i], grads["ssd_a_log"][i] = ddtb[0, :SSD_HEADS], dalog[0, :SSD_HEADS]
        grads["ssd_d"][i], grads["ssd_norm_g"][i] = dd[0, :SSD_HEADS], dng.reshape(-1)
        dh, dwi, dg1 = _in_bwd(dproj, s["h1"], row(lw["mix_norm_g"]), s["w_in"].T, dh, tt=256, name=f"in_bwd_{i}")
        grads["w_in"][i], grads["mix_norm_g"][i] = _from_proj_cols(dwi), dg1.reshape(-1)

    out = {k: (jnp.stack(v) if isinstance(v, list) else v) for k, v in grads.items()}
    return loss_blk, dh, out


def kernel(x, p, mix_norm_g, w_in, ssd_conv_w, ssd_conv_b, ssd_dt_bias, ssd_a_log, ssd_d, ssd_norm_g, pool_w, pool_scale, w_out, ffn_norm_g, ffn_w_up, ffn_conv_w, ffn_conv_b, ffn_w_down, ple_norm_g, ple_w_gate, ple_w_proj, final_norm_g, loss_target, m_mix_norm_g, m_w_in, m_ssd_conv_w, m_ssd_conv_b, m_ssd_dt_bias, m_ssd_a_log, m_ssd_d, m_ssd_norm_g, m_pool_w, m_pool_scale, m_w_out, m_ffn_norm_g, m_ffn_w_up, m_ffn_conv_w, m_ffn_conv_b, m_ffn_w_down, m_ple_norm_g, m_ple_w_gate, m_ple_w_proj, m_final_norm_g, v_mix_norm_g, v_w_in, v_ssd_conv_w, v_ssd_conv_b, v_ssd_dt_bias, v_ssd_a_log, v_ssd_d, v_ssd_norm_g, v_pool_w, v_pool_scale, v_w_out, v_ffn_norm_g, v_ffn_w_up, v_ffn_conv_w, v_ffn_conv_b, v_ffn_w_down, v_ple_norm_g, v_ple_w_gate, v_ple_w_proj, v_final_norm_g):
    w = dict(mix_norm_g=mix_norm_g, w_in=w_in, ssd_conv_w=ssd_conv_w, ssd_conv_b=ssd_conv_b, ssd_dt_bias=ssd_dt_bias,
             ssd_a_log=ssd_a_log, ssd_d=ssd_d, ssd_norm_g=ssd_norm_g, pool_w=pool_w, pool_scale=pool_scale, w_out=w_out,
             ffn_norm_g=ffn_norm_g, ffn_w_up=ffn_w_up, ffn_conv_w=ffn_conv_w, ffn_conv_b=ffn_conv_b,
             ffn_w_down=ffn_w_down, ple_norm_g=ple_norm_g, ple_w_gate=ple_w_gate, ple_w_proj=ple_w_proj,
             final_norm_g=final_norm_g)
    m = dict(mix_norm_g=m_mix_norm_g, w_in=m_w_in, ssd_conv_w=m_ssd_conv_w, ssd_conv_b=m_ssd_conv_b,
             ssd_dt_bias=m_ssd_dt_bias, ssd_a_log=m_ssd_a_log, ssd_d=m_ssd_d, ssd_norm_g=m_ssd_norm_g, pool_w=m_pool_w,
             pool_scale=m_pool_scale, w_out=m_w_out, ffn_norm_g=m_ffn_norm_g, ffn_w_up=m_ffn_w_up,
             ffn_conv_w=m_ffn_conv_w, ffn_conv_b=m_ffn_conv_b, ffn_w_down=m_ffn_w_down, ple_norm_g=m_ple_norm_g,
             ple_w_gate=m_ple_w_gate, ple_w_proj=m_ple_w_proj, final_norm_g=m_final_norm_g)
    v = dict(mix_norm_g=v_mix_norm_g, w_in=v_w_in, ssd_conv_w=v_ssd_conv_w, ssd_conv_b=v_ssd_conv_b,
             ssd_dt_bias=v_ssd_dt_bias, ssd_a_log=v_ssd_a_log, ssd_d=v_ssd_d, ssd_norm_g=v_ssd_norm_g, pool_w=v_pool_w,
             pool_scale=v_pool_scale, w_out=v_w_out, ffn_norm_g=v_ffn_norm_g, ffn_w_up=v_ffn_w_up,
             ffn_conv_w=v_ffn_conv_w, ffn_conv_b=v_ffn_conv_b, ffn_w_down=v_ffn_w_down, ple_norm_g=v_ple_norm_g,
             ple_w_gate=v_ple_w_gate, ple_w_proj=v_ple_w_proj, final_norm_g=v_final_norm_g)

    mm_shapes = [w[k].shape for k in MATMUL_W]
    mm_axes = [SHARD_AXIS[k] for k in MATMUL_W]
    cv_shapes = [w[k].shape for k in CONV_W]
    cv_axes = [SHARD_AXIS[k] for k in CONV_W]
    mm_all = _exchange(_pack([w[k] for k in MATMUL_W], BF16), scatter=False, name="gather_matmul_weights")
    cv_all = _exchange(_pack([w[k] for k in CONV_W], F32), scatter=False, name="gather_conv_weights")
    full = {k: w[k] for k in REPLICATED}
    full.update(zip(MATMUL_W, _unpack_gathered(mm_all, mm_shapes, mm_axes)))
    full.update(zip(CONV_W, _unpack_gathered(cv_all, cv_shapes, cv_axes)))

    loss_blk, dx, grads = _local_grads(x[0], p[:, 0], loss_target[0], full)
    loss = lax.psum(loss_blk[0, 0], ("x", "y", "c"))

    sh_shapes = [w[k].shape for k in SHARDED]
    parts = _exchange(_split_pack([grads[k] for k in SHARDED], [SHARD_AXIS[k] for k in SHARDED]), scatter=True,
                      name="scatter_weight_grads")
    g_sh, d_sh, m_sh, v_sh = _sum_adamw(parts, _pack([w[k] for k in SHARDED], F32), _pack([m[k] for k in SHARDED], F32),
                                        _pack([v[k] for k in SHARDED], F32), tr=PACK_ROWS, name="adamw_sharded")
    rp_shapes = [w[k].shape for k in REPLICATED]
    parts = _exchange(_pack([grads[k] for k in REPLICATED], F32), scatter=False, name="gather_replicated_grads")
    g_rp, d_rp, m_rp, v_rp = _sum_adamw(parts, _pack([w[k] for k in REPLICATED], F32),
                                        _pack([m[k] for k in REPLICATED], F32), _pack([v[k] for k in REPLICATED], F32),
                                        tr=PACK_ROWS, name="adamw_replicated")

    results = []
    for sh_buf, rp_buf in ((g_sh, g_rp), (d_sh, d_rp), (m_sh, m_rp), (v_sh, v_rp)):
        named = dict(zip(SHARDED, _unpack(sh_buf, sh_shapes)))
        named.update(zip(REPLICATED, _unpack(rp_buf, rp_shapes)))
        results.extend(named[k] for k in WEIGHTS)
    return (loss, dx[None], *results)
```

```python
import functools
import math

import jax
import jax.numpy as jnp
from jax import lax
from jax.experimental import pallas as pl
from jax.experimental.pallas import tpu as pltpu

F32 = jnp.float32
BF16 = jnp.bfloat16
HI = lax.Precision.HIGHEST

N_DEV = 8
EPS = 1e-6
DEPTH = 4
D_MODEL = 1024
D_PLE = 256
SSD_WIDTH = 512
SSD_HEADS = 8
SSD_HEAD_DIM = 64
SSD_GROUPS = 2
SSD_STATE = 128
SSD_CONV = 4
CHUNK = 128
SSD_XBC = 1024
POOL_WINDOWS = (2, 4, 8, 16)
POOL_WIDTH = 512
POOL_GROUP = 128
POOL_HALO = 16
D_IN_PROJ = 2056
D_FF = 2816
D_UP = 2 * D_FF
FFN_CONV = 3
SUBLANES = 8
LANES = 128
N_PROJ = 2176
COL_XBC, COL_Z, COL_U, COL_DT = 0, 1024, 1536, 2048
N_PAIRS = SSD_HEADS // 2
ADAM_LR, ADAM_B1, ADAM_B2, ADAM_EPS, ADAM_WD, ADAM_STEP = 0.001, 0.9, 0.999, 1e-08, 0.01, 10
GELU_C = math.sqrt(2.0 / math.pi)
GELU_A = 0.044715
VMEM_LIMIT = 56 * 1024 * 1024

NT_DIMS = (((1,), (1,)), ((), ()))
TN_DIMS = (((0,), (0,)), ((), ()))


def _params(*sem):
    return pltpu.CompilerParams(dimension_semantics=sem, vmem_limit_bytes=VMEM_LIMIT)


def _dot(a, b):
    return jnp.dot(a, b, preferred_element_type=F32)


def _dot_nt(a, b):
    return lax.dot_general(a, b, NT_DIMS, preferred_element_type=F32)


def _dot_tn(a, b):
    return lax.dot_general(a, b, TN_DIMS, preferred_element_type=F32)


def _hdot(a, b):
    return jnp.dot(a, b, preferred_element_type=F32, precision=HI)


def _headsum(q, e):
    return lax.dot_general(q, e, NT_DIMS, preferred_element_type=F32, precision=HI)


def _colsum(v):
    return jnp.sum(v, axis=0, keepdims=True)


def _sigmoid(v):
    return 1.0 / (1.0 + jnp.exp(-v))


def _softplus(v):
    e = jnp.exp(-jnp.abs(v))
    return jnp.maximum(v, 0.0) + jnp.where(e < 1e-4, e * (1.0 - 0.5 * e), jnp.log(1.0 + e))


def _rms_r(x):
    return lax.rsqrt(jnp.mean(x * x, axis=-1, keepdims=True) + EPS)


def _rms_bwd(x, r, g, dn):
    xhat = x * r
    gd = dn * g
    dx = r * (gd - xhat * jnp.mean(gd * xhat, axis=-1, keepdims=True))
    return dx, _colsum(dn * xhat)


def _gelu(v):
    return 0.5 * v * (1.0 + jnp.tanh(GELU_C * (v + GELU_A * v * v * v)))


def _gelu_grad(v):
    th = jnp.tanh(GELU_C * (v + GELU_A * v * v * v))
    return 0.5 * (1.0 + th) + 0.5 * v * (1.0 - th * th) * GELU_C * (1.0 + 3.0 * GELU_A * v * v)


def _tile(t, want):
    return min(t, want)


class _Exchange:
    def __init__(self, srcs, *, scatter, layer, into=None):
        self.srcs, self.scatter, self.layer = list(srcs), scatter, layer
        self.into = None if into is None else list(into)
        n = len(self.srcs)
        self.args = self.srcs + (self.into or [])
        self.in_specs = [pl.BlockSpec(memory_space=pl.ANY)] * len(self.args)
        if scatter:
            self.out_shape = [jax.ShapeDtypeStruct((DEPTH,) + s.shape, s.dtype) for s in self.srcs]
        else:
            self.out_shape = [jax.ShapeDtypeStruct((N_DEV,) + s.shape[1:], s.dtype) for s in self.srcs]
        self.out_specs = [pl.BlockSpec(memory_space=pl.ANY)] * n
        self.scratch = [pltpu.SemaphoreType.DMA((n, N_DEV - 1)), pltpu.SemaphoreType.DMA((n, N_DEV - 1)),
                        pltpu.SemaphoreType.DMA((n,))]

    def aliases(self, n_in_before, n_out_before):
        if self.into is None:
            return {}
        n = len(self.srcs)
        return {n_in_before + n + a: n_out_before + a for a in range(n)}

    def ops(self, in_refs, out_refs, sems):
        send_sems, recv_sems, local_sems = sems
        n = len(self.srcs)

        def copies():
            x, y, c = lax.axis_index("x"), lax.axis_index("y"), lax.axis_index("c")
            me = 4 * x + 2 * y + c

            def block(a, idx):
                return in_refs[a].at[idx] if self.scatter else in_refs[a].at[self.layer]

            def slot(a, idx):
                return out_refs[a].at[self.layer].at[idx] if self.scatter else out_refs[a].at[idx]

            local = [pltpu.make_async_copy(block(a, me), slot(a, me), local_sems.at[a]) for a in range(n)]
            sends, recvs = [], []
            for k in range(1, N_DEV):
                px = 1 - x if k & 4 else x
                py = 1 - y if k & 2 else y
                pc = 1 - c if k & 1 else c
                peer = 4 * px + 2 * py + pc
                for a in range(n):
                    kw = dict(send_sem=send_sems.at[a, k - 1], recv_sem=recv_sems.at[a, k - 1], device_id=(px, py, pc),
                              device_id_type=pl.DeviceIdType.MESH)
                    sends.append(pltpu.make_async_remote_copy(src_ref=block(a, peer), dst_ref=slot(a, me), **kw))
                    recvs.append(pltpu.make_async_remote_copy(src_ref=block(a, peer), dst_ref=slot(a, peer), **kw))
            return local, sends, recvs

        def start():
            local, sends, _ = copies()
            for cp in local + sends:
                cp.start()

        def wait():
            local, sends, recvs = copies()
            for send, recv in zip(sends, recvs):
                send.wait_send()
                recv.wait_recv()
            for cp in local:
                cp.wait()

        return start, wait


def _exchange_call(ex, name):
    n_in, n = len(ex.args), len(ex.srcs)

    def body(*refs):
        start, wait = ex.ops(refs[:n_in], refs[n_in:n_in + n], refs[n_in + n:])
        start()
        wait()

    return pl.pallas_call(
        body, name=name, in_specs=ex.in_specs, out_specs=ex.out_specs, out_shape=ex.out_shape,
        scratch_shapes=ex.scratch, input_output_aliases=ex.aliases(0, 0))(*ex.args)


def _split_refs(refs, counts):
    out, k = [], 0
    for cnt in counts:
        out.append(refs[k:k + cnt])
        k += cnt
    return out


def _ex_parts(ex):
    if ex is None:
        return [], [], [], [], [], (0, 0, 0)
    return ex.args, ex.in_specs, ex.out_shape, ex.out_specs, ex.scratch, (len(ex.args), len(ex.srcs), 3)


def _norm_matmul(h, w, g=None, *, tt, tn, name):
    t, k = h.shape
    n = w.shape[1]
    tt, tn = _tile(t, tt), _tile(n, tn)
    normed = g is not None

    def body(*refs):
        if normed:
            h_ref, g_ref, w_ref, o_ref = refs
            x = h_ref[...]
            xn = (x * _rms_r(x) * g_ref[...]).astype(BF16)
        else:
            h_ref, w_ref, o_ref = refs
            xn = h_ref[...].astype(BF16)
        o_ref[...] = _dot(xn, w_ref[...])

    in_specs = [pl.BlockSpec((tt, k), lambda j, i: (i, 0))]
    args = [h]
    if normed:
        in_specs.append(pl.BlockSpec((1, k), lambda j, i: (0, 0)))
        args.append(g)
    in_specs.append(pl.BlockSpec((k, tn), lambda j, i: (0, j)))
    args.append(w)
    return pl.pallas_call(
        body, name=name, grid=(n // tn, t // tt), in_specs=in_specs,
        out_specs=pl.BlockSpec((tt, tn), lambda j, i: (i, j)), out_shape=jax.ShapeDtypeStruct((t, n), F32),
        compiler_params=_params("arbitrary", "arbitrary"))(*args)


def _matmul_tn(a, b, *, tm, tn, tk, name):
    t, m = a.shape
    n = b.shape[1]
    tm, tn, tk = _tile(m, tm), _tile(n, tn), _tile(t, tk)

    def body(a_ref, b_ref, o_ref):
        @pl.when(pl.program_id(2) == 0)
        def _():
            o_ref[...] = jnp.zeros_like(o_ref)

        o_ref[...] += _dot_tn(a_ref[...].astype(BF16), b_ref[...].astype(BF16))

    return pl.pallas_call(
        body, name=name, grid=(m // tm, n // tn, t // tk),
        in_specs=[pl.BlockSpec((tk, tm), lambda i, j, kk: (kk, i)), pl.BlockSpec((tk, tn), lambda i, j, kk: (kk, j))],
        out_specs=pl.BlockSpec((tm, tn), lambda i, j, kk: (i, j)),
        out_shape=jax.ShapeDtypeStruct((m, n), F32),
        compiler_params=_params("arbitrary", "arbitrary", "arbitrary"))(a, b)


def _matmul_rmsbwd(a, wt, x, g, dh, *, tt, tk, name):
    t, k = a.shape
    d = wt.shape[1]
    tt, tk = _tile(t, tt), _tile(k, tk)
    nk = k // tk

    def body(a_ref, w_ref, x_ref, g_ref, dh_ref, o_ref, dg_ref, acc):
        i, kk = pl.program_id(0), pl.program_id(1)

        @pl.when(kk == 0)
        def _():
            acc[...] = jnp.zeros_like(acc)

        @pl.when((i == 0) & (kk == 0))
        def _():
            dg_ref[...] = jnp.zeros_like(dg_ref)

        acc[...] += _dot(a_ref[...], w_ref[...])

        @pl.when(kk == nk - 1)
        def _():
            xv = x_ref[...]
            dx, dg = _rms_bwd(xv, _rms_r(xv), g_ref[...], acc[...])
            o_ref[...] = dh_ref[...] + dx
            dg_ref[...] += dg

    return pl.pallas_call(
        body, name=name, grid=(t // tt, nk),
        in_specs=[pl.BlockSpec((tt, tk), lambda i, kk: (i, kk)), pl.BlockSpec((tk, d), lambda i, kk: (kk, 0)),
                  pl.BlockSpec((tt, d), lambda i, kk: (i, 0)), pl.BlockSpec((1, d), lambda i, kk: (0, 0)),
                  pl.BlockSpec((tt, d), lambda i, kk: (i, 0))],
        out_specs=[pl.BlockSpec((tt, d), lambda i, kk: (i, 0)), pl.BlockSpec((1, d), lambda i, kk: (0, 0))],
        out_shape=[jax.ShapeDtypeStruct((t, d), F32), jax.ShapeDtypeStruct((1, d), F32)],
        scratch_shapes=[pltpu.VMEM((tt, d), F32)],
        compiler_params=_params("arbitrary", "arbitrary"))(a, wt, x, g, dh)


def _ssd_tile_prologue(i_is_first, xbc_ref, halo_ref, dt_ref, cw_ref, cb_ref, dtb_ref, alog_ref, e_ref, buf, xc_scr,
                       xa_scr, a_scr, dte_scr, x_scr, ts):
    buf[0:SUBLANES, :] = jnp.where(i_is_first, 0.0, halo_ref[...])
    buf[SUBLANES:SUBLANES + ts, :] = xbc_ref[...]
    cw = cw_ref[...]
    xc = cb_ref[...]
    for k in range(SSD_CONV):
        off = SUBLANES - (SSD_CONV - 1) + k
        xc = xc + cw[k:k + 1, :] * buf[off:off + ts, :]
    if xc_scr is not None:
        xc_scr[...] = xc
    xa_scr[...] = xc * _sigmoid(xc)
    dt = _softplus(dt_ref[...] + dtb_ref[...])
    a_neg = -jnp.exp(alog_ref[...])
    a_scr[...] = dt * a_neg
    dte = _hdot(dt, e_ref[...])
    dte_scr[...] = dte
    x_scr[...] = xa_scr[:, 0:SSD_WIDTH] * dte
    return dt, a_neg


def _chunk_decays(a_c, tril, e):
    cs = _hdot(tril, a_c)
    cs_t = cs.T
    cs_e = _hdot(cs, e)
    last_e = cs_e[CHUNK - 1:CHUNK, :]
    return cs, cs_t, cs_e, last_e


def _ssd_fwd(proj, cw, cb, dtb, alog, dexp, ng, tril, e, *, ts, name):
    t = proj.shape[0]
    ts = _tile(t, ts)
    nch = ts // CHUNK
    hb = ts // SUBLANES

    def body(xbc_ref, halo_ref, z_ref, dt_ref, cw_ref, cb_ref, dtb_ref, alog_ref, dexp_ref, ng_ref, tril_ref, e_ref,
             y_ref, ypre_ref, st_ref, buf, xa_scr, a_scr, dte_scr, x_scr, ys_scr, hstate):
        i = pl.program_id(0)

        @pl.when(i == 0)
        def _():
            hstate[...] = jnp.zeros_like(hstate)

        _ssd_tile_prologue(i == 0, xbc_ref, halo_ref, dt_ref, cw_ref, cb_ref, dtb_ref, alog_ref, e_ref, buf, None,
                           xa_scr, a_scr, dte_scr, x_scr, ts)
        tril = tril_ref[...]
        e_mat = e_ref[...]
        causal = tril > 0.5
        lane = lax.broadcasted_iota(jnp.int32, (CHUNK, LANES), 1)

        def chunk(c, carry):
            r0 = pl.multiple_of(c * CHUNK, CHUNK)
            rows = pl.ds(r0, CHUNK)
            cs, cs_t, cs_e, last_e = _chunk_decays(a_scr[rows, :], tril, e_mat)
            decay_e = jnp.exp(last_e - cs_e)
            ecs_e = jnp.exp(cs_e)
            xc = x_scr[rows, :]
            xb = xc.astype(BF16)
            xd = (xc * decay_e).astype(BF16)
            for g in range(SSD_GROUPS):
                bg = xa_scr[rows, SSD_WIDTH + g * SSD_STATE:SSD_WIDTH + (g + 1) * SSD_STATE].astype(BF16)
                cg = xa_scr[rows, SSD_WIDTH + (SSD_GROUPS + g) * SSD_STATE:
                            SSD_WIDTH + (SSD_GROUPS + g + 1) * SSD_STATE].astype(BF16)
                cbm = _dot_nt(cg, bg)
                for jj in range(2):
                    j = 2 * g + jj
                    cols = slice(j * LANES, (j + 1) * LANES)
                    xp = xb[:, cols]
                    ypair = jnp.zeros((CHUNK, LANES), F32)
                    for hh in range(2):
                        h = 2 * j + hh
                        seg = jnp.exp(jnp.where(causal, cs[:, h:h + 1] - cs_t[h:h + 1, :], -jnp.inf))
                        m = (cbm * seg).astype(BF16)
                        half = (lane < SSD_HEAD_DIM) if hh == 0 else (lane >= SSD_HEAD_DIM)
                        ypair = ypair + _dot(m, jnp.where(half, xp, jnp.zeros_like(xp)))
                    hp = hstate[j]
                    st_ref[c, j] = hp
                    ypair = ypair + _dot(cg, hp.astype(BF16)) * ecs_e[:, cols]
                    ys_scr[rows, cols] = ypair
                    hstate[j] = hp * jnp.exp(last_e[:, cols]) + _dot_tn(bg, xd[:, cols])
            return carry

        lax.fori_loop(0, nch, chunk, 0)
        ypre = ys_scr[...] + xa_scr[:, 0:SSD_WIDTH] * dexp_ref[...]
        ypre_ref[...] = ypre
        z = z_ref[...]
        yg = ypre * (z * _sigmoid(z))
        gw = SSD_WIDTH // SSD_GROUPS
        outs = []
        for g in range(SSD_GROUPS):
            v = yg[:, g * gw:(g + 1) * gw]
            outs.append(v * _rms_r(v))
        y_ref[...] = jnp.concatenate(outs, axis=1) * ng_ref[...]

    full = lambda shape: pl.BlockSpec(shape, lambda i: tuple(0 for _ in shape))
    return pl.pallas_call(
        body, name=name, grid=(t // ts,),
        in_specs=[pl.BlockSpec((ts, SSD_XBC), lambda i: (i, COL_XBC // SSD_XBC)),
                  pl.BlockSpec((SUBLANES, SSD_XBC), lambda i: (jnp.maximum(i * hb - 1, 0), COL_XBC // SSD_XBC)),
                  pl.BlockSpec((ts, SSD_WIDTH), lambda i: (i, COL_Z // SSD_WIDTH)),
                  pl.BlockSpec((ts, LANES), lambda i: (i, COL_DT // LANES)),
                  full((SSD_CONV, SSD_XBC)), full((1, SSD_XBC)), full((1, LANES)), full((1, LANES)),
                  full((1, SSD_WIDTH)), full((1, SSD_WIDTH)), full((CHUNK, CHUNK)), full((LANES, SSD_WIDTH))],
        out_specs=[pl.BlockSpec((ts, SSD_WIDTH), lambda i: (i, 0)), pl.BlockSpec((ts, SSD_WIDTH), lambda i: (i, 0)),
                   pl.BlockSpec((nch, N_PAIRS, SSD_STATE, LANES), lambda i: (i, 0, 0, 0))],
        out_shape=[jax.ShapeDtypeStruct((t, SSD_WIDTH), F32), jax.ShapeDtypeStruct((t, SSD_WIDTH), F32),
                   jax.ShapeDtypeStruct((t // CHUNK, N_PAIRS, SSD_STATE, LANES), F32)],
        scratch_shapes=[pltpu.VMEM((SUBLANES + ts, SSD_XBC), F32), pltpu.VMEM((ts, SSD_XBC), F32),
                        pltpu.VMEM((ts, LANES), F32), pltpu.VMEM((ts, SSD_WIDTH), F32),
                        pltpu.VMEM((ts, SSD_WIDTH), F32), pltpu.VMEM((ts, SSD_WIDTH), F32),
                        pltpu.VMEM((N_PAIRS, SSD_STATE, LANES), F32)],
        compiler_params=_params("arbitrary"))(proj, proj, proj, proj, cw, cb, dtb, alog, dexp, ng, tril, e)


def _ssd_bwd(dymix, proj, ypre, states, du, cw, cb, dtb, alog, dexp, ng, tril, triu, e, *, ts, name):
    t = proj.shape[0]
    ts = _tile(t, ts)
    nch = ts // CHUNK
    hb = ts // SUBLANES
    nt = t // ts

    def body(dy_ref, xbc_ref, halo_ref, z_ref, dt_ref, ypre_ref, st_ref, du_ref, cw_ref, cb_ref, dtb_ref, alog_ref,
             dexp_ref, ng_ref, tril_ref, triu_ref, e_ref,
             dproj_ref, dcw_ref, dcb_ref, ddtb_ref, dalog_ref, dd_ref, dng_ref,
             buf, xc_scr, xa_scr, a_scr, dte_scr, x_scr, dyp_scr, dxa_scr, dx_scr, dbuf, carry, gstate):
        i = pl.program_id(0)

        @pl.when(i == 0)
        def _():
            gstate[...] = jnp.zeros_like(gstate)
            carry[...] = jnp.zeros_like(carry)
            for ref in (dcw_ref, dcb_ref, ddtb_ref, dalog_ref, dd_ref, dng_ref):
                ref[...] = jnp.zeros_like(ref)

        dt, a_neg = _ssd_tile_prologue(i == nt - 1, xbc_ref, halo_ref, dt_ref, cw_ref, cb_ref, dtb_ref, alog_ref, e_ref,
                                       buf, xc_scr, xa_scr, a_scr, dte_scr, x_scr, ts)
        tril = tril_ref[...]
        triu = triu_ref[...]
        e_mat = e_ref[...]
        causal = tril > 0.5
        lane = lax.broadcasted_iota(jnp.int32, (CHUNK, LANES), 1)
        sub = lax.broadcasted_iota(jnp.int32, (CHUNK, LANES), 0)

        z = z_ref[...]
        sig = _sigmoid(z)
        zs = z * sig
        ypre = ypre_ref[...]
        yg = ypre * zs
        dout = dy_ref[...]
        ngv = ng_ref[...]
        gw = SSD_WIDTH // SSD_GROUPS
        dyg_parts, dng_parts = [], []
        for g in range(SSD_GROUPS):
            cols = slice(g * gw, (g + 1) * gw)
            v = yg[:, cols]
            dx, dg = _rms_bwd(v, _rms_r(v), ngv[:, cols], dout[:, cols])
            dyg_parts.append(dx)
            dng_parts.append(dg)
        dyg = jnp.concatenate(dyg_parts, axis=1)
        dng_ref[...] += jnp.concatenate(dng_parts, axis=1)
        dyp = dyg * zs
        dyp_scr[...] = dyp
        dproj_ref[:, COL_Z:COL_Z + SSD_WIDTH] = dyg * ypre * (sig * (1.0 + z * (1.0 - sig)))
        dproj_ref[:, COL_U:COL_U + POOL_WIDTH] = du_ref[...]
        xs_all = xa_scr[:, 0:SSD_WIDTH]
        dd_ref[...] += _headsum(jnp.broadcast_to(_colsum(dyp * xs_all), (SUBLANES, SSD_WIDTH)), e_mat)[0:1, :]

        def chunk(k, carry_):
            c = nch - 1 - k
            r0 = pl.multiple_of(c * CHUNK, CHUNK)
            rows = pl.ds(r0, CHUNK)
            a_c = a_scr[rows, :]
            cs, cs_t, cs_e, last_e = _chunk_decays(a_c, tril, e_mat)
            decay_e = jnp.exp(last_e - cs_e)
            ecs_e = jnp.exp(cs_e)
            elast_e = jnp.exp(last_e)
            xc = x_scr[rows, :]
            xb = xc.astype(BF16)
            xd = (xc * decay_e).astype(BF16)
            dyc = dyp_scr[rows, :]
            dcs = jnp.zeros((CHUNK, LANES), F32)
            dcs_neg_t = jnp.zeros((LANES, CHUNK), F32)
            qoff, rin, ghrow = [], [], []
            for g in range(SSD_GROUPS):
                b_cols = slice(SSD_WIDTH + g * SSD_STATE, SSD_WIDTH + (g + 1) * SSD_STATE)
                c_cols = slice(SSD_WIDTH + (SSD_GROUPS + g) * SSD_STATE, SSD_WIDTH + (SSD_GROUPS + g + 1) * SSD_STATE)
                bg = xa_scr[rows, b_cols].astype(BF16)
                cg = xa_scr[rows, c_cols].astype(BF16)
                cbm = _dot_nt(cg, bg)
                dcb_m = jnp.zeros((CHUNK, CHUNK), F32)
                dbg = jnp.zeros((CHUNK, SSD_STATE), F32)
                dcg = jnp.zeros((CHUNK, SSD_STATE), F32)
                for jj in range(2):
                    j = 2 * g + jj
                    cols = slice(j * LANES, (j + 1) * LANES)
                    dyp_j = dyc[:, cols]
                    hp = st_ref[c, j]
                    hpb = hp.astype(BF16)
                    gt = gstate[j]
                    gtb = gt.astype(BF16)
                    ecs = ecs_e[:, cols]
                    yoff = _dot(cg, hpb) * ecs
                    dye = (dyp_j * ecs).astype(BF16)
                    dcg = dcg + _dot_nt(dye, hpb)
                    dht = _dot_tn(cg, dye)
                    qoff.append(dyp_j * yoff)
                    xg = _dot(bg, gtb)
                    dxp = xg * decay_e[:, cols]
                    rin.append(xg * xc[:, cols])
                    dbg = dbg + _dot_nt(xd[:, cols], gtb)
                    ghrow.append(_colsum(gt * hp) * elast_e[:, cols])
                    gstate[j] = dht + gt * elast_e[:, cols]
                    for hh in range(2):
                        h = 2 * j + hh
                        seg = jnp.exp(jnp.where(causal, cs[:, h:h + 1] - cs_t[h:h + 1, :], -jnp.inf))
                        m = cbm * seg
                        half = (lane < SSD_HEAD_DIM) if hh == 0 else (lane >= SSD_HEAD_DIM)
                        dym = jnp.where(half, dyp_j, 0.0).astype(BF16)
                        w = _dot_nt(dym, xb[:, cols])
                        pm = w * m
                        dcs = dcs + jnp.where(lane == h, jnp.sum(pm, axis=1, keepdims=True), 0.0)
                        dcs_neg_t = dcs_neg_t + jnp.where(sub == h, _colsum(pm), 0.0)
                        dcb_m = dcb_m + w * seg
                        dxp = dxp + _dot_tn(m.astype(BF16), dym)
                    dx_scr[:, cols] = dxp
                dcbb = dcb_m.astype(BF16)
                dxa_scr[rows, c_cols] = dcg + _dot(dcbb, bg)
                dxa_scr[rows, b_cols] = dbg + _dot_tn(dcbb, cg)
            decay_th = jnp.exp(cs[CHUNK - 1:CHUNK, :] - cs)
            rd = _headsum(jnp.concatenate(rin, axis=1), e_mat) * decay_th
            dcs = dcs - dcs_neg_t.T + _headsum(jnp.concatenate(qoff, axis=1), e_mat) - rd
            gh = _headsum(jnp.broadcast_to(jnp.concatenate(ghrow, axis=1), (SUBLANES, SSD_WIDTH)), e_mat)[0:1, :]
            dcs = dcs + jnp.where(sub == CHUNK - 1, _colsum(rd) + gh, 0.0)
            da = _hdot(triu, dcs)
            dx_all = dx_scr[...]
            xs = xa_scr[rows, 0:SSD_WIDTH]
            dt_c = _softplus(dt_ref[rows, :] + dtb_ref[...])
            ddt = da * a_neg + _headsum(dx_all * xs, e_mat)
            dalog_ref[...] += _colsum(da * dt_c) * a_neg
            ddtraw = ddt * _sigmoid(dt_ref[rows, :] + dtb_ref[...])
            dproj_ref[rows, COL_DT:COL_DT + LANES] = ddtraw
            ddtb_ref[...] += _colsum(ddtraw)
            dxa_scr[rows, 0:SSD_WIDTH] = dx_all * dte_scr[rows, :] + dyc * dexp_ref[...]
            return carry_

        lax.fori_loop(0, nch, chunk, 0)

        xcv = xc_scr[...]
        sgc = _sigmoid(xcv)
        dxc = dxa_scr[...] * (sgc * (1.0 + xcv * (1.0 - sgc)))
        dcb_ref[...] += _colsum(dxc)
        dbuf[0:ts, :] = dxc
        dbuf[ts:ts + SUBLANES, :] = carry[...]
        cwv = cw_ref[...]
        dxbc = jnp.zeros((ts, SSD_XBC), F32)
        dcw_rows = []
        for k in range(SSD_CONV):
            off = SUBLANES - (SSD_CONV - 1) + k
            dcw_rows.append(_colsum(dxc * buf[off:off + ts, :]))
            back = SSD_CONV - 1 - k
            dxbc = dxbc + cwv[k:k + 1, :] * dbuf[back:back + ts, :]
        dcw_ref[...] += jnp.concatenate(dcw_rows, axis=0)
        dproj_ref[:, COL_XBC:COL_XBC + SSD_XBC] = dxbc
        carry[...] = dxc[0:SUBLANES, :]

    rev = lambda i: nt - 1 - i
    full = lambda shape: pl.BlockSpec(shape, lambda i: tuple(0 for _ in shape))
    outs = pl.pallas_call(
        body, name=name, grid=(nt,),
        in_specs=[pl.BlockSpec((ts, SSD_WIDTH), lambda i: (rev(i), 0)),
                  pl.BlockSpec((ts, SSD_XBC), lambda i: (rev(i), COL_XBC // SSD_XBC)),
                  pl.BlockSpec((SUBLANES, SSD_XBC), lambda i: (jnp.maximum(rev(i) * hb - 1, 0), COL_XBC // SSD_XBC)),
                  pl.BlockSpec((ts, SSD_WIDTH), lambda i: (rev(i), COL_Z // SSD_WIDTH)),
                  pl.BlockSpec((ts, LANES), lambda i: (rev(i), COL_DT // LANES)),
                  pl.BlockSpec((ts, SSD_WIDTH), lambda i: (rev(i), 0)),
                  pl.BlockSpec((nch, N_PAIRS, SSD_STATE, LANES), lambda i: (rev(i), 0, 0, 0)),
                  pl.BlockSpec((ts, POOL_WIDTH), lambda i: (rev(i), 0)),
                  full((SSD_CONV, SSD_XBC)), full((1, SSD_XBC)), full((1, LANES)), full((1, LANES)),
                  full((1, SSD_WIDTH)), full((1, SSD_WIDTH)), full((CHUNK, CHUNK)), full((CHUNK, CHUNK)),
                  full((LANES, SSD_WIDTH))],
        out_specs=[pl.BlockSpec((ts, N_PROJ), lambda i: (rev(i), 0)),
                   full((SSD_CONV, SSD_XBC)), full((1, SSD_XBC)), full((1, LANES)), full((1, LANES)),
                   full((1, LANES)), full((1, SSD_WIDTH))],
        out_shape=[jax.ShapeDtypeStruct((t, N_PROJ), F32),
                   jax.ShapeDtypeStruct((SSD_CONV, SSD_XBC), F32), jax.ShapeDtypeStruct((1, SSD_XBC), F32),
                   jax.ShapeDtypeStruct((1, LANES), F32), jax.ShapeDtypeStruct((1, LANES), F32),
                   jax.ShapeDtypeStruct((1, LANES), F32), jax.ShapeDtypeStruct((1, SSD_WIDTH), F32)],
        scratch_shapes=[pltpu.VMEM((SUBLANES + ts, SSD_XBC), F32), pltpu.VMEM((ts, SSD_XBC), F32),
                        pltpu.VMEM((ts, SSD_XBC), F32), pltpu.VMEM((ts, LANES), F32),
                        pltpu.VMEM((ts, SSD_WIDTH), F32), pltpu.VMEM((ts, SSD_WIDTH), F32),
                        pltpu.VMEM((ts, SSD_WIDTH), F32), pltpu.VMEM((ts, SSD_XBC), F32),
                        pltpu.VMEM((CHUNK, SSD_WIDTH), F32), pltpu.VMEM((ts + SUBLANES, SSD_XBC), F32),
                        pltpu.VMEM((SUBLANES, SSD_XBC), F32), pltpu.VMEM((N_PAIRS, SSD_STATE, LANES), F32)],
        compiler_params=_params("arbitrary"))(
            dymix, proj, proj, proj, proj, ypre, states, du, cw, cb, dtb, alog, dexp, ng, tril, triu, e)
    return outs


def _pooled(ubuf, u, pos, tt):
    out = []
    for gi, w in enumerate(POOL_WINDOWS):
        cols = slice(gi * POOL_GROUP, (gi + 1) * POOL_GROUP)
        acc = u[:, cols]
        for j in range(1, w):
            acc = acc + ubuf[POOL_HALO - j:POOL_HALO - j + tt, cols]
        out.append(acc / jnp.minimum(pos, float(w)) - u[:, cols])
    return out


def _mix_out(h, yssd, proj, pool_w, pool_scale, w_out, g_next, *, tt, name):
    t = h.shape[0]
    tt = _tile(t, tt)
    hb = tt // POOL_HALO

    def body(h_ref, ys_ref, u_ref, uh_ref, pw_ref, sc_ref, wo_ref, gn_ref, o_ref, ym_ref, n_ref, ubuf):
        i = pl.program_id(0)
        ubuf[0:POOL_HALO, :] = jnp.where(i == 0, 0.0, uh_ref[...])
        u = u_ref[...]
        ubuf[POOL_HALO:POOL_HALO + tt, :] = u
        pos = (i * tt + 1 + lax.broadcasted_iota(jnp.int32, (tt, 1), 0)).astype(F32)
        sc = sc_ref[...]
        parts = [ys_ref[...]]
        for gi, pooled in enumerate(_pooled(ubuf, u, pos, tt)):
            cols = slice(gi * POOL_GROUP, (gi + 1) * POOL_GROUP)
            parts.append(_dot(pooled.astype(BF16), pw_ref[gi]) * sc[:, cols])
        ymix = jnp.concatenate(parts, axis=1).astype(BF16)
        ym_ref[...] = ymix
        h2 = h_ref[...] + _dot(ymix, wo_ref[...])
        o_ref[...] = h2
        n_ref[...] = (h2 * _rms_r(h2) * gn_ref[...]).astype(BF16)

    full = lambda shape: pl.BlockSpec(shape, lambda i: tuple(0 for _ in shape))
    return pl.pallas_call(
        body, name=name, grid=(t // tt,),
        in_specs=[pl.BlockSpec((tt, D_MODEL), lambda i: (i, 0)), pl.BlockSpec((tt, SSD_WIDTH), lambda i: (i, 0)),
                  pl.BlockSpec((tt, POOL_WIDTH), lambda i: (i, COL_U // POOL_WIDTH)),
                  pl.BlockSpec((POOL_HALO, POOL_WIDTH), lambda i: (jnp.maximum(i * hb - 1, 0), COL_U // POOL_WIDTH)),
                  full((len(POOL_WINDOWS), POOL_GROUP, POOL_GROUP)), full((1, POOL_WIDTH)),
                  full((D_MODEL, D_MODEL)), full((1, D_MODEL))],
        out_specs=[pl.BlockSpec((tt, D_MODEL), lambda i: (i, 0))] * 3,
        out_shape=[jax.ShapeDtypeStruct((t, D_MODEL), F32), jax.ShapeDtypeStruct((t, D_MODEL), BF16),
                   jax.ShapeDtypeStruct((t, D_MODEL), BF16)],
        scratch_shapes=[pltpu.VMEM((POOL_HALO + tt, POOL_WIDTH), F32)],
        compiler_params=_params("arbitrary"))(h, yssd, proj, proj, pool_w, pool_scale, w_out, g_next)


def _out_bwd(dh, ymix, w_out_t, *, tt, name):
    t = dh.shape[0]
    tt = _tile(t, tt)

    def body(dh_ref, ym_ref, wt_ref, dym_ref, dw_ref):
        @pl.when(pl.program_id(0) == 0)
        def _():
            dw_ref[...] = jnp.zeros_like(dw_ref)

        dhb = dh_ref[...].astype(BF16)
        dym_ref[...] = _dot(dhb, wt_ref[...])
        dw_ref[...] += _dot_tn(ym_ref[...], dhb)

    return pl.pallas_call(
        body, name=name, grid=(t // tt,),
        in_specs=[pl.BlockSpec((tt, D_MODEL), lambda i: (i, 0)), pl.BlockSpec((tt, D_MODEL), lambda i: (i, 0)),
                  pl.BlockSpec((D_MODEL, D_MODEL), lambda i: (0, 0))],
        out_specs=[pl.BlockSpec((tt, D_MODEL), lambda i: (i, 0)), pl.BlockSpec((D_MODEL, D_MODEL), lambda i: (0, 0))],
        out_shape=[jax.ShapeDtypeStruct((t, D_MODEL), F32), jax.ShapeDtypeStruct((D_MODEL, D_MODEL), F32)],
        compiler_params=_params("arbitrary"))(dh, ymix, w_out_t)


def _pool_bwd(dymix, proj, pool_w, pool_w_t, pool_scale, *, tt, name):
    t = proj.shape[0]
    tt = _tile(t, tt)
    hb = tt // POOL_HALO
    nt = t // tt
    ng = len(POOL_WINDOWS)

    def body(dy_ref, dyh_ref, u_ref, uh_ref, pw_ref, pwt_ref, sc_ref, du_ref, dpw_ref, dsc_ref, ubuf, dbuf):
        i = pl.program_id(0)

        @pl.when(i == 0)
        def _():
            dpw_ref[...] = jnp.zeros_like(dpw_ref)
            dsc_ref[...] = jnp.zeros_like(dsc_ref)

        ubuf[0:POOL_HALO, :] = jnp.where(i == 0, 0.0, uh_ref[...])
        u = u_ref[...]
        ubuf[POOL_HALO:POOL_HALO + tt, :] = u
        pos = (i * tt + 1 + lax.broadcasted_iota(jnp.int32, (tt, 1), 0)).astype(F32)
        sc = sc_ref[...]
        dy = dy_ref[...]
        dyh = jnp.where(i == nt - 1, 0.0, dyh_ref[...])
        dsc_parts, du_parts = [], []
        for gi, pooled in enumerate(_pooled(ubuf, u, pos, tt)):
            w = POOL_WINDOWS[gi]
            cols = slice(gi * POOL_GROUP, (gi + 1) * POOL_GROUP)
            pb = pooled.astype(BF16)
            dsc_parts.append(_colsum(dy[:, cols] * _dot(pb, pw_ref[gi])))
            dmx = (dy[:, cols] * sc[:, cols]).astype(BF16)
            dpw_ref[gi] += _dot_tn(pb, dmx)
            dpool = _dot(dmx, pwt_ref[gi])
            dpool_h = _dot((dyh[:, cols] * sc[:, cols]).astype(BF16), pwt_ref[gi])
            dbuf[0:tt, cols] = dpool / jnp.minimum(pos, float(w))
            dbuf[tt:tt + POOL_HALO, cols] = dpool_h / float(w)
            acc = -dpool
            for j in range(w):
                acc = acc + dbuf[j:j + tt, cols]
            du_parts.append(acc)
        du_ref[...] = jnp.concatenate(du_parts, axis=1)
        dsc_ref[...] += jnp.concatenate(dsc_parts, axis=1)

    full = lambda shape: pl.BlockSpec(shape, lambda i: tuple(0 for _ in shape))
    ucol = COL_U // POOL_WIDTH
    return pl.pallas_call(
        body, name=name, grid=(nt,),
        in_specs=[pl.BlockSpec((tt, POOL_WIDTH), lambda i: (i, 1)),
                  pl.BlockSpec((POOL_HALO, POOL_WIDTH), lambda i: (jnp.minimum((i + 1) * hb, t // POOL_HALO - 1), 1)),
                  pl.BlockSpec((tt, POOL_WIDTH), lambda i: (i, ucol)),
                  pl.BlockSpec((POOL_HALO, POOL_WIDTH), lambda i: (jnp.maximum(i * hb - 1, 0), ucol)),
                  full((ng, POOL_GROUP, POOL_GROUP)), full((ng, POOL_GROUP, POOL_GROUP)), full((1, POOL_WIDTH))],
        out_specs=[pl.BlockSpec((tt, POOL_WIDTH), lambda i: (i, 0)), full((ng, POOL_GROUP, POOL_GROUP)),
                   full((1, POOL_WIDTH))],
        out_shape=[jax.ShapeDtypeStruct((t, POOL_WIDTH), F32), jax.ShapeDtypeStruct((ng, POOL_GROUP, POOL_GROUP), F32),
                   jax.ShapeDtypeStruct((1, POOL_WIDTH), F32)],
        scratch_shapes=[pltpu.VMEM((POOL_HALO + tt, POOL_WIDTH), F32), pltpu.VMEM((tt + POOL_HALO, POOL_WIDTH), F32)],
        compiler_params=_params("arbitrary"))(dymix, dymix, proj, proj, pool_w, pool_w_t, pool_scale)


def _in_bwd(dproj, h, g, w_in_t, dh, *, tt, name):
    t = h.shape[0]
    tt = _tile(t, tt)

    def body(dp_ref, h_ref, g_ref, wt_ref, dh_ref, o_ref, dw_ref, dg_ref):
        @pl.when(pl.program_id(0) == 0)
        def _():
            dw_ref[...] = jnp.zeros_like(dw_ref)
            dg_ref[...] = jnp.zeros_like(dg_ref)

        x = h_ref[...]
        r = _rms_r(x)
        gv = g_ref[...]
        dpb = dp_ref[...].astype(BF16)
        dw_ref[...] += _dot_tn((x * r * gv).astype(BF16), dpb)
        dx, dg = _rms_bwd(x, r, gv, _dot(dpb, wt_ref[...]))
        o_ref[...] = dh_ref[...] + dx
        dg_ref[...] += dg

    full = lambda shape: pl.BlockSpec(shape, lambda i: tuple(0 for _ in shape))
    row = lambda n: pl.BlockSpec((tt, n), lambda i: (i, 0))
    return pl.pallas_call(
        body, name=name, grid=(t // tt,),
        in_specs=[row(N_PROJ), row(D_MODEL), full((1, D_MODEL)), full((N_PROJ, D_MODEL)), row(D_MODEL)],
        out_specs=[row(D_MODEL), full((D_MODEL, N_PROJ)), full((1, D_MODEL))],
        out_shape=[jax.ShapeDtypeStruct((t, D_MODEL), F32), jax.ShapeDtypeStruct((D_MODEL, N_PROJ), F32),
                   jax.ShapeDtypeStruct((1, D_MODEL), F32)],
        compiler_params=_params("arbitrary"))(dproj, h, g, w_in_t, dh)


FFN_COLS = 256
N_SLABS = D_FF // FFN_COLS
FFN_ROWS = 16


def _conv3_block(ref, halo_ref, b, r0, cols, halo_is_zero):
    rp = pl.multiple_of(jnp.maximum(r0 - SUBLANES, 0), SUBLANES)
    prev = jnp.where(b == 0, jnp.where(halo_is_zero, 0.0, halo_ref[:, cols]), ref[pl.ds(rp, SUBLANES), cols])
    x = ref[pl.ds(r0, FFN_ROWS), cols]
    return x, jnp.concatenate([prev, x], axis=0)


def _conv3(x, ext, w, b):
    acc = b + w[FFN_CONV - 1:FFN_CONV] * x
    for k in range(FFN_CONV - 1):
        off = SUBLANES - (FFN_CONV - 1) + k
        acc = acc + w[k:k + 1] * ext[off:off + FFN_ROWS]
    return acc


def _part8(v):
    acc = v[0:SUBLANES]
    for r in range(SUBLANES, FFN_ROWS, SUBLANES):
        acc = acc + v[r:r + SUBLANES]
    return acc


def _ffn_down(h, up, cw, cb, w_down, *, tt, name, ex=None):
    t = h.shape[0]
    tt = _tile(t, tt)
    hb = tt // SUBLANES
    nt = t // tt
    nb = tt // FFN_ROWS
    ex_args, ex_in_specs, ex_out_shape, ex_out_specs, ex_scratch, ex_counts = _ex_parts(ex)

    def body(*refs):
        (h_ref, up_ref, uh_ref, cw_ref, cb_ref, wd_ref), ex_in, (o_ref, act_ref), ex_out, ex_sems = _split_refs(
            refs, (6, ex_counts[0], 2, ex_counts[1], ex_counts[2]))
        i = pl.program_id(0)
        if ex is not None:
            ex_start, ex_wait = ex.ops(ex_in, ex_out, ex_sems)
            pl.when(i == 0)(ex_start)
        for s in range(N_SLABS):
            gcols = slice(s * FFN_COLS, (s + 1) * FFN_COLS)
            vcols = slice(D_FF + s * FFN_COLS, D_FF + (s + 1) * FFN_COLS)
            wg, wv, bg, bv = cw_ref[:, gcols], cw_ref[:, vcols], cb_ref[:, gcols], cb_ref[:, vcols]

            def blk(b, carry):
                r0 = pl.multiple_of(b * FFN_ROWS, FFN_ROWS)
                gate = _conv3(*_conv3_block(up_ref, uh_ref, b, r0, gcols, i == 0), wg, bg)
                val = _conv3(*_conv3_block(up_ref, uh_ref, b, r0, vcols, i == 0), wv, bv)
                act_ref[pl.ds(r0, FFN_ROWS), gcols] = (_gelu(gate) * val).astype(BF16)
                return carry

            lax.fori_loop(0, nb, blk, 0)
        o_ref[...] = h_ref[...] + _dot(act_ref[...], wd_ref[...])
        if ex is not None:
            pl.when(i == nt - 1)(ex_wait)

    full = lambda shape: pl.BlockSpec(shape, lambda i: tuple(0 for _ in shape))
    outs = pl.pallas_call(
        body, name=name, grid=(nt,),
        in_specs=[pl.BlockSpec((tt, D_MODEL), lambda i: (i, 0)), pl.BlockSpec((tt, D_UP), lambda i: (i, 0)),
                  pl.BlockSpec((SUBLANES, D_UP), lambda i: (jnp.maximum(i * hb - 1, 0), 0)),
                  full((FFN_CONV, D_UP)), full((1, D_UP)), full((D_FF, D_MODEL))] + ex_in_specs,
        out_specs=[pl.BlockSpec((tt, D_MODEL), lambda i: (i, 0)), pl.BlockSpec((tt, D_FF), lambda i: (i, 0))]
        + ex_out_specs,
        out_shape=[jax.ShapeDtypeStruct((t, D_MODEL), F32), jax.ShapeDtypeStruct((t, D_FF), BF16)] + ex_out_shape,
        scratch_shapes=ex_scratch,
        input_output_aliases={} if ex is None else ex.aliases(6, 2),
        compiler_params=_params("arbitrary"))(h, up, up, cw, cb, w_down, *ex_args)
    return outs[0], outs[1], outs[2:]


def _ffn_act_bwd(up, dact, cw, cb, *, tt, name, ex=None):
    t = up.shape[0]
    tt = _tile(t, tt)
    hb = tt // SUBLANES
    nt = t // tt
    nb = tt // FFN_ROWS
    ex_args, ex_in_specs, ex_out_shape, ex_out_specs, ex_scratch, ex_counts = _ex_parts(ex)

    def body(*refs):
        (up_ref, uh_ref, da_ref, cw_ref, cb_ref), ex_in, (dup_ref, dcw_ref, dcb_ref), ex_out, (carry,), ex_sems = (
            _split_refs(refs, (5, ex_counts[0], 3, ex_counts[1], 1, ex_counts[2])))
        i = pl.program_id(0)
        if ex is not None:
            ex_start, ex_wait = ex.ops(ex_in, ex_out, ex_sems)
            pl.when(i == 0)(ex_start)

        @pl.when(i == 0)
        def _():
            dcw_ref[...] = jnp.zeros_like(dcw_ref)
            dcb_ref[...] = jnp.zeros_like(dcb_ref)
            carry[...] = jnp.zeros_like(carry)

        seq_start = i == nt - 1
        for s in range(N_SLABS):
            gcols = slice(s * FFN_COLS, (s + 1) * FFN_COLS)
            vcols = slice(D_FF + s * FFN_COLS, D_FF + (s + 1) * FFN_COLS)
            wg, wv, bg, bv = cw_ref[:, gcols], cw_ref[:, vcols], cb_ref[:, gcols], cb_ref[:, vcols]

            def blk(j, c):
                nxt_g, nxt_v, sums = c
                b = nb - 1 - j
                r0 = pl.multiple_of(b * FFN_ROWS, FFN_ROWS)
                xg, eg = _conv3_block(up_ref, uh_ref, b, r0, gcols, seq_start)
                xv, ev = _conv3_block(up_ref, uh_ref, b, r0, vcols, seq_start)
                gate, val = _conv3(xg, eg, wg, bg), _conv3(xv, ev, wv, bv)
                gelu, dgelu = _gelu(gate), _gelu_grad(gate)
                da = da_ref[pl.ds(r0, FFN_ROWS), gcols]
                heads, new = [], []
                for dp, nxt, x, w, cols in ((da * val * dgelu, nxt_g, xg, wg, gcols), (da * gelu, nxt_v, xv, wv, vcols)):
                    de = jnp.concatenate([dp, nxt], axis=0)
                    shifted = [de[FFN_CONV - 1 - k:FFN_CONV - 1 - k + FFN_ROWS] for k in range(FFN_CONV - 1)] + [dp]
                    dup = w[0:1] * shifted[0]
                    for k in range(1, FFN_CONV):
                        dup = dup + w[k:k + 1] * shifted[k]
                    dup_ref[pl.ds(r0, FFN_ROWS), cols] = dup.astype(BF16)
                    heads.append(dp[0:SUBLANES])
                    new += [_part8(dp)] + [_part8(sh * x) for sh in shifted]
                return heads[0], heads[1], tuple(a + v for a, v in zip(sums, new))

            zero = jnp.zeros((SUBLANES, FFN_COLS), F32)
            nxt_g, nxt_v, sums = lax.fori_loop(
                0, nb, blk, (carry[:, gcols], carry[:, vcols], (zero,) * (2 * (1 + FFN_CONV))))
            carry[:, gcols] = nxt_g
            carry[:, vcols] = nxt_v
            for half, cols in enumerate((gcols, vcols)):
                part = sums[half * (1 + FFN_CONV):(half + 1) * (1 + FFN_CONV)]
                dcb_ref[:, cols] += _colsum(part[0])
                dcw_ref[:, cols] += jnp.concatenate([_colsum(v) for v in part[1:]], axis=0)
        if ex is not None:
            pl.when(i == nt - 1)(ex_wait)

    rev = lambda i: nt - 1 - i
    full = lambda shape: pl.BlockSpec(shape, lambda i: tuple(0 for _ in shape))
    outs = pl.pallas_call(
        body, name=name, grid=(nt,),
        in_specs=[pl.BlockSpec((tt, D_UP), lambda i: (rev(i), 0)),
                  pl.BlockSpec((SUBLANES, D_UP), lambda i: (jnp.maximum(rev(i) * hb - 1, 0), 0)),
                  pl.BlockSpec((tt, D_FF), lambda i: (rev(i), 0)), full((FFN_CONV, D_UP)), full((1, D_UP))]
        + ex_in_specs,
        out_specs=[pl.BlockSpec((tt, D_UP), lambda i: (rev(i), 0)), full((FFN_CONV, D_UP)), full((1, D_UP))]
        + ex_out_specs,
        out_shape=[jax.ShapeDtypeStruct((t, D_UP), BF16), jax.ShapeDtypeStruct((FFN_CONV, D_UP), F32),
                   jax.ShapeDtypeStruct((1, D_UP), F32)] + ex_out_shape,
        scratch_shapes=[pltpu.VMEM((SUBLANES, D_UP), F32)] + ex_scratch,
        input_output_aliases={} if ex is None else ex.aliases(5, 3),
        compiler_params=_params("arbitrary"))(up, up, dact, cw, cb, *ex_args)
    return outs[0], outs[1], outs[2], outs[3:]


def _ple_fwd(h, p, g, w_gate, w_proj, *, tt, name):
    t = h.shape[0]
    tt = _tile(t, tt)

    def body(h_ref, p_ref, g_ref, wg_ref, wp_ref, o_ref):
        x = h_ref[...]
        n = (x * _rms_r(x) * g_ref[...]).astype(BF16)
        gate = _sigmoid(_dot(n, wg_ref[...]))
        o_ref[...] = x + _dot(p_ref[...].astype(BF16), wp_ref[...]) * gate

    full = lambda shape: pl.BlockSpec(shape, lambda i: tuple(0 for _ in shape))
    return pl.pallas_call(
        body, name=name, grid=(t // tt,),
        in_specs=[pl.BlockSpec((tt, D_MODEL), lambda i: (i, 0)), pl.BlockSpec((tt, D_PLE), lambda i: (i, 0)),
                  full((1, D_MODEL)), full((D_MODEL, D_MODEL)), full((D_PLE, D_MODEL))],
        out_specs=pl.BlockSpec((tt, D_MODEL), lambda i: (i, 0)),
        out_shape=jax.ShapeDtypeStruct((t, D_MODEL), F32),
        compiler_params=_params("arbitrary"))(h, p, g, w_gate, w_proj)


def _ple_bwd(dh, h, p, g, w_gate, w_gate_t, w_proj, *, tt, name):
    t = h.shape[0]
    tt = _tile(t, tt)

    def body(dh_ref, h_ref, p_ref, g_ref, wg_ref, wgt_ref, wp_ref, o_ref, dwg_ref, dwp_ref, dg_ref):
        @pl.when(pl.program_id(0) == 0)
        def _():
            dwg_ref[...] = jnp.zeros_like(dwg_ref)
            dwp_ref[...] = jnp.zeros_like(dwp_ref)
            dg_ref[...] = jnp.zeros_like(dg_ref)

        x = h_ref[...]
        r = _rms_r(x)
        gv = g_ref[...]
        n = (x * r * gv).astype(BF16)
        gate = _sigmoid(_dot(n, wg_ref[...]))
        pb = p_ref[...].astype(BF16)
        pe = _dot(pb, wp_ref[...])
        dhv = dh_ref[...]
        dwp_ref[...] += _dot_tn(pb, (dhv * gate).astype(BF16))
        ds = (dhv * pe * gate * (1.0 - gate)).astype(BF16)
        dwg_ref[...] += _dot_tn(n, ds)
        dx, dg = _rms_bwd(x, r, gv, _dot(ds, wgt_ref[...]))
        o_ref[...] = dhv + dx
        dg_ref[...] += dg

    full = lambda shape: pl.BlockSpec(shape, lambda i: tuple(0 for _ in shape))
    row = lambda n: pl.BlockSpec((tt, n), lambda i: (i, 0))
    return pl.pallas_call(
        body, name=name, grid=(t // tt,),
        in_specs=[row(D_MODEL), row(D_MODEL), row(D_PLE), full((1, D_MODEL)), full((D_MODEL, D_MODEL)),
                  full((D_MODEL, D_MODEL)), full((D_PLE, D_MODEL))],
        out_specs=[row(D_MODEL), full((D_MODEL, D_MODEL)), full((D_PLE, D_MODEL)), full((1, D_MODEL))],
        out_shape=[jax.ShapeDtypeStruct((t, D_MODEL), F32), jax.ShapeDtypeStruct((D_MODEL, D_MODEL), F32),
                   jax.ShapeDtypeStruct((D_PLE, D_MODEL), F32), jax.ShapeDtypeStruct((1, D_MODEL), F32)],
        compiler_params=_params("arbitrary"))(dh, h, p, g, w_gate, w_gate_t, w_proj)


def _loss_head(h, g, target, *, tt, name):
    t = h.shape[0]
    tt = _tile(t, tt)

    def body(h_ref, g_ref, tg_ref, dh_ref, loss_ref, dg_ref):
        @pl.when(pl.program_id(0) == 0)
        def _():
            loss_ref[...] = jnp.zeros_like(loss_ref)
            dg_ref[...] = jnp.zeros_like(dg_ref)

        x = h_ref[...]
        r = _rms_r(x)
        gv = g_ref[...]
        diff = x * r * gv - tg_ref[...]
        loss_ref[...] += 0.5 * jnp.sum(jnp.mean(diff * diff, axis=-1, keepdims=True), axis=0, keepdims=True)
        dx, dg = _rms_bwd(x, r, gv, diff * (1.0 / D_MODEL))
        dh_ref[...] = dx
        dg_ref[...] += dg

    return pl.pallas_call(
        body, name=name, grid=(t // tt,),
        in_specs=[pl.BlockSpec((tt, D_MODEL), lambda i: (i, 0)), pl.BlockSpec((1, D_MODEL), lambda i: (0, 0)),
                  pl.BlockSpec((tt, D_MODEL), lambda i: (i, 0))],
        out_specs=[pl.BlockSpec((tt, D_MODEL), lambda i: (i, 0)), pl.BlockSpec((SUBLANES, LANES), lambda i: (0, 0)),
                   pl.BlockSpec((1, D_MODEL), lambda i: (0, 0))],
        out_shape=[jax.ShapeDtypeStruct((t, D_MODEL), F32), jax.ShapeDtypeStruct((SUBLANES, LANES), F32),
                   jax.ShapeDtypeStruct((1, D_MODEL), F32)],
        compiler_params=_params("arbitrary"))(h, g, target)


ADAM_BLOCK_BYTES = 4 * 1024 * 1024


def _adam_rows(rows, cols):
    lanes = -(-cols // LANES) * LANES
    for cand in (1024, 512, 256, 128, 64, 32, 16, 8):
        if rows % cand == 0 and N_DEV * cand * lanes * 4 <= ADAM_BLOCK_BYTES:
            return cand
    return rows


def _sum_adamw(parts, w, m, v, *, name):
    nl, rows, cols = w.shape
    tr = _adam_rows(rows, cols)

    def body(p_ref, w_ref, m_ref, v_ref, g_ref, d_ref, nm_ref, nv_ref):
        g = p_ref[0]
        for k in range(1, N_DEV):
            g = g + p_ref[k]
        g_ref[...] = g
        nm = ADAM_B1 * m_ref[...] + (1.0 - ADAM_B1) * g
        nv = ADAM_B2 * v_ref[...] + (1.0 - ADAM_B2) * (g * g)
        m_hat = nm / (1.0 - ADAM_B1 ** ADAM_STEP)
        v_hat = nv / (1.0 - ADAM_B2 ** ADAM_STEP)
        d_ref[...] = -ADAM_LR * (m_hat / (jnp.sqrt(v_hat) + ADAM_EPS) + ADAM_WD * w_ref[...])
        nm_ref[...] = nm
        nv_ref[...] = nv

    blk = pl.BlockSpec((None, tr, cols), lambda l, r: (l, r, 0))
    return pl.pallas_call(
        body, name=name, grid=(nl, rows // tr),
        in_specs=[pl.BlockSpec((None, N_DEV, tr, cols), lambda l, r: (l, 0, r, 0)), blk, blk, blk],
        out_specs=[blk, blk, blk, blk],
        out_shape=[jax.ShapeDtypeStruct((nl, rows, cols), F32)] * 4,
        compiler_params=_params("arbitrary", "arbitrary"))(parts, w, m, v)


PACK_ROWS = 512


def _pack(arrays):
    flat = jnp.concatenate([a.astype(F32).reshape(-1) for a in arrays])
    pad = (-flat.shape[0]) % (PACK_ROWS * LANES)
    return jnp.pad(flat, (0, pad)).reshape(-1, LANES)


def _unpack(buf, shapes):
    flat = buf.reshape(-1)
    out, off = [], 0
    for s in shapes:
        n = math.prod(s)
        out.append(flat[off:off + n].reshape(s))
        off += n
    return out


def _to_proj_cols(w):
    z, xbc, dtc, u = jnp.split(w, [SSD_WIDTH, SSD_WIDTH + SSD_XBC, SSD_WIDTH + SSD_XBC + SSD_HEADS], axis=-1)
    pad = jnp.zeros(w.shape[:-1] + (LANES - SSD_HEADS,), w.dtype)
    return jnp.concatenate([xbc, z, u, dtc, pad], axis=-1)


def _from_proj_cols(w):
    xbc, z, u, dtc = (w[..., COL_XBC:COL_Z], w[..., COL_Z:COL_U], w[..., COL_U:COL_DT],
                      w[..., COL_DT:COL_DT + SSD_HEADS])
    return jnp.concatenate([z, xbc, dtc, u], axis=-1)


def _pad_heads(v):
    return jnp.pad(v, (0, LANES - SSD_HEADS)).reshape(1, LANES)


def _cat_cols(g):
    return jnp.transpose(g, (1, 0, 2)).reshape(g.shape[1], N_DEV * g.shape[2])


def _split_cols(w):
    r, c = w.shape
    return jnp.transpose(w.reshape(r, N_DEV, c // N_DEV), (1, 0, 2))


def _cat_rows(g):
    return g.reshape(N_DEV * g.shape[1], g.shape[2])


def _split_rows(w):
    return w.reshape(N_DEV, w.shape[0] // N_DEV, w.shape[1])


SHARDED = ("w_in", "w_out", "ffn_w_up", "ffn_w_down", "ple_w_gate", "ple_w_proj", "ssd_conv_w", "ffn_conv_w")
COL_SHARDED = ("w_in", "ffn_w_up", "ple_w_proj", "ssd_conv_w", "ffn_conv_w")
MATMUL_W = SHARDED[:6]
REPLICATED = ("mix_norm_g", "ssd_conv_b", "ssd_dt_bias", "ssd_a_log", "ssd_d", "ssd_norm_g", "pool_w", "pool_scale",
              "ffn_norm_g", "ffn_conv_b", "ple_norm_g", "final_norm_g")
WEIGHTS = ("mix_norm_g", "w_in", "ssd_conv_w", "ssd_conv_b", "ssd_dt_bias", "ssd_a_log", "ssd_d", "ssd_norm_g",
           "pool_w", "pool_scale", "w_out", "ffn_norm_g", "ffn_w_up", "ffn_conv_w", "ffn_conv_b", "ffn_w_down",
           "ple_norm_g", "ple_w_gate", "ple_w_proj", "final_norm_g")


def _assemble(gathered):
    full = {k: (_cat_cols(g) if k in COL_SHARDED else _cat_rows(g)) for k, g in zip(SHARDED, gathered)}
    full["w_in"] = _to_proj_cols(full["w_in"])
    for k in ("w_in", "w_out", "ffn_w_up", "ffn_w_down", "ple_w_gate"):
        full[k + "_t"] = full[k].T
    return full


def _grad_shards(grads):
    grads = dict(grads, w_in=_from_proj_cols(grads["w_in"]))
    return [(_split_cols(grads[k]) if k in COL_SHARDED else _split_rows(grads[k])) for k in SHARDED]


def _layer_fwd(i, h1, p_i, lw, rep, consts, ex):
    tril, e_mat = consts
    row = lambda v: v.reshape(1, -1)
    dtb, alog = _pad_heads(rep["ssd_dt_bias"]), _pad_heads(rep["ssd_a_log"])
    dexp = row(jnp.repeat(rep["ssd_d"], SSD_HEAD_DIM))
    pw = rep["pool_w"].astype(BF16)
    proj = _norm_matmul(h1, lw["w_in"], row(rep["mix_norm_g"]), tt=512, tn=N_PROJ, name=f"in_proj_{i}")
    yssd, ypre, states = _ssd_fwd(proj, lw["ssd_conv_w"], row(rep["ssd_conv_b"]), dtb, alog, dexp,
                                  row(rep["ssd_norm_g"]), tril, e_mat, ts=512, name=f"ssd_fwd_{i}")
    h2, ymix, n2 = _mix_out(h1, yssd, proj, pw, row(rep["pool_scale"]), lw["w_out"], row(rep["ffn_norm_g"]), tt=512,
                            name=f"mix_out_{i}")
    up = _norm_matmul(n2, lw["ffn_w_up"], tt=512, tn=D_FF, name=f"ffn_up_{i}")
    h3, act, gathered = _ffn_down(h2, up, lw["ffn_conv_w"], row(rep["ffn_conv_b"]), lw["ffn_w_down"], tt=256,
                                  name=f"ffn_down_{i}", ex=ex)
    h4 = _ple_fwd(h3, p_i, row(rep["ple_norm_g"]), lw["ple_w_gate"], lw["ple_w_proj"], tt=512, name=f"ple_fwd_{i}")
    saved = dict(h1=h1, proj=proj, ypre=ypre, states=states, ymix=ymix, h2=h2, n2=n2, up=up, act=act, h3=h3,
                 dtb=dtb, alog=alog, dexp=dexp, pw=pw)
    return h4, saved, gathered


def _layer_bwd(i, dh, p_i, lw, rep, s, consts, ex):
    tril, triu, e_mat = consts
    row = lambda v: v.reshape(1, -1)
    g = {}
    dh, g["ple_w_gate"], g["ple_w_proj"], dg3 = _ple_bwd(dh, s["h3"], p_i, row(rep["ple_norm_g"]), lw["ple_w_gate"],
                                                         lw["ple_w_gate_t"], lw["ple_w_proj"], tt=512,
                                                         name=f"ple_bwd_{i}")
    g["ple_norm_g"] = dg3.reshape(-1)
    g["ffn_w_down"] = _matmul_tn(s["act"], dh, tm=D_FF // 2, tn=D_MODEL, tk=1024, name=f"dw_down_{i}")
    dact = _norm_matmul(dh, lw["ffn_w_down_t"], tt=512, tn=D_FF, name=f"d_act_{i}")
    dup, g["ffn_conv_w"], dcb, scattered = _ffn_act_bwd(s["up"], dact, lw["ffn_conv_w"], row(rep["ffn_conv_b"]),
                                                        tt=256, name=f"ffn_act_bwd_{i}", ex=ex)
    g["ffn_conv_b"] = dcb.reshape(-1)
    g["ffn_w_up"] = _matmul_tn(s["n2"], dup, tm=D_MODEL, tn=D_UP // 4, tk=1024, name=f"dw_up_{i}")
    dh, dg2 = _matmul_rmsbwd(dup, lw["ffn_w_up_t"], s["h2"], row(rep["ffn_norm_g"]), dh, tt=512, tk=D_UP // 4,
                             name=f"ffn_up_bwd_{i}")
    g["ffn_norm_g"] = dg2.reshape(-1)
    dymix, g["w_out"] = _out_bwd(dh, s["ymix"], lw["w_out_t"], tt=512, name=f"out_bwd_{i}")
    du, g["pool_w"], dsc = _pool_bwd(dymix, s["proj"], s["pw"], jnp.swapaxes(s["pw"], 1, 2), row(rep["pool_scale"]),
                                     tt=512, name=f"pool_bwd_{i}")
    g["pool_scale"] = dsc.reshape(-1)
    dproj, g["ssd_conv_w"], dcb, ddtb, dalog, dd, dng = _ssd_bwd(
        dymix, s["proj"], s["ypre"], s["states"], du, lw["ssd_conv_w"], row(rep["ssd_conv_b"]), s["dtb"], s["alog"],
        s["dexp"], row(rep["ssd_norm_g"]), tril, triu, e_mat, ts=512, name=f"ssd_bwd_{i}")
    g["ssd_conv_b"], g["ssd_norm_g"] = dcb.reshape(-1), dng.reshape(-1)
    g["ssd_dt_bias"], g["ssd_a_log"], g["ssd_d"] = ddtb[0, :SSD_HEADS], dalog[0, :SSD_HEADS], dd[0, :SSD_HEADS]
    dh, g["w_in"], dg1 = _in_bwd(dproj, s["h1"], row(rep["mix_norm_g"]), lw["w_in_t"], dh, tt=256, name=f"in_bwd_{i}")
    g["mix_norm_g"] = dg1.reshape(-1)
    return dh, g, scattered


def kernel(x, p, mix_norm_g, w_in, ssd_conv_w, ssd_conv_b, ssd_dt_bias, ssd_a_log, ssd_d, ssd_norm_g, pool_w, pool_scale, w_out, ffn_norm_g, ffn_w_up, ffn_conv_w, ffn_conv_b, ffn_w_down, ple_norm_g, ple_w_gate, ple_w_proj, final_norm_g, loss_target, m_mix_norm_g, m_w_in, m_ssd_conv_w, m_ssd_conv_b, m_ssd_dt_bias, m_ssd_a_log, m_ssd_d, m_ssd_norm_g, m_pool_w, m_pool_scale, m_w_out, m_ffn_norm_g, m_ffn_w_up, m_ffn_conv_w, m_ffn_conv_b, m_ffn_w_down, m_ple_norm_g, m_ple_w_gate, m_ple_w_proj, m_final_norm_g, v_mix_norm_g, v_w_in, v_ssd_conv_w, v_ssd_conv_b, v_ssd_dt_bias, v_ssd_a_log, v_ssd_d, v_ssd_norm_g, v_pool_w, v_pool_scale, v_w_out, v_ffn_norm_g, v_ffn_w_up, v_ffn_conv_w, v_ffn_conv_b, v_ffn_w_down, v_ple_norm_g, v_ple_w_gate, v_ple_w_proj, v_final_norm_g):
    w = dict(mix_norm_g=mix_norm_g, w_in=w_in, ssd_conv_w=ssd_conv_w, ssd_conv_b=ssd_conv_b, ssd_dt_bias=ssd_dt_bias,
             ssd_a_log=ssd_a_log, ssd_d=ssd_d, ssd_norm_g=ssd_norm_g, pool_w=pool_w, pool_scale=pool_scale, w_out=w_out,
             ffn_norm_g=ffn_norm_g, ffn_w_up=ffn_w_up, ffn_conv_w=ffn_conv_w, ffn_conv_b=ffn_conv_b,
             ffn_w_down=ffn_w_down, ple_norm_g=ple_norm_g, ple_w_gate=ple_w_gate, ple_w_proj=ple_w_proj,
             final_norm_g=final_norm_g)
    m = dict(mix_norm_g=m_mix_norm_g, w_in=m_w_in, ssd_conv_w=m_ssd_conv_w, ssd_conv_b=m_ssd_conv_b,
             ssd_dt_bias=m_ssd_dt_bias, ssd_a_log=m_ssd_a_log, ssd_d=m_ssd_d, ssd_norm_g=m_ssd_norm_g, pool_w=m_pool_w,
             pool_scale=m_pool_scale, w_out=m_w_out, ffn_norm_g=m_ffn_norm_g, ffn_w_up=m_ffn_w_up,
             ffn_conv_w=m_ffn_conv_w, ffn_conv_b=m_ffn_conv_b, ffn_w_down=m_ffn_w_down, ple_norm_g=m_ple_norm_g,
             ple_w_gate=m_ple_w_gate, ple_w_proj=m_ple_w_proj, final_norm_g=m_final_norm_g)
    v = dict(mix_norm_g=v_mix_norm_g, w_in=v_w_in, ssd_conv_w=v_ssd_conv_w, ssd_conv_b=v_ssd_conv_b,
             ssd_dt_bias=v_ssd_dt_bias, ssd_a_log=v_ssd_a_log, ssd_d=v_ssd_d, ssd_norm_g=v_ssd_norm_g, pool_w=v_pool_w,
             pool_scale=v_pool_scale, w_out=v_w_out, ffn_norm_g=v_ffn_norm_g, ffn_w_up=v_ffn_w_up,
             ffn_conv_w=v_ffn_conv_w, ffn_conv_b=v_ffn_conv_b, ffn_w_down=v_ffn_w_down, ple_norm_g=v_ple_norm_g,
             ple_w_gate=v_ple_w_gate, ple_w_proj=v_ple_w_proj, final_norm_g=v_final_norm_g)

    tril = jnp.tril(jnp.ones((CHUNK, CHUNK), F32))
    triu = tril.T
    e_mat = (jnp.arange(SSD_WIDTH)[None, :] // SSD_HEAD_DIM == jnp.arange(LANES)[:, None]).astype(F32)
    rep = [{k: w[k][i] for k in REPLICATED if k != "final_norm_g"} for i in range(DEPTH)]
    p_loc = p[:, 0]

    shards = [w[k].astype(BF16) if k in MATMUL_W else w[k] for k in SHARDED]
    gathered = _exchange_call(_Exchange(shards, scatter=False, layer=0), "gather_weights_0")
    h, saved, layer_w = x[0], [], []
    for i in range(DEPTH):
        lw = _assemble(gathered)
        ex = _Exchange(shards, scatter=False, layer=i + 1) if i + 1 < DEPTH else None
        h, s, gathered = _layer_fwd(i, h, p_loc[i], lw, rep[i], (tril, e_mat), ex)
        saved.append(s)
        layer_w.append(lw)

    dh, loss_blk, dgf = _loss_head(h, final_norm_g.reshape(1, -1), loss_target[0], tt=512, name="loss_head")
    loss = lax.psum(loss_blk[0, 0], ("x", "y", "c"))

    rep_grads = [None] * DEPTH
    pending, parts = None, None
    for i in reversed(range(DEPTH)):
        ex = None if pending is None else _Exchange(pending, scatter=True, layer=i + 1, into=parts)
        dh, g, scattered = _layer_bwd(i, dh, p_loc[i], layer_w[i], rep[i], saved[i], (tril, triu, e_mat), ex)
        if ex is not None:
            parts = scattered
        pending = _grad_shards(g)
        rep_grads[i] = g
    parts = _exchange_call(_Exchange(pending, scatter=True, layer=0, into=parts), "scatter_grads_0")

    out = {}
    for k, part in zip(SHARDED, parts):
        out[k] = _sum_adamw(part, w[k], m[k], v[k], name=f"adamw_{k}")

    rp_grads = [dgf.reshape(-1) if k == "final_norm_g" else jnp.stack([rep_grads[i][k] for i in range(DEPTH)])
                for k in REPLICATED]
    rp_shapes = [w[k].shape for k in REPLICATED]
    rp_parts = _exchange_call(_Exchange([_pack(rp_grads)[None]], scatter=False, layer=0), "gather_replicated_grads")[0]
    rp_out = _sum_adamw(rp_parts[None], *[_pack([d[k] for k in REPLICATED])[None] for d in (w, m, v)],
                        name="adamw_replicated")
    for j in range(4):
        for k, arr in zip(REPLICATED, _unpack(rp_out[j][0], rp_shapes)):
            out.setdefault(k, [None] * 4)[j] = arr
    results = [out[k][j] for j in range(4) for k in WEIGHTS]
    return (loss, dh[None], *results)
```

```python
import functools
import math

import jax
import jax.numpy as jnp
from jax import lax
from jax.experimental import pallas as pl
from jax.experimental.pallas import tpu as pltpu

F32 = jnp.float32
BF16 = jnp.bfloat16
HI = lax.Precision.HIGHEST

N_DEV = 8
EPS = 1e-6
DEPTH = 4
D_MODEL = 1024
D_PLE = 256
SSD_WIDTH = 512
SSD_HEADS = 8
SSD_HEAD_DIM = 64
SSD_GROUPS = 2
SSD_STATE = 128
SSD_CONV = 4
CHUNK = 128
SSD_XBC = 1024
POOL_WINDOWS = (2, 4, 8, 16)
POOL_WIDTH = 512
POOL_GROUP = 128
POOL_HALO = 16
D_IN_PROJ = 2056
D_FF = 2816
D_UP = 2 * D_FF
FFN_CONV = 3
SUBLANES = 8
LANES = 128
N_PROJ = 2176
COL_XBC, COL_Z, COL_U, COL_DT = 0, 1024, 1536, 2048
N_PAIRS = SSD_HEADS // 2
ADAM_LR, ADAM_B1, ADAM_B2, ADAM_EPS, ADAM_WD, ADAM_STEP = 0.001, 0.9, 0.999, 1e-08, 0.01, 10
GELU_C = math.sqrt(2.0 / math.pi)
GELU_A = 0.044715
VMEM_LIMIT = 56 * 1024 * 1024

NT_DIMS = (((1,), (1,)), ((), ()))
TN_DIMS = (((0,), (0,)), ((), ()))


def _params(*sem):
    return pltpu.CompilerParams(dimension_semantics=sem, vmem_limit_bytes=VMEM_LIMIT)


def _dot(a, b):
    return jnp.dot(a, b, preferred_element_type=F32)


def _dot_nt(a, b):
    return lax.dot_general(a, b, NT_DIMS, preferred_element_type=F32)


def _dot_tn(a, b):
    return lax.dot_general(a, b, TN_DIMS, preferred_element_type=F32)


def _hdot(a, b):
    return jnp.dot(a, b, preferred_element_type=F32, precision=HI)


def _headsum(q, e):
    return lax.dot_general(q, e, NT_DIMS, preferred_element_type=F32, precision=HI)


def _colsum(v):
    return jnp.sum(v, axis=0, keepdims=True)


def _sigmoid(v):
    return 1.0 / (1.0 + jnp.exp(-v))


def _softplus(v):
    e = jnp.exp(-jnp.abs(v))
    return jnp.maximum(v, 0.0) + jnp.where(e < 1e-4, e * (1.0 - 0.5 * e), jnp.log(1.0 + e))


def _rms_r(x):
    return lax.rsqrt(jnp.mean(x * x, axis=-1, keepdims=True) + EPS)


def _rms_bwd(x, r, g, dn):
    xhat = x * r
    gd = dn * g
    dx = r * (gd - xhat * jnp.mean(gd * xhat, axis=-1, keepdims=True))
    return dx, _colsum(dn * xhat)


def _gelu(v):
    return 0.5 * v * (1.0 + jnp.tanh(GELU_C * (v + GELU_A * v * v * v)))


def _gelu_grad(v):
    th = jnp.tanh(GELU_C * (v + GELU_A * v * v * v))
    return 0.5 * (1.0 + th) + 0.5 * v * (1.0 - th * th) * GELU_C * (1.0 + 3.0 * GELU_A * v * v)


def _tile(t, want):
    return min(t, want)


class _Exchange:
    def __init__(self, srcs, *, scatter, layer, into=None):
        self.srcs, self.scatter, self.layer = list(srcs), scatter, layer
        self.into = None if into is None else list(into)
        n = len(self.srcs)
        self.args = self.srcs + (self.into or [])
        self.in_specs = [pl.BlockSpec(memory_space=pl.ANY)] * len(self.args)
        if scatter:
            self.out_shape = [jax.ShapeDtypeStruct((DEPTH,) + s.shape, s.dtype) for s in self.srcs]
        else:
            self.out_shape = [jax.ShapeDtypeStruct((N_DEV,) + s.shape[1:], s.dtype) for s in self.srcs]
        self.out_specs = [pl.BlockSpec(memory_space=pl.ANY)] * n
        self.scratch = [pltpu.SemaphoreType.DMA((n, N_DEV - 1)), pltpu.SemaphoreType.DMA((n, N_DEV - 1)),
                        pltpu.SemaphoreType.DMA((n,))]

    def aliases(self, n_in_before, n_out_before):
        if self.into is None:
            return {}
        n = len(self.srcs)
        return {n_in_before + n + a: n_out_before + a for a in range(n)}

    def ops(self, in_refs, out_refs, sems):
        send_sems, recv_sems, local_sems = sems
        n = len(self.srcs)

        def copies():
            x, y, c = lax.axis_index("x"), lax.axis_index("y"), lax.axis_index("c")
            me = 4 * x + 2 * y + c

            def block(a, idx):
                return in_refs[a].at[idx] if self.scatter else in_refs[a].at[self.layer]

            def slot(a, idx):
                return out_refs[a].at[self.layer].at[idx] if self.scatter else out_refs[a].at[idx]

            local = [pltpu.make_async_copy(block(a, me), slot(a, me), local_sems.at[a]) for a in range(n)]
            sends, recvs = [], []
            for k in range(1, N_DEV):
                px = 1 - x if k & 4 else x
                py = 1 - y if k & 2 else y
                pc = 1 - c if k & 1 else c
                peer = 4 * px + 2 * py + pc
                for a in range(n):
                    kw = dict(send_sem=send_sems.at[a, k - 1], recv_sem=recv_sems.at[a, k - 1], device_id=(px, py, pc),
                              device_id_type=pl.DeviceIdType.MESH)
                    sends.append(pltpu.make_async_remote_copy(src_ref=block(a, peer), dst_ref=slot(a, me), **kw))
                    recvs.append(pltpu.make_async_remote_copy(src_ref=block(a, peer), dst_ref=slot(a, peer), **kw))
            return local, sends, recvs

        def start():
            local, sends, _ = copies()
            for cp in local + sends:
                cp.start()

        def wait():
            local, sends, recvs = copies()
            for send, recv in zip(sends, recvs):
                send.wait_send()
                recv.wait_recv()
            for cp in local:
                cp.wait()

        return start, wait


def _exchange_call(ex, name):
    n_in, n = len(ex.args), len(ex.srcs)

    def body(*refs):
        start, wait = ex.ops(refs[:n_in], refs[n_in:n_in + n], refs[n_in + n:])
        start()
        wait()

    return pl.pallas_call(
        body, name=name, in_specs=ex.in_specs, out_specs=ex.out_specs, out_shape=ex.out_shape,
        scratch_shapes=ex.scratch, input_output_aliases=ex.aliases(0, 0))(*ex.args)


def _split_refs(refs, counts):
    out, k = [], 0
    for cnt in counts:
        out.append(refs[k:k + cnt])
        k += cnt
    return out


def _ex_parts(ex):
    if ex is None:
        return [], [], [], [], [], (0, 0, 0)
    return ex.args, ex.in_specs, ex.out_shape, ex.out_specs, ex.scratch, (len(ex.args), len(ex.srcs), 3)


def _norm_matmul(h, w, g=None, *, tt, tn, name):
    t, k = h.shape
    n = w.shape[1]
    tt, tn = _tile(t, tt), _tile(n, tn)
    normed = g is not None

    def body(*refs):
        if normed:
            h_ref, g_ref, w_ref, o_ref = refs
            x = h_ref[...]
            xn = (x * _rms_r(x) * g_ref[...]).astype(BF16)
        else:
            h_ref, w_ref, o_ref = refs
            xn = h_ref[...].astype(BF16)
        o_ref[...] = _dot(xn, w_ref[...])

    in_specs = [pl.BlockSpec((tt, k), lambda j, i: (i, 0))]
    args = [h]
    if normed:
        in_specs.append(pl.BlockSpec((1, k), lambda j, i: (0, 0)))
        args.append(g)
    in_specs.append(pl.BlockSpec((k, tn), lambda j, i: (0, j)))
    args.append(w)
    return pl.pallas_call(
        body, name=name, grid=(n // tn, t // tt), in_specs=in_specs,
        out_specs=pl.BlockSpec((tt, tn), lambda j, i: (i, j)), out_shape=jax.ShapeDtypeStruct((t, n), F32),
        compiler_params=_params("arbitrary", "arbitrary"))(*args)


def _matmul_tn(a, b, *, tm, tn, tk, name):
    t, m = a.shape
    n = b.shape[1]
    tm, tn, tk = _tile(m, tm), _tile(n, tn), _tile(t, tk)

    def body(a_ref, b_ref, o_ref):
        @pl.when(pl.program_id(2) == 0)
        def _():
            o_ref[...] = jnp.zeros_like(o_ref)

        o_ref[...] += _dot_tn(a_ref[...].astype(BF16), b_ref[...].astype(BF16))

    return pl.pallas_call(
        body, name=name, grid=(m // tm, n // tn, t // tk),
        in_specs=[pl.BlockSpec((tk, tm), lambda i, j, kk: (kk, i)), pl.BlockSpec((tk, tn), lambda i, j, kk: (kk, j))],
        out_specs=pl.BlockSpec((tm, tn), lambda i, j, kk: (i, j)),
        out_shape=jax.ShapeDtypeStruct((m, n), F32),
        compiler_params=_params("arbitrary", "arbitrary", "arbitrary"))(a, b)


def _matmul_rmsbwd(a, wt, x, g, dh, *, tt, tk, name):
    t, k = a.shape
    d = wt.shape[1]
    tt, tk = _tile(t, tt), _tile(k, tk)
    nk = k // tk

    def body(a_ref, w_ref, x_ref, g_ref, dh_ref, o_ref, dg_ref, *scratch):
        i, kk = pl.program_id(0), pl.program_id(1)

        @pl.when((i == 0) & (kk == 0))
        def _():
            dg_ref[...] = jnp.zeros_like(dg_ref)

        def finish(dn):
            xv = x_ref[...]
            dx, dg = _rms_bwd(xv, _rms_r(xv), g_ref[...], dn)
            o_ref[...] = dh_ref[...] + dx
            dg_ref[...] += dg

        if nk == 1:
            finish(_dot(a_ref[...], w_ref[...]))
        else:
            acc, = scratch

            @pl.when(kk == 0)
            def _():
                acc[...] = jnp.zeros_like(acc)

            acc[...] += _dot(a_ref[...], w_ref[...])
            pl.when(kk == nk - 1)(lambda: finish(acc[...]))

    return pl.pallas_call(
        body, name=name, grid=(t // tt, nk),
        in_specs=[pl.BlockSpec((tt, tk), lambda i, kk: (i, kk)), pl.BlockSpec((tk, d), lambda i, kk: (kk, 0)),
                  pl.BlockSpec((tt, d), lambda i, kk: (i, 0)), pl.BlockSpec((1, d), lambda i, kk: (0, 0)),
                  pl.BlockSpec((tt, d), lambda i, kk: (i, 0))],
        out_specs=[pl.BlockSpec((tt, d), lambda i, kk: (i, 0)), pl.BlockSpec((1, d), lambda i, kk: (0, 0))],
        out_shape=[jax.ShapeDtypeStruct((t, d), F32), jax.ShapeDtypeStruct((1, d), F32)],
        scratch_shapes=[] if nk == 1 else [pltpu.VMEM((tt, d), F32)],
        compiler_params=_params("arbitrary", "arbitrary"))(a, wt, x, g, dh)


def _ssd_tile_prologue(i_is_first, xbc_ref, halo_ref, dt_ref, cw_ref, cb_ref, dtb_ref, alog_ref, e_ref, buf, xc_scr,
                       xa_scr, a_scr, dte_scr, x_scr, ts):
    buf[0:SUBLANES, :] = jnp.where(i_is_first, 0.0, halo_ref[...])
    buf[SUBLANES:SUBLANES + ts, :] = xbc_ref[...]
    cw = cw_ref[...]
    xc = cb_ref[...]
    for k in range(SSD_CONV):
        off = SUBLANES - (SSD_CONV - 1) + k
        xc = xc + cw[k:k + 1, :] * buf[off:off + ts, :]
    if xc_scr is not None:
        xc_scr[...] = xc
    xa_scr[...] = xc * _sigmoid(xc)
    dt = _softplus(dt_ref[...] + dtb_ref[...])
    a_neg = -jnp.exp(alog_ref[...])
    a_scr[...] = dt * a_neg
    dte = _hdot(dt, e_ref[...])
    dte_scr[...] = dte
    x_scr[...] = xa_scr[:, 0:SSD_WIDTH] * dte
    return dt, a_neg


def _chunk_decays(a_c, tril, e):
    cs = _hdot(tril, a_c)
    cs_t = cs.T
    cs_e = _hdot(cs, e)
    last_e = cs_e[CHUNK - 1:CHUNK, :]
    return cs, cs_t, cs_e, last_e


def _ssd_fwd(proj, cw, cb, dtb, alog, dexp, ng, tril, e, *, ts, name, ex=None):
    t = proj.shape[0]
    ts = _tile(t, ts)
    nch = ts // CHUNK
    hb = ts // SUBLANES
    nt = t // ts
    ex_args, ex_in_specs, ex_out_shape, ex_out_specs, ex_scratch, ex_counts = _ex_parts(ex)

    def body(*refs):
        ((xbc_ref, halo_ref, z_ref, dt_ref, cw_ref, cb_ref, dtb_ref, alog_ref, dexp_ref, ng_ref, tril_ref, e_ref),
         ex_in, (y_ref, ypre_ref, st_ref), ex_out, (buf, xa_scr, a_scr, dte_scr, x_scr, ys_scr, hstate),
         ex_sems) = _split_refs(refs, (12, ex_counts[0], 3, ex_counts[1], 7, ex_counts[2]))
        i = pl.program_id(0)
        if ex is not None:
            ex_start, ex_wait = ex.ops(ex_in, ex_out, ex_sems)
            pl.when(i == 0)(ex_start)

        @pl.when(i == 0)
        def _():
            hstate[...] = jnp.zeros_like(hstate)

        _ssd_tile_prologue(i == 0, xbc_ref, halo_ref, dt_ref, cw_ref, cb_ref, dtb_ref, alog_ref, e_ref, buf, None,
                           xa_scr, a_scr, dte_scr, x_scr, ts)
        tril = tril_ref[...]
        e_mat = e_ref[...]
        causal = tril > 0.5
        lane = lax.broadcasted_iota(jnp.int32, (CHUNK, LANES), 1)

        def chunk(c, carry):
            r0 = pl.multiple_of(c * CHUNK, CHUNK)
            rows = pl.ds(r0, CHUNK)
            cs, cs_t, cs_e, last_e = _chunk_decays(a_scr[rows, :], tril, e_mat)
            decay_e = jnp.exp(last_e - cs_e)
            ecs_e = jnp.exp(cs_e)
            xc = x_scr[rows, :]
            xb = xc.astype(BF16)
            xd = (xc * decay_e).astype(BF16)
            for g in range(SSD_GROUPS):
                bg = xa_scr[rows, SSD_WIDTH + g * SSD_STATE:SSD_WIDTH + (g + 1) * SSD_STATE].astype(BF16)
                cg = xa_scr[rows, SSD_WIDTH + (SSD_GROUPS + g) * SSD_STATE:
                            SSD_WIDTH + (SSD_GROUPS + g + 1) * SSD_STATE].astype(BF16)
                cbm = _dot_nt(cg, bg)
                for jj in range(2):
                    j = 2 * g + jj
                    cols = slice(j * LANES, (j + 1) * LANES)
                    xp = xb[:, cols]
                    ypair = jnp.zeros((CHUNK, LANES), F32)
                    for hh in range(2):
                        h = 2 * j + hh
                        seg = jnp.exp(jnp.where(causal, cs[:, h:h + 1] - cs_t[h:h + 1, :], -jnp.inf))
                        m = (cbm * seg).astype(BF16)
                        half = (lane < SSD_HEAD_DIM) if hh == 0 else (lane >= SSD_HEAD_DIM)
                        ypair = ypair + _dot(m, jnp.where(half, xp, jnp.zeros_like(xp)))
                    hp = hstate[j]
                    st_ref[c, j] = hp
                    ypair = ypair + _dot(cg, hp.astype(BF16)) * ecs_e[:, cols]
                    ys_scr[rows, cols] = ypair
                    hstate[j] = hp * jnp.exp(last_e[:, cols]) + _dot_tn(bg, xd[:, cols])
            return carry

        lax.fori_loop(0, nch, chunk, 0)
        ypre = ys_scr[...] + xa_scr[:, 0:SSD_WIDTH] * dexp_ref[...]
        ypre_ref[...] = ypre
        z = z_ref[...]
        yg = ypre * (z * _sigmoid(z))
        gw = SSD_WIDTH // SSD_GROUPS
        outs = []
        for g in range(SSD_GROUPS):
            v = yg[:, g * gw:(g + 1) * gw]
            outs.append(v * _rms_r(v))
        y_ref[...] = jnp.concatenate(outs, axis=1) * ng_ref[...]
        if ex is not None:
            pl.when(i == nt - 1)(ex_wait)

    full = lambda shape: pl.BlockSpec(shape, lambda i: tuple(0 for _ in shape))
    outs = pl.pallas_call(
        body, name=name, grid=(nt,),
        in_specs=[pl.BlockSpec((ts, SSD_XBC), lambda i: (i, COL_XBC // SSD_XBC)),
                  pl.BlockSpec((SUBLANES, SSD_XBC), lambda i: (jnp.maximum(i * hb - 1, 0), COL_XBC // SSD_XBC)),
                  pl.BlockSpec((ts, SSD_WIDTH), lambda i: (i, COL_Z // SSD_WIDTH)),
                  pl.BlockSpec((ts, LANES), lambda i: (i, COL_DT // LANES)),
                  full((SSD_CONV, SSD_XBC)), full((1, SSD_XBC)), full((1, LANES)), full((1, LANES)),
                  full((1, SSD_WIDTH)), full((1, SSD_WIDTH)), full((CHUNK, CHUNK)), full((LANES, SSD_WIDTH))]
        + ex_in_specs,
        out_specs=[pl.BlockSpec((ts, SSD_WIDTH), lambda i: (i, 0)), pl.BlockSpec((ts, SSD_WIDTH), lambda i: (i, 0)),
                   pl.BlockSpec((nch, N_PAIRS, SSD_STATE, LANES), lambda i: (i, 0, 0, 0))] + ex_out_specs,
        out_shape=[jax.ShapeDtypeStruct((t, SSD_WIDTH), F32), jax.ShapeDtypeStruct((t, SSD_WIDTH), F32),
                   jax.ShapeDtypeStruct((t // CHUNK, N_PAIRS, SSD_STATE, LANES), F32)] + ex_out_shape,
        scratch_shapes=[pltpu.VMEM((SUBLANES + ts, SSD_XBC), F32), pltpu.VMEM((ts, SSD_XBC), F32),
                        pltpu.VMEM((ts, LANES), F32), pltpu.VMEM((ts, SSD_WIDTH), F32),
                        pltpu.VMEM((ts, SSD_WIDTH), F32), pltpu.VMEM((ts, SSD_WIDTH), F32),
                        pltpu.VMEM((N_PAIRS, SSD_STATE, LANES), F32)] + ex_scratch,
        input_output_aliases={} if ex is None else ex.aliases(12, 3),
        compiler_params=_params("arbitrary"))(proj, proj, proj, proj, cw, cb, dtb, alog, dexp, ng, tril, e, *ex_args)
    return outs[0], outs[1], outs[2], outs[3:]


def _ssd_bwd(dymix, proj, ypre, states, du, cw, cb, dtb, alog, dexp, ng, tril, triu, e, *, ts, name, ex=None):
    t = proj.shape[0]
    ts = _tile(t, ts)
    nch = ts // CHUNK
    hb = ts // SUBLANES
    nt = t // ts
    ex_args, ex_in_specs, ex_out_shape, ex_out_specs, ex_scratch, ex_counts = _ex_parts(ex)

    def body(*refs):
        ((dy_ref, xbc_ref, halo_ref, z_ref, dt_ref, ypre_ref, st_ref, du_ref, cw_ref, cb_ref, dtb_ref, alog_ref,
          dexp_ref, ng_ref, tril_ref, triu_ref, e_ref), ex_in,
         (dproj_ref, dcw_ref, dcb_ref, ddtb_ref, dalog_ref, dd_ref, dng_ref), ex_out,
         (buf, xc_scr, xa_scr, a_scr, dte_scr, x_scr, dyp_scr, dxa_scr, dx_scr, dbuf, carry, gstate),
         ex_sems) = _split_refs(refs, (17, ex_counts[0], 7, ex_counts[1], 12, ex_counts[2]))
        i = pl.program_id(0)
        if ex is not None:
            ex_start, ex_wait = ex.ops(ex_in, ex_out, ex_sems)
            pl.when(i == 0)(ex_start)

        @pl.when(i == 0)
        def _():
            gstate[...] = jnp.zeros_like(gstate)
            carry[...] = jnp.zeros_like(carry)
            for ref in (dcw_ref, dcb_ref, ddtb_ref, dalog_ref, dd_ref, dng_ref):
                ref[...] = jnp.zeros_like(ref)

        dt, a_neg = _ssd_tile_prologue(i == nt - 1, xbc_ref, halo_ref, dt_ref, cw_ref, cb_ref, dtb_ref, alog_ref, e_ref,
                                       buf, xc_scr, xa_scr, a_scr, dte_scr, x_scr, ts)
        tril = tril_ref[...]
        triu = triu_ref[...]
        e_mat = e_ref[...]
        causal = tril > 0.5
        lane = lax.broadcasted_iota(jnp.int32, (CHUNK, LANES), 1)
        sub = lax.broadcasted_iota(jnp.int32, (CHUNK, LANES), 0)

        z = z_ref[...]
        sig = _sigmoid(z)
        zs = z * sig
        ypre = ypre_ref[...]
        yg = ypre * zs
        dout = dy_ref[...]
        ngv = ng_ref[...]
        gw = SSD_WIDTH // SSD_GROUPS
        dyg_parts, dng_parts = [], []
        for g in range(SSD_GROUPS):
            cols = slice(g * gw, (g + 1) * gw)
            v = yg[:, cols]
            dx, dg = _rms_bwd(v, _rms_r(v), ngv[:, cols], dout[:, cols])
            dyg_parts.append(dx)
            dng_parts.append(dg)
        dyg = jnp.concatenate(dyg_parts, axis=1)
        dng_ref[...] += jnp.concatenate(dng_parts, axis=1)
        dyp = dyg * zs
        dyp_scr[...] = dyp
        dproj_ref[:, COL_Z:COL_Z + SSD_WIDTH] = dyg * ypre * (sig * (1.0 + z * (1.0 - sig)))
        dproj_ref[:, COL_U:COL_U + POOL_WIDTH] = du_ref[...]
        xs_all = xa_scr[:, 0:SSD_WIDTH]
        dd_ref[...] += _headsum(jnp.broadcast_to(_colsum(dyp * xs_all), (SUBLANES, SSD_WIDTH)), e_mat)[0:1, :]

        def chunk(k, carry_):
            c = nch - 1 - k
            r0 = pl.multiple_of(c * CHUNK, CHUNK)
            rows = pl.ds(r0, CHUNK)
            a_c = a_scr[rows, :]
            cs, cs_t, cs_e, last_e = _chunk_decays(a_c, tril, e_mat)
            decay_e = jnp.exp(last_e - cs_e)
            ecs_e = jnp.exp(cs_e)
            elast_e = jnp.exp(last_e)
            xc = x_scr[rows, :]
            xb = xc.astype(BF16)
            xd = (xc * decay_e).astype(BF16)
            dyc = dyp_scr[rows, :]
            dcs = jnp.zeros((CHUNK, LANES), F32)
            dcs_neg_t = jnp.zeros((LANES, CHUNK), F32)
            qoff, rin, ghrow = [], [], []
            for g in range(SSD_GROUPS):
                b_cols = slice(SSD_WIDTH + g * SSD_STATE, SSD_WIDTH + (g + 1) * SSD_STATE)
                c_cols = slice(SSD_WIDTH + (SSD_GROUPS + g) * SSD_STATE, SSD_WIDTH + (SSD_GROUPS + g + 1) * SSD_STATE)
                bg = xa_scr[rows, b_cols].astype(BF16)
                cg = xa_scr[rows, c_cols].astype(BF16)
                cbm = _dot_nt(cg, bg)
                dcb_m = jnp.zeros((CHUNK, CHUNK), F32)
                dbg = jnp.zeros((CHUNK, SSD_STATE), F32)
                dcg = jnp.zeros((CHUNK, SSD_STATE), F32)
                for jj in range(2):
                    j = 2 * g + jj
                    cols = slice(j * LANES, (j + 1) * LANES)
                    dyp_j = dyc[:, cols]
                    hp = st_ref[c, j]
                    hpb = hp.astype(BF16)
                    gt = gstate[j]
                    gtb = gt.astype(BF16)
                    ecs = ecs_e[:, cols]
                    yoff = _dot(cg, hpb) * ecs
                    dye = (dyp_j * ecs).astype(BF16)
                    dcg = dcg + _dot_nt(dye, hpb)
                    dht = _dot_tn(cg, dye)
                    qoff.append(dyp_j * yoff)
                    xg = _dot(bg, gtb)
                    dxp = xg * decay_e[:, cols]
                    rin.append(xg * xc[:, cols])
                    dbg = dbg + _dot_nt(xd[:, cols], gtb)
                    ghrow.append(_colsum(gt * hp) * elast_e[:, cols])
                    gstate[j] = dht + gt * elast_e[:, cols]
                    for hh in range(2):
                        h = 2 * j + hh
                        seg = jnp.exp(jnp.where(causal, cs[:, h:h + 1] - cs_t[h:h + 1, :], -jnp.inf))
                        m = cbm * seg
                        half = (lane < SSD_HEAD_DIM) if hh == 0 else (lane >= SSD_HEAD_DIM)
                        dym = jnp.where(half, dyp_j, 0.0).astype(BF16)
                        w = _dot_nt(dym, xb[:, cols])
                        pm = w * m
                        dcs = dcs + jnp.where(lane == h, jnp.sum(pm, axis=1, keepdims=True), 0.0)
                        dcs_neg_t = dcs_neg_t + jnp.where(sub == h, _colsum(pm), 0.0)
                        dcb_m = dcb_m + w * seg
                        dxp = dxp + _dot_tn(m.astype(BF16), dym)
                    dx_scr[:, cols] = dxp
                dcbb = dcb_m.astype(BF16)
                dxa_scr[rows, c_cols] = dcg + _dot(dcbb, bg)
                dxa_scr[rows, b_cols] = dbg + _dot_tn(dcbb, cg)
            decay_th = jnp.exp(cs[CHUNK - 1:CHUNK, :] - cs)
            rd = _headsum(jnp.concatenate(rin, axis=1), e_mat) * decay_th
            dcs = dcs - dcs_neg_t.T + _headsum(jnp.concatenate(qoff, axis=1), e_mat) - rd
            gh = _headsum(jnp.broadcast_to(jnp.concatenate(ghrow, axis=1), (SUBLANES, SSD_WIDTH)), e_mat)[0:1, :]
            dcs = dcs + jnp.where(sub == CHUNK - 1, _colsum(rd) + gh, 0.0)
            da = _hdot(triu, dcs)
            dx_all = dx_scr[...]
            xs = xa_scr[rows, 0:SSD_WIDTH]
            dt_c = _softplus(dt_ref[rows, :] + dtb_ref[...])
            ddt = da * a_neg + _headsum(dx_all * xs, e_mat)
            dalog_ref[...] += _colsum(da * dt_c) * a_neg
            ddtraw = ddt * _sigmoid(dt_ref[rows, :] + dtb_ref[...])
            dproj_ref[rows, COL_DT:COL_DT + LANES] = ddtraw
            ddtb_ref[...] += _colsum(ddtraw)
            dxa_scr[rows, 0:SSD_WIDTH] = dx_all * dte_scr[rows, :] + dyc * dexp_ref[...]
            return carry_

        lax.fori_loop(0, nch, chunk, 0)

        xcv = xc_scr[...]
        sgc = _sigmoid(xcv)
        dxc = dxa_scr[...] * (sgc * (1.0 + xcv * (1.0 - sgc)))
        dcb_ref[...] += _colsum(dxc)
        dbuf[0:ts, :] = dxc
        dbuf[ts:ts + SUBLANES, :] = carry[...]
        cwv = cw_ref[...]
        dxbc = jnp.zeros((ts, SSD_XBC), F32)
        dcw_rows = []
        for k in range(SSD_CONV):
            off = SUBLANES - (SSD_CONV - 1) + k
            dcw_rows.append(_colsum(dxc * buf[off:off + ts, :]))
            back = SSD_CONV - 1 - k
            dxbc = dxbc + cwv[k:k + 1, :] * dbuf[back:back + ts, :]
        dcw_ref[...] += jnp.concatenate(dcw_rows, axis=0)
        dproj_ref[:, COL_XBC:COL_XBC + SSD_XBC] = dxbc
        carry[...] = dxc[0:SUBLANES, :]
        if ex is not None:
            pl.when(i == nt - 1)(ex_wait)

    rev = lambda i: nt - 1 - i
    full = lambda shape: pl.BlockSpec(shape, lambda i: tuple(0 for _ in shape))
    outs = pl.pallas_call(
        body, name=name, grid=(nt,),
        in_specs=[pl.BlockSpec((ts, SSD_WIDTH), lambda i: (rev(i), 0)),
                  pl.BlockSpec((ts, SSD_XBC), lambda i: (rev(i), COL_XBC // SSD_XBC)),
                  pl.BlockSpec((SUBLANES, SSD_XBC), lambda i: (jnp.maximum(rev(i) * hb - 1, 0), COL_XBC // SSD_XBC)),
                  pl.BlockSpec((ts, SSD_WIDTH), lambda i: (rev(i), COL_Z // SSD_WIDTH)),
                  pl.BlockSpec((ts, LANES), lambda i: (rev(i), COL_DT // LANES)),
                  pl.BlockSpec((ts, SSD_WIDTH), lambda i: (rev(i), 0)),
                  pl.BlockSpec((nch, N_PAIRS, SSD_STATE, LANES), lambda i: (rev(i), 0, 0, 0)),
                  pl.BlockSpec((ts, POOL_WIDTH), lambda i: (rev(i), 0)),
                  full((SSD_CONV, SSD_XBC)), full((1, SSD_XBC)), full((1, LANES)), full((1, LANES)),
                  full((1, SSD_WIDTH)), full((1, SSD_WIDTH)), full((CHUNK, CHUNK)), full((CHUNK, CHUNK)),
                  full((LANES, SSD_WIDTH))] + ex_in_specs,
        out_specs=[pl.BlockSpec((ts, N_PROJ), lambda i: (rev(i), 0)),
                   full((SSD_CONV, SSD_XBC)), full((1, SSD_XBC)), full((1, LANES)), full((1, LANES)),
                   full((1, LANES)), full((1, SSD_WIDTH))] + ex_out_specs,
        out_shape=[jax.ShapeDtypeStruct((t, N_PROJ), F32),
                   jax.ShapeDtypeStruct((SSD_CONV, SSD_XBC), F32), jax.ShapeDtypeStruct((1, SSD_XBC), F32),
                   jax.ShapeDtypeStruct((1, LANES), F32), jax.ShapeDtypeStruct((1, LANES), F32),
                   jax.ShapeDtypeStruct((1, LANES), F32), jax.ShapeDtypeStruct((1, SSD_WIDTH), F32)] + ex_out_shape,
        scratch_shapes=[pltpu.VMEM((SUBLANES + ts, SSD_XBC), F32), pltpu.VMEM((ts, SSD_XBC), F32),
                        pltpu.VMEM((ts, SSD_XBC), F32), pltpu.VMEM((ts, LANES), F32),
                        pltpu.VMEM((ts, SSD_WIDTH), F32), pltpu.VMEM((ts, SSD_WIDTH), F32),
                        pltpu.VMEM((ts, SSD_WIDTH), F32), pltpu.VMEM((ts, SSD_XBC), F32),
                        pltpu.VMEM((CHUNK, SSD_WIDTH), F32), pltpu.VMEM((ts + SUBLANES, SSD_XBC), F32),
                        pltpu.VMEM((SUBLANES, SSD_XBC), F32), pltpu.VMEM((N_PAIRS, SSD_STATE, LANES), F32)]
        + ex_scratch,
        input_output_aliases={} if ex is None else ex.aliases(17, 7),
        compiler_params=_params("arbitrary"))(
            dymix, proj, proj, proj, proj, ypre, states, du, cw, cb, dtb, alog, dexp, ng, tril, triu, e, *ex_args)
    return outs[:7], outs[7:]


def _pooled(ubuf, u, pos, tt):
    out = []
    for gi, w in enumerate(POOL_WINDOWS):
        cols = slice(gi * POOL_GROUP, (gi + 1) * POOL_GROUP)
        acc = u[:, cols]
        for j in range(1, w):
            acc = acc + ubuf[POOL_HALO - j:POOL_HALO - j + tt, cols]
        out.append(acc / jnp.minimum(pos, float(w)) - u[:, cols])
    return out


def _mix_out(h, yssd, proj, pool_w, pool_scale, w_out, g_next, *, tt, name):
    t = h.shape[0]
    tt = _tile(t, tt)
    hb = tt // POOL_HALO

    def body(h_ref, ys_ref, u_ref, uh_ref, pw_ref, sc_ref, wo_ref, gn_ref, o_ref, ym_ref, n_ref, ubuf):
        i = pl.program_id(0)
        ubuf[0:POOL_HALO, :] = jnp.where(i == 0, 0.0, uh_ref[...])
        u = u_ref[...]
        ubuf[POOL_HALO:POOL_HALO + tt, :] = u
        pos = (i * tt + 1 + lax.broadcasted_iota(jnp.int32, (tt, 1), 0)).astype(F32)
        sc = sc_ref[...]
        parts = [ys_ref[...]]
        for gi, pooled in enumerate(_pooled(ubuf, u, pos, tt)):
            cols = slice(gi * POOL_GROUP, (gi + 1) * POOL_GROUP)
            parts.append(_dot(pooled.astype(BF16), pw_ref[gi]) * sc[:, cols])
        ymix = jnp.concatenate(parts, axis=1).astype(BF16)
        ym_ref[...] = ymix
        h2 = h_ref[...] + _dot(ymix, wo_ref[...])
        o_ref[...] = h2
        n_ref[...] = (h2 * _rms_r(h2) * gn_ref[...]).astype(BF16)

    full = lambda shape: pl.BlockSpec(shape, lambda i: tuple(0 for _ in shape))
    return pl.pallas_call(
        body, name=name, grid=(t // tt,),
        in_specs=[pl.BlockSpec((tt, D_MODEL), lambda i: (i, 0)), pl.BlockSpec((tt, SSD_WIDTH), lambda i: (i, 0)),
                  pl.BlockSpec((tt, POOL_WIDTH), lambda i: (i, COL_U // POOL_WIDTH)),
                  pl.BlockSpec((POOL_HALO, POOL_WIDTH), lambda i: (jnp.maximum(i * hb - 1, 0), COL_U // POOL_WIDTH)),
                  full((len(POOL_WINDOWS), POOL_GROUP, POOL_GROUP)), full((1, POOL_WIDTH)),
                  full((D_MODEL, D_MODEL)), full((1, D_MODEL))],
        out_specs=[pl.BlockSpec((tt, D_MODEL), lambda i: (i, 0))] * 3,
        out_shape=[jax.ShapeDtypeStruct((t, D_MODEL), F32), jax.ShapeDtypeStruct((t, D_MODEL), BF16),
                   jax.ShapeDtypeStruct((t, D_MODEL), BF16)],
        scratch_shapes=[pltpu.VMEM((POOL_HALO + tt, POOL_WIDTH), F32)],
        compiler_params=_params("arbitrary"))(h, yssd, proj, proj, pool_w, pool_scale, w_out, g_next)


def _out_bwd(dh, ymix, w_out_t, *, tt, name):
    t = dh.shape[0]
    tt = _tile(t, tt)

    def body(dh_ref, ym_ref, wt_ref, dym_ref, dw_ref):
        @pl.when(pl.program_id(0) == 0)
        def _():
            dw_ref[...] = jnp.zeros_like(dw_ref)

        dhb = dh_ref[...].astype(BF16)
        dym_ref[...] = _dot(dhb, wt_ref[...])
        dw_ref[...] += _dot_tn(ym_ref[...], dhb)

    return pl.pallas_call(
        body, name=name, grid=(t // tt,),
        in_specs=[pl.BlockSpec((tt, D_MODEL), lambda i: (i, 0)), pl.BlockSpec((tt, D_MODEL), lambda i: (i, 0)),
                  pl.BlockSpec((D_MODEL, D_MODEL), lambda i: (0, 0))],
        out_specs=[pl.BlockSpec((tt, D_MODEL), lambda i: (i, 0)), pl.BlockSpec((D_MODEL, D_MODEL), lambda i: (0, 0))],
        out_shape=[jax.ShapeDtypeStruct((t, D_MODEL), F32), jax.ShapeDtypeStruct((D_MODEL, D_MODEL), F32)],
        compiler_params=_params("arbitrary"))(dh, ymix, w_out_t)


def _pool_bwd(dymix, proj, pool_w, pool_w_t, pool_scale, *, tt, name):
    t = proj.shape[0]
    tt = _tile(t, tt)
    hb = tt // POOL_HALO
    nt = t // tt
    ng = len(POOL_WINDOWS)

    def body(dy_ref, dyh_ref, u_ref, uh_ref, pw_ref, pwt_ref, sc_ref, du_ref, dpw_ref, dsc_ref, ubuf, dbuf):
        i = pl.program_id(0)

        @pl.when(i == 0)
        def _():
            dpw_ref[...] = jnp.zeros_like(dpw_ref)
            dsc_ref[...] = jnp.zeros_like(dsc_ref)

        ubuf[0:POOL_HALO, :] = jnp.where(i == 0, 0.0, uh_ref[...])
        u = u_ref[...]
        ubuf[POOL_HALO:POOL_HALO + tt, :] = u
        pos = (i * tt + 1 + lax.broadcasted_iota(jnp.int32, (tt, 1), 0)).astype(F32)
        sc = sc_ref[...]
        dy = dy_ref[...]
        dyh = jnp.where(i == nt - 1, 0.0, dyh_ref[...])
        dsc_parts, du_parts = [], []
        for gi, pooled in enumerate(_pooled(ubuf, u, pos, tt)):
            w = POOL_WINDOWS[gi]
            cols = slice(gi * POOL_GROUP, (gi + 1) * POOL_GROUP)
            pb = pooled.astype(BF16)
            dsc_parts.append(_colsum(dy[:, cols] * _dot(pb, pw_ref[gi])))
            dmx = (dy[:, cols] * sc[:, cols]).astype(BF16)
            dpw_ref[gi] += _dot_tn(pb, dmx)
            dpool = _dot(dmx, pwt_ref[gi])
            dpool_h = _dot((dyh[:, cols] * sc[:, cols]).astype(BF16), pwt_ref[gi])
            dbuf[0:tt, cols] = dpool / jnp.minimum(pos, float(w))
            dbuf[tt:tt + POOL_HALO, cols] = dpool_h / float(w)
            acc = -dpool
            for j in range(w):
                acc = acc + dbuf[j:j + tt, cols]
            du_parts.append(acc)
        du_ref[...] = jnp.concatenate(du_parts, axis=1)
        dsc_ref[...] += jnp.concatenate(dsc_parts, axis=1)

    full = lambda shape: pl.BlockSpec(shape, lambda i: tuple(0 for _ in shape))
    ucol = COL_U // POOL_WIDTH
    return pl.pallas_call(
        body, name=name, grid=(nt,),
        in_specs=[pl.BlockSpec((tt, POOL_WIDTH), lambda i: (i, 1)),
                  pl.BlockSpec((POOL_HALO, POOL_WIDTH), lambda i: (jnp.minimum((i + 1) * hb, t // POOL_HALO - 1), 1)),
                  pl.BlockSpec((tt, POOL_WIDTH), lambda i: (i, ucol)),
                  pl.BlockSpec((POOL_HALO, POOL_WIDTH), lambda i: (jnp.maximum(i * hb - 1, 0), ucol)),
                  full((ng, POOL_GROUP, POOL_GROUP)), full((ng, POOL_GROUP, POOL_GROUP)), full((1, POOL_WIDTH))],
        out_specs=[pl.BlockSpec((tt, POOL_WIDTH), lambda i: (i, 0)), full((ng, POOL_GROUP, POOL_GROUP)),
                   full((1, POOL_WIDTH))],
        out_shape=[jax.ShapeDtypeStruct((t, POOL_WIDTH), F32), jax.ShapeDtypeStruct((ng, POOL_GROUP, POOL_GROUP), F32),
                   jax.ShapeDtypeStruct((1, POOL_WIDTH), F32)],
        scratch_shapes=[pltpu.VMEM((POOL_HALO + tt, POOL_WIDTH), F32), pltpu.VMEM((tt + POOL_HALO, POOL_WIDTH), F32)],
        compiler_params=_params("arbitrary"))(dymix, dymix, proj, proj, pool_w, pool_w_t, pool_scale)


def _in_bwd(dproj, h, g, w_in_t, dh, *, tt, name):
    t = h.shape[0]
    tt = _tile(t, tt)

    def body(dp_ref, h_ref, g_ref, wt_ref, dh_ref, o_ref, dw_ref, dg_ref):
        @pl.when(pl.program_id(0) == 0)
        def _():
            dw_ref[...] = jnp.zeros_like(dw_ref)
            dg_ref[...] = jnp.zeros_like(dg_ref)

        x = h_ref[...]
        r = _rms_r(x)
        gv = g_ref[...]
        dpb = dp_ref[...].astype(BF16)
        dw_ref[...] += _dot_tn((x * r * gv).astype(BF16), dpb)
        dx, dg = _rms_bwd(x, r, gv, _dot(dpb, wt_ref[...]))
        o_ref[...] = dh_ref[...] + dx
        dg_ref[...] += dg

    full = lambda shape: pl.BlockSpec(shape, lambda i: tuple(0 for _ in shape))
    row = lambda n: pl.BlockSpec((tt, n), lambda i: (i, 0))
    return pl.pallas_call(
        body, name=name, grid=(t // tt,),
        in_specs=[row(N_PROJ), row(D_MODEL), full((1, D_MODEL)), full((N_PROJ, D_MODEL)), row(D_MODEL)],
        out_specs=[row(D_MODEL), full((D_MODEL, N_PROJ)), full((1, D_MODEL))],
        out_shape=[jax.ShapeDtypeStruct((t, D_MODEL), F32), jax.ShapeDtypeStruct((D_MODEL, N_PROJ), F32),
                   jax.ShapeDtypeStruct((1, D_MODEL), F32)],
        compiler_params=_params("arbitrary"))(dproj, h, g, w_in_t, dh)


FFN_COLS = 256
N_SLABS = D_FF // FFN_COLS
FFN_ROWS = 16


def _conv3_block(ref, halo_ref, b, r0, cols, halo_is_zero):
    rp = pl.multiple_of(jnp.maximum(r0 - SUBLANES, 0), SUBLANES)
    prev = jnp.where(b == 0, jnp.where(halo_is_zero, 0.0, halo_ref[:, cols]), ref[pl.ds(rp, SUBLANES), cols])
    x = ref[pl.ds(r0, FFN_ROWS), cols]
    return x, jnp.concatenate([prev, x], axis=0)


def _conv3(x, ext, w, b):
    acc = b + w[FFN_CONV - 1:FFN_CONV] * x
    for k in range(FFN_CONV - 1):
        off = SUBLANES - (FFN_CONV - 1) + k
        acc = acc + w[k:k + 1] * ext[off:off + FFN_ROWS]
    return acc


def _part8(v):
    acc = v[0:SUBLANES]
    for r in range(SUBLANES, FFN_ROWS, SUBLANES):
        acc = acc + v[r:r + SUBLANES]
    return acc


def _ffn_down(h, up, cw, cb, w_down, *, tt, name, ex=None):
    t = h.shape[0]
    tt = _tile(t, tt)
    hb = tt // SUBLANES
    nt = t // tt
    ex_args, ex_in_specs, ex_out_shape, ex_out_specs, ex_scratch, ex_counts = _ex_parts(ex)

    def body(*refs):
        (h_ref, up_ref, uh_ref, cw_ref, cb_ref, wd_ref), ex_in, (o_ref, act_ref), ex_out, (buf,), ex_sems = (
            _split_refs(refs, (6, ex_counts[0], 2, ex_counts[1], 1, ex_counts[2])))
        i = pl.program_id(0)
        if ex is not None:
            ex_start, ex_wait = ex.ops(ex_in, ex_out, ex_sems)
            pl.when(i == 0)(ex_start)
        buf[0:SUBLANES, :] = jnp.where(i == 0, 0.0, uh_ref[...])
        buf[SUBLANES:SUBLANES + tt, :] = up_ref[...]

        def conv(cols):
            acc = cb_ref[:, cols]
            for k in range(FFN_CONV):
                off = SUBLANES - (FFN_CONV - 1) + k
                acc = acc + cw_ref[k:k + 1, cols] * buf[off:off + tt, cols]
            return acc

        out = h_ref[...]
        for s in range(N_SLABS):
            gate = conv(slice(s * FFN_COLS, (s + 1) * FFN_COLS))
            val = conv(slice(D_FF + s * FFN_COLS, D_FF + (s + 1) * FFN_COLS))
            act = (_gelu(gate) * val).astype(BF16)
            act_ref[:, s * FFN_COLS:(s + 1) * FFN_COLS] = act
            out = out + _dot(act, wd_ref[s * FFN_COLS:(s + 1) * FFN_COLS, :])
        o_ref[...] = out
        if ex is not None:
            pl.when(i == nt - 1)(ex_wait)

    full = lambda shape: pl.BlockSpec(shape, lambda i: tuple(0 for _ in shape))
    outs = pl.pallas_call(
        body, name=name, grid=(nt,),
        in_specs=[pl.BlockSpec((tt, D_MODEL), lambda i: (i, 0)), pl.BlockSpec((tt, D_UP), lambda i: (i, 0)),
                  pl.BlockSpec((SUBLANES, D_UP), lambda i: (jnp.maximum(i * hb - 1, 0), 0)),
                  full((FFN_CONV, D_UP)), full((1, D_UP)), full((D_FF, D_MODEL))] + ex_in_specs,
        out_specs=[pl.BlockSpec((tt, D_MODEL), lambda i: (i, 0)), pl.BlockSpec((tt, D_FF), lambda i: (i, 0))]
        + ex_out_specs,
        out_shape=[jax.ShapeDtypeStruct((t, D_MODEL), F32), jax.ShapeDtypeStruct((t, D_FF), BF16)] + ex_out_shape,
        scratch_shapes=[pltpu.VMEM((SUBLANES + tt, D_UP), F32)] + ex_scratch,
        input_output_aliases={} if ex is None else ex.aliases(6, 2),
        compiler_params=_params("arbitrary"))(h, up, up, cw, cb, w_down, *ex_args)
    return outs[0], outs[1], outs[2:]


def _ffn_act_bwd(up, dact, cw, cb, *, tt, name, ex=None):
    t = up.shape[0]
    tt = _tile(t, tt)
    hb = tt // SUBLANES
    nt = t // tt
    nb = tt // FFN_ROWS
    ex_args, ex_in_specs, ex_out_shape, ex_out_specs, ex_scratch, ex_counts = _ex_parts(ex)

    def body(*refs):
        (up_ref, uh_ref, da_ref, cw_ref, cb_ref), ex_in, (dup_ref, dcw_ref, dcb_ref), ex_out, (carry,), ex_sems = (
            _split_refs(refs, (5, ex_counts[0], 3, ex_counts[1], 1, ex_counts[2])))
        i = pl.program_id(0)
        if ex is not None:
            ex_start, ex_wait = ex.ops(ex_in, ex_out, ex_sems)
            pl.when(i == 0)(ex_start)

        @pl.when(i == 0)
        def _():
            dcw_ref[...] = jnp.zeros_like(dcw_ref)
            dcb_ref[...] = jnp.zeros_like(dcb_ref)
            carry[...] = jnp.zeros_like(carry)

        seq_start = i == nt - 1
        for s in range(N_SLABS):
            gcols = slice(s * FFN_COLS, (s + 1) * FFN_COLS)
            vcols = slice(D_FF + s * FFN_COLS, D_FF + (s + 1) * FFN_COLS)
            wg, wv, bg, bv = cw_ref[:, gcols], cw_ref[:, vcols], cb_ref[:, gcols], cb_ref[:, vcols]

            def blk(j, c):
                nxt_g, nxt_v, sums = c
                b = nb - 1 - j
                r0 = pl.multiple_of(b * FFN_ROWS, FFN_ROWS)
                xg, eg = _conv3_block(up_ref, uh_ref, b, r0, gcols, seq_start)
                xv, ev = _conv3_block(up_ref, uh_ref, b, r0, vcols, seq_start)
                gate, val = _conv3(xg, eg, wg, bg), _conv3(xv, ev, wv, bv)
                gelu, dgelu = _gelu(gate), _gelu_grad(gate)
                da = da_ref[pl.ds(r0, FFN_ROWS), gcols]
                heads, new = [], []
                for dp, nxt, x, w, cols in ((da * val * dgelu, nxt_g, xg, wg, gcols), (da * gelu, nxt_v, xv, wv, vcols)):
                    de = jnp.concatenate([dp, nxt], axis=0)
                    shifted = [de[FFN_CONV - 1 - k:FFN_CONV - 1 - k + FFN_ROWS] for k in range(FFN_CONV - 1)] + [dp]
                    dup = w[0:1] * shifted[0]
                    for k in range(1, FFN_CONV):
                        dup = dup + w[k:k + 1] * shifted[k]
                    dup_ref[pl.ds(r0, FFN_ROWS), cols] = dup.astype(BF16)
                    heads.append(dp[0:SUBLANES])
                    new += [_part8(dp)] + [_part8(sh * x) for sh in shifted]
                return heads[0], heads[1], tuple(a + v for a, v in zip(sums, new))

            zero = jnp.zeros((SUBLANES, FFN_COLS), F32)
            nxt_g, nxt_v, sums = lax.fori_loop(
                0, nb, blk, (carry[:, gcols], carry[:, vcols], (zero,) * (2 * (1 + FFN_CONV))))
            carry[:, gcols] = nxt_g
            carry[:, vcols] = nxt_v
            for half, cols in enumerate((gcols, vcols)):
                part = sums[half * (1 + FFN_CONV):(half + 1) * (1 + FFN_CONV)]
                dcb_ref[:, cols] += _colsum(part[0])
                dcw_ref[:, cols] += jnp.concatenate([_colsum(v) for v in part[1:]], axis=0)
        if ex is not None:
            pl.when(i == nt - 1)(ex_wait)

    rev = lambda i: nt - 1 - i
    full = lambda shape: pl.BlockSpec(shape, lambda i: tuple(0 for _ in shape))
    outs = pl.pallas_call(
        body, name=name, grid=(nt,),
        in_specs=[pl.BlockSpec((tt, D_UP), lambda i: (rev(i), 0)),
                  pl.BlockSpec((SUBLANES, D_UP), lambda i: (jnp.maximum(rev(i) * hb - 1, 0), 0)),
                  pl.BlockSpec((tt, D_FF), lambda i: (rev(i), 0)), full((FFN_CONV, D_UP)), full((1, D_UP))]
        + ex_in_specs,
        out_specs=[pl.BlockSpec((tt, D_UP), lambda i: (rev(i), 0)), full((FFN_CONV, D_UP)), full((1, D_UP))]
        + ex_out_specs,
        out_shape=[jax.ShapeDtypeStruct((t, D_UP), BF16), jax.ShapeDtypeStruct((FFN_CONV, D_UP), F32),
                   jax.ShapeDtypeStruct((1, D_UP), F32)] + ex_out_shape,
        scratch_shapes=[pltpu.VMEM((SUBLANES, D_UP), F32)] + ex_scratch,
        input_output_aliases={} if ex is None else ex.aliases(5, 3),
        compiler_params=_params("arbitrary"))(up, up, dact, cw, cb, *ex_args)
    return outs[0], outs[1], outs[2], outs[3:]


def _ple_fwd(h, p, g, w_gate, w_proj, *, tt, name):
    t = h.shape[0]
    tt = _tile(t, tt)

    def body(h_ref, p_ref, g_ref, wg_ref, wp_ref, o_ref):
        x = h_ref[...]
        n = (x * _rms_r(x) * g_ref[...]).astype(BF16)
        gate = _sigmoid(_dot(n, wg_ref[...]))
        o_ref[...] = x + _dot(p_ref[...].astype(BF16), wp_ref[...]) * gate

    full = lambda shape: pl.BlockSpec(shape, lambda i: tuple(0 for _ in shape))
    return pl.pallas_call(
        body, name=name, grid=(t // tt,),
        in_specs=[pl.BlockSpec((tt, D_MODEL), lambda i: (i, 0)), pl.BlockSpec((tt, D_PLE), lambda i: (i, 0)),
                  full((1, D_MODEL)), full((D_MODEL, D_MODEL)), full((D_PLE, D_MODEL))],
        out_specs=pl.BlockSpec((tt, D_MODEL), lambda i: (i, 0)),
        out_shape=jax.ShapeDtypeStruct((t, D_MODEL), F32),
        compiler_params=_params("arbitrary"))(h, p, g, w_gate, w_proj)


def _ple_bwd(dh, h, p, g, w_gate, w_gate_t, w_proj, *, tt, name):
    t = h.shape[0]
    tt = _tile(t, tt)

    def body(dh_ref, h_ref, p_ref, g_ref, wg_ref, wgt_ref, wp_ref, o_ref, dwg_ref, dwp_ref, dg_ref):
        @pl.when(pl.program_id(0) == 0)
        def _():
            dwg_ref[...] = jnp.zeros_like(dwg_ref)
            dwp_ref[...] = jnp.zeros_like(dwp_ref)
            dg_ref[...] = jnp.zeros_like(dg_ref)

        x = h_ref[...]
        r = _rms_r(x)
        gv = g_ref[...]
        n = (x * r * gv).astype(BF16)
        gate = _sigmoid(_dot(n, wg_ref[...]))
        pb = p_ref[...].astype(BF16)
        pe = _dot(pb, wp_ref[...])
        dhv = dh_ref[...]
        dwp_ref[...] += _dot_tn(pb, (dhv * gate).astype(BF16))
        ds = (dhv * pe * gate * (1.0 - gate)).astype(BF16)
        dwg_ref[...] += _dot_tn(n, ds)
        dx, dg = _rms_bwd(x, r, gv, _dot(ds, wgt_ref[...]))
        o_ref[...] = dhv + dx
        dg_ref[...] += dg

    full = lambda shape: pl.BlockSpec(shape, lambda i: tuple(0 for _ in shape))
    row = lambda n: pl.BlockSpec((tt, n), lambda i: (i, 0))
    return pl.pallas_call(
        body, name=name, grid=(t // tt,),
        in_specs=[row(D_MODEL), row(D_MODEL), row(D_PLE), full((1, D_MODEL)), full((D_MODEL, D_MODEL)),
                  full((D_MODEL, D_MODEL)), full((D_PLE, D_MODEL))],
        out_specs=[row(D_MODEL), full((D_MODEL, D_MODEL)), full((D_PLE, D_MODEL)), full((1, D_MODEL))],
        out_shape=[jax.ShapeDtypeStruct((t, D_MODEL), F32), jax.ShapeDtypeStruct((D_MODEL, D_MODEL), F32),
                   jax.ShapeDtypeStruct((D_PLE, D_MODEL), F32), jax.ShapeDtypeStruct((1, D_MODEL), F32)],
        compiler_params=_params("arbitrary"))(dh, h, p, g, w_gate, w_gate_t, w_proj)


def _loss_head(h, g, target, *, tt, name):
    t = h.shape[0]
    tt = _tile(t, tt)

    def body(h_ref, g_ref, tg_ref, dh_ref, loss_ref, dg_ref):
        @pl.when(pl.program_id(0) == 0)
        def _():
            loss_ref[...] = jnp.zeros_like(loss_ref)
            dg_ref[...] = jnp.zeros_like(dg_ref)

        x = h_ref[...]
        r = _rms_r(x)
        gv = g_ref[...]
        diff = x * r * gv - tg_ref[...]
        loss_ref[...] += 0.5 * jnp.sum(jnp.mean(diff * diff, axis=-1, keepdims=True), axis=0, keepdims=True)
        dx, dg = _rms_bwd(x, r, gv, diff * (1.0 / D_MODEL))
        dh_ref[...] = dx
        dg_ref[...] += dg

    return pl.pallas_call(
        body, name=name, grid=(t // tt,),
        in_specs=[pl.BlockSpec((tt, D_MODEL), lambda i: (i, 0)), pl.BlockSpec((1, D_MODEL), lambda i: (0, 0)),
                  pl.BlockSpec((tt, D_MODEL), lambda i: (i, 0))],
        out_specs=[pl.BlockSpec((tt, D_MODEL), lambda i: (i, 0)), pl.BlockSpec((SUBLANES, LANES), lambda i: (0, 0)),
                   pl.BlockSpec((1, D_MODEL), lambda i: (0, 0))],
        out_shape=[jax.ShapeDtypeStruct((t, D_MODEL), F32), jax.ShapeDtypeStruct((SUBLANES, LANES), F32),
                   jax.ShapeDtypeStruct((1, D_MODEL), F32)],
        compiler_params=_params("arbitrary"))(h, g, target)


ADAM_BLOCK_BYTES = 4 * 1024 * 1024


def _adam_rows(rows, cols):
    lanes = -(-cols // LANES) * LANES
    for cand in (1024, 512, 256, 128, 64, 32, 16, 8):
        if rows % cand == 0 and N_DEV * cand * lanes * 4 <= ADAM_BLOCK_BYTES:
            return cand
    return rows


def _sum_adamw(parts, w, m, v, *, name):
    nl, rows, cols = w.shape
    tr = _adam_rows(rows, cols)

    def body(p_ref, w_ref, m_ref, v_ref, g_ref, d_ref, nm_ref, nv_ref):
        g = p_ref[0]
        for k in range(1, N_DEV):
            g = g + p_ref[k]
        g_ref[...] = g
        nm = ADAM_B1 * m_ref[...] + (1.0 - ADAM_B1) * g
        nv = ADAM_B2 * v_ref[...] + (1.0 - ADAM_B2) * (g * g)
        m_hat = nm / (1.0 - ADAM_B1 ** ADAM_STEP)
        v_hat = nv / (1.0 - ADAM_B2 ** ADAM_STEP)
        d_ref[...] = -ADAM_LR * (m_hat / (jnp.sqrt(v_hat) + ADAM_EPS) + ADAM_WD * w_ref[...])
        nm_ref[...] = nm
        nv_ref[...] = nv

    blk = pl.BlockSpec((None, tr, cols), lambda l, r: (l, r, 0))
    return pl.pallas_call(
        body, name=name, grid=(nl, rows // tr),
        in_specs=[pl.BlockSpec((None, N_DEV, tr, cols), lambda l, r: (l, 0, r, 0)), blk, blk, blk],
        out_specs=[blk, blk, blk, blk],
        out_shape=[jax.ShapeDtypeStruct((nl, rows, cols), F32)] * 4,
        compiler_params=_params("arbitrary", "arbitrary"))(parts, w, m, v)


PACK_ROWS = 512


def _pack(arrays):
    flat = jnp.concatenate([a.astype(F32).reshape(-1) for a in arrays])
    pad = (-flat.shape[0]) % (PACK_ROWS * LANES)
    return jnp.pad(flat, (0, pad)).reshape(-1, LANES)


def _unpack(buf, shapes):
    flat = buf.reshape(-1)
    out, off = [], 0
    for s in shapes:
        n = math.prod(s)
        out.append(flat[off:off + n].reshape(s))
        off += n
    return out


def _to_proj_cols(w):
    z, xbc, dtc, u = jnp.split(w, [SSD_WIDTH, SSD_WIDTH + SSD_XBC, SSD_WIDTH + SSD_XBC + SSD_HEADS], axis=-1)
    pad = jnp.zeros(w.shape[:-1] + (LANES - SSD_HEADS,), w.dtype)
    return jnp.concatenate([xbc, z, u, dtc, pad], axis=-1)


def _from_proj_cols(w):
    xbc, z, u, dtc = (w[..., COL_XBC:COL_Z], w[..., COL_Z:COL_U], w[..., COL_U:COL_DT],
                      w[..., COL_DT:COL_DT + SSD_HEADS])
    return jnp.concatenate([z, xbc, dtc, u], axis=-1)


def _pad_heads(v):
    return jnp.pad(v, (0, LANES - SSD_HEADS)).reshape(1, LANES)


def _cat_cols(g):
    return jnp.transpose(g, (1, 0, 2)).reshape(g.shape[1], N_DEV * g.shape[2])


def _split_cols(w):
    r, c = w.shape
    return jnp.transpose(w.reshape(r, N_DEV, c // N_DEV), (1, 0, 2))


def _cat_rows(g):
    return g.reshape(N_DEV * g.shape[1], g.shape[2])


def _split_rows(w):
    return w.reshape(N_DEV, w.shape[0] // N_DEV, w.shape[1])


SHARDED = ("w_in", "w_out", "ffn_w_up", "ffn_w_down", "ple_w_gate", "ple_w_proj", "ssd_conv_w", "ffn_conv_w")
COL_SHARDED = ("w_in", "ffn_w_up", "ple_w_proj", "ssd_conv_w", "ffn_conv_w")
MATMUL_W = SHARDED[:6]
REPLICATED = ("mix_norm_g", "ssd_conv_b", "ssd_dt_bias", "ssd_a_log", "ssd_d", "ssd_norm_g", "pool_w", "pool_scale",
              "ffn_norm_g", "ffn_conv_b", "ple_norm_g", "final_norm_g")
WEIGHTS = ("mix_norm_g", "w_in", "ssd_conv_w", "ssd_conv_b", "ssd_dt_bias", "ssd_a_log", "ssd_d", "ssd_norm_g",
           "pool_w", "pool_scale", "w_out", "ffn_norm_g", "ffn_w_up", "ffn_conv_w", "ffn_conv_b", "ffn_w_down",
           "ple_norm_g", "ple_w_gate", "ple_w_proj", "final_norm_g")


FIRST_USED = ("w_in", "ssd_conv_w")
LATER_USED = tuple(k for k in SHARDED if k not in FIRST_USED)
LAST_MADE = ("w_out", "ssd_conv_w", "w_in")
EARLY_MADE = tuple(k for k in SHARDED if k not in LAST_MADE)
TRANSPOSED = ("w_in", "w_out", "ffn_w_up", "ffn_w_down", "ple_w_gate")


def _pick(names, per_sharded):
    return [per_sharded[SHARDED.index(k)] for k in names]


def _put(names, per_sharded, values):
    out = list(per_sharded)
    for k, val in zip(names, values):
        out[SHARDED.index(k)] = val
    return out


def _assemble(names, gathered):
    full = {}
    for k, g in zip(names, gathered):
        full[k] = _cat_cols(g) if k in COL_SHARDED else _cat_rows(g)
        if k == "w_in":
            full[k] = _to_proj_cols(full[k])
        if k in TRANSPOSED:
            full[k + "_t"] = full[k].T
    return full


def _grad_shards(names, grads):
    out = []
    for k in names:
        g = _from_proj_cols(grads[k]) if k == "w_in" else grads[k]
        out.append(_split_cols(g) if k in COL_SHARDED else _split_rows(g))
    return out


def _layer_fwd(i, h1, p_i, lw, rep, consts, ex_own, ex):
    tril, e_mat = consts
    row = lambda v: v.reshape(1, -1)
    dtb, alog = _pad_heads(rep["ssd_dt_bias"]), _pad_heads(rep["ssd_a_log"])
    dexp = row(jnp.repeat(rep["ssd_d"], SSD_HEAD_DIM))
    pw = rep["pool_w"].astype(BF16)
    proj = _norm_matmul(h1, lw["w_in"], row(rep["mix_norm_g"]), tt=512, tn=N_PROJ, name=f"in_proj_{i}")
    yssd, ypre, states, own = _ssd_fwd(proj, lw["ssd_conv_w"], row(rep["ssd_conv_b"]), dtb, alog, dexp,
                                       row(rep["ssd_norm_g"]), tril, e_mat, ts=512, name=f"ssd_fwd_{i}", ex=ex_own)
    if ex_own is not None:
        lw = dict(lw, **_assemble(LATER_USED, own))
    h2, ymix, n2 = _mix_out(h1, yssd, proj, pw, row(rep["pool_scale"]), lw["w_out"], row(rep["ffn_norm_g"]), tt=512,
                            name=f"mix_out_{i}")
    up = _norm_matmul(n2, lw["ffn_w_up"], tt=512, tn=D_FF, name=f"ffn_up_{i}")
    h3, act, gathered = _ffn_down(h2, up, lw["ffn_conv_w"], row(rep["ffn_conv_b"]), lw["ffn_w_down"], tt=256,
                                  name=f"ffn_down_{i}", ex=ex)
    h4 = _ple_fwd(h3, p_i, row(rep["ple_norm_g"]), lw["ple_w_gate"], lw["ple_w_proj"], tt=512, name=f"ple_fwd_{i}")
    saved = dict(h1=h1, proj=proj, ypre=ypre, states=states, ymix=ymix, h2=h2, n2=n2, up=up, act=act, h3=h3,
                 dtb=dtb, alog=alog, dexp=dexp, pw=pw)
    return h4, saved, lw, gathered


def _layer_bwd(i, dh, p_i, lw, rep, s, consts, pending, parts, own_early):
    tril, triu, e_mat = consts
    ex = None if pending is None else _Exchange(pending, scatter=True, layer=i + 1, into=parts)
    row = lambda v: v.reshape(1, -1)
    g = {}
    dh, g["ple_w_gate"], g["ple_w_proj"], dg3 = _ple_bwd(dh, s["h3"], p_i, row(rep["ple_norm_g"]), lw["ple_w_gate"],
                                                         lw["ple_w_gate_t"], lw["ple_w_proj"], tt=512,
                                                         name=f"ple_bwd_{i}")
    g["ple_norm_g"] = dg3.reshape(-1)
    g["ffn_w_down"] = _matmul_tn(s["act"], dh, tm=D_FF // 2, tn=D_MODEL, tk=1024, name=f"dw_down_{i}")
    dact = _norm_matmul(dh, lw["ffn_w_down_t"], tt=512, tn=D_FF, name=f"d_act_{i}")
    dup, g["ffn_conv_w"], dcb, scattered = _ffn_act_bwd(s["up"], dact, lw["ffn_conv_w"], row(rep["ffn_conv_b"]),
                                                        tt=256, name=f"ffn_act_bwd_{i}", ex=ex)
    if ex is not None:
        parts = scattered
    g["ffn_conv_b"] = dcb.reshape(-1)
    g["ffn_w_up"] = _matmul_tn(s["n2"], dup, tm=D_MODEL, tn=D_UP // 4, tk=1024, name=f"dw_up_{i}")
    dh, dg2 = _matmul_rmsbwd(dup, lw["ffn_w_up_t"], s["h2"], row(rep["ffn_norm_g"]), dh, tt=512, tk=D_UP,
                             name=f"ffn_up_bwd_{i}")
    g["ffn_norm_g"] = dg2.reshape(-1)
    dymix, g["w_out"] = _out_bwd(dh, s["ymix"], lw["w_out_t"], tt=512, name=f"out_bwd_{i}")
    du, g["pool_w"], dsc = _pool_bwd(dymix, s["proj"], s["pw"], jnp.swapaxes(s["pw"], 1, 2), row(rep["pool_scale"]),
                                     tt=512, name=f"pool_bwd_{i}")
    g["pool_scale"] = dsc.reshape(-1)
    ex_own = None
    if own_early:
        ex_own = _Exchange(_grad_shards(EARLY_MADE, g), scatter=True, layer=i, into=_pick(EARLY_MADE, parts))
    (dproj, g["ssd_conv_w"], dcb, ddtb, dalog, dd, dng), own = _ssd_bwd(
        dymix, s["proj"], s["ypre"], s["states"], du, lw["ssd_conv_w"], row(rep["ssd_conv_b"]), s["dtb"], s["alog"],
        s["dexp"], row(rep["ssd_norm_g"]), tril, triu, e_mat, ts=512, name=f"ssd_bwd_{i}", ex=ex_own)
    if own_early:
        parts = _put(EARLY_MADE, parts, own)
    g["ssd_conv_b"], g["ssd_norm_g"] = dcb.reshape(-1), dng.reshape(-1)
    g["ssd_dt_bias"], g["ssd_a_log"], g["ssd_d"] = ddtb[0, :SSD_HEADS], dalog[0, :SSD_HEADS], dd[0, :SSD_HEADS]
    dh, g["w_in"], dg1 = _in_bwd(dproj, s["h1"], row(rep["mix_norm_g"]), lw["w_in_t"], dh, tt=256, name=f"in_bwd_{i}")
    g["mix_norm_g"] = dg1.reshape(-1)
    return dh, g, parts


def kernel(x, p, mix_norm_g, w_in, ssd_conv_w, ssd_conv_b, ssd_dt_bias, ssd_a_log, ssd_d, ssd_norm_g, pool_w, pool_scale, w_out, ffn_norm_g, ffn_w_up, ffn_conv_w, ffn_conv_b, ffn_w_down, ple_norm_g, ple_w_gate, ple_w_proj, final_norm_g, loss_target, m_mix_norm_g, m_w_in, m_ssd_conv_w, m_ssd_conv_b, m_ssd_dt_bias, m_ssd_a_log, m_ssd_d, m_ssd_norm_g, m_pool_w, m_pool_scale, m_w_out, m_ffn_norm_g, m_ffn_w_up, m_ffn_conv_w, m_ffn_conv_b, m_ffn_w_down, m_ple_norm_g, m_ple_w_gate, m_ple_w_proj, m_final_norm_g, v_mix_norm_g, v_w_in, v_ssd_conv_w, v_ssd_conv_b, v_ssd_dt_bias, v_ssd_a_log, v_ssd_d, v_ssd_norm_g, v_pool_w, v_pool_scale, v_w_out, v_ffn_norm_g, v_ffn_w_up, v_ffn_conv_w, v_ffn_conv_b, v_ffn_w_down, v_ple_norm_g, v_ple_w_gate, v_ple_w_proj, v_final_norm_g):
    w = dict(mix_norm_g=mix_norm_g, w_in=w_in, ssd_conv_w=ssd_conv_w, ssd_conv_b=ssd_conv_b, ssd_dt_bias=ssd_dt_bias,
             ssd_a_log=ssd_a_log, ssd_d=ssd_d, ssd_norm_g=ssd_norm_g, pool_w=pool_w, pool_scale=pool_scale, w_out=w_out,
             ffn_norm_g=ffn_norm_g, ffn_w_up=ffn_w_up, ffn_conv_w=ffn_conv_w, ffn_conv_b=ffn_conv_b,
             ffn_w_down=ffn_w_down, ple_norm_g=ple_norm_g, ple_w_gate=ple_w_gate, ple_w_proj=ple_w_proj,
             final_norm_g=final_norm_g)
    m = dict(mix_norm_g=m_mix_norm_g, w_in=m_w_in, ssd_conv_w=m_ssd_conv_w, ssd_conv_b=m_ssd_conv_b,
             ssd_dt_bias=m_ssd_dt_bias, ssd_a_log=m_ssd_a_log, ssd_d=m_ssd_d, ssd_norm_g=m_ssd_norm_g, pool_w=m_pool_w,
             pool_scale=m_pool_scale, w_out=m_w_out, ffn_norm_g=m_ffn_norm_g, ffn_w_up=m_ffn_w_up,
             ffn_conv_w=m_ffn_conv_w, ffn_conv_b=m_ffn_conv_b, ffn_w_down=m_ffn_w_down, ple_norm_g=m_ple_norm_g,
             ple_w_gate=m_ple_w_gate, ple_w_proj=m_ple_w_proj, final_norm_g=m_final_norm_g)
    v = dict(mix_norm_g=v_mix_norm_g, w_in=v_w_in, ssd_conv_w=v_ssd_conv_w, ssd_conv_b=v_ssd_conv_b,
             ssd_dt_bias=v_ssd_dt_bias, ssd_a_log=v_ssd_a_log, ssd_d=v_ssd_d, ssd_norm_g=v_ssd_norm_g, pool_w=v_pool_w,
             pool_scale=v_pool_scale, w_out=v_w_out, ffn_norm_g=v_ffn_norm_g, ffn_w_up=v_ffn_w_up,
             ffn_conv_w=v_ffn_conv_w, ffn_conv_b=v_ffn_conv_b, ffn_w_down=v_ffn_w_down, ple_norm_g=v_ple_norm_g,
             ple_w_gate=v_ple_w_gate, ple_w_proj=v_ple_w_proj, final_norm_g=v_final_norm_g)

    tril = jnp.tril(jnp.ones((CHUNK, CHUNK), F32))
    triu = tril.T
    e_mat = (jnp.arange(SSD_WIDTH)[None, :] // SSD_HEAD_DIM == jnp.arange(LANES)[:, None]).astype(F32)
    rep = [{k: w[k][i] for k in REPLICATED if k != "final_norm_g"} for i in range(DEPTH)]
    p_loc = p[:, 0]

    shards = [w[k].astype(BF16) if k in MATMUL_W else w[k] for k in SHARDED]
    lw = _assemble(FIRST_USED, _exchange_call(_Exchange(_pick(FIRST_USED, shards), scatter=False, layer=0),
                                              "gather_weights_0"))
    h, saved, layer_w = x[0], [], []
    for i in range(DEPTH):
        ex_own = _Exchange(_pick(LATER_USED, shards), scatter=False, layer=0) if i == 0 else None
        ex = _Exchange(shards, scatter=False, layer=i + 1) if i + 1 < DEPTH else None
        h, s, lw, gathered = _layer_fwd(i, h, p_loc[i], lw, rep[i], (tril, e_mat), ex_own, ex)
        saved.append(s)
        layer_w.append(lw)
        lw = _assemble(SHARDED, gathered)

    dh, loss_blk, dgf = _loss_head(h, final_norm_g.reshape(1, -1), loss_target[0], tt=512, name="loss_head")
    loss = lax.psum(loss_blk[0, 0], ("x", "y", "c"))

    rep_grads = [None] * DEPTH
    pending, parts = None, None
    for i in reversed(range(DEPTH)):
        dh, g, parts = _layer_bwd(i, dh, p_loc[i], layer_w[i], rep[i], saved[i], (tril, triu, e_mat), pending, parts,
                                  own_early=(i == 0))
        pending = _grad_shards(SHARDED, g) if i > 0 else _grad_shards(LAST_MADE, g)
        rep_grads[i] = g
    parts = _put(LAST_MADE, parts, _exchange_call(
        _Exchange(pending, scatter=True, layer=0, into=_pick(LAST_MADE, parts)), "scatter_grads_0"))

    out = {}
    for k, part in zip(SHARDED, parts):
        out[k] = _sum_adamw(part, w[k], m[k], v[k], name=f"adamw_{k}")

    rp_grads = [dgf.reshape(-1) if k == "final_norm_g" else jnp.stack([rep_grads[i][k] for i in range(DEPTH)])
                for k in REPLICATED]
    rp_shapes = [w[k].shape for k in REPLICATED]
    rp_parts = _exchange_call(_Exchange([_pack(rp_grads)[None]], scatter=False, layer=0), "gather_replicated_grads")[0]
    rp_out = _sum_adamw(rp_parts[None], *[_pack([d[k] for k in REPLICATED])[None] for d in (w, m, v)],
                        name="adamw_replicated")
    for j in range(4):
        for k, arr in zip(REPLICATED, _unpack(rp_out[j][0], rp_shapes)):
            out.setdefault(k, [None] * 4)[j] = arr
    results = [out[k][j] for j in range(4) for k in WEIGHTS]
    return (loss, dh[None], *results)
```

```python
import functools
import math

import jax
import jax.numpy as jnp
from jax import lax
from jax.experimental import pallas as pl
from jax.experimental.pallas import tpu as pltpu

F32 = jnp.float32
BF16 = jnp.bfloat16

N_DEV = 8
EPS = 1e-6
DEPTH = 4
D_MODEL = 1024
D_PLE = 256
SSD_WIDTH = 512
SSD_HEADS = 8
SSD_HEAD_DIM = 64
SSD_GROUPS = 2
SSD_STATE = 128
SSD_CONV = 4
CHUNK = 128
SSD_XBC = 1024
POOL_WINDOWS = (2, 4, 8, 16)
POOL_WIDTH = 512
POOL_GROUP = 128
POOL_HALO = 16
D_IN_PROJ = 2056
D_FF = 2816
D_UP = 2 * D_FF
FFN_CONV = 3
SUBLANES = 8
LANES = 128
N_PROJ = 2176
COL_XBC, COL_Z, COL_U, COL_DT = 0, 1024, 1536, 2048
N_PAIRS = SSD_HEADS // 2
ADAM_LR, ADAM_B1, ADAM_B2, ADAM_EPS, ADAM_WD, ADAM_STEP = 0.001, 0.9, 0.999, 1e-08, 0.01, 10
GELU_C = math.sqrt(2.0 / math.pi)
GELU_A = 0.044715
VMEM_LIMIT = 56 * 1024 * 1024

NT_DIMS = (((1,), (1,)), ((), ()))
TN_DIMS = (((0,), (0,)), ((), ()))


def _params(*sem):
    return pltpu.CompilerParams(dimension_semantics=sem, vmem_limit_bytes=VMEM_LIMIT)


def _dot(a, b):
    return jnp.dot(a, b, preferred_element_type=F32)


def _dot_nt(a, b):
    return lax.dot_general(a, b, NT_DIMS, preferred_element_type=F32)


def _dot_tn(a, b):
    return lax.dot_general(a, b, TN_DIMS, preferred_element_type=F32)


def _split3(a):
    hi = a.astype(BF16)
    r1 = a - hi.astype(F32)
    mid = r1.astype(BF16)
    return hi, mid, (r1 - mid.astype(F32)).astype(BF16)


def _hdot(a, b):
    if a.dtype == BF16:
        return sum(_dot(a, piece) for piece in _split3(b))
    return sum(_dot(piece, b) for piece in _split3(a))


def _headsum(q, e):
    return sum(_dot_nt(piece, e) for piece in _split3(q))


def _colsum(v):
    return jnp.sum(v, axis=0, keepdims=True)


def _sigmoid(v):
    return 1.0 / (1.0 + jnp.exp(-v))


def _softplus(v):
    e = jnp.exp(-jnp.abs(v))
    return jnp.maximum(v, 0.0) + jnp.where(e < 1e-4, e * (1.0 - 0.5 * e), jnp.log(1.0 + e))


def _rms_r(x):
    return lax.rsqrt(jnp.mean(x * x, axis=-1, keepdims=True) + EPS)


def _rms_bwd(x, r, g, dn):
    xhat = x * r
    gd = dn * g
    dx = r * (gd - xhat * jnp.mean(gd * xhat, axis=-1, keepdims=True))
    return dx, _colsum(dn * xhat)


def _gelu(v):
    return 0.5 * v * (1.0 + jnp.tanh(GELU_C * (v + GELU_A * v * v * v)))


def _gelu_grad(v):
    th = jnp.tanh(GELU_C * (v + GELU_A * v * v * v))
    return 0.5 * (1.0 + th) + 0.5 * v * (1.0 - th * th) * GELU_C * (1.0 + 3.0 * GELU_A * v * v)


def _tile(t, want):
    return min(t, want)


class _Exchange:
    def __init__(self, srcs, *, scatter, layer, into=None):
        self.srcs, self.scatter, self.layer = list(srcs), scatter, layer
        self.into = None if into is None else list(into)
        n = len(self.srcs)
        self.args = self.srcs + (self.into or [])
        self.in_specs = [pl.BlockSpec(memory_space=pl.ANY)] * len(self.args)
        if scatter:
            self.out_shape = [jax.ShapeDtypeStruct((DEPTH,) + s.shape, s.dtype) for s in self.srcs]
        else:
            self.out_shape = [jax.ShapeDtypeStruct((N_DEV,) + s.shape[1:], s.dtype) for s in self.srcs]
        self.out_specs = [pl.BlockSpec(memory_space=pl.ANY)] * n
        self.scratch = [pltpu.SemaphoreType.DMA((n, N_DEV - 1)), pltpu.SemaphoreType.DMA((n, N_DEV - 1)),
                        pltpu.SemaphoreType.DMA((n,))]

    def aliases(self, n_in_before, n_out_before):
        if self.into is None:
            return {}
        n = len(self.srcs)
        return {n_in_before + n + a: n_out_before + a for a in range(n)}

    def ops(self, in_refs, out_refs, sems):
        send_sems, recv_sems, local_sems = sems
        n = len(self.srcs)

        def copies():
            x, y, c = lax.axis_index("x"), lax.axis_index("y"), lax.axis_index("c")
            me = 4 * x + 2 * y + c

            def block(a, idx):
                return in_refs[a].at[idx] if self.scatter else in_refs[a].at[self.layer]

            def slot(a, idx):
                return out_refs[a].at[self.layer].at[idx] if self.scatter else out_refs[a].at[idx]

            local = [pltpu.make_async_copy(block(a, me), slot(a, me), local_sems.at[a]) for a in range(n)]
            sends, recvs = [], []
            for k in range(1, N_DEV):
                px = 1 - x if k & 4 else x
                py = 1 - y if k & 2 else y
                pc = 1 - c if k & 1 else c
                peer = 4 * px + 2 * py + pc
                for a in range(n):
                    kw = dict(send_sem=send_sems.at[a, k - 1], recv_sem=recv_sems.at[a, k - 1], device_id=(px, py, pc),
                              device_id_type=pl.DeviceIdType.MESH)
                    sends.append(pltpu.make_async_remote_copy(src_ref=block(a, peer), dst_ref=slot(a, me), **kw))
                    recvs.append(pltpu.make_async_remote_copy(src_ref=block(a, peer), dst_ref=slot(a, peer), **kw))
            return local, sends, recvs

        def start():
            local, sends, _ = copies()
            for cp in local + sends:
                cp.start()

        def wait():
            local, sends, recvs = copies()
            for send, recv in zip(sends, recvs):
                send.wait_send()
                recv.wait_recv()
            for cp in local:
                cp.wait()

        return start, wait


def _exchange_call(ex, name):
    n_in, n = len(ex.args), len(ex.srcs)

    def body(*refs):
        start, wait = ex.ops(refs[:n_in], refs[n_in:n_in + n], refs[n_in + n:])
        start()
        wait()

    return pl.pallas_call(
        body, name=name, in_specs=ex.in_specs, out_specs=ex.out_specs, out_shape=ex.out_shape,
        scratch_shapes=ex.scratch, input_output_aliases=ex.aliases(0, 0))(*ex.args)


def _split_refs(refs, counts):
    out, k = [], 0
    for cnt in counts:
        out.append(refs[k:k + cnt])
        k += cnt
    return out


def _ex_parts(ex):
    if ex is None:
        return [], [], [], [], [], (0, 0, 0)
    return ex.args, ex.in_specs, ex.out_shape, ex.out_specs, ex.scratch, (len(ex.args), len(ex.srcs), 3)


def _norm_matmul(h, w, g=None, *, tt, tn, name):
    t, k = h.shape
    n = w.shape[1]
    tt, tn = _tile(t, tt), _tile(n, tn)
    normed = g is not None

    def body(*refs):
        if normed:
            h_ref, g_ref, w_ref, o_ref = refs
            x = h_ref[...]
            xn = (x * _rms_r(x) * g_ref[...]).astype(BF16)
        else:
            h_ref, w_ref, o_ref = refs
            xn = h_ref[...].astype(BF16)
        o_ref[...] = _dot(xn, w_ref[...])

    in_specs = [pl.BlockSpec((tt, k), lambda j, i: (i, 0))]
    args = [h]
    if normed:
        in_specs.append(pl.BlockSpec((1, k), lambda j, i: (0, 0)))
        args.append(g)
    in_specs.append(pl.BlockSpec((k, tn), lambda j, i: (0, j)))
    args.append(w)
    return pl.pallas_call(
        body, name=name, grid=(n // tn, t // tt), in_specs=in_specs,
        out_specs=pl.BlockSpec((tt, tn), lambda j, i: (i, j)), out_shape=jax.ShapeDtypeStruct((t, n), F32),
        compiler_params=_params("arbitrary", "arbitrary"))(*args)


def _matmul_tn(a, b, *, tm, tn, tk, name):
    t, m = a.shape
    n = b.shape[1]
    tm, tn, tk = _tile(m, tm), _tile(n, tn), _tile(t, tk)

    def body(a_ref, b_ref, o_ref):
        @pl.when(pl.program_id(2) == 0)
        def _():
            o_ref[...] = jnp.zeros_like(o_ref)

        o_ref[...] += _dot_tn(a_ref[...].astype(BF16), b_ref[...].astype(BF16))

    return pl.pallas_call(
        body, name=name, grid=(m // tm, n // tn, t // tk),
        in_specs=[pl.BlockSpec((tk, tm), lambda i, j, kk: (kk, i)), pl.BlockSpec((tk, tn), lambda i, j, kk: (kk, j))],
        out_specs=pl.BlockSpec((tm, tn), lambda i, j, kk: (i, j)),
        out_shape=jax.ShapeDtypeStruct((m, n), F32),
        compiler_params=_params("arbitrary", "arbitrary", "arbitrary"))(a, b)


def _matmul_rmsbwd(a, wt, x, g, dh, *, tt, tk, name):
    t, k = a.shape
    d = wt.shape[1]
    tt, tk = _tile(t, tt), _tile(k, tk)
    nk = k // tk

    def body(a_ref, w_ref, x_ref, g_ref, dh_ref, o_ref, dg_ref, *scratch):
        i, kk = pl.program_id(0), pl.program_id(1)

        @pl.when((i == 0) & (kk == 0))
        def _():
            dg_ref[...] = jnp.zeros_like(dg_ref)

        def finish(dn):
            xv = x_ref[...]
            dx, dg = _rms_bwd(xv, _rms_r(xv), g_ref[...], dn)
            o_ref[...] = dh_ref[...] + dx
            dg_ref[...] += dg

        if nk == 1:
            finish(_dot(a_ref[...], w_ref[...]))
        else:
            acc, = scratch

            @pl.when(kk == 0)
            def _():
                acc[...] = jnp.zeros_like(acc)

            acc[...] += _dot(a_ref[...], w_ref[...])
            pl.when(kk == nk - 1)(lambda: finish(acc[...]))

    return pl.pallas_call(
        body, name=name, grid=(t // tt, nk),
        in_specs=[pl.BlockSpec((tt, tk), lambda i, kk: (i, kk)), pl.BlockSpec((tk, d), lambda i, kk: (kk, 0)),
                  pl.BlockSpec((tt, d), lambda i, kk: (i, 0)), pl.BlockSpec((1, d), lambda i, kk: (0, 0)),
                  pl.BlockSpec((tt, d), lambda i, kk: (i, 0))],
        out_specs=[pl.BlockSpec((tt, d), lambda i, kk: (i, 0)), pl.BlockSpec((1, d), lambda i, kk: (0, 0))],
        out_shape=[jax.ShapeDtypeStruct((t, d), F32), jax.ShapeDtypeStruct((1, d), F32)],
        scratch_shapes=[] if nk == 1 else [pltpu.VMEM((tt, d), F32)],
        compiler_params=_params("arbitrary", "arbitrary"))(a, wt, x, g, dh)


def _ssd_tile_prologue(i_is_first, xbc_ref, halo_ref, dt_ref, cw_ref, cb_ref, dtb_ref, alog_ref, e_ref, buf, xc_scr,
                       xa_scr, a_scr, dte_scr, x_scr, ts):
    buf[0:SUBLANES, :] = jnp.where(i_is_first, 0.0, halo_ref[...])
    buf[SUBLANES:SUBLANES + ts, :] = xbc_ref[...]
    cw = cw_ref[...]
    xc = cb_ref[...]
    for k in range(SSD_CONV):
        off = SUBLANES - (SSD_CONV - 1) + k
        xc = xc + cw[k:k + 1, :] * buf[off:off + ts, :]
    if xc_scr is not None:
        xc_scr[...] = xc
    xa_scr[...] = xc * _sigmoid(xc)
    dt = _softplus(dt_ref[...] + dtb_ref[...])
    a_neg = -jnp.exp(alog_ref[...])
    a_scr[...] = dt * a_neg
    dte = _hdot(dt, e_ref[...])
    dte_scr[...] = dte
    x_scr[...] = xa_scr[:, 0:SSD_WIDTH] * dte
    return dt, a_neg


def _chunk_decays(a_c, tril, e):
    cs = _hdot(tril, a_c)
    cs_t = cs.T
    cs_e = _hdot(cs, e)
    last_e = cs_e[CHUNK - 1:CHUNK, :]
    return cs, cs_t, cs_e, last_e


def _ssd_fwd(proj, cw, cb, dtb, alog, dexp, ng, tril, e, *, ts, name, ex=None):
    t = proj.shape[0]
    ts = _tile(t, ts)
    nch = ts // CHUNK
    hb = ts // SUBLANES
    nt = t // ts
    ex_args, ex_in_specs, ex_out_shape, ex_out_specs, ex_scratch, ex_counts = _ex_parts(ex)

    def body(*refs):
        ((xbc_ref, halo_ref, z_ref, dt_ref, cw_ref, cb_ref, dtb_ref, alog_ref, dexp_ref, ng_ref, tril_ref, e_ref),
         ex_in, (y_ref, ypre_ref, st_ref), ex_out, (buf, xa_scr, a_scr, dte_scr, x_scr, ys_scr, hstate),
         ex_sems) = _split_refs(refs, (12, ex_counts[0], 3, ex_counts[1], 7, ex_counts[2]))
        i = pl.program_id(0)
        if ex is not None:
            ex_start, ex_wait = ex.ops(ex_in, ex_out, ex_sems)
            pl.when(i == 0)(ex_start)

        @pl.when(i == 0)
        def _():
            hstate[...] = jnp.zeros_like(hstate)

        _ssd_tile_prologue(i == 0, xbc_ref, halo_ref, dt_ref, cw_ref, cb_ref, dtb_ref, alog_ref, e_ref, buf, None,
                           xa_scr, a_scr, dte_scr, x_scr, ts)
        tril = tril_ref[...]
        e_mat = e_ref[...]
        causal = (lax.broadcasted_iota(jnp.int32, (CHUNK, CHUNK), 0)
                  >= lax.broadcasted_iota(jnp.int32, (CHUNK, CHUNK), 1))
        lane = lax.broadcasted_iota(jnp.int32, (CHUNK, LANES), 1)

        def chunk(c, carry):
            r0 = pl.multiple_of(c * CHUNK, CHUNK)
            rows = pl.ds(r0, CHUNK)
            cs, cs_t, cs_e, last_e = _chunk_decays(a_scr[rows, :], tril, e_mat)
            decay_e = jnp.exp(last_e - cs_e)
            ecs_e = jnp.exp(cs_e)
            xc = x_scr[rows, :]
            xb = xc.astype(BF16)
            xd = (xc * decay_e).astype(BF16)
            for g in range(SSD_GROUPS):
                bg = xa_scr[rows, SSD_WIDTH + g * SSD_STATE:SSD_WIDTH + (g + 1) * SSD_STATE].astype(BF16)
                cg = xa_scr[rows, SSD_WIDTH + (SSD_GROUPS + g) * SSD_STATE:
                            SSD_WIDTH + (SSD_GROUPS + g + 1) * SSD_STATE].astype(BF16)
                cbm = _dot_nt(cg, bg)
                for jj in range(2):
                    j = 2 * g + jj
                    cols = slice(j * LANES, (j + 1) * LANES)
                    xp = xb[:, cols]
                    ypair = jnp.zeros((CHUNK, LANES), F32)
                    for hh in range(2):
                        h = 2 * j + hh
                        seg = jnp.exp(jnp.where(causal, cs[:, h:h + 1] - cs_t[h:h + 1, :], -jnp.inf))
                        m = (cbm * seg).astype(BF16)
                        half = (lane < SSD_HEAD_DIM) if hh == 0 else (lane >= SSD_HEAD_DIM)
                        ypair = ypair + _dot(m, jnp.where(half, xp, jnp.zeros_like(xp)))
                    hp = hstate[j]
                    st_ref[c, j] = hp
                    ypair = ypair + _dot(cg, hp.astype(BF16)) * ecs_e[:, cols]
                    ys_scr[rows, cols] = ypair
                    hstate[j] = hp * jnp.exp(last_e[:, cols]) + _dot_tn(bg, xd[:, cols])
            return carry

        lax.fori_loop(0, nch, chunk, 0)
        ypre = ys_scr[...] + xa_scr[:, 0:SSD_WIDTH] * dexp_ref[...]
        ypre_ref[...] = ypre
        z = z_ref[...]
        yg = ypre * (z * _sigmoid(z))
        gw = SSD_WIDTH // SSD_GROUPS
        outs = []
        for g in range(SSD_GROUPS):
            v = yg[:, g * gw:(g + 1) * gw]
            outs.append(v * _rms_r(v))
        y_ref[...] = jnp.concatenate(outs, axis=1) * ng_ref[...]
        if ex is not None:
            pl.when(i == nt - 1)(ex_wait)

    full = lambda shape: pl.BlockSpec(shape, lambda i: tuple(0 for _ in shape))
    outs = pl.pallas_call(
        body, name=name, grid=(nt,),
        in_specs=[pl.BlockSpec((ts, SSD_XBC), lambda i: (i, COL_XBC // SSD_XBC)),
                  pl.BlockSpec((SUBLANES, SSD_XBC), lambda i: (jnp.maximum(i * hb - 1, 0), COL_XBC // SSD_XBC)),
                  pl.BlockSpec((ts, SSD_WIDTH), lambda i: (i, COL_Z // SSD_WIDTH)),
                  pl.BlockSpec((ts, LANES), lambda i: (i, COL_DT // LANES)),
                  full((SSD_CONV, SSD_XBC)), full((1, SSD_XBC)), full((1, LANES)), full((1, LANES)),
                  full((1, SSD_WIDTH)), full((1, SSD_WIDTH)), full((CHUNK, CHUNK)), full((LANES, SSD_WIDTH))]
        + ex_in_specs,
        out_specs=[pl.BlockSpec((ts, SSD_WIDTH), lambda i: (i, 0)), pl.BlockSpec((ts, SSD_WIDTH), lambda i: (i, 0)),
                   pl.BlockSpec((nch, N_PAIRS, SSD_STATE, LANES), lambda i: (i, 0, 0, 0))] + ex_out_specs,
        out_shape=[jax.ShapeDtypeStruct((t, SSD_WIDTH), F32), jax.ShapeDtypeStruct((t, SSD_WIDTH), F32),
                   jax.ShapeDtypeStruct((t // CHUNK, N_PAIRS, SSD_STATE, LANES), F32)] + ex_out_shape,
        scratch_shapes=[pltpu.VMEM((SUBLANES + ts, SSD_XBC), F32), pltpu.VMEM((ts, SSD_XBC), F32),
                        pltpu.VMEM((ts, LANES), F32), pltpu.VMEM((ts, SSD_WIDTH), F32),
                        pltpu.VMEM((ts, SSD_WIDTH), F32), pltpu.VMEM((ts, SSD_WIDTH), F32),
                        pltpu.VMEM((N_PAIRS, SSD_STATE, LANES), F32)] + ex_scratch,
        input_output_aliases={} if ex is None else ex.aliases(12, 3),
        compiler_params=_params("arbitrary"))(proj, proj, proj, proj, cw, cb, dtb, alog, dexp, ng, tril, e, *ex_args)
    return outs[0], outs[1], outs[2], outs[3:]


def _ssd_bwd(dymix, proj, ypre, states, du, cw, cb, dtb, alog, dexp, ng, tril, triu, e, *, ts, name, ex=None):
    t = proj.shape[0]
    ts = _tile(t, ts)
    nch = ts // CHUNK
    hb = ts // SUBLANES
    nt = t // ts
    ex_args, ex_in_specs, ex_out_shape, ex_out_specs, ex_scratch, ex_counts = _ex_parts(ex)

    def body(*refs):
        ((dy_ref, xbc_ref, halo_ref, z_ref, dt_ref, ypre_ref, st_ref, du_ref, cw_ref, cb_ref, dtb_ref, alog_ref,
          dexp_ref, ng_ref, tril_ref, triu_ref, e_ref), ex_in,
         (dproj_ref, dcw_ref, dcb_ref, ddtb_ref, dalog_ref, dd_ref, dng_ref), ex_out,
         (buf, xc_scr, xa_scr, a_scr, dte_scr, x_scr, dyp_scr, dxa_scr, dx_scr, dbuf, carry, gstate),
         ex_sems) = _split_refs(refs, (17, ex_counts[0], 7, ex_counts[1], 12, ex_counts[2]))
        i = pl.program_id(0)
        if ex is not None:
            ex_start, ex_wait = ex.ops(ex_in, ex_out, ex_sems)
            pl.when(i == 0)(ex_start)

        @pl.when(i == 0)
        def _():
            gstate[...] = jnp.zeros_like(gstate)
            carry[...] = jnp.zeros_like(carry)
            for ref in (dcw_ref, dcb_ref, ddtb_ref, dalog_ref, dd_ref, dng_ref):
                ref[...] = jnp.zeros_like(ref)

        dt, a_neg = _ssd_tile_prologue(i == nt - 1, xbc_ref, halo_ref, dt_ref, cw_ref, cb_ref, dtb_ref, alog_ref, e_ref,
                                       buf, xc_scr, xa_scr, a_scr, dte_scr, x_scr, ts)
        tril = tril_ref[...]
        triu = triu_ref[...]
        e_mat = e_ref[...]
        causal = (lax.broadcasted_iota(jnp.int32, (CHUNK, CHUNK), 0)
                  >= lax.broadcasted_iota(jnp.int32, (CHUNK, CHUNK), 1))
        lane = lax.broadcasted_iota(jnp.int32, (CHUNK, LANES), 1)
        sub = lax.broadcasted_iota(jnp.int32, (CHUNK, LANES), 0)

        z = z_ref[...]
        sig = _sigmoid(z)
        zs = z * sig
        ypre = ypre_ref[...]
        yg = ypre * zs
        dout = dy_ref[...]
        ngv = ng_ref[...]
        gw = SSD_WIDTH // SSD_GROUPS
        dyg_parts, dng_parts = [], []
        for g in range(SSD_GROUPS):
            cols = slice(g * gw, (g + 1) * gw)
            v = yg[:, cols]
            dx, dg = _rms_bwd(v, _rms_r(v), ngv[:, cols], dout[:, cols])
            dyg_parts.append(dx)
            dng_parts.append(dg)
        dyg = jnp.concatenate(dyg_parts, axis=1)
        dng_ref[...] += jnp.concatenate(dng_parts, axis=1)
        dyp = dyg * zs
        dyp_scr[...] = dyp
        dproj_ref[:, COL_Z:COL_Z + SSD_WIDTH] = dyg * ypre * (sig * (1.0 + z * (1.0 - sig)))
        dproj_ref[:, COL_U:COL_U + POOL_WIDTH] = du_ref[...]
        xs_all = xa_scr[:, 0:SSD_WIDTH]
        dd_ref[...] += _headsum(jnp.broadcast_to(_colsum(dyp * xs_all), (SUBLANES, SSD_WIDTH)), e_mat)[0:1, :]

        def chunk(k, carry_):
            c = nch - 1 - k
            r0 = pl.multiple_of(c * CHUNK, CHUNK)
            rows = pl.ds(r0, CHUNK)
            a_c = a_scr[rows, :]
            cs, cs_t, cs_e, last_e = _chunk_decays(a_c, tril, e_mat)
            decay_e = jnp.exp(last_e - cs_e)
            ecs_e = jnp.exp(cs_e)
            elast_e = jnp.exp(last_e)
            xc = x_scr[rows, :]
            xb = xc.astype(BF16)
            xd = (xc * decay_e).astype(BF16)
            dyc = dyp_scr[rows, :]
            dcs = jnp.zeros((CHUNK, LANES), F32)
            dcs_neg_t = jnp.zeros((LANES, CHUNK), F32)
            qoff, rin, ghrow = [], [], []
            for g in range(SSD_GROUPS):
                b_cols = slice(SSD_WIDTH + g * SSD_STATE, SSD_WIDTH + (g + 1) * SSD_STATE)
                c_cols = slice(SSD_WIDTH + (SSD_GROUPS + g) * SSD_STATE, SSD_WIDTH + (SSD_GROUPS + g + 1) * SSD_STATE)
                bg = xa_scr[rows, b_cols].astype(BF16)
                cg = xa_scr[rows, c_cols].astype(BF16)
                cbm = _dot_nt(cg, bg)
                dcb_m = jnp.zeros((CHUNK, CHUNK), F32)
                dbg = jnp.zeros((CHUNK, SSD_STATE), F32)
                dcg = jnp.zeros((CHUNK, SSD_STATE), F32)
                for jj in range(2):
                    j = 2 * g + jj
                    cols = slice(j * LANES, (j + 1) * LANES)
                    dyp_j = dyc[:, cols]
                    hp = st_ref[c, j]
                    hpb = hp.astype(BF16)
                    gt = gstate[j]
                    gtb = gt.astype(BF16)
                    ecs = ecs_e[:, cols]
                    yoff = _dot(cg, hpb) * ecs
                    dye = (dyp_j * ecs).astype(BF16)
                    dcg = dcg + _dot_nt(dye, hpb)
                    dht = _dot_tn(cg, dye)
                    qoff.append(dyp_j * yoff)
                    xg = _dot(bg, gtb)
                    dxp = xg * decay_e[:, cols]
                    rin.append(xg * xc[:, cols])
                    dbg = dbg + _dot_nt(xd[:, cols], gtb)
                    ghrow.append(_colsum(gt * hp) * elast_e[:, cols])
                    gstate[j] = dht + gt * elast_e[:, cols]
                    for hh in range(2):
                        h = 2 * j + hh
                        seg = jnp.exp(jnp.where(causal, cs[:, h:h + 1] - cs_t[h:h + 1, :], -jnp.inf))
                        m = cbm * seg
                        half = (lane < SSD_HEAD_DIM) if hh == 0 else (lane >= SSD_HEAD_DIM)
                        dym = jnp.where(half, dyp_j, 0.0).astype(BF16)
                        w = _dot_nt(dym, xb[:, cols])
                        pm = w * m
                        dcs = dcs + jnp.where(lane == h, jnp.sum(pm, axis=1, keepdims=True), 0.0)
                        dcs_neg_t = dcs_neg_t + jnp.where(sub == h, _colsum(pm), 0.0)
                        dcb_m = dcb_m + w * seg
                        dxp = dxp + _dot_tn(m.astype(BF16), dym)
                    dx_scr[:, cols] = dxp
                dcbb = dcb_m.astype(BF16)
                dxa_scr[rows, c_cols] = dcg + _dot(dcbb, bg)
                dxa_scr[rows, b_cols] = dbg + _dot_tn(dcbb, cg)
            decay_th = jnp.exp(cs[CHUNK - 1:CHUNK, :] - cs)
            rd = _headsum(jnp.concatenate(rin, axis=1), e_mat) * decay_th
            dcs = dcs - dcs_neg_t.T + _headsum(jnp.concatenate(qoff, axis=1), e_mat) - rd
            gh = _headsum(jnp.broadcast_to(jnp.concatenate(ghrow, axis=1), (SUBLANES, SSD_WIDTH)), e_mat)[0:1, :]
            dcs = dcs + jnp.where(sub == CHUNK - 1, _colsum(rd) + gh, 0.0)
            da = _hdot(triu, dcs)
            dx_all = dx_scr[...]
            xs = xa_scr[rows, 0:SSD_WIDTH]
            dt_c = _softplus(dt_ref[rows, :] + dtb_ref[...])
            ddt = da * a_neg + _headsum(dx_all * xs, e_mat)
            dalog_ref[...] += _colsum(da * dt_c) * a_neg
            ddtraw = ddt * _sigmoid(dt_ref[rows, :] + dtb_ref[...])
            dproj_ref[rows, COL_DT:COL_DT + LANES] = ddtraw
            ddtb_ref[...] += _colsum(ddtraw)
            dxa_scr[rows, 0:SSD_WIDTH] = dx_all * dte_scr[rows, :] + dyc * dexp_ref[...]
            return carry_

        lax.fori_loop(0, nch, chunk, 0)

        xcv = xc_scr[...]
        sgc = _sigmoid(xcv)
        dxc = dxa_scr[...] * (sgc * (1.0 + xcv * (1.0 - sgc)))
        dcb_ref[...] += _colsum(dxc)
        dbuf[0:ts, :] = dxc
        dbuf[ts:ts + SUBLANES, :] = carry[...]
        cwv = cw_ref[...]
        dxbc = jnp.zeros((ts, SSD_XBC), F32)
        dcw_rows = []
        for k in range(SSD_CONV):
            off = SUBLANES - (SSD_CONV - 1) + k
            dcw_rows.append(_colsum(dxc * buf[off:off + ts, :]))
            back = SSD_CONV - 1 - k
            dxbc = dxbc + cwv[k:k + 1, :] * dbuf[back:back + ts, :]
        dcw_ref[...] += jnp.concatenate(dcw_rows, axis=0)
        dproj_ref[:, COL_XBC:COL_XBC + SSD_XBC] = dxbc
        carry[...] = dxc[0:SUBLANES, :]
        if ex is not None:
            pl.when(i == nt - 1)(ex_wait)

    rev = lambda i: nt - 1 - i
    full = lambda shape: pl.BlockSpec(shape, lambda i: tuple(0 for _ in shape))
    outs = pl.pallas_call(
        body, name=name, grid=(nt,),
        in_specs=[pl.BlockSpec((ts, SSD_WIDTH), lambda i: (rev(i), 0)),
                  pl.BlockSpec((ts, SSD_XBC), lambda i: (rev(i), COL_XBC // SSD_XBC)),
                  pl.BlockSpec((SUBLANES, SSD_XBC), lambda i: (jnp.maximum(rev(i) * hb - 1, 0), COL_XBC // SSD_XBC)),
                  pl.BlockSpec((ts, SSD_WIDTH), lambda i: (rev(i), COL_Z // SSD_WIDTH)),
                  pl.BlockSpec((ts, LANES), lambda i: (rev(i), COL_DT // LANES)),
                  pl.BlockSpec((ts, SSD_WIDTH), lambda i: (rev(i), 0)),
                  pl.BlockSpec((nch, N_PAIRS, SSD_STATE, LANES), lambda i: (rev(i), 0, 0, 0)),
                  pl.BlockSpec((ts, POOL_WIDTH), lambda i: (rev(i), 0)),
                  full((SSD_CONV, SSD_XBC)), full((1, SSD_XBC)), full((1, LANES)), full((1, LANES)),
                  full((1, SSD_WIDTH)), full((1, SSD_WIDTH)), full((CHUNK, CHUNK)), full((CHUNK, CHUNK)),
                  full((LANES, SSD_WIDTH))] + ex_in_specs,
        out_specs=[pl.BlockSpec((ts, N_PROJ), lambda i: (rev(i), 0)),
                   full((SSD_CONV, SSD_XBC)), full((1, SSD_XBC)), full((1, LANES)), full((1, LANES)),
                   full((1, LANES)), full((1, SSD_WIDTH))] + ex_out_specs,
        out_shape=[jax.ShapeDtypeStruct((t, N_PROJ), F32),
                   jax.ShapeDtypeStruct((SSD_CONV, SSD_XBC), F32), jax.ShapeDtypeStruct((1, SSD_XBC), F32),
                   jax.ShapeDtypeStruct((1, LANES), F32), jax.ShapeDtypeStruct((1, LANES), F32),
                   jax.ShapeDtypeStruct((1, LANES), F32), jax.ShapeDtypeStruct((1, SSD_WIDTH), F32)] + ex_out_shape,
        scratch_shapes=[pltpu.VMEM((SUBLANES + ts, SSD_XBC), F32), pltpu.VMEM((ts, SSD_XBC), F32),
                        pltpu.VMEM((ts, SSD_XBC), F32), pltpu.VMEM((ts, LANES), F32),
                        pltpu.VMEM((ts, SSD_WIDTH), F32), pltpu.VMEM((ts, SSD_WIDTH), F32),
                        pltpu.VMEM((ts, SSD_WIDTH), F32), pltpu.VMEM((ts, SSD_XBC), F32),
                        pltpu.VMEM((CHUNK, SSD_WIDTH), F32), pltpu.VMEM((ts + SUBLANES, SSD_XBC), F32),
                        pltpu.VMEM((SUBLANES, SSD_XBC), F32), pltpu.VMEM((N_PAIRS, SSD_STATE, LANES), F32)]
        + ex_scratch,
        input_output_aliases={} if ex is None else ex.aliases(17, 7),
        compiler_params=_params("arbitrary"))(
            dymix, proj, proj, proj, proj, ypre, states, du, cw, cb, dtb, alog, dexp, ng, tril, triu, e, *ex_args)
    return outs[:7], outs[7:]


def _pooled(ubuf, u, pos, tt):
    out = []
    for gi, w in enumerate(POOL_WINDOWS):
        cols = slice(gi * POOL_GROUP, (gi + 1) * POOL_GROUP)
        acc = u[:, cols]
        for j in range(1, w):
            acc = acc + ubuf[POOL_HALO - j:POOL_HALO - j + tt, cols]
        out.append(acc / jnp.minimum(pos, float(w)) - u[:, cols])
    return out


def _mix_out(h, yssd, proj, pool_w, pool_scale, w_out, g_next, *, tt, name):
    t = h.shape[0]
    tt = _tile(t, tt)
    hb = tt // POOL_HALO

    def body(h_ref, ys_ref, u_ref, uh_ref, pw_ref, sc_ref, wo_ref, gn_ref, o_ref, ym_ref, n_ref, ubuf):
        i = pl.program_id(0)
        ubuf[0:POOL_HALO, :] = jnp.where(i == 0, 0.0, uh_ref[...])
        u = u_ref[...]
        ubuf[POOL_HALO:POOL_HALO + tt, :] = u
        pos = (i * tt + 1 + lax.broadcasted_iota(jnp.int32, (tt, 1), 0)).astype(F32)
        sc = sc_ref[...]
        parts = [ys_ref[...]]
        for gi, pooled in enumerate(_pooled(ubuf, u, pos, tt)):
            cols = slice(gi * POOL_GROUP, (gi + 1) * POOL_GROUP)
            parts.append(_dot(pooled.astype(BF16), pw_ref[gi]) * sc[:, cols])
        ymix = jnp.concatenate(parts, axis=1).astype(BF16)
        ym_ref[...] = ymix
        h2 = h_ref[...] + _dot(ymix, wo_ref[...])
        o_ref[...] = h2
        n_ref[...] = (h2 * _rms_r(h2) * gn_ref[...]).astype(BF16)

    full = lambda shape: pl.BlockSpec(shape, lambda i: tuple(0 for _ in shape))
    return pl.pallas_call(
        body, name=name, grid=(t // tt,),
        in_specs=[pl.BlockSpec((tt, D_MODEL), lambda i: (i, 0)), pl.BlockSpec((tt, SSD_WIDTH), lambda i: (i, 0)),
                  pl.BlockSpec((tt, POOL_WIDTH), lambda i: (i, COL_U // POOL_WIDTH)),
                  pl.BlockSpec((POOL_HALO, POOL_WIDTH), lambda i: (jnp.maximum(i * hb - 1, 0), COL_U // POOL_WIDTH)),
                  full((len(POOL_WINDOWS), POOL_GROUP, POOL_GROUP)), full((1, POOL_WIDTH)),
                  full((D_MODEL, D_MODEL)), full((1, D_MODEL))],
        out_specs=[pl.BlockSpec((tt, D_MODEL), lambda i: (i, 0))] * 3,
        out_shape=[jax.ShapeDtypeStruct((t, D_MODEL), F32), jax.ShapeDtypeStruct((t, D_MODEL), BF16),
                   jax.ShapeDtypeStruct((t, D_MODEL), BF16)],
        scratch_shapes=[pltpu.VMEM((POOL_HALO + tt, POOL_WIDTH), F32)],
        compiler_params=_params("arbitrary"))(h, yssd, proj, proj, pool_w, pool_scale, w_out, g_next)


def _out_bwd(dh, ymix, w_out_t, *, tt, name):
    t = dh.shape[0]
    tt = _tile(t, tt)

    def body(dh_ref, ym_ref, wt_ref, dym_ref, dw_ref):
        @pl.when(pl.program_id(0) == 0)
        def _():
            dw_ref[...] = jnp.zeros_like(dw_ref)

        dhb = dh_ref[...].astype(BF16)
        dym_ref[...] = _dot(dhb, wt_ref[...])
        dw_ref[...] += _dot_tn(ym_ref[...], dhb)

    return pl.pallas_call(
        body, name=name, grid=(t // tt,),
        in_specs=[pl.BlockSpec((tt, D_MODEL), lambda i: (i, 0)), pl.BlockSpec((tt, D_MODEL), lambda i: (i, 0)),
                  pl.BlockSpec((D_MODEL, D_MODEL), lambda i: (0, 0))],
        out_specs=[pl.BlockSpec((tt, D_MODEL), lambda i: (i, 0)), pl.BlockSpec((D_MODEL, D_MODEL), lambda i: (0, 0))],
        out_shape=[jax.ShapeDtypeStruct((t, D_MODEL), F32), jax.ShapeDtypeStruct((D_MODEL, D_MODEL), F32)],
        compiler_params=_params("arbitrary"))(dh, ymix, w_out_t)


def _pool_bwd(dymix, proj, pool_w, pool_w_t, pool_scale, *, tt, name):
    t = proj.shape[0]
    tt = _tile(t, tt)
    hb = tt // POOL_HALO
    nt = t // tt
    ng = len(POOL_WINDOWS)

    def body(dy_ref, dyh_ref, u_ref, uh_ref, pw_ref, pwt_ref, sc_ref, du_ref, dpw_ref, dsc_ref, ubuf, dbuf):
        i = pl.program_id(0)

        @pl.when(i == 0)
        def _():
            dpw_ref[...] = jnp.zeros_like(dpw_ref)
            dsc_ref[...] = jnp.zeros_like(dsc_ref)

        ubuf[0:POOL_HALO, :] = jnp.where(i == 0, 0.0, uh_ref[...])
        u = u_ref[...]
        ubuf[POOL_HALO:POOL_HALO + tt, :] = u
        pos = (i * tt + 1 + lax.broadcasted_iota(jnp.int32, (tt, 1), 0)).astype(F32)
        sc = sc_ref[...]
        dy = dy_ref[...]
        dyh = jnp.where(i == nt - 1, 0.0, dyh_ref[...])
        dsc_parts, du_parts = [], []
        for gi, pooled in enumerate(_pooled(ubuf, u, pos, tt)):
            w = POOL_WINDOWS[gi]
            cols = slice(gi * POOL_GROUP, (gi + 1) * POOL_GROUP)
            pb = pooled.astype(BF16)
            dsc_parts.append(_colsum(dy[:, cols] * _dot(pb, pw_ref[gi])))
            dmx = (dy[:, cols] * sc[:, cols]).astype(BF16)
            dpw_ref[gi] += _dot_tn(pb, dmx)
            dpool = _dot(dmx, pwt_ref[gi])
            dpool_h = _dot((dyh[:, cols] * sc[:, cols]).astype(BF16), pwt_ref[gi])
            dbuf[0:tt, cols] = dpool / jnp.minimum(pos, float(w))
            dbuf[tt:tt + POOL_HALO, cols] = dpool_h / float(w)
            acc = -dpool
            for j in range(w):
                acc = acc + dbuf[j:j + tt, cols]
            du_parts.append(acc)
        du_ref[...] = jnp.concatenate(du_parts, axis=1)
        dsc_ref[...] += jnp.concatenate(dsc_parts, axis=1)

    full = lambda shape: pl.BlockSpec(shape, lambda i: tuple(0 for _ in shape))
    ucol = COL_U // POOL_WIDTH
    return pl.pallas_call(
        body, name=name, grid=(nt,),
        in_specs=[pl.BlockSpec((tt, POOL_WIDTH), lambda i: (i, 1)),
                  pl.BlockSpec((POOL_HALO, POOL_WIDTH), lambda i: (jnp.minimum((i + 1) * hb, t // POOL_HALO - 1), 1)),
                  pl.BlockSpec((tt, POOL_WIDTH), lambda i: (i, ucol)),
                  pl.BlockSpec((POOL_HALO, POOL_WIDTH), lambda i: (jnp.maximum(i * hb - 1, 0), ucol)),
                  full((ng, POOL_GROUP, POOL_GROUP)), full((ng, POOL_GROUP, POOL_GROUP)), full((1, POOL_WIDTH))],
        out_specs=[pl.BlockSpec((tt, POOL_WIDTH), lambda i: (i, 0)), full((ng, POOL_GROUP, POOL_GROUP)),
                   full((1, POOL_WIDTH))],
        out_shape=[jax.ShapeDtypeStruct((t, POOL_WIDTH), F32), jax.ShapeDtypeStruct((ng, POOL_GROUP, POOL_GROUP), F32),
                   jax.ShapeDtypeStruct((1, POOL_WIDTH), F32)],
        scratch_shapes=[pltpu.VMEM((POOL_HALO + tt, POOL_WIDTH), F32), pltpu.VMEM((tt + POOL_HALO, POOL_WIDTH), F32)],
        compiler_params=_params("arbitrary"))(dymix, dymix, proj, proj, pool_w, pool_w_t, pool_scale)


def _in_bwd(dproj, h, g, w_in_t, dh, *, tt, name):
    t = h.shape[0]
    tt = _tile(t, tt)

    def body(dp_ref, h_ref, g_ref, wt_ref, dh_ref, o_ref, dw_ref, dg_ref):
        @pl.when(pl.program_id(0) == 0)
        def _():
            dw_ref[...] = jnp.zeros_like(dw_ref)
            dg_ref[...] = jnp.zeros_like(dg_ref)

        x = h_ref[...]
        r = _rms_r(x)
        gv = g_ref[...]
        dpb = dp_ref[...].astype(BF16)
        dw_ref[...] += _dot_tn((x * r * gv).astype(BF16), dpb)
        dx, dg = _rms_bwd(x, r, gv, _dot(dpb, wt_ref[...]))
        o_ref[...] = dh_ref[...] + dx
        dg_ref[...] += dg

    full = lambda shape: pl.BlockSpec(shape, lambda i: tuple(0 for _ in shape))
    row = lambda n: pl.BlockSpec((tt, n), lambda i: (i, 0))
    return pl.pallas_call(
        body, name=name, grid=(t // tt,),
        in_specs=[row(N_PROJ), row(D_MODEL), full((1, D_MODEL)), full((N_PROJ, D_MODEL)), row(D_MODEL)],
        out_specs=[row(D_MODEL), full((D_MODEL, N_PROJ)), full((1, D_MODEL))],
        out_shape=[jax.ShapeDtypeStruct((t, D_MODEL), F32), jax.ShapeDtypeStruct((D_MODEL, N_PROJ), F32),
                   jax.ShapeDtypeStruct((1, D_MODEL), F32)],
        compiler_params=_params("arbitrary"))(dproj, h, g, w_in_t, dh)


FFN_COLS = 256
N_SLABS = D_FF // FFN_COLS
FFN_ROWS = 16


def _part8(v):
    acc = v[0:SUBLANES]
    for r in range(SUBLANES, FFN_ROWS, SUBLANES):
        acc = acc + v[r:r + SUBLANES]
    return acc


def _ffn_down(h, up, cw, cb, w_down, *, tt, name, ex=None):
    t = h.shape[0]
    tt = _tile(t, tt)
    hb = tt // SUBLANES
    nt = t // tt
    ex_args, ex_in_specs, ex_out_shape, ex_out_specs, ex_scratch, ex_counts = _ex_parts(ex)

    def body(*refs):
        (h_ref, up_ref, uh_ref, cw_ref, cb_ref, wd_ref), ex_in, (o_ref, act_ref, pre_ref), ex_out, (buf,), ex_sems = (
            _split_refs(refs, (6, ex_counts[0], 3, ex_counts[1], 1, ex_counts[2])))
        i = pl.program_id(0)
        if ex is not None:
            ex_start, ex_wait = ex.ops(ex_in, ex_out, ex_sems)
            pl.when(i == 0)(ex_start)
        buf[0:SUBLANES, :] = jnp.where(i == 0, 0.0, uh_ref[...])
        buf[SUBLANES:SUBLANES + tt, :] = up_ref[...]

        def conv(cols):
            acc = cb_ref[:, cols]
            for k in range(FFN_CONV):
                off = SUBLANES - (FFN_CONV - 1) + k
                acc = acc + cw_ref[k:k + 1, cols] * buf[off:off + tt, cols]
            return acc

        out = h_ref[...]
        for s in range(N_SLABS):
            gcols = slice(s * FFN_COLS, (s + 1) * FFN_COLS)
            vcols = slice(D_FF + s * FFN_COLS, D_FF + (s + 1) * FFN_COLS)
            gate, val = conv(gcols), conv(vcols)
            pre_ref[:, gcols] = gate
            pre_ref[:, vcols] = val
            act = (_gelu(gate) * val).astype(BF16)
            act_ref[:, s * FFN_COLS:(s + 1) * FFN_COLS] = act
            out = out + _dot(act, wd_ref[s * FFN_COLS:(s + 1) * FFN_COLS, :])
        o_ref[...] = out
        if ex is not None:
            pl.when(i == nt - 1)(ex_wait)

    full = lambda shape: pl.BlockSpec(shape, lambda i: tuple(0 for _ in shape))
    outs = pl.pallas_call(
        body, name=name, grid=(nt,),
        in_specs=[pl.BlockSpec((tt, D_MODEL), lambda i: (i, 0)), pl.BlockSpec((tt, D_UP), lambda i: (i, 0)),
                  pl.BlockSpec((SUBLANES, D_UP), lambda i: (jnp.maximum(i * hb - 1, 0), 0)),
                  full((FFN_CONV, D_UP)), full((1, D_UP)), full((D_FF, D_MODEL))] + ex_in_specs,
        out_specs=[pl.BlockSpec((tt, D_MODEL), lambda i: (i, 0)), pl.BlockSpec((tt, D_FF), lambda i: (i, 0)),
                   pl.BlockSpec((tt, D_UP), lambda i: (i, 0))] + ex_out_specs,
        out_shape=[jax.ShapeDtypeStruct((t, D_MODEL), F32), jax.ShapeDtypeStruct((t, D_FF), BF16),
                   jax.ShapeDtypeStruct((t, D_UP), F32)] + ex_out_shape,
        scratch_shapes=[pltpu.VMEM((SUBLANES + tt, D_UP), F32)] + ex_scratch,
        input_output_aliases={} if ex is None else ex.aliases(6, 3),
        compiler_params=_params("arbitrary"))(h, up, up, cw, cb, w_down, *ex_args)
    return outs[0], outs[1], outs[2], outs[3:]


def _ffn_act_bwd(up, pre, dact, cw, *, tt, name, ex=None):
    t = up.shape[0]
    tt = _tile(t, tt)
    nt = t // tt
    nb = tt // FFN_ROWS
    ex_args, ex_in_specs, ex_out_shape, ex_out_specs, ex_scratch, ex_counts = _ex_parts(ex)

    def body(*refs):
        (up_ref, pre_ref, da_ref, cw_ref), ex_in, (dup_ref, dcw_ref, dcb_ref), ex_out, (carry,), ex_sems = (
            _split_refs(refs, (4, ex_counts[0], 3, ex_counts[1], 1, ex_counts[2])))
        i = pl.program_id(0)
        if ex is not None:
            ex_start, ex_wait = ex.ops(ex_in, ex_out, ex_sems)
            pl.when(i == 0)(ex_start)

        @pl.when(i == 0)
        def _():
            dcw_ref[...] = jnp.zeros_like(dcw_ref)
            dcb_ref[...] = jnp.zeros_like(dcb_ref)
            carry[...] = jnp.zeros_like(carry)

        for s in range(N_SLABS):
            gcols = slice(s * FFN_COLS, (s + 1) * FFN_COLS)
            vcols = slice(D_FF + s * FFN_COLS, D_FF + (s + 1) * FFN_COLS)
            wg, wv = cw_ref[:, gcols], cw_ref[:, vcols]

            def blk(j, c):
                nxt_g, nxt_v, sums = c
                rows = pl.ds(pl.multiple_of((nb - 1 - j) * FFN_ROWS, FFN_ROWS), FFN_ROWS)
                r0 = rows.start
                xg, xv = up_ref[rows, gcols], up_ref[rows, vcols]
                gate, val = pre_ref[rows, gcols], pre_ref[rows, vcols]
                gelu, dgelu = _gelu(gate), _gelu_grad(gate)
                da = da_ref[rows, gcols]
                heads, new = [], []
                for dp, nxt, x, w, cols in ((da * val * dgelu, nxt_g, xg, wg, gcols), (da * gelu, nxt_v, xv, wv, vcols)):
                    de = jnp.concatenate([dp, nxt], axis=0)
                    shifted = [de[FFN_CONV - 1 - k:FFN_CONV - 1 - k + FFN_ROWS] for k in range(FFN_CONV - 1)] + [dp]
                    dup = w[0:1] * shifted[0]
                    for k in range(1, FFN_CONV):
                        dup = dup + w[k:k + 1] * shifted[k]
                    dup_ref[pl.ds(r0, FFN_ROWS), cols] = dup.astype(BF16)
                    heads.append(dp[0:SUBLANES])
                    new += [_part8(dp)] + [_part8(sh * x) for sh in shifted]
                return heads[0], heads[1], tuple(a + v for a, v in zip(sums, new))

            zero = jnp.zeros((SUBLANES, FFN_COLS), F32)
            nxt_g, nxt_v, sums = lax.fori_loop(
                0, nb, blk, (carry[:, gcols], carry[:, vcols], (zero,) * (2 * (1 + FFN_CONV))))
            carry[:, gcols] = nxt_g
            carry[:, vcols] = nxt_v
            for half, cols in enumerate((gcols, vcols)):
                part = sums[half * (1 + FFN_CONV):(half + 1) * (1 + FFN_CONV)]
                dcb_ref[:, cols] += _colsum(part[0])
                dcw_ref[:, cols] += jnp.concatenate([_colsum(v) for v in part[1:]], axis=0)
        if ex is not None:
            pl.when(i == nt - 1)(ex_wait)

    rev = lambda i: nt - 1 - i
    full = lambda shape: pl.BlockSpec(shape, lambda i: tuple(0 for _ in shape))
    outs = pl.pallas_call(
        body, name=name, grid=(nt,),
        in_specs=[pl.BlockSpec((tt, D_UP), lambda i: (rev(i), 0)), pl.BlockSpec((tt, D_UP), lambda i: (rev(i), 0)),
                  pl.BlockSpec((tt, D_FF), lambda i: (rev(i), 0)), full((FFN_CONV, D_UP))] + ex_in_specs,
        out_specs=[pl.BlockSpec((tt, D_UP), lambda i: (rev(i), 0)), full((FFN_CONV, D_UP)), full((1, D_UP))]
        + ex_out_specs,
        out_shape=[jax.ShapeDtypeStruct((t, D_UP), BF16), jax.ShapeDtypeStruct((FFN_CONV, D_UP), F32),
                   jax.ShapeDtypeStruct((1, D_UP), F32)] + ex_out_shape,
        scratch_shapes=[pltpu.VMEM((SUBLANES, D_UP), F32)] + ex_scratch,
        input_output_aliases={} if ex is None else ex.aliases(4, 3),
        compiler_params=_params("arbitrary"))(up, pre, dact, cw, *ex_args)
    return outs[0], outs[1], outs[2], outs[3:]


def _ple_fwd(h, p, g, w_gate, w_proj, *, tt, name):
    t = h.shape[0]
    tt = _tile(t, tt)

    def body(h_ref, p_ref, g_ref, wg_ref, wp_ref, o_ref):
        x = h_ref[...]
        n = (x * _rms_r(x) * g_ref[...]).astype(BF16)
        gate = _sigmoid(_dot(n, wg_ref[...]))
        o_ref[...] = x + _dot(p_ref[...].astype(BF16), wp_ref[...]) * gate

    full = lambda shape: pl.BlockSpec(shape, lambda i: tuple(0 for _ in shape))
    return pl.pallas_call(
        body, name=name, grid=(t // tt,),
        in_specs=[pl.BlockSpec((tt, D_MODEL), lambda i: (i, 0)), pl.BlockSpec((tt, D_PLE), lambda i: (i, 0)),
                  full((1, D_MODEL)), full((D_MODEL, D_MODEL)), full((D_PLE, D_MODEL))],
        out_specs=pl.BlockSpec((tt, D_MODEL), lambda i: (i, 0)),
        out_shape=jax.ShapeDtypeStruct((t, D_MODEL), F32),
        compiler_params=_params("arbitrary"))(h, p, g, w_gate, w_proj)


def _ple_bwd(dh, h, p, g, w_gate, w_gate_t, w_proj, *, tt, name):
    t = h.shape[0]
    tt = _tile(t, tt)

    def body(dh_ref, h_ref, p_ref, g_ref, wg_ref, wgt_ref, wp_ref, o_ref, dwg_ref, dwp_ref, dg_ref):
        @pl.when(pl.program_id(0) == 0)
        def _():
            dwg_ref[...] = jnp.zeros_like(dwg_ref)
            dwp_ref[...] = jnp.zeros_like(dwp_ref)
            dg_ref[...] = jnp.zeros_like(dg_ref)

        x = h_ref[...]
        r = _rms_r(x)
        gv = g_ref[...]
        n = (x * r * gv).astype(BF16)
        gate = _sigmoid(_dot(n, wg_ref[...]))
        pb = p_ref[...].astype(BF16)
        pe = _dot(pb, wp_ref[...])
        dhv = dh_ref[...]
        dwp_ref[...] += _dot_tn(pb, (dhv * gate).astype(BF16))
        ds = (dhv * pe * gate * (1.0 - gate)).astype(BF16)
        dwg_ref[...] += _dot_tn(n, ds)
        dx, dg = _rms_bwd(x, r, gv, _dot(ds, wgt_ref[...]))
        o_ref[...] = dhv + dx
        dg_ref[...] += dg

    full = lambda shape: pl.BlockSpec(shape, lambda i: tuple(0 for _ in shape))
    row = lambda n: pl.BlockSpec((tt, n), lambda i: (i, 0))
    return pl.pallas_call(
        body, name=name, grid=(t // tt,),
        in_specs=[row(D_MODEL), row(D_MODEL), row(D_PLE), full((1, D_MODEL)), full((D_MODEL, D_MODEL)),
                  full((D_MODEL, D_MODEL)), full((D_PLE, D_MODEL))],
        out_specs=[row(D_MODEL), full((D_MODEL, D_MODEL)), full((D_PLE, D_MODEL)), full((1, D_MODEL))],
        out_shape=[jax.ShapeDtypeStruct((t, D_MODEL), F32), jax.ShapeDtypeStruct((D_MODEL, D_MODEL), F32),
                   jax.ShapeDtypeStruct((D_PLE, D_MODEL), F32), jax.ShapeDtypeStruct((1, D_MODEL), F32)],
        compiler_params=_params("arbitrary"))(dh, h, p, g, w_gate, w_gate_t, w_proj)


def _loss_head(h, g, target, *, tt, name):
    t = h.shape[0]
    tt = _tile(t, tt)

    def body(h_ref, g_ref, tg_ref, dh_ref, loss_ref, dg_ref):
        @pl.when(pl.program_id(0) == 0)
        def _():
            loss_ref[...] = jnp.zeros_like(loss_ref)
            dg_ref[...] = jnp.zeros_like(dg_ref)

        x = h_ref[...]
        r = _rms_r(x)
        gv = g_ref[...]
        diff = x * r * gv - tg_ref[...]
        loss_ref[...] += 0.5 * jnp.sum(jnp.mean(diff * diff, axis=-1, keepdims=True), axis=0, keepdims=True)
        dx, dg = _rms_bwd(x, r, gv, diff * (1.0 / D_MODEL))
        dh_ref[...] = dx
        dg_ref[...] += dg

    return pl.pallas_call(
        body, name=name, grid=(t // tt,),
        in_specs=[pl.BlockSpec((tt, D_MODEL), lambda i: (i, 0)), pl.BlockSpec((1, D_MODEL), lambda i: (0, 0)),
                  pl.BlockSpec((tt, D_MODEL), lambda i: (i, 0))],
        out_specs=[pl.BlockSpec((tt, D_MODEL), lambda i: (i, 0)), pl.BlockSpec((SUBLANES, LANES), lambda i: (0, 0)),
                   pl.BlockSpec((1, D_MODEL), lambda i: (0, 0))],
        out_shape=[jax.ShapeDtypeStruct((t, D_MODEL), F32), jax.ShapeDtypeStruct((SUBLANES, LANES), F32),
                   jax.ShapeDtypeStruct((1, D_MODEL), F32)],
        compiler_params=_params("arbitrary"))(h, g, target)


ADAM_BLOCK_BYTES = 4 * 1024 * 1024


def _adam_rows(rows, cols):
    lanes = -(-cols // LANES) * LANES
    for cand in (1024, 512, 256, 128, 64, 32, 16, 8):
        if rows % cand == 0 and N_DEV * cand * lanes * 4 <= ADAM_BLOCK_BYTES:
            return cand
    return rows


def _sum_adamw(parts, w, m, v, *, name):
    nl, rows, cols = w.shape
    tr = _adam_rows(rows, cols)

    def body(p_ref, w_ref, m_ref, v_ref, g_ref, d_ref, nm_ref, nv_ref):
        g = p_ref[0]
        for k in range(1, N_DEV):
            g = g + p_ref[k]
        g_ref[...] = g
        nm = ADAM_B1 * m_ref[...] + (1.0 - ADAM_B1) * g
        nv = ADAM_B2 * v_ref[...] + (1.0 - ADAM_B2) * (g * g)
        m_hat = nm / (1.0 - ADAM_B1 ** ADAM_STEP)
        v_hat = nv / (1.0 - ADAM_B2 ** ADAM_STEP)
        d_ref[...] = -ADAM_LR * (m_hat / (jnp.sqrt(v_hat) + ADAM_EPS) + ADAM_WD * w_ref[...])
        nm_ref[...] = nm
        nv_ref[...] = nv

    blk = pl.BlockSpec((None, tr, cols), lambda l, r: (l, r, 0))
    return pl.pallas_call(
        body, name=name, grid=(nl, rows // tr),
        in_specs=[pl.BlockSpec((None, N_DEV, tr, cols), lambda l, r: (l, 0, r, 0)), blk, blk, blk],
        out_specs=[blk, blk, blk, blk],
        out_shape=[jax.ShapeDtypeStruct((nl, rows, cols), F32)] * 4,
        compiler_params=_params("arbitrary", "arbitrary"))(parts, w, m, v)


PACK_ROWS = 512


def _pack(arrays):
    flat = jnp.concatenate([a.astype(F32).reshape(-1) for a in arrays])
    pad = (-flat.shape[0]) % (PACK_ROWS * LANES)
    return jnp.pad(flat, (0, pad)).reshape(-1, LANES)


def _unpack(buf, shapes):
    flat = buf.reshape(-1)
    out, off = [], 0
    for s in shapes:
        n = math.prod(s)
        out.append(flat[off:off + n].reshape(s))
        off += n
    return out


def _to_proj_cols(w):
    z, xbc, dtc, u = jnp.split(w, [SSD_WIDTH, SSD_WIDTH + SSD_XBC, SSD_WIDTH + SSD_XBC + SSD_HEADS], axis=-1)
    pad = jnp.zeros(w.shape[:-1] + (LANES - SSD_HEADS,), w.dtype)
    return jnp.concatenate([xbc, z, u, dtc, pad], axis=-1)


def _from_proj_cols(w):
    xbc, z, u, dtc = (w[..., COL_XBC:COL_Z], w[..., COL_Z:COL_U], w[..., COL_U:COL_DT],
                      w[..., COL_DT:COL_DT + SSD_HEADS])
    return jnp.concatenate([z, xbc, dtc, u], axis=-1)


def _pad_heads(v):
    return jnp.pad(v, (0, LANES - SSD_HEADS)).reshape(1, LANES)


def _cat_cols(g):
    return jnp.transpose(g, (1, 0, 2)).reshape(g.shape[1], N_DEV * g.shape[2])


def _split_cols(w):
    r, c = w.shape
    return jnp.transpose(w.reshape(r, N_DEV, c // N_DEV), (1, 0, 2))


def _cat_rows(g):
    return g.reshape(N_DEV * g.shape[1], g.shape[2])


def _split_rows(w):
    return w.reshape(N_DEV, w.shape[0] // N_DEV, w.shape[1])


SHARDED = ("w_in", "w_out", "ffn_w_up", "ffn_w_down", "ple_w_gate", "ple_w_proj", "ssd_conv_w", "ffn_conv_w")
COL_SHARDED = ("w_in", "ffn_w_up", "ple_w_proj", "ssd_conv_w", "ffn_conv_w")
MATMUL_W = SHARDED[:6]
REPLICATED = ("mix_norm_g", "ssd_conv_b", "ssd_dt_bias", "ssd_a_log", "ssd_d", "ssd_norm_g", "pool_w", "pool_scale",
              "ffn_norm_g", "ffn_conv_b", "ple_norm_g", "final_norm_g")
WEIGHTS = ("mix_norm_g", "w_in", "ssd_conv_w", "ssd_conv_b", "ssd_dt_bias", "ssd_a_log", "ssd_d", "ssd_norm_g",
           "pool_w", "pool_scale", "w_out", "ffn_norm_g", "ffn_w_up", "ffn_conv_w", "ffn_conv_b", "ffn_w_down",
           "ple_norm_g", "ple_w_gate", "ple_w_proj", "final_norm_g")


FIRST_USED = ("w_in", "ssd_conv_w")
LATER_USED = tuple(k for k in SHARDED if k not in FIRST_USED)
LAST_MADE = ("w_out", "ssd_conv_w", "w_in")
EARLY_MADE = tuple(k for k in SHARDED if k not in LAST_MADE)
TRANSPOSED = ("w_in", "w_out", "ffn_w_up", "ffn_w_down", "ple_w_gate")


def _pick(names, per_sharded):
    return [per_sharded[SHARDED.index(k)] for k in names]


def _put(names, per_sharded, values):
    out = list(per_sharded)
    for k, val in zip(names, values):
        out[SHARDED.index(k)] = val
    return out


def _assemble(names, gathered):
    full = {}
    for k, g in zip(names, gathered):
        full[k] = _cat_cols(g) if k in COL_SHARDED else _cat_rows(g)
        if k == "w_in":
            full[k] = _to_proj_cols(full[k])
        if k in TRANSPOSED:
            full[k + "_t"] = full[k].T
    return full


def _grad_shards(names, grads):
    out = []
    for k in names:
        g = _from_proj_cols(grads[k]) if k == "w_in" else grads[k]
        out.append(_split_cols(g) if k in COL_SHARDED else _split_rows(g))
    return out


def _layer_fwd(i, h1, p_i, lw, rep, consts, ex_own, ex):
    tril, e_mat = consts
    row = lambda v: v.reshape(1, -1)
    dtb, alog = _pad_heads(rep["ssd_dt_bias"]), _pad_heads(rep["ssd_a_log"])
    dexp = row(jnp.repeat(rep["ssd_d"], SSD_HEAD_DIM))
    pw = rep["pool_w"].astype(BF16)
    proj = _norm_matmul(h1, lw["w_in"], row(rep["mix_norm_g"]), tt=512, tn=N_PROJ, name=f"in_proj_{i}")
    yssd, ypre, states, own = _ssd_fwd(proj, lw["ssd_conv_w"], row(rep["ssd_conv_b"]), dtb, alog, dexp,
                                       row(rep["ssd_norm_g"]), tril, e_mat, ts=512, name=f"ssd_fwd_{i}", ex=ex_own)
    if ex_own is not None:
        lw = dict(lw, **_assemble(LATER_USED, own))
    h2, ymix, n2 = _mix_out(h1, yssd, proj, pw, row(rep["pool_scale"]), lw["w_out"], row(rep["ffn_norm_g"]), tt=512,
                            name=f"mix_out_{i}")
    up = _norm_matmul(n2, lw["ffn_w_up"], tt=512, tn=D_FF, name=f"ffn_up_{i}")
    h3, act, pre, gathered = _ffn_down(h2, up, lw["ffn_conv_w"], row(rep["ffn_conv_b"]), lw["ffn_w_down"], tt=256,
                                       name=f"ffn_down_{i}", ex=ex)
    h4 = _ple_fwd(h3, p_i, row(rep["ple_norm_g"]), lw["ple_w_gate"], lw["ple_w_proj"], tt=512, name=f"ple_fwd_{i}")
    saved = dict(h1=h1, proj=proj, ypre=ypre, states=states, ymix=ymix, h2=h2, n2=n2, up=up, pre=pre, act=act, h3=h3,
                 dtb=dtb, alog=alog, dexp=dexp, pw=pw)
    return h4, saved, lw, gathered


def _layer_bwd(i, dh, p_i, lw, rep, s, consts, pending, parts, own_early):
    tril, triu, e_mat = consts
    ex = None if pending is None else _Exchange(pending, scatter=True, layer=i + 1, into=parts)
    row = lambda v: v.reshape(1, -1)
    g = {}
    dh, g["ple_w_gate"], g["ple_w_proj"], dg3 = _ple_bwd(dh, s["h3"], p_i, row(rep["ple_norm_g"]), lw["ple_w_gate"],
                                                         lw["ple_w_gate_t"], lw["ple_w_proj"], tt=512,
                                                         name=f"ple_bwd_{i}")
    g["ple_norm_g"] = dg3.reshape(-1)
    g["ffn_w_down"] = _matmul_tn(s["act"], dh, tm=D_FF // 2, tn=D_MODEL, tk=1024, name=f"dw_down_{i}")
    dact = _norm_matmul(dh, lw["ffn_w_down_t"], tt=512, tn=D_FF, name=f"d_act_{i}")
    dup, g["ffn_conv_w"], dcb, scattered = _ffn_act_bwd(s["up"], s["pre"], dact, lw["ffn_conv_w"], tt=256,
                                                        name=f"ffn_act_bwd_{i}", ex=ex)
    if ex is not None:
        parts = scattered
    g["ffn_conv_b"] = dcb.reshape(-1)
    g["ffn_w_up"] = _matmul_tn(s["n2"], dup, tm=D_MODEL, tn=D_UP // 4, tk=1024, name=f"dw_up_{i}")
    dh, dg2 = _matmul_rmsbwd(dup, lw["ffn_w_up_t"], s["h2"], row(rep["ffn_norm_g"]), dh, tt=512, tk=D_UP,
                             name=f"ffn_up_bwd_{i}")
    g["ffn_norm_g"] = dg2.reshape(-1)
    dymix, g["w_out"] = _out_bwd(dh, s["ymix"], lw["w_out_t"], tt=512, name=f"out_bwd_{i}")
    du, g["pool_w"], dsc = _pool_bwd(dymix, s["proj"], s["pw"], jnp.swapaxes(s["pw"], 1, 2), row(rep["pool_scale"]),
                                     tt=512, name=f"pool_bwd_{i}")
    g["pool_scale"] = dsc.reshape(-1)
    ex_own = None
    if own_early:
        ex_own = _Exchange(_grad_shards(EARLY_MADE, g), scatter=True, layer=i, into=_pick(EARLY_MADE, parts))
    (dproj, g["ssd_conv_w"], dcb, ddtb, dalog, dd, dng), own = _ssd_bwd(
        dymix, s["proj"], s["ypre"], s["states"], du, lw["ssd_conv_w"], row(rep["ssd_conv_b"]), s["dtb"], s["alog"],
        s["dexp"], row(rep["ssd_norm_g"]), tril, triu, e_mat, ts=512, name=f"ssd_bwd_{i}", ex=ex_own)
    if own_early:
        parts = _put(EARLY_MADE, parts, own)
    g["ssd_conv_b"], g["ssd_norm_g"] = dcb.reshape(-1), dng.reshape(-1)
    g["ssd_dt_bias"], g["ssd_a_log"], g["ssd_d"] = ddtb[0, :SSD_HEADS], dalog[0, :SSD_HEADS], dd[0, :SSD_HEADS]
    dh, g["w_in"], dg1 = _in_bwd(dproj, s["h1"], row(rep["mix_norm_g"]), lw["w_in_t"], dh, tt=256, name=f"in_bwd_{i}")
    g["mix_norm_g"] = dg1.reshape(-1)
    return dh, g, parts


def kernel(x, p, mix_norm_g, w_in, ssd_conv_w, ssd_conv_b, ssd_dt_bias, ssd_a_log, ssd_d, ssd_norm_g, pool_w, pool_scale, w_out, ffn_norm_g, ffn_w_up, ffn_conv_w, ffn_conv_b, ffn_w_down, ple_norm_g, ple_w_gate, ple_w_proj, final_norm_g, loss_target, m_mix_norm_g, m_w_in, m_ssd_conv_w, m_ssd_conv_b, m_ssd_dt_bias, m_ssd_a_log, m_ssd_d, m_ssd_norm_g, m_pool_w, m_pool_scale, m_w_out, m_ffn_norm_g, m_ffn_w_up, m_ffn_conv_w, m_ffn_conv_b, m_ffn_w_down, m_ple_norm_g, m_ple_w_gate, m_ple_w_proj, m_final_norm_g, v_mix_norm_g, v_w_in, v_ssd_conv_w, v_ssd_conv_b, v_ssd_dt_bias, v_ssd_a_log, v_ssd_d, v_ssd_norm_g, v_pool_w, v_pool_scale, v_w_out, v_ffn_norm_g, v_ffn_w_up, v_ffn_conv_w, v_ffn_conv_b, v_ffn_w_down, v_ple_norm_g, v_ple_w_gate, v_ple_w_proj, v_final_norm_g):
    w = dict(mix_norm_g=mix_norm_g, w_in=w_in, ssd_conv_w=ssd_conv_w, ssd_conv_b=ssd_conv_b, ssd_dt_bias=ssd_dt_bias,
             ssd_a_log=ssd_a_log, ssd_d=ssd_d, ssd_norm_g=ssd_norm_g, pool_w=pool_w, pool_scale=pool_scale, w_out=w_out,
             ffn_norm_g=ffn_norm_g, ffn_w_up=ffn_w_up, ffn_conv_w=ffn_conv_w, ffn_conv_b=ffn_conv_b,
             ffn_w_down=ffn_w_down, ple_norm_g=ple_norm_g, ple_w_gate=ple_w_gate, ple_w_proj=ple_w_proj,
             final_norm_g=final_norm_g)
    m = dict(mix_norm_g=m_mix_norm_g, w_in=m_w_in, ssd_conv_w=m_ssd_conv_w, ssd_conv_b=m_ssd_conv_b,
             ssd_dt_bias=m_ssd_dt_bias, ssd_a_log=m_ssd_a_log, ssd_d=m_ssd_d, ssd_norm_g=m_ssd_norm_g, pool_w=m_pool_w,
             pool_scale=m_pool_scale, w_out=m_w_out, ffn_norm_g=m_ffn_norm_g, ffn_w_up=m_ffn_w_up,
             ffn_conv_w=m_ffn_conv_w, ffn_conv_b=m_ffn_conv_b, ffn_w_down=m_ffn_w_down, ple_norm_g=m_ple_norm_g,
             ple_w_gate=m_ple_w_gate, ple_w_proj=m_ple_w_proj, final_norm_g=m_final_norm_g)
    v = dict(mix_norm_g=v_mix_norm_g, w_in=v_w_in, ssd_conv_w=v_ssd_conv_w, ssd_conv_b=v_ssd_conv_b,
             ssd_dt_bias=v_ssd_dt_bias, ssd_a_log=v_ssd_a_log, ssd_d=v_ssd_d, ssd_norm_g=v_ssd_norm_g, pool_w=v_pool_w,
             pool_scale=v_pool_scale, w_out=v_w_out, ffn_norm_g=v_ffn_norm_g, ffn_w_up=v_ffn_w_up,
             ffn_conv_w=v_ffn_conv_w, ffn_conv_b=v_ffn_conv_b, ffn_w_down=v_ffn_w_down, ple_norm_g=v_ple_norm_g,
             ple_w_gate=v_ple_w_gate, ple_w_proj=v_ple_w_proj, final_norm_g=v_final_norm_g)

    tril = jnp.tril(jnp.ones((CHUNK, CHUNK), BF16))
    triu = tril.T
    e_mat = (jnp.arange(SSD_WIDTH)[None, :] // SSD_HEAD_DIM == jnp.arange(LANES)[:, None]).astype(BF16)
    rep = [{k: w[k][i] for k in REPLICATED if k != "final_norm_g"} for i in range(DEPTH)]
    p_loc = p[:, 0]

    shards = [w[k].astype(BF16) if k in MATMUL_W else w[k] for k in SHARDED]
    lw = _assemble(FIRST_USED, _exchange_call(_Exchange(_pick(FIRST_USED, shards), scatter=False, layer=0),
                                              "gather_weights_0"))
    h, saved, layer_w = x[0], [], []
    for i in range(DEPTH):
        ex_own = _Exchange(_pick(LATER_USED, shards), scatter=False, layer=0) if i == 0 else None
        ex = _Exchange(shards, scatter=False, layer=i + 1) if i + 1 < DEPTH else None
        h, s, lw, gathered = _layer_fwd(i, h, p_loc[i], lw, rep[i], (tril, e_mat), ex_own, ex)
        saved.append(s)
        layer_w.append(lw)
        lw = _assemble(SHARDED, gathered)

    dh, loss_blk, dgf = _loss_head(h, final_norm_g.reshape(1, -1), loss_target[0], tt=512, name="loss_head")
    loss = lax.psum(loss_blk[0, 0], ("x", "y", "c"))

    rep_grads = [None] * DEPTH
    pending, parts = None, None
    for i in reversed(range(DEPTH)):
        dh, g, parts = _layer_bwd(i, dh, p_loc[i], layer_w[i], rep[i], saved[i], (tril, triu, e_mat), pending, parts,
                                  own_early=(i == 0))
        pending = _grad_shards(SHARDED, g) if i > 0 else _grad_shards(LAST_MADE, g)
        rep_grads[i] = g
    parts = _put(LAST_MADE, parts, _exchange_call(
        _Exchange(pending, scatter=True, layer=0, into=_pick(LAST_MADE, parts)), "scatter_grads_0"))

    out = {}
    for k, part in zip(SHARDED, parts):
        out[k] = _sum_adamw(part, w[k], m[k], v[k], name=f"adamw_{k}")

    rp_grads = [dgf.reshape(-1) if k == "final_norm_g" else jnp.stack([rep_grads[i][k] for i in range(DEPTH)])
                for k in REPLICATED]
    rp_shapes = [w[k].shape for k in REPLICATED]
    rp_parts = _exchange_call(_Exchange([_pack(rp_grads)[None]], scatter=False, layer=0), "gather_replicated_grads")[0]
    rp_out = _sum_adamw(rp_parts[None], *[_pack([d[k] for k in REPLICATED])[None] for d in (w, m, v)],
                        name="adamw_replicated")
    for j in range(4):
        for k, arr in zip(REPLICATED, _unpack(rp_out[j][0], rp_shapes)):
            out.setdefault(k, [None] * 4)[j] = arr
    results = [out[k][j] for j in range(4) for k in WEIGHTS]
    return (loss, dh[None], *results)
```

```python
import functools
import math

import jax
import jax.numpy as jnp
from jax import lax
from jax.experimental import pallas as pl
from jax.experimental.pallas import tpu as pltpu

F32 = jnp.float32
BF16 = jnp.bfloat16

N_DEV = 8
EPS = 1e-6
DEPTH = 4
D_MODEL = 1024
D_PLE = 256
SSD_WIDTH = 512
SSD_HEADS = 8
SSD_HEAD_DIM = 64
SSD_GROUPS = 2
SSD_STATE = 128
SSD_CONV = 4
CHUNK = 128
SSD_XBC = 1024
POOL_WINDOWS = (2, 4, 8, 16)
POOL_WIDTH = 512
POOL_GROUP = 128
POOL_HALO = 16
D_IN_PROJ = 2056
D_FF = 2816
D_UP = 2 * D_FF
FFN_CONV = 3
SUBLANES = 8
LANES = 128
N_PROJ = 2176
COL_XBC, COL_Z, COL_U, COL_DT = 0, 1024, 1536, 2048
N_PAIRS = SSD_HEADS // 2
ADAM_LR, ADAM_B1, ADAM_B2, ADAM_EPS, ADAM_WD, ADAM_STEP = 0.001, 0.9, 0.999, 1e-08, 0.01, 10
GELU_C = math.sqrt(2.0 / math.pi)
GELU_A = 0.044715
VMEM_LIMIT = 56 * 1024 * 1024

NT_DIMS = (((1,), (1,)), ((), ()))
TN_DIMS = (((0,), (0,)), ((), ()))


def _params(*sem):
    return pltpu.CompilerParams(dimension_semantics=sem, vmem_limit_bytes=VMEM_LIMIT)


def _dot(a, b):
    return jnp.dot(a, b, preferred_element_type=F32)


def _dot_nt(a, b):
    return lax.dot_general(a, b, NT_DIMS, preferred_element_type=F32)


def _dot_tn(a, b):
    return lax.dot_general(a, b, TN_DIMS, preferred_element_type=F32)


def _split3(a):
    hi = a.astype(BF16)
    r1 = a - hi.astype(F32)
    mid = r1.astype(BF16)
    return hi, mid, (r1 - mid.astype(F32)).astype(BF16)


def _hdot(a, b):
    if a.dtype == BF16:
        return sum(_dot(a, piece) for piece in _split3(b))
    return sum(_dot(piece, b) for piece in _split3(a))


def _headsum(q, e):
    return sum(_dot_nt(piece, e) for piece in _split3(q))


def _colsum(v):
    return jnp.sum(v, axis=0, keepdims=True)


def _sigmoid(v):
    return 1.0 / (1.0 + jnp.exp(-v))


def _softplus(v):
    e = jnp.exp(-jnp.abs(v))
    return jnp.maximum(v, 0.0) + jnp.where(e < 1e-4, e * (1.0 - 0.5 * e), jnp.log(1.0 + e))


def _rms_r(x):
    return lax.rsqrt(jnp.mean(x * x, axis=-1, keepdims=True) + EPS)


def _rms_bwd(x, r, g, dn):
    xhat = x * r
    gd = dn * g
    dx = r * (gd - xhat * jnp.mean(gd * xhat, axis=-1, keepdims=True))
    return dx, _colsum(dn * xhat)


def _gelu(v):
    return 0.5 * v * (1.0 + jnp.tanh(GELU_C * (v + GELU_A * v * v * v)))


def _gelu_grad(v):
    th = jnp.tanh(GELU_C * (v + GELU_A * v * v * v))
    return 0.5 * (1.0 + th) + 0.5 * v * (1.0 - th * th) * GELU_C * (1.0 + 3.0 * GELU_A * v * v)


def _tile(t, want):
    return min(t, want)


class _Exchange:
    def __init__(self, srcs, *, scatter, layer, into=None):
        self.srcs, self.scatter, self.layer = list(srcs), scatter, layer
        self.into = None if into is None else list(into)
        n = len(self.srcs)
        self.args = self.srcs + (self.into or [])
        self.in_specs = [pl.BlockSpec(memory_space=pl.ANY)] * len(self.args)
        if scatter:
            self.out_shape = [jax.ShapeDtypeStruct((DEPTH,) + s.shape, s.dtype) for s in self.srcs]
        else:
            self.out_shape = [jax.ShapeDtypeStruct((N_DEV,) + s.shape[1:], s.dtype) for s in self.srcs]
        self.out_specs = [pl.BlockSpec(memory_space=pl.ANY)] * n
        self.scratch = [pltpu.SemaphoreType.DMA((n, N_DEV - 1)), pltpu.SemaphoreType.DMA((n, N_DEV - 1)),
                        pltpu.SemaphoreType.DMA((n,))]

    def aliases(self, n_in_before, n_out_before):
        if self.into is None:
            return {}
        n = len(self.srcs)
        return {n_in_before + n + a: n_out_before + a for a in range(n)}

    def ops(self, in_refs, out_refs, sems):
        send_sems, recv_sems, local_sems = sems
        n = len(self.srcs)

        def copies():
            x, y, c = lax.axis_index("x"), lax.axis_index("y"), lax.axis_index("c")
            me = 4 * x + 2 * y + c

            def block(a, idx):
                return in_refs[a].at[idx] if self.scatter else in_refs[a].at[self.layer]

            def slot(a, idx):
                return out_refs[a].at[self.layer].at[idx] if self.scatter else out_refs[a].at[idx]

            local = [pltpu.make_async_copy(block(a, me), slot(a, me), local_sems.at[a]) for a in range(n)]
            sends, recvs = [], []
            for k in range(1, N_DEV):
                px = 1 - x if k & 4 else x
                py = 1 - y if k & 2 else y
                pc = 1 - c if k & 1 else c
                peer = 4 * px + 2 * py + pc
                for a in range(n):
                    kw = dict(send_sem=send_sems.at[a, k - 1], recv_sem=recv_sems.at[a, k - 1], device_id=(px, py, pc),
                              device_id_type=pl.DeviceIdType.MESH)
                    sends.append(pltpu.make_async_remote_copy(src_ref=block(a, peer), dst_ref=slot(a, me), **kw))
                    recvs.append(pltpu.make_async_remote_copy(src_ref=block(a, peer), dst_ref=slot(a, peer), **kw))
            return local, sends, recvs

        def start():
            local, sends, _ = copies()
            for cp in local + sends:
                cp.start()

        def wait():
            local, sends, recvs = copies()
            for send, recv in zip(sends, recvs):
                send.wait_send()
                recv.wait_recv()
            for cp in local:
                cp.wait()

        return start, wait


def _exchange_call(ex, name):
    n_in, n = len(ex.args), len(ex.srcs)

    def body(*refs):
        start, wait = ex.ops(refs[:n_in], refs[n_in:n_in + n], refs[n_in + n:])
        start()
        wait()

    return pl.pallas_call(
        body, name=name, in_specs=ex.in_specs, out_specs=ex.out_specs, out_shape=ex.out_shape,
        scratch_shapes=ex.scratch, input_output_aliases=ex.aliases(0, 0))(*ex.args)


def _split_refs(refs, counts):
    out, k = [], 0
    for cnt in counts:
        out.append(refs[k:k + cnt])
        k += cnt
    return out


def _ex_parts(ex):
    if ex is None:
        return [], [], [], [], [], (0, 0, 0)
    return ex.args, ex.in_specs, ex.out_shape, ex.out_specs, ex.scratch, (len(ex.args), len(ex.srcs), 3)


def _norm_matmul(h, w, g=None, *, tt, tn, name):
    t, k = h.shape
    n = w.shape[1]
    tt, tn = _tile(t, tt), _tile(n, tn)
    normed = g is not None

    def body(*refs):
        if normed:
            h_ref, g_ref, w_ref, o_ref = refs
            x = h_ref[...]
            xn = (x * _rms_r(x) * g_ref[...]).astype(BF16)
        else:
            h_ref, w_ref, o_ref = refs
            xn = h_ref[...].astype(BF16)
        o_ref[...] = _dot(xn, w_ref[...])

    in_specs = [pl.BlockSpec((tt, k), lambda j, i: (i, 0))]
    args = [h]
    if normed:
        in_specs.append(pl.BlockSpec((1, k), lambda j, i: (0, 0)))
        args.append(g)
    in_specs.append(pl.BlockSpec((k, tn), lambda j, i: (0, j)))
    args.append(w)
    return pl.pallas_call(
        body, name=name, grid=(n // tn, t // tt), in_specs=in_specs,
        out_specs=pl.BlockSpec((tt, tn), lambda j, i: (i, j)), out_shape=jax.ShapeDtypeStruct((t, n), F32),
        compiler_params=_params("arbitrary", "arbitrary"))(*args)


def _matmul_tn(a, b, *, tm, tn, tk, name):
    t, m = a.shape
    n = b.shape[1]
    tm, tn, tk = _tile(m, tm), _tile(n, tn), _tile(t, tk)

    def body(a_ref, b_ref, o_ref):
        @pl.when(pl.program_id(2) == 0)
        def _():
            o_ref[...] = jnp.zeros_like(o_ref)

        o_ref[...] += _dot_tn(a_ref[...].astype(BF16), b_ref[...].astype(BF16))

    return pl.pallas_call(
        body, name=name, grid=(m // tm, n // tn, t // tk),
        in_specs=[pl.BlockSpec((tk, tm), lambda i, j, kk: (kk, i)), pl.BlockSpec((tk, tn), lambda i, j, kk: (kk, j))],
        out_specs=pl.BlockSpec((tm, tn), lambda i, j, kk: (i, j)),
        out_shape=jax.ShapeDtypeStruct((m, n), F32),
        compiler_params=_params("arbitrary", "arbitrary", "arbitrary"))(a, b)


def _matmul_rmsbwd(a, wt, x, g, dh, *, tt, tk, name):
    t, k = a.shape
    d = wt.shape[1]
    tt, tk = _tile(t, tt), _tile(k, tk)
    nk = k // tk

    def body(a_ref, w_ref, x_ref, g_ref, dh_ref, o_ref, dg_ref, *scratch):
        i, kk = pl.program_id(0), pl.program_id(1)

        @pl.when((i == 0) & (kk == 0))
        def _():
            dg_ref[...] = jnp.zeros_like(dg_ref)

        def finish(dn):
            xv = x_ref[...]
            dx, dg = _rms_bwd(xv, _rms_r(xv), g_ref[...], dn)
            o_ref[...] = dh_ref[...] + dx
            dg_ref[...] += dg

        if nk == 1:
            finish(_dot(a_ref[...], w_ref[...]))
        else:
            acc, = scratch

            @pl.when(kk == 0)
            def _():
                acc[...] = jnp.zeros_like(acc)

            acc[...] += _dot(a_ref[...], w_ref[...])
            pl.when(kk == nk - 1)(lambda: finish(acc[...]))

    return pl.pallas_call(
        body, name=name, grid=(t // tt, nk),
        in_specs=[pl.BlockSpec((tt, tk), lambda i, kk: (i, kk)), pl.BlockSpec((tk, d), lambda i, kk: (kk, 0)),
                  pl.BlockSpec((tt, d), lambda i, kk: (i, 0)), pl.BlockSpec((1, d), lambda i, kk: (0, 0)),
                  pl.BlockSpec((tt, d), lambda i, kk: (i, 0))],
        out_specs=[pl.BlockSpec((tt, d), lambda i, kk: (i, 0)), pl.BlockSpec((1, d), lambda i, kk: (0, 0))],
        out_shape=[jax.ShapeDtypeStruct((t, d), F32), jax.ShapeDtypeStruct((1, d), F32)],
        scratch_shapes=[] if nk == 1 else [pltpu.VMEM((tt, d), F32)],
        compiler_params=_params("arbitrary", "arbitrary"))(a, wt, x, g, dh)


def _ssd_tile_prologue(i_is_first, xbc_ref, halo_ref, dt_ref, cw_ref, cb_ref, dtb_ref, alog_ref, e_ref, buf, xc_scr,
                       xa_scr, a_scr, dte_scr, x_scr, ts):
    buf[0:SUBLANES, :] = jnp.where(i_is_first, 0.0, halo_ref[...])
    buf[SUBLANES:SUBLANES + ts, :] = xbc_ref[...]
    cw = cw_ref[...]
    xc = cb_ref[...]
    for k in range(SSD_CONV):
        off = SUBLANES - (SSD_CONV - 1) + k
        xc = xc + cw[k:k + 1, :] * buf[off:off + ts, :]
    if xc_scr is not None:
        xc_scr[...] = xc
    xa_scr[...] = xc * _sigmoid(xc)
    dt = _softplus(dt_ref[...] + dtb_ref[...])
    a_neg = -jnp.exp(alog_ref[...])
    a_scr[...] = dt * a_neg
    dte = _hdot(dt, e_ref[...])
    dte_scr[...] = dte
    x_scr[...] = xa_scr[:, 0:SSD_WIDTH] * dte
    return dt, a_neg


def _chunk_decays(a_c, tril, e):
    cs = _hdot(tril, a_c)
    cs_t = cs.T
    cs_e = _hdot(cs, e)
    last_e = cs_e[CHUNK - 1:CHUNK, :]
    return cs, cs_t, cs_e, last_e


def _ssd_fwd(proj, cw, cb, dtb, alog, dexp, ng, tril, e, *, ts, name, ex=None):
    t = proj.shape[0]
    ts = _tile(t, ts)
    nch = ts // CHUNK
    hb = ts // SUBLANES
    nt = t // ts
    ex_args, ex_in_specs, ex_out_shape, ex_out_specs, ex_scratch, ex_counts = _ex_parts(ex)

    def body(*refs):
        ((xbc_ref, halo_ref, z_ref, dt_ref, cw_ref, cb_ref, dtb_ref, alog_ref, dexp_ref, ng_ref, tril_ref, e_ref),
         ex_in, (y_ref, ypre_ref, st_ref), ex_out, (buf, xa_scr, a_scr, dte_scr, x_scr, ys_scr, hstate),
         ex_sems) = _split_refs(refs, (12, ex_counts[0], 3, ex_counts[1], 7, ex_counts[2]))
        i = pl.program_id(0)
        if ex is not None:
            ex_start, ex_wait = ex.ops(ex_in, ex_out, ex_sems)
            pl.when(i == 0)(ex_start)

        @pl.when(i == 0)
        def _():
            hstate[...] = jnp.zeros_like(hstate)

        _ssd_tile_prologue(i == 0, xbc_ref, halo_ref, dt_ref, cw_ref, cb_ref, dtb_ref, alog_ref, e_ref, buf, None,
                           xa_scr, a_scr, dte_scr, x_scr, ts)
        tril = tril_ref[...]
        e_mat = e_ref[...]
        causal = (lax.broadcasted_iota(jnp.int32, (CHUNK, CHUNK), 0)
                  >= lax.broadcasted_iota(jnp.int32, (CHUNK, CHUNK), 1))
        lane = lax.broadcasted_iota(jnp.int32, (CHUNK, LANES), 1)

        def chunk(c, carry):
            r0 = pl.multiple_of(c * CHUNK, CHUNK)
            rows = pl.ds(r0, CHUNK)
            cs, cs_t, cs_e, last_e = _chunk_decays(a_scr[rows, :], tril, e_mat)
            decay_e = jnp.exp(last_e - cs_e)
            ecs_e = jnp.exp(cs_e)
            xc = x_scr[rows, :]
            xb = xc.astype(BF16)
            xd = (xc * decay_e).astype(BF16)
            for g in range(SSD_GROUPS):
                bg = xa_scr[rows, SSD_WIDTH + g * SSD_STATE:SSD_WIDTH + (g + 1) * SSD_STATE].astype(BF16)
                cg = xa_scr[rows, SSD_WIDTH + (SSD_GROUPS + g) * SSD_STATE:
                            SSD_WIDTH + (SSD_GROUPS + g + 1) * SSD_STATE].astype(BF16)
                cbm = _dot_nt(cg, bg)
                for jj in range(2):
                    j = 2 * g + jj
                    cols = slice(j * LANES, (j + 1) * LANES)
                    xp = xb[:, cols]
                    ypair = jnp.zeros((CHUNK, LANES), F32)
                    for hh in range(2):
                        h = 2 * j + hh
                        seg = jnp.exp(jnp.where(causal, cs[:, h:h + 1] - cs_t[h:h + 1, :], -jnp.inf))
                        m = (cbm * seg).astype(BF16)
                        half = (lane < SSD_HEAD_DIM) if hh == 0 else (lane >= SSD_HEAD_DIM)
                        ypair = ypair + _dot(m, jnp.where(half, xp, jnp.zeros_like(xp)))
                    hp = hstate[j]
                    st_ref[c, j] = hp
                    ypair = ypair + _dot(cg, hp.astype(BF16)) * ecs_e[:, cols]
                    ys_scr[rows, cols] = ypair
                    hstate[j] = hp * jnp.exp(last_e[:, cols]) + _dot_tn(bg, xd[:, cols])
            return carry

        lax.fori_loop(0, nch, chunk, 0)
        ypre = ys_scr[...] + xa_scr[:, 0:SSD_WIDTH] * dexp_ref[...]
        ypre_ref[...] = ypre
        z = z_ref[...]
        yg = ypre * (z * _sigmoid(z))
        gw = SSD_WIDTH // SSD_GROUPS
        outs = []
        for g in range(SSD_GROUPS):
            v = yg[:, g * gw:(g + 1) * gw]
            outs.append(v * _rms_r(v))
        y_ref[...] = jnp.concatenate(outs, axis=1) * ng_ref[...]
        if ex is not None:
            pl.when(i == nt - 1)(ex_wait)

    full = lambda shape: pl.BlockSpec(shape, lambda i: tuple(0 for _ in shape))
    outs = pl.pallas_call(
        body, name=name, grid=(nt,),
        in_specs=[pl.BlockSpec((ts, SSD_XBC), lambda i: (i, COL_XBC // SSD_XBC)),
                  pl.BlockSpec((SUBLANES, SSD_XBC), lambda i: (jnp.maximum(i * hb - 1, 0), COL_XBC // SSD_XBC)),
                  pl.BlockSpec((ts, SSD_WIDTH), lambda i: (i, COL_Z // SSD_WIDTH)),
                  pl.BlockSpec((ts, LANES), lambda i: (i, COL_DT // LANES)),
                  full((SSD_CONV, SSD_XBC)), full((1, SSD_XBC)), full((1, LANES)), full((1, LANES)),
                  full((1, SSD_WIDTH)), full((1, SSD_WIDTH)), full((CHUNK, CHUNK)), full((LANES, SSD_WIDTH))]
        + ex_in_specs,
        out_specs=[pl.BlockSpec((ts, SSD_WIDTH), lambda i: (i, 0)), pl.BlockSpec((ts, SSD_WIDTH), lambda i: (i, 0)),
                   pl.BlockSpec((nch, N_PAIRS, SSD_STATE, LANES), lambda i: (i, 0, 0, 0))] + ex_out_specs,
        out_shape=[jax.ShapeDtypeStruct((t, SSD_WIDTH), F32), jax.ShapeDtypeStruct((t, SSD_WIDTH), F32),
                   jax.ShapeDtypeStruct((t // CHUNK, N_PAIRS, SSD_STATE, LANES), F32)] + ex_out_shape,
        scratch_shapes=[pltpu.VMEM((SUBLANES + ts, SSD_XBC), F32), pltpu.VMEM((ts, SSD_XBC), F32),
                        pltpu.VMEM((ts, LANES), F32), pltpu.VMEM((ts, SSD_WIDTH), F32),
                        pltpu.VMEM((ts, SSD_WIDTH), F32), pltpu.VMEM((ts, SSD_WIDTH), F32),
                        pltpu.VMEM((N_PAIRS, SSD_STATE, LANES), F32)] + ex_scratch,
        input_output_aliases={} if ex is None else ex.aliases(12, 3),
        compiler_params=_params("arbitrary"))(proj, proj, proj, proj, cw, cb, dtb, alog, dexp, ng, tril, e, *ex_args)
    return outs[0], outs[1], outs[2], outs[3:]


def _ssd_bwd(dymix, proj, ypre, states, du, cw, cb, dtb, alog, dexp, ng, tril, triu, e, *, ts, name, ex=None):
    t = proj.shape[0]
    ts = _tile(t, ts)
    nch = ts // CHUNK
    hb = ts // SUBLANES
    nt = t // ts
    ex_args, ex_in_specs, ex_out_shape, ex_out_specs, ex_scratch, ex_counts = _ex_parts(ex)

    def body(*refs):
        ((dy_ref, xbc_ref, halo_ref, z_ref, dt_ref, ypre_ref, st_ref, du_ref, cw_ref, cb_ref, dtb_ref, alog_ref,
          dexp_ref, ng_ref, tril_ref, triu_ref, e_ref), ex_in,
         (dproj_ref, dcw_ref, dcb_ref, ddtb_ref, dalog_ref, dd_ref, dng_ref), ex_out,
         (buf, xc_scr, xa_scr, a_scr, dte_scr, x_scr, dyp_scr, dxa_scr, dx_scr, dbuf, carry, gstate),
         ex_sems) = _split_refs(refs, (17, ex_counts[0], 7, ex_counts[1], 12, ex_counts[2]))
        i = pl.program_id(0)
        if ex is not None:
            ex_start, ex_wait = ex.ops(ex_in, ex_out, ex_sems)
            pl.when(i == 0)(ex_start)

        @pl.when(i == 0)
        def _():
            gstate[...] = jnp.zeros_like(gstate)
            carry[...] = jnp.zeros_like(carry)
            for ref in (dcw_ref, dcb_ref, ddtb_ref, dalog_ref, dd_ref, dng_ref):
                ref[...] = jnp.zeros_like(ref)

        dt, a_neg = _ssd_tile_prologue(i == nt - 1, xbc_ref, halo_ref, dt_ref, cw_ref, cb_ref, dtb_ref, alog_ref, e_ref,
                                       buf, xc_scr, xa_scr, a_scr, dte_scr, x_scr, ts)
        tril = tril_ref[...]
        triu = triu_ref[...]
        e_mat = e_ref[...]
        causal = (lax.broadcasted_iota(jnp.int32, (CHUNK, CHUNK), 0)
                  >= lax.broadcasted_iota(jnp.int32, (CHUNK, CHUNK), 1))
        lane = lax.broadcasted_iota(jnp.int32, (CHUNK, LANES), 1)
        sub = lax.broadcasted_iota(jnp.int32, (CHUNK, LANES), 0)

        z = z_ref[...]
        sig = _sigmoid(z)
        zs = z * sig
        ypre = ypre_ref[...]
        yg = ypre * zs
        dout = dy_ref[...]
        ngv = ng_ref[...]
        gw = SSD_WIDTH // SSD_GROUPS
        dyg_parts, dng_parts = [], []
        for g in range(SSD_GROUPS):
            cols = slice(g * gw, (g + 1) * gw)
            v = yg[:, cols]
            dx, dg = _rms_bwd(v, _rms_r(v), ngv[:, cols], dout[:, cols])
            dyg_parts.append(dx)
            dng_parts.append(dg)
        dyg = jnp.concatenate(dyg_parts, axis=1)
        dng_ref[...] += jnp.concatenate(dng_parts, axis=1)
        dyp = dyg * zs
        dyp_scr[...] = dyp
        dproj_ref[:, COL_Z:COL_Z + SSD_WIDTH] = dyg * ypre * (sig * (1.0 + z * (1.0 - sig)))
        dproj_ref[:, COL_U:COL_U + POOL_WIDTH] = du_ref[...]
        xs_all = xa_scr[:, 0:SSD_WIDTH]
        dd_ref[...] += _headsum(jnp.broadcast_to(_colsum(dyp * xs_all), (SUBLANES, SSD_WIDTH)), e_mat)[0:1, :]

        def chunk(k, carry_):
            c = nch - 1 - k
            r0 = pl.multiple_of(c * CHUNK, CHUNK)
            rows = pl.ds(r0, CHUNK)
            a_c = a_scr[rows, :]
            cs, cs_t, cs_e, last_e = _chunk_decays(a_c, tril, e_mat)
            decay_e = jnp.exp(last_e - cs_e)
            ecs_e = jnp.exp(cs_e)
            elast_e = jnp.exp(last_e)
            xc = x_scr[rows, :]
            xb = xc.astype(BF16)
            xd = (xc * decay_e).astype(BF16)
            dyc = dyp_scr[rows, :]
            dcs = jnp.zeros((CHUNK, LANES), F32)
            dcs_neg_t = jnp.zeros((LANES, CHUNK), F32)
            qoff, rin, ghrow = [], [], []
            for g in range(SSD_GROUPS):
                b_cols = slice(SSD_WIDTH + g * SSD_STATE, SSD_WIDTH + (g + 1) * SSD_STATE)
                c_cols = slice(SSD_WIDTH + (SSD_GROUPS + g) * SSD_STATE, SSD_WIDTH + (SSD_GROUPS + g + 1) * SSD_STATE)
                bg = xa_scr[rows, b_cols].astype(BF16)
                cg = xa_scr[rows, c_cols].astype(BF16)
                cbm = _dot_nt(cg, bg)
                dcb_m = jnp.zeros((CHUNK, CHUNK), F32)
                dbg = jnp.zeros((CHUNK, SSD_STATE), F32)
                dcg = jnp.zeros((CHUNK, SSD_STATE), F32)
                for jj in range(2):
                    j = 2 * g + jj
                    cols = slice(j * LANES, (j + 1) * LANES)
                    dyp_j = dyc[:, cols]
                    hp = st_ref[c, j]
                    hpb = hp.astype(BF16)
                    gt = gstate[j]
                    gtb = gt.astype(BF16)
                    ecs = ecs_e[:, cols]
                    yoff = _dot(cg, hpb) * ecs
                    dye = (dyp_j * ecs).astype(BF16)
                    dcg = dcg + _dot_nt(dye, hpb)
                    dht = _dot_tn(cg, dye)
                    qoff.append(dyp_j * yoff)
                    xg = _dot(bg, gtb)
                    dxp = xg * decay_e[:, cols]
                    rin.append(xg * xc[:, cols])
                    dbg = dbg + _dot_nt(xd[:, cols], gtb)
                    ghrow.append(_colsum(gt * hp) * elast_e[:, cols])
                    gstate[j] = dht + gt * elast_e[:, cols]
                    for hh in range(2):
                        h = 2 * j + hh
                        seg = jnp.exp(jnp.where(causal, cs[:, h:h + 1] - cs_t[h:h + 1, :], -jnp.inf))
                        m = cbm * seg
                        half = (lane < SSD_HEAD_DIM) if hh == 0 else (lane >= SSD_HEAD_DIM)
                        dym = jnp.where(half, dyp_j, 0.0).astype(BF16)
                        w = _dot_nt(dym, xb[:, cols])
                        pm = w * m
                        dcs = dcs + jnp.where(lane == h, jnp.sum(pm, axis=1, keepdims=True), 0.0)
                        dcs_neg_t = dcs_neg_t + jnp.where(sub == h, _colsum(pm), 0.0)
                        dcb_m = dcb_m + w * seg
                        dxp = dxp + _dot_tn(m.astype(BF16), dym)
                    dx_scr[:, cols] = dxp
                dcbb = dcb_m.astype(BF16)
                dxa_scr[rows, c_cols] = dcg + _dot(dcbb, bg)
                dxa_scr[rows, b_cols] = dbg + _dot_tn(dcbb, cg)
            decay_th = jnp.exp(cs[CHUNK - 1:CHUNK, :] - cs)
            rd = _headsum(jnp.concatenate(rin, axis=1), e_mat) * decay_th
            dcs = dcs - dcs_neg_t.T + _headsum(jnp.concatenate(qoff, axis=1), e_mat) - rd
            gh = _headsum(jnp.broadcast_to(jnp.concatenate(ghrow, axis=1), (SUBLANES, SSD_WIDTH)), e_mat)[0:1, :]
            dcs = dcs + jnp.where(sub == CHUNK - 1, _colsum(rd) + gh, 0.0)
            da = _hdot(triu, dcs)
            dx_all = dx_scr[...]
            xs = xa_scr[rows, 0:SSD_WIDTH]
            dt_c = _softplus(dt_ref[rows, :] + dtb_ref[...])
            ddt = da * a_neg + _headsum(dx_all * xs, e_mat)
            dalog_ref[...] += _colsum(da * dt_c) * a_neg
            ddtraw = ddt * _sigmoid(dt_ref[rows, :] + dtb_ref[...])
            dproj_ref[rows, COL_DT:COL_DT + LANES] = ddtraw
            ddtb_ref[...] += _colsum(ddtraw)
            dxa_scr[rows, 0:SSD_WIDTH] = dx_all * dte_scr[rows, :] + dyc * dexp_ref[...]
            return carry_

        lax.fori_loop(0, nch, chunk, 0)

        xcv = xc_scr[...]
        sgc = _sigmoid(xcv)
        dxc = dxa_scr[...] * (sgc * (1.0 + xcv * (1.0 - sgc)))
        dcb_ref[...] += _colsum(dxc)
        dbuf[0:ts, :] = dxc
        dbuf[ts:ts + SUBLANES, :] = carry[...]
        cwv = cw_ref[...]
        dxbc = jnp.zeros((ts, SSD_XBC), F32)
        dcw_rows = []
        for k in range(SSD_CONV):
            off = SUBLANES - (SSD_CONV - 1) + k
            dcw_rows.append(_colsum(dxc * buf[off:off + ts, :]))
            back = SSD_CONV - 1 - k
            dxbc = dxbc + cwv[k:k + 1, :] * dbuf[back:back + ts, :]
        dcw_ref[...] += jnp.concatenate(dcw_rows, axis=0)
        dproj_ref[:, COL_XBC:COL_XBC + SSD_XBC] = dxbc
        carry[...] = dxc[0:SUBLANES, :]
        if ex is not None:
            pl.when(i == nt - 1)(ex_wait)

    rev = lambda i: nt - 1 - i
    full = lambda shape: pl.BlockSpec(shape, lambda i: tuple(0 for _ in shape))
    outs = pl.pallas_call(
        body, name=name, grid=(nt,),
        in_specs=[pl.BlockSpec((ts, SSD_WIDTH), lambda i: (rev(i), 0)),
                  pl.BlockSpec((ts, SSD_XBC), lambda i: (rev(i), COL_XBC // SSD_XBC)),
                  pl.BlockSpec((SUBLANES, SSD_XBC), lambda i: (jnp.maximum(rev(i) * hb - 1, 0), COL_XBC // SSD_XBC)),
                  pl.BlockSpec((ts, SSD_WIDTH), lambda i: (rev(i), COL_Z // SSD_WIDTH)),
                  pl.BlockSpec((ts, LANES), lambda i: (rev(i), COL_DT // LANES)),
                  pl.BlockSpec((ts, SSD_WIDTH), lambda i: (rev(i), 0)),
                  pl.BlockSpec((nch, N_PAIRS, SSD_STATE, LANES), lambda i: (rev(i), 0, 0, 0)),
                  pl.BlockSpec((ts, POOL_WIDTH), lambda i: (rev(i), 0)),
                  full((SSD_CONV, SSD_XBC)), full((1, SSD_XBC)), full((1, LANES)), full((1, LANES)),
                  full((1, SSD_WIDTH)), full((1, SSD_WIDTH)), full((CHUNK, CHUNK)), full((CHUNK, CHUNK)),
                  full((LANES, SSD_WIDTH))] + ex_in_specs,
        out_specs=[pl.BlockSpec((ts, N_PROJ), lambda i: (rev(i), 0)),
                   full((SSD_CONV, SSD_XBC)), full((1, SSD_XBC)), full((1, LANES)), full((1, LANES)),
                   full((1, LANES)), full((1, SSD_WIDTH))] + ex_out_specs,
        out_shape=[jax.ShapeDtypeStruct((t, N_PROJ), F32),
                   jax.ShapeDtypeStruct((SSD_CONV, SSD_XBC), F32), jax.ShapeDtypeStruct((1, SSD_XBC), F32),
                   jax.ShapeDtypeStruct((1, LANES), F32), jax.ShapeDtypeStruct((1, LANES), F32),
                   jax.ShapeDtypeStruct((1, LANES), F32), jax.ShapeDtypeStruct((1, SSD_WIDTH), F32)] + ex_out_shape,
        scratch_shapes=[pltpu.VMEM((SUBLANES + ts, SSD_XBC), F32), pltpu.VMEM((ts, SSD_XBC), F32),
                        pltpu.VMEM((ts, SSD_XBC), F32), pltpu.VMEM((ts, LANES), F32),
                        pltpu.VMEM((ts, SSD_WIDTH), F32), pltpu.VMEM((ts, SSD_WIDTH), F32),
                        pltpu.VMEM((ts, SSD_WIDTH), F32), pltpu.VMEM((ts, SSD_XBC), F32),
                        pltpu.VMEM((CHUNK, SSD_WIDTH), F32), pltpu.VMEM((ts + SUBLANES, SSD_XBC), F32),
                        pltpu.VMEM((SUBLANES, SSD_XBC), F32), pltpu.VMEM((N_PAIRS, SSD_STATE, LANES), F32)]
        + ex_scratch,
        input_output_aliases={} if ex is None else ex.aliases(17, 7),
        compiler_params=_params("arbitrary"))(
            dymix, proj, proj, proj, proj, ypre, states, du, cw, cb, dtb, alog, dexp, ng, tril, triu, e, *ex_args)
    return outs[:7], outs[7:]


def _pooled(ubuf, u, pos, tt):
    out = []
    for gi, w in enumerate(POOL_WINDOWS):
        cols = slice(gi * POOL_GROUP, (gi + 1) * POOL_GROUP)
        acc = u[:, cols]
        for j in range(1, w):
            acc = acc + ubuf[POOL_HALO - j:POOL_HALO - j + tt, cols]
        out.append(acc / jnp.minimum(pos, float(w)) - u[:, cols])
    return out


def _mix_out(h, yssd, proj, pool_w, pool_scale, w_out, g_next, *, tt, name):
    t = h.shape[0]
    tt = _tile(t, tt)
    hb = tt // POOL_HALO

    def body(h_ref, ys_ref, u_ref, uh_ref, pw_ref, sc_ref, wo_ref, gn_ref, o_ref, ym_ref, n_ref, ubuf):
        i = pl.program_id(0)
        ubuf[0:POOL_HALO, :] = jnp.where(i == 0, 0.0, uh_ref[...])
        u = u_ref[...]
        ubuf[POOL_HALO:POOL_HALO + tt, :] = u
        pos = (i * tt + 1 + lax.broadcasted_iota(jnp.int32, (tt, 1), 0)).astype(F32)
        sc = sc_ref[...]
        parts = [ys_ref[...]]
        for gi, pooled in enumerate(_pooled(ubuf, u, pos, tt)):
            cols = slice(gi * POOL_GROUP, (gi + 1) * POOL_GROUP)
            parts.append(_dot(pooled.astype(BF16), pw_ref[gi]) * sc[:, cols])
        ymix = jnp.concatenate(parts, axis=1).astype(BF16)
        ym_ref[...] = ymix
        h2 = h_ref[...] + _dot(ymix, wo_ref[...])
        o_ref[...] = h2
        n_ref[...] = (h2 * _rms_r(h2) * gn_ref[...]).astype(BF16)

    full = lambda shape: pl.BlockSpec(shape, lambda i: tuple(0 for _ in shape))
    return pl.pallas_call(
        body, name=name, grid=(t // tt,),
        in_specs=[pl.BlockSpec((tt, D_MODEL), lambda i: (i, 0)), pl.BlockSpec((tt, SSD_WIDTH), lambda i: (i, 0)),
                  pl.BlockSpec((tt, POOL_WIDTH), lambda i: (i, COL_U // POOL_WIDTH)),
                  pl.BlockSpec((POOL_HALO, POOL_WIDTH), lambda i: (jnp.maximum(i * hb - 1, 0), COL_U // POOL_WIDTH)),
                  full((len(POOL_WINDOWS), POOL_GROUP, POOL_GROUP)), full((1, POOL_WIDTH)),
                  full((D_MODEL, D_MODEL)), full((1, D_MODEL))],
        out_specs=[pl.BlockSpec((tt, D_MODEL), lambda i: (i, 0))] * 3,
        out_shape=[jax.ShapeDtypeStruct((t, D_MODEL), F32), jax.ShapeDtypeStruct((t, D_MODEL), BF16),
                   jax.ShapeDtypeStruct((t, D_MODEL), BF16)],
        scratch_shapes=[pltpu.VMEM((POOL_HALO + tt, POOL_WIDTH), F32)],
        compiler_params=_params("arbitrary"))(h, yssd, proj, proj, pool_w, pool_scale, w_out, g_next)


def _out_bwd(dh, ymix, w_out_t, *, tt, name):
    t = dh.shape[0]
    tt = _tile(t, tt)

    def body(dh_ref, ym_ref, wt_ref, dym_ref, dw_ref):
        @pl.when(pl.program_id(0) == 0)
        def _():
            dw_ref[...] = jnp.zeros_like(dw_ref)

        dhb = dh_ref[...].astype(BF16)
        dym_ref[...] = _dot(dhb, wt_ref[...])
        dw_ref[...] += _dot_tn(ym_ref[...], dhb)

    return pl.pallas_call(
        body, name=name, grid=(t // tt,),
        in_specs=[pl.BlockSpec((tt, D_MODEL), lambda i: (i, 0)), pl.BlockSpec((tt, D_MODEL), lambda i: (i, 0)),
                  pl.BlockSpec((D_MODEL, D_MODEL), lambda i: (0, 0))],
        out_specs=[pl.BlockSpec((tt, D_MODEL), lambda i: (i, 0)), pl.BlockSpec((D_MODEL, D_MODEL), lambda i: (0, 0))],
        out_shape=[jax.ShapeDtypeStruct((t, D_MODEL), F32), jax.ShapeDtypeStruct((D_MODEL, D_MODEL), F32)],
        compiler_params=_params("arbitrary"))(dh, ymix, w_out_t)


def _pool_bwd(dymix, proj, pool_w, pool_w_t, pool_scale, *, tt, name):
    t = proj.shape[0]
    tt = _tile(t, tt)
    hb = tt // POOL_HALO
    nt = t // tt
    ng = len(POOL_WINDOWS)

    def body(dy_ref, dyh_ref, u_ref, uh_ref, pw_ref, pwt_ref, sc_ref, du_ref, dpw_ref, dsc_ref, ubuf, dbuf):
        i = pl.program_id(0)

        @pl.when(i == 0)
        def _():
            dpw_ref[...] = jnp.zeros_like(dpw_ref)
            dsc_ref[...] = jnp.zeros_like(dsc_ref)

        ubuf[0:POOL_HALO, :] = jnp.where(i == 0, 0.0, uh_ref[...])
        u = u_ref[...]
        ubuf[POOL_HALO:POOL_HALO + tt, :] = u
        pos = (i * tt + 1 + lax.broadcasted_iota(jnp.int32, (tt, 1), 0)).astype(F32)
        sc = sc_ref[...]
        dy = dy_ref[...]
        dyh = jnp.where(i == nt - 1, 0.0, dyh_ref[...])
        dsc_parts, du_parts = [], []
        for gi, pooled in enumerate(_pooled(ubuf, u, pos, tt)):
            w = POOL_WINDOWS[gi]
            cols = slice(gi * POOL_GROUP, (gi + 1) * POOL_GROUP)
            pb = pooled.astype(BF16)
            dsc_parts.append(_colsum(dy[:, cols] * _dot(pb, pw_ref[gi])))
            dmx = (dy[:, cols] * sc[:, cols]).astype(BF16)
            dpw_ref[gi] += _dot_tn(pb, dmx)
            dpool = _dot(dmx, pwt_ref[gi])
            dpool_h = _dot((dyh[:, cols] * sc[:, cols]).astype(BF16), pwt_ref[gi])
            dbuf[0:tt, cols] = dpool / jnp.minimum(pos, float(w))
            dbuf[tt:tt + POOL_HALO, cols] = dpool_h / float(w)
            acc = -dpool
            for j in range(w):
                acc = acc + dbuf[j:j + tt, cols]
            du_parts.append(acc)
        du_ref[...] = jnp.concatenate(du_parts, axis=1)
        dsc_ref[...] += jnp.concatenate(dsc_parts, axis=1)

    full = lambda shape: pl.BlockSpec(shape, lambda i: tuple(0 for _ in shape))
    ucol = COL_U // POOL_WIDTH
    return pl.pallas_call(
        body, name=name, grid=(nt,),
        in_specs=[pl.BlockSpec((tt, POOL_WIDTH), lambda i: (i, 1)),
                  pl.BlockSpec((POOL_HALO, POOL_WIDTH), lambda i: (jnp.minimum((i + 1) * hb, t // POOL_HALO - 1), 1)),
                  pl.BlockSpec((tt, POOL_WIDTH), lambda i: (i, ucol)),
                  pl.BlockSpec((POOL_HALO, POOL_WIDTH), lambda i: (jnp.maximum(i * hb - 1, 0), ucol)),
                  full((ng, POOL_GROUP, POOL_GROUP)), full((ng, POOL_GROUP, POOL_GROUP)), full((1, POOL_WIDTH))],
        out_specs=[pl.BlockSpec((tt, POOL_WIDTH), lambda i: (i, 0)), full((ng, POOL_GROUP, POOL_GROUP)),
                   full((1, POOL_WIDTH))],
        out_shape=[jax.ShapeDtypeStruct((t, POOL_WIDTH), F32), jax.ShapeDtypeStruct((ng, POOL_GROUP, POOL_GROUP), F32),
                   jax.ShapeDtypeStruct((1, POOL_WIDTH), F32)],
        scratch_shapes=[pltpu.VMEM((POOL_HALO + tt, POOL_WIDTH), F32), pltpu.VMEM((tt + POOL_HALO, POOL_WIDTH), F32)],
        compiler_params=_params("arbitrary"))(dymix, dymix, proj, proj, pool_w, pool_w_t, pool_scale)


def _in_bwd(dproj, h, g, w_in_t, dh, *, tt, name):
    t = h.shape[0]
    tt = _tile(t, tt)

    def body(dp_ref, h_ref, g_ref, wt_ref, dh_ref, o_ref, dw_ref, dg_ref):
        @pl.when(pl.program_id(0) == 0)
        def _():
            dw_ref[...] = jnp.zeros_like(dw_ref)
            dg_ref[...] = jnp.zeros_like(dg_ref)

        x = h_ref[...]
        r = _rms_r(x)
        gv = g_ref[...]
        dpb = dp_ref[...].astype(BF16)
        dw_ref[...] += _dot_tn((x * r * gv).astype(BF16), dpb)
        dx, dg = _rms_bwd(x, r, gv, _dot(dpb, wt_ref[...]))
        o_ref[...] = dh_ref[...] + dx
        dg_ref[...] += dg

    full = lambda shape: pl.BlockSpec(shape, lambda i: tuple(0 for _ in shape))
    row = lambda n: pl.BlockSpec((tt, n), lambda i: (i, 0))
    return pl.pallas_call(
        body, name=name, grid=(t // tt,),
        in_specs=[row(N_PROJ), row(D_MODEL), full((1, D_MODEL)), full((N_PROJ, D_MODEL)), row(D_MODEL)],
        out_specs=[row(D_MODEL), full((D_MODEL, N_PROJ)), full((1, D_MODEL))],
        out_shape=[jax.ShapeDtypeStruct((t, D_MODEL), F32), jax.ShapeDtypeStruct((D_MODEL, N_PROJ), F32),
                   jax.ShapeDtypeStruct((1, D_MODEL), F32)],
        compiler_params=_params("arbitrary"))(dproj, h, g, w_in_t, dh)


FFN_COLS = 256
N_SLABS = D_FF // FFN_COLS
FFN_ROWS = 16


def _part8(v):
    acc = v[0:SUBLANES]
    for r in range(SUBLANES, FFN_ROWS, SUBLANES):
        acc = acc + v[r:r + SUBLANES]
    return acc


N_SLAB_BUFS = 4


def _ffn_fwd(h, n2, w_up, cw, cb, w_down, *, tt, name, ex=None):
    t = h.shape[0]
    tt = _tile(t, tt)
    nt = t // tt
    ex_args, ex_in_specs, ex_out_shape, ex_out_specs, ex_scratch, ex_counts = _ex_parts(ex)

    def body(*refs):
        ((h_ref, n2_ref, wu_ref, cw_ref, cb_ref, wd_ref), ex_in, (o_ref, act_ref, pre_ref, up_ref), ex_out,
         (slab, halo), ex_sems) = _split_refs(refs, (6, ex_counts[0], 4, ex_counts[1], 2, ex_counts[2]))
        i = pl.program_id(0)
        if ex is not None:
            ex_start, ex_wait = ex.ops(ex_in, ex_out, ex_sems)
            pl.when(i == 0)(ex_start)

        @pl.when(i == 0)
        def _():
            halo[...] = jnp.zeros_like(halo)

        n2v = n2_ref[...]

        def slab_cols(s):
            return slice(s * FFN_COLS, (s + 1) * FFN_COLS), slice(D_FF + s * FFN_COLS, D_FF + (s + 1) * FFN_COLS)

        def project(s):
            return [_dot(n2v, wu_ref[:, cols]) for cols in slab_cols(s)]

        def conv(u, cols, buf_id):
            up_ref[:, cols] = u
            sb = slab.at[buf_id]
            sb[0:SUBLANES, :] = halo[:, cols]
            sb[SUBLANES:SUBLANES + tt, :] = u
            halo[:, cols] = u[tt - SUBLANES:tt, :]
            acc = cb_ref[:, cols] + cw_ref[FFN_CONV - 1:FFN_CONV, cols] * u
            for k in range(FFN_CONV - 1):
                off = SUBLANES - (FFN_CONV - 1) + k
                acc = acc + cw_ref[k:k + 1, cols] * sb[off:off + tt, :]
            pre_ref[:, cols] = acc
            return acc

        out = h_ref[...]
        ahead = project(0)
        for s in range(N_SLABS):
            (ug, uv), (gcols, vcols) = ahead, slab_cols(s)
            if s + 1 < N_SLABS:
                ahead = project(s + 1)
            gate = conv(ug, gcols, (2 * s) % N_SLAB_BUFS)
            val = conv(uv, vcols, (2 * s + 1) % N_SLAB_BUFS)
            act = (_gelu(gate) * val).astype(BF16)
            act_ref[:, s * FFN_COLS:(s + 1) * FFN_COLS] = act
            out = out + _dot(act, wd_ref[s * FFN_COLS:(s + 1) * FFN_COLS, :])
        o_ref[...] = out
        if ex is not None:
            pl.when(i == nt - 1)(ex_wait)

    full = lambda shape: pl.BlockSpec(shape, lambda i: tuple(0 for _ in shape))
    row = lambda n: pl.BlockSpec((tt, n), lambda i: (i, 0))
    outs = pl.pallas_call(
        body, name=name, grid=(nt,),
        in_specs=[row(D_MODEL), row(D_MODEL), full((D_MODEL, D_UP)), full((FFN_CONV, D_UP)), full((1, D_UP)),
                  full((D_FF, D_MODEL))] + ex_in_specs,
        out_specs=[row(D_MODEL), row(D_FF), row(D_UP), row(D_UP)] + ex_out_specs,
        out_shape=[jax.ShapeDtypeStruct((t, D_MODEL), F32), jax.ShapeDtypeStruct((t, D_FF), BF16),
                   jax.ShapeDtypeStruct((t, D_UP), F32), jax.ShapeDtypeStruct((t, D_UP), F32)] + ex_out_shape,
        scratch_shapes=[pltpu.VMEM((N_SLAB_BUFS, SUBLANES + tt, FFN_COLS), F32), pltpu.VMEM((SUBLANES, D_UP), F32)]
        + ex_scratch,
        input_output_aliases={} if ex is None else ex.aliases(6, 4),
        compiler_params=_params("arbitrary"))(h, n2, w_up, cw, cb, w_down, *ex_args)
    return outs[0], outs[1], outs[2], outs[3], outs[4:]


def _ffn_act_bwd(up, pre, dact, cw, *, tt, name, ex=None):
    t = up.shape[0]
    tt = _tile(t, tt)
    nt = t // tt
    nb = tt // FFN_ROWS
    ex_args, ex_in_specs, ex_out_shape, ex_out_specs, ex_scratch, ex_counts = _ex_parts(ex)

    def body(*refs):
        (up_ref, pre_ref, da_ref, cw_ref), ex_in, (dup_ref, dcw_ref, dcb_ref), ex_out, (carry,), ex_sems = (
            _split_refs(refs, (4, ex_counts[0], 3, ex_counts[1], 1, ex_counts[2])))
        i = pl.program_id(0)
        if ex is not None:
            ex_start, ex_wait = ex.ops(ex_in, ex_out, ex_sems)
            pl.when(i == 0)(ex_start)

        @pl.when(i == 0)
        def _():
            dcw_ref[...] = jnp.zeros_like(dcw_ref)
            dcb_ref[...] = jnp.zeros_like(dcb_ref)
            carry[...] = jnp.zeros_like(carry)

        for s in range(N_SLABS):
            gcols = slice(s * FFN_COLS, (s + 1) * FFN_COLS)
            vcols = slice(D_FF + s * FFN_COLS, D_FF + (s + 1) * FFN_COLS)
            wg, wv = cw_ref[:, gcols], cw_ref[:, vcols]

            def blk(j, c):
                nxt_g, nxt_v, sums = c
                rows = pl.ds(pl.multiple_of((nb - 1 - j) * FFN_ROWS, FFN_ROWS), FFN_ROWS)
                r0 = rows.start
                xg, xv = up_ref[rows, gcols], up_ref[rows, vcols]
                gate, val = pre_ref[rows, gcols], pre_ref[rows, vcols]
                gelu, dgelu = _gelu(gate), _gelu_grad(gate)
                da = da_ref[rows, gcols]
                heads, new = [], []
                for dp, nxt, x, w, cols in ((da * val * dgelu, nxt_g, xg, wg, gcols), (da * gelu, nxt_v, xv, wv, vcols)):
                    de = jnp.concatenate([dp, nxt], axis=0)
                    shifted = [de[FFN_CONV - 1 - k:FFN_CONV - 1 - k + FFN_ROWS] for k in range(FFN_CONV - 1)] + [dp]
                    dup = w[0:1] * shifted[0]
                    for k in range(1, FFN_CONV):
                        dup = dup + w[k:k + 1] * shifted[k]
                    dup_ref[pl.ds(r0, FFN_ROWS), cols] = dup.astype(BF16)
                    heads.append(dp[0:SUBLANES])
                    new += [_part8(dp)] + [_part8(sh * x) for sh in shifted]
                return heads[0], heads[1], tuple(a + v for a, v in zip(sums, new))

            zero = jnp.zeros((SUBLANES, FFN_COLS), F32)
            nxt_g, nxt_v, sums = lax.fori_loop(
                0, nb, blk, (carry[:, gcols], carry[:, vcols], (zero,) * (2 * (1 + FFN_CONV))))
            carry[:, gcols] = nxt_g
            carry[:, vcols] = nxt_v
            for half, cols in enumerate((gcols, vcols)):
                part = sums[half * (1 + FFN_CONV):(half + 1) * (1 + FFN_CONV)]
                dcb_ref[:, cols] += _colsum(part[0])
                dcw_ref[:, cols] += jnp.concatenate([_colsum(v) for v in part[1:]], axis=0)
        if ex is not None:
            pl.when(i == nt - 1)(ex_wait)

    rev = lambda i: nt - 1 - i
    full = lambda shape: pl.BlockSpec(shape, lambda i: tuple(0 for _ in shape))
    outs = pl.pallas_call(
        body, name=name, grid=(nt,),
        in_specs=[pl.BlockSpec((tt, D_UP), lambda i: (rev(i), 0)), pl.BlockSpec((tt, D_UP), lambda i: (rev(i), 0)),
                  pl.BlockSpec((tt, D_FF), lambda i: (rev(i), 0)), full((FFN_CONV, D_UP))] + ex_in_specs,
        out_specs=[pl.BlockSpec((tt, D_UP), lambda i: (rev(i), 0)), full((FFN_CONV, D_UP)), full((1, D_UP))]
        + ex_out_specs,
        out_shape=[jax.ShapeDtypeStruct((t, D_UP), BF16), jax.ShapeDtypeStruct((FFN_CONV, D_UP), F32),
                   jax.ShapeDtypeStruct((1, D_UP), F32)] + ex_out_shape,
        scratch_shapes=[pltpu.VMEM((SUBLANES, D_UP), F32)] + ex_scratch,
        input_output_aliases={} if ex is None else ex.aliases(4, 3),
        compiler_params=_params("arbitrary"))(up, pre, dact, cw, *ex_args)
    return outs[0], outs[1], outs[2], outs[3:]


def _ple_fwd(h, p, g, w_gate, w_proj, *, tt, name):
    t = h.shape[0]
    tt = _tile(t, tt)

    def body(h_ref, p_ref, g_ref, wg_ref, wp_ref, o_ref):
        x = h_ref[...]
        n = (x * _rms_r(x) * g_ref[...]).astype(BF16)
        gate = _sigmoid(_dot(n, wg_ref[...]))
        o_ref[...] = x + _dot(p_ref[...].astype(BF16), wp_ref[...]) * gate

    full = lambda shape: pl.BlockSpec(shape, lambda i: tuple(0 for _ in shape))
    return pl.pallas_call(
        body, name=name, grid=(t // tt,),
        in_specs=[pl.BlockSpec((tt, D_MODEL), lambda i: (i, 0)), pl.BlockSpec((tt, D_PLE), lambda i: (i, 0)),
                  full((1, D_MODEL)), full((D_MODEL, D_MODEL)), full((D_PLE, D_MODEL))],
        out_specs=pl.BlockSpec((tt, D_MODEL), lambda i: (i, 0)),
        out_shape=jax.ShapeDtypeStruct((t, D_MODEL), F32),
        compiler_params=_params("arbitrary"))(h, p, g, w_gate, w_proj)


def _ple_bwd(dh, h, p, g, w_gate, w_gate_t, w_proj, *, tt, name):
    t = h.shape[0]
    tt = _tile(t, tt)

    def body(dh_ref, h_ref, p_ref, g_ref, wg_ref, wgt_ref, wp_ref, o_ref, dwg_ref, dwp_ref, dg_ref):
        @pl.when(pl.program_id(0) == 0)
        def _():
            dwg_ref[...] = jnp.zeros_like(dwg_ref)
            dwp_ref[...] = jnp.zeros_like(dwp_ref)
            dg_ref[...] = jnp.zeros_like(dg_ref)

        x = h_ref[...]
        r = _rms_r(x)
        gv = g_ref[...]
        n = (x * r * gv).astype(BF16)
        gate = _sigmoid(_dot(n, wg_ref[...]))
        pb = p_ref[...].astype(BF16)
        pe = _dot(pb, wp_ref[...])
        dhv = dh_ref[...]
        dwp_ref[...] += _dot_tn(pb, (dhv * gate).astype(BF16))
        ds = (dhv * pe * gate * (1.0 - gate)).astype(BF16)
        dwg_ref[...] += _dot_tn(n, ds)
        dx, dg = _rms_bwd(x, r, gv, _dot(ds, wgt_ref[...]))
        o_ref[...] = dhv + dx
        dg_ref[...] += dg

    full = lambda shape: pl.BlockSpec(shape, lambda i: tuple(0 for _ in shape))
    row = lambda n: pl.BlockSpec((tt, n), lambda i: (i, 0))
    return pl.pallas_call(
        body, name=name, grid=(t // tt,),
        in_specs=[row(D_MODEL), row(D_MODEL), row(D_PLE), full((1, D_MODEL)), full((D_MODEL, D_MODEL)),
                  full((D_MODEL, D_MODEL)), full((D_PLE, D_MODEL))],
        out_specs=[row(D_MODEL), full((D_MODEL, D_MODEL)), full((D_PLE, D_MODEL)), full((1, D_MODEL))],
        out_shape=[jax.ShapeDtypeStruct((t, D_MODEL), F32), jax.ShapeDtypeStruct((D_MODEL, D_MODEL), F32),
                   jax.ShapeDtypeStruct((D_PLE, D_MODEL), F32), jax.ShapeDtypeStruct((1, D_MODEL), F32)],
        compiler_params=_params("arbitrary"))(dh, h, p, g, w_gate, w_gate_t, w_proj)


def _loss_head(h, g, target, *, tt, name):
    t = h.shape[0]
    tt = _tile(t, tt)

    def body(h_ref, g_ref, tg_ref, dh_ref, loss_ref, dg_ref):
        @pl.when(pl.program_id(0) == 0)
        def _():
            loss_ref[...] = jnp.zeros_like(loss_ref)
            dg_ref[...] = jnp.zeros_like(dg_ref)

        x = h_ref[...]
        r = _rms_r(x)
        gv = g_ref[...]
        diff = x * r * gv - tg_ref[...]
        loss_ref[...] += 0.5 * jnp.sum(jnp.mean(diff * diff, axis=-1, keepdims=True), axis=0, keepdims=True)
        dx, dg = _rms_bwd(x, r, gv, diff * (1.0 / D_MODEL))
        dh_ref[...] = dx
        dg_ref[...] += dg

    return pl.pallas_call(
        body, name=name, grid=(t // tt,),
        in_specs=[pl.BlockSpec((tt, D_MODEL), lambda i: (i, 0)), pl.BlockSpec((1, D_MODEL), lambda i: (0, 0)),
                  pl.BlockSpec((tt, D_MODEL), lambda i: (i, 0))],
        out_specs=[pl.BlockSpec((tt, D_MODEL), lambda i: (i, 0)), pl.BlockSpec((SUBLANES, LANES), lambda i: (0, 0)),
                   pl.BlockSpec((1, D_MODEL), lambda i: (0, 0))],
        out_shape=[jax.ShapeDtypeStruct((t, D_MODEL), F32), jax.ShapeDtypeStruct((SUBLANES, LANES), F32),
                   jax.ShapeDtypeStruct((1, D_MODEL), F32)],
        compiler_params=_params("arbitrary"))(h, g, target)


ADAM_BLOCK_BYTES = 4 * 1024 * 1024


def _adam_rows(rows, cols):
    lanes = -(-cols // LANES) * LANES
    for cand in (1024, 512, 256, 128, 64, 32, 16, 8):
        if rows % cand == 0 and N_DEV * cand * lanes * 4 <= ADAM_BLOCK_BYTES:
            return cand
    return rows


def _sum_adamw(parts, w, m, v, *, name):
    nl, rows, cols = w.shape
    tr = _adam_rows(rows, cols)

    def body(p_ref, w_ref, m_ref, v_ref, g_ref, d_ref, nm_ref, nv_ref):
        g = p_ref[0]
        for k in range(1, N_DEV):
            g = g + p_ref[k]
        g_ref[...] = g
        nm = ADAM_B1 * m_ref[...] + (1.0 - ADAM_B1) * g
        nv = ADAM_B2 * v_ref[...] + (1.0 - ADAM_B2) * (g * g)
        m_hat = nm / (1.0 - ADAM_B1 ** ADAM_STEP)
        v_hat = nv / (1.0 - ADAM_B2 ** ADAM_STEP)
        d_ref[...] = -ADAM_LR * (m_hat / (jnp.sqrt(v_hat) + ADAM_EPS) + ADAM_WD * w_ref[...])
        nm_ref[...] = nm
        nv_ref[...] = nv

    blk = pl.BlockSpec((None, tr, cols), lambda l, r: (l, r, 0))
    return pl.pallas_call(
        body, name=name, grid=(nl, rows // tr),
        in_specs=[pl.BlockSpec((None, N_DEV, tr, cols), lambda l, r: (l, 0, r, 0)), blk, blk, blk],
        out_specs=[blk, blk, blk, blk],
        out_shape=[jax.ShapeDtypeStruct((nl, rows, cols), F32)] * 4,
        compiler_params=_params("arbitrary", "arbitrary"))(parts, w, m, v)


PACK_ROWS = 512


def _pack(arrays):
    flat = jnp.concatenate([a.astype(F32).reshape(-1) for a in arrays])
    pad = (-flat.shape[0]) % (PACK_ROWS * LANES)
    return jnp.pad(flat, (0, pad)).reshape(-1, LANES)


def _unpack(buf, shapes):
    flat = buf.reshape(-1)
    out, off = [], 0
    for s in shapes:
        n = math.prod(s)
        out.append(flat[off:off + n].reshape(s))
        off += n
    return out


def _to_proj_cols(w):
    z, xbc, dtc, u = jnp.split(w, [SSD_WIDTH, SSD_WIDTH + SSD_XBC, SSD_WIDTH + SSD_XBC + SSD_HEADS], axis=-1)
    pad = jnp.zeros(w.shape[:-1] + (LANES - SSD_HEADS,), w.dtype)
    return jnp.concatenate([xbc, z, u, dtc, pad], axis=-1)


def _from_proj_cols(w):
    xbc, z, u, dtc = (w[..., COL_XBC:COL_Z], w[..., COL_Z:COL_U], w[..., COL_U:COL_DT],
                      w[..., COL_DT:COL_DT + SSD_HEADS])
    return jnp.concatenate([z, xbc, dtc, u], axis=-1)


def _pad_heads(v):
    return jnp.pad(v, (0, LANES - SSD_HEADS)).reshape(1, LANES)


def _cat_cols(g):
    return jnp.transpose(g, (1, 0, 2)).reshape(g.shape[1], N_DEV * g.shape[2])


def _split_cols(w):
    r, c = w.shape
    return jnp.transpose(w.reshape(r, N_DEV, c // N_DEV), (1, 0, 2))


def _cat_rows(g):
    return g.reshape(N_DEV * g.shape[1], g.shape[2])


def _split_rows(w):
    return w.reshape(N_DEV, w.shape[0] // N_DEV, w.shape[1])


SHARDED = ("w_in", "w_out", "ffn_w_up", "ffn_w_down", "ple_w_gate", "ple_w_proj", "ssd_conv_w", "ffn_conv_w")
COL_SHARDED = ("w_in", "ffn_w_up", "ple_w_proj", "ssd_conv_w", "ffn_conv_w")
MATMUL_W = SHARDED[:6]
REPLICATED = ("mix_norm_g", "ssd_conv_b", "ssd_dt_bias", "ssd_a_log", "ssd_d", "ssd_norm_g", "pool_w", "pool_scale",
              "ffn_norm_g", "ffn_conv_b", "ple_norm_g", "final_norm_g")
WEIGHTS = ("mix_norm_g", "w_in", "ssd_conv_w", "ssd_conv_b", "ssd_dt_bias", "ssd_a_log", "ssd_d", "ssd_norm_g",
           "pool_w", "pool_scale", "w_out", "ffn_norm_g", "ffn_w_up", "ffn_conv_w", "ffn_conv_b", "ffn_w_down",
           "ple_norm_g", "ple_w_gate", "ple_w_proj", "final_norm_g")


FIRST_USED = ("w_in", "ssd_conv_w")
LATER_USED = tuple(k for k in SHARDED if k not in FIRST_USED)
LAST_MADE = ("w_out", "ssd_conv_w", "w_in")
EARLY_MADE = tuple(k for k in SHARDED if k not in LAST_MADE)
TRANSPOSED = ("w_in", "w_out", "ffn_w_up", "ffn_w_down", "ple_w_gate")


def _pick(names, per_sharded):
    return [per_sharded[SHARDED.index(k)] for k in names]


def _put(names, per_sharded, values):
    out = list(per_sharded)
    for k, val in zip(names, values):
        out[SHARDED.index(k)] = val
    return out


def _assemble(names, gathered):
    full = {}
    for k, g in zip(names, gathered):
        full[k] = _cat_cols(g) if k in COL_SHARDED else _cat_rows(g)
        if k == "w_in":
            full[k] = _to_proj_cols(full[k])
        if k in TRANSPOSED:
            full[k + "_t"] = full[k].T
    return full


def _grad_shards(names, grads):
    out = []
    for k in names:
        g = _from_proj_cols(grads[k]) if k == "w_in" else grads[k]
        out.append(_split_cols(g) if k in COL_SHARDED else _split_rows(g))
    return out


def _layer_fwd(i, h1, p_i, lw, rep, consts, ex_own, ex):
    tril, e_mat = consts
    row = lambda v: v.reshape(1, -1)
    dtb, alog = _pad_heads(rep["ssd_dt_bias"]), _pad_heads(rep["ssd_a_log"])
    dexp = row(jnp.repeat(rep["ssd_d"], SSD_HEAD_DIM))
    pw = rep["pool_w"].astype(BF16)
    proj = _norm_matmul(h1, lw["w_in"], row(rep["mix_norm_g"]), tt=512, tn=N_PROJ, name=f"in_proj_{i}")
    yssd, ypre, states, own = _ssd_fwd(proj, lw["ssd_conv_w"], row(rep["ssd_conv_b"]), dtb, alog, dexp,
                                       row(rep["ssd_norm_g"]), tril, e_mat, ts=512, name=f"ssd_fwd_{i}", ex=ex_own)
    if ex_own is not None:
        lw = dict(lw, **_assemble(LATER_USED, own))
    h2, ymix, n2 = _mix_out(h1, yssd, proj, pw, row(rep["pool_scale"]), lw["w_out"], row(rep["ffn_norm_g"]), tt=512,
                            name=f"mix_out_{i}")
    h3, act, pre, up, gathered = _ffn_fwd(h2, n2, lw["ffn_w_up"], lw["ffn_conv_w"], row(rep["ffn_conv_b"]),
                                          lw["ffn_w_down"], tt=256, name=f"ffn_fwd_{i}", ex=ex)
    h4 = _ple_fwd(h3, p_i, row(rep["ple_norm_g"]), lw["ple_w_gate"], lw["ple_w_proj"], tt=512, name=f"ple_fwd_{i}")
    saved = dict(h1=h1, proj=proj, ypre=ypre, states=states, ymix=ymix, h2=h2, n2=n2, up=up, pre=pre, act=act, h3=h3,
                 dtb=dtb, alog=alog, dexp=dexp, pw=pw)
    return h4, saved, lw, gathered


def _layer_bwd(i, dh, p_i, lw, rep, s, consts, pending, parts, own_early):
    tril, triu, e_mat = consts
    ex = None if pending is None else _Exchange(pending, scatter=True, layer=i + 1, into=parts)
    row = lambda v: v.reshape(1, -1)
    g = {}
    dh, g["ple_w_gate"], g["ple_w_proj"], dg3 = _ple_bwd(dh, s["h3"], p_i, row(rep["ple_norm_g"]), lw["ple_w_gate"],
                                                         lw["ple_w_gate_t"], lw["ple_w_proj"], tt=512,
                                                         name=f"ple_bwd_{i}")
    g["ple_norm_g"] = dg3.reshape(-1)
    g["ffn_w_down"] = _matmul_tn(s["act"], dh, tm=D_FF // 2, tn=D_MODEL, tk=1024, name=f"dw_down_{i}")
    dact = _norm_matmul(dh, lw["ffn_w_down_t"], tt=512, tn=D_FF, name=f"d_act_{i}")
    dup, g["ffn_conv_w"], dcb, scattered = _ffn_act_bwd(s["up"], s["pre"], dact, lw["ffn_conv_w"], tt=256,
                                                        name=f"ffn_act_bwd_{i}", ex=ex)
    if ex is not None:
        parts = scattered
    g["ffn_conv_b"] = dcb.reshape(-1)
    g["ffn_w_up"] = _matmul_tn(s["n2"], dup, tm=D_MODEL, tn=D_UP // 4, tk=1024, name=f"dw_up_{i}")
    dh, dg2 = _matmul_rmsbwd(dup, lw["ffn_w_up_t"], s["h2"], row(rep["ffn_norm_g"]), dh, tt=512, tk=D_UP,
                             name=f"ffn_up_bwd_{i}")
    g["ffn_norm_g"] = dg2.reshape(-1)
    dymix, g["w_out"] = _out_bwd(dh, s["ymix"], lw["w_out_t"], tt=512, name=f"out_bwd_{i}")
    du, g["pool_w"], dsc = _pool_bwd(dymix, s["proj"], s["pw"], jnp.swapaxes(s["pw"], 1, 2), row(rep["pool_scale"]),
                                     tt=512, name=f"pool_bwd_{i}")
    g["pool_scale"] = dsc.reshape(-1)
    ex_own = None
    if own_early:
        ex_own = _Exchange(_grad_shards(EARLY_MADE, g), scatter=True, layer=i, into=_pick(EARLY_MADE, parts))
    (dproj, g["ssd_conv_w"], dcb, ddtb, dalog, dd, dng), own = _ssd_bwd(
        dymix, s["proj"], s["ypre"], s["states"], du, lw["ssd_conv_w"], row(rep["ssd_conv_b"]), s["dtb"], s["alog"],
        s["dexp"], row(rep["ssd_norm_g"]), tril, triu, e_mat, ts=512, name=f"ssd_bwd_{i}", ex=ex_own)
    if own_early:
        parts = _put(EARLY_MADE, parts, own)
    g["ssd_conv_b"], g["ssd_norm_g"] = dcb.reshape(-1), dng.reshape(-1)
    g["ssd_dt_bias"], g["ssd_a_log"], g["ssd_d"] = ddtb[0, :SSD_HEADS], dalog[0, :SSD_HEADS], dd[0, :SSD_HEADS]
    dh, g["w_in"], dg1 = _in_bwd(dproj, s["h1"], row(rep["mix_norm_g"]), lw["w_in_t"], dh, tt=256, name=f"in_bwd_{i}")
    g["mix_norm_g"] = dg1.reshape(-1)
    return dh, g, parts


def kernel(x, p, mix_norm_g, w_in, ssd_conv_w, ssd_conv_b, ssd_dt_bias, ssd_a_log, ssd_d, ssd_norm_g, pool_w, pool_scale, w_out, ffn_norm_g, ffn_w_up, ffn_conv_w, ffn_conv_b, ffn_w_down, ple_norm_g, ple_w_gate, ple_w_proj, final_norm_g, loss_target, m_mix_norm_g, m_w_in, m_ssd_conv_w, m_ssd_conv_b, m_ssd_dt_bias, m_ssd_a_log, m_ssd_d, m_ssd_norm_g, m_pool_w, m_pool_scale, m_w_out, m_ffn_norm_g, m_ffn_w_up, m_ffn_conv_w, m_ffn_conv_b, m_ffn_w_down, m_ple_norm_g, m_ple_w_gate, m_ple_w_proj, m_final_norm_g, v_mix_norm_g, v_w_in, v_ssd_conv_w, v_ssd_conv_b, v_ssd_dt_bias, v_ssd_a_log, v_ssd_d, v_ssd_norm_g, v_pool_w, v_pool_scale, v_w_out, v_ffn_norm_g, v_ffn_w_up, v_ffn_conv_w, v_ffn_conv_b, v_ffn_w_down, v_ple_norm_g, v_ple_w_gate, v_ple_w_proj, v_final_norm_g):
    w = dict(mix_norm_g=mix_norm_g, w_in=w_in, ssd_conv_w=ssd_conv_w, ssd_conv_b=ssd_conv_b, ssd_dt_bias=ssd_dt_bias,
             ssd_a_log=ssd_a_log, ssd_d=ssd_d, ssd_norm_g=ssd_norm_g, pool_w=pool_w, pool_scale=pool_scale, w_out=w_out,
             ffn_norm_g=ffn_norm_g, ffn_w_up=ffn_w_up, ffn_conv_w=ffn_conv_w, ffn_conv_b=ffn_conv_b,
             ffn_w_down=ffn_w_down, ple_norm_g=ple_norm_g, ple_w_gate=ple_w_gate, ple_w_proj=ple_w_proj,
             final_norm_g=final_norm_g)
    m = dict(mix_norm_g=m_mix_norm_g, w_in=m_w_in, ssd_conv_w=m_ssd_conv_w, ssd_conv_b=m_ssd_conv_b,
             ssd_dt_bias=m_ssd_dt_bias, ssd_a_log=m_ssd_a_log, ssd_d=m_ssd_d, ssd_norm_g=m_ssd_norm_g, pool_w=m_pool_w,
             pool_scale=m_pool_scale, w_out=m_w_out, ffn_norm_g=m_ffn_norm_g, ffn_w_up=m_ffn_w_up,
             ffn_conv_w=m_ffn_conv_w, ffn_conv_b=m_ffn_conv_b, ffn_w_down=m_ffn_w_down, ple_norm_g=m_ple_norm_g,
             ple_w_gate=m_ple_w_gate, ple_w_proj=m_ple_w_proj, final_norm_g=m_final_norm_g)
    v = dict(mix_norm_g=v_mix_norm_g, w_in=v_w_in, ssd_conv_w=v_ssd_conv_w, ssd_conv_b=v_ssd_conv_b,
             ssd_dt_bias=v_ssd_dt_bias, ssd_a_log=v_ssd_a_log, ssd_d=v_ssd_d, ssd_norm_g=v_ssd_norm_g, pool_w=v_pool_w,
             pool_scale=v_pool_scale, w_out=v_w_out, ffn_norm_g=v_ffn_norm_g, ffn_w_up=v_ffn_w_up,
             ffn_conv_w=v_ffn_conv_w, ffn_conv_b=v_ffn_conv_b, ffn_w_down=v_ffn_w_down, ple_norm_g=v_ple_norm_g,
             ple_w_gate=v_ple_w_gate, ple_w_proj=v_ple_w_proj, final_norm_g=v_final_norm_g)

    tril = jnp.tril(jnp.ones((CHUNK, CHUNK), BF16))
    triu = tril.T
    e_mat = (jnp.arange(SSD_WIDTH)[None, :] // SSD_HEAD_DIM == jnp.arange(LANES)[:, None]).astype(BF16)
    rep = [{k: w[k][i] for k in REPLICATED if k != "final_norm_g"} for i in range(DEPTH)]
    p_loc = p[:, 0]

    shards = [w[k].astype(BF16) if k in MATMUL_W else w[k] for k in SHARDED]
    lw = _assemble(FIRST_USED, _exchange_call(_Exchange(_pick(FIRST_USED, shards), scatter=False, layer=0),
                                              "gather_weights_0"))
    h, saved, layer_w = x[0], [], []
    for i in range(DEPTH):
        ex_own = _Exchange(_pick(LATER_USED, shards), scatter=False, layer=0) if i == 0 else None
        ex = _Exchange(shards, scatter=False, layer=i + 1) if i + 1 < DEPTH else None
        h, s, lw, gathered = _layer_fwd(i, h, p_loc[i], lw, rep[i], (tril, e_mat), ex_own, ex)
        saved.append(s)
        layer_w.append(lw)
        lw = _assemble(SHARDED, gathered)

    dh, loss_blk, dgf = _loss_head(h, final_norm_g.reshape(1, -1), loss_target[0], tt=512, name="loss_head")
    loss = lax.psum(loss_blk[0, 0], ("x", "y", "c"))

    rep_grads = [None] * DEPTH
    pending, parts = None, None
    for i in reversed(range(DEPTH)):
        dh, g, parts = _layer_bwd(i, dh, p_loc[i], layer_w[i], rep[i], saved[i], (tril, triu, e_mat), pending, parts,
                                  own_early=(i == 0))
        pending = _grad_shards(SHARDED, g) if i > 0 else _grad_shards(LAST_MADE, g)
        rep_grads[i] = g
    parts = _put(LAST_MADE, parts, _exchange_call(
        _Exchange(pending, scatter=True, layer=0, into=_pick(LAST_MADE, parts)), "scatter_grads_0"))

    out = {}
    for k, part in zip(SHARDED, parts):
        out[k] = _sum_adamw(part, w[k], m[k], v[k], name=f"adamw_{k}")

    rp_grads = [dgf.reshape(-1) if k == "final_norm_g" else jnp.stack([rep_grads[i][k] for i in range(DEPTH)])
                for k in REPLICATED]
    rp_shapes = [w[k].shape for k in REPLICATED]
    rp_parts = _exchange_call(_Exchange([_pack(rp_grads)[None]], scatter=False, layer=0), "gather_replicated_grads")[0]
    rp_out = _sum_adamw(rp_parts[None], *[_pack([d[k] for k in REPLICATED])[None] for d in (w, m, v)],
                        name="adamw_replicated")
    for j in range(4):
        for k, arr in zip(REPLICATED, _unpack(rp_out[j][0], rp_shapes)):
            out.setdefault(k, [None] * 4)[j] = arr
    results = [out[k][j] for j in range(4) for k in WEIGHTS]
    return (loss, dh[None], *results)
```

```python
import functools
import math

import jax
import jax.numpy as jnp
from jax import lax
from jax.experimental import pallas as pl
from jax.experimental.pallas import tpu as pltpu

F32 = jnp.float32
BF16 = jnp.bfloat16

N_DEV = 8
EPS = 1e-6
DEPTH = 4
D_MODEL = 1024
D_PLE = 256
SSD_WIDTH = 512
SSD_HEADS = 8
SSD_HEAD_DIM = 64
SSD_GROUPS = 2
SSD_STATE = 128
SSD_CONV = 4
CHUNK = 128
SSD_XBC = 1024
POOL_WINDOWS = (2, 4, 8, 16)
POOL_WIDTH = 512
POOL_GROUP = 128
POOL_HALO = 16
D_IN_PROJ = 2056
D_FF = 2816
D_UP = 2 * D_FF
FFN_CONV = 3
SUBLANES = 8
LANES = 128
N_PROJ = 2176
COL_XBC, COL_Z, COL_U, COL_DT = 0, 1024, 1536, 2048
N_PAIRS = SSD_HEADS // 2
ADAM_LR, ADAM_B1, ADAM_B2, ADAM_EPS, ADAM_WD, ADAM_STEP = 0.001, 0.9, 0.999, 1e-08, 0.01, 10
GELU_C = math.sqrt(2.0 / math.pi)
GELU_A = 0.044715
VMEM_LIMIT = 56 * 1024 * 1024

NT_DIMS = (((1,), (1,)), ((), ()))
TN_DIMS = (((0,), (0,)), ((), ()))


def _params(*sem):
    return pltpu.CompilerParams(dimension_semantics=sem, vmem_limit_bytes=VMEM_LIMIT)


def _dot(a, b):
    return jnp.dot(a, b, preferred_element_type=F32)


def _dot_nt(a, b):
    return lax.dot_general(a, b, NT_DIMS, preferred_element_type=F32)


def _dot_tn(a, b):
    return lax.dot_general(a, b, TN_DIMS, preferred_element_type=F32)


def _split3(a):
    hi = a.astype(BF16)
    r1 = a - hi.astype(F32)
    mid = r1.astype(BF16)
    return hi, mid, (r1 - mid.astype(F32)).astype(BF16)


def _hdot(a, b):
    if a.dtype == BF16:
        return sum(_dot(a, piece) for piece in _split3(b))
    return sum(_dot(piece, b) for piece in _split3(a))


def _headsum(q, e):
    return sum(_dot_nt(piece, e) for piece in _split3(q))


def _colsum(v):
    return jnp.sum(v, axis=0, keepdims=True)


def _sigmoid(v):
    return 1.0 / (1.0 + jnp.exp(-v))


def _softplus(v):
    e = jnp.exp(-jnp.abs(v))
    return jnp.maximum(v, 0.0) + jnp.where(e < 1e-4, e * (1.0 - 0.5 * e), jnp.log(1.0 + e))


def _rms_r(x):
    return lax.rsqrt(jnp.mean(x * x, axis=-1, keepdims=True) + EPS)


def _rms_bwd(x, r, g, dn):
    xhat = x * r
    gd = dn * g
    dx = r * (gd - xhat * jnp.mean(gd * xhat, axis=-1, keepdims=True))
    return dx, _colsum(dn * xhat)


def _gelu(v):
    return 0.5 * v * (1.0 + jnp.tanh(GELU_C * (v + GELU_A * v * v * v)))


def _gelu_grad(v):
    th = jnp.tanh(GELU_C * (v + GELU_A * v * v * v))
    return 0.5 * (1.0 + th) + 0.5 * v * (1.0 - th * th) * GELU_C * (1.0 + 3.0 * GELU_A * v * v)


def _tile(t, want):
    return min(t, want)


class _Exchange:
    def __init__(self, srcs, *, scatter, layer, into=None):
        self.srcs, self.scatter, self.layer = list(srcs), scatter, layer
        self.into = None if into is None else list(into)
        n = len(self.srcs)
        self.args = self.srcs + (self.into or [])
        self.in_specs = [pl.BlockSpec(memory_space=pl.ANY)] * len(self.args)
        if scatter:
            self.out_shape = [jax.ShapeDtypeStruct((DEPTH,) + s.shape, s.dtype) for s in self.srcs]
        else:
            self.out_shape = [jax.ShapeDtypeStruct((N_DEV,) + s.shape[1:], s.dtype) for s in self.srcs]
        self.out_specs = [pl.BlockSpec(memory_space=pl.ANY)] * n
        self.scratch = [pltpu.SemaphoreType.DMA((n, N_DEV - 1)), pltpu.SemaphoreType.DMA((n, N_DEV - 1)),
                        pltpu.SemaphoreType.DMA((n,))]

    def aliases(self, n_in_before, n_out_before):
        if self.into is None:
            return {}
        n = len(self.srcs)
        return {n_in_before + n + a: n_out_before + a for a in range(n)}

    def ops(self, in_refs, out_refs, sems):
        send_sems, recv_sems, local_sems = sems
        n = len(self.srcs)

        def copies():
            x, y, c = lax.axis_index("x"), lax.axis_index("y"), lax.axis_index("c")
            me = 4 * x + 2 * y + c

            def block(a, idx):
                return in_refs[a].at[idx] if self.scatter else in_refs[a].at[self.layer]

            def slot(a, idx):
                return out_refs[a].at[self.layer].at[idx] if self.scatter else out_refs[a].at[idx]

            local = [pltpu.make_async_copy(block(a, me), slot(a, me), local_sems.at[a]) for a in range(n)]
            sends, recvs = [], []
            for k in range(1, N_DEV):
                px = 1 - x if k & 4 else x
                py = 1 - y if k & 2 else y
                pc = 1 - c if k & 1 else c
                peer = 4 * px + 2 * py + pc
                for a in range(n):
                    kw = dict(send_sem=send_sems.at[a, k - 1], recv_sem=recv_sems.at[a, k - 1], device_id=(px, py, pc),
                              device_id_type=pl.DeviceIdType.MESH)
                    sends.append(pltpu.make_async_remote_copy(src_ref=block(a, peer), dst_ref=slot(a, me), **kw))
                    recvs.append(pltpu.make_async_remote_copy(src_ref=block(a, peer), dst_ref=slot(a, peer), **kw))
            return local, sends, recvs

        def start():
            local, sends, _ = copies()
            for cp in local + sends:
                cp.start()

        def wait():
            local, sends, recvs = copies()
            for send, recv in zip(sends, recvs):
                send.wait_send()
                recv.wait_recv()
            for cp in local:
                cp.wait()

        return start, wait


def _exchange_call(ex, name):
    n_in, n = len(ex.args), len(ex.srcs)

    def body(*refs):
        start, wait = ex.ops(refs[:n_in], refs[n_in:n_in + n], refs[n_in + n:])
        start()
        wait()

    return pl.pallas_call(
        body, name=name, in_specs=ex.in_specs, out_specs=ex.out_specs, out_shape=ex.out_shape,
        scratch_shapes=ex.scratch, input_output_aliases=ex.aliases(0, 0))(*ex.args)


def _split_refs(refs, counts):
    out, k = [], 0
    for cnt in counts:
        out.append(refs[k:k + cnt])
        k += cnt
    return out


def _ex_parts(ex):
    if ex is None:
        return [], [], [], [], [], (0, 0, 0)
    return ex.args, ex.in_specs, ex.out_shape, ex.out_specs, ex.scratch, (len(ex.args), len(ex.srcs), 3)


def _norm_matmul(h, w, g=None, *, tt, tn, name):
    t, k = h.shape
    n = w.shape[1]
    tt, tn = _tile(t, tt), _tile(n, tn)
    normed = g is not None

    def body(*refs):
        if normed:
            h_ref, g_ref, w_ref, o_ref = refs
            x = h_ref[...]
            xn = (x * _rms_r(x) * g_ref[...]).astype(BF16)
        else:
            h_ref, w_ref, o_ref = refs
            xn = h_ref[...].astype(BF16)
        o_ref[...] = _dot(xn, w_ref[...])

    in_specs = [pl.BlockSpec((tt, k), lambda j, i: (i, 0))]
    args = [h]
    if normed:
        in_specs.append(pl.BlockSpec((1, k), lambda j, i: (0, 0)))
        args.append(g)
    in_specs.append(pl.BlockSpec((k, tn), lambda j, i: (0, j)))
    args.append(w)
    return pl.pallas_call(
        body, name=name, grid=(n // tn, t // tt), in_specs=in_specs,
        out_specs=pl.BlockSpec((tt, tn), lambda j, i: (i, j)), out_shape=jax.ShapeDtypeStruct((t, n), F32),
        compiler_params=_params("arbitrary", "arbitrary"))(*args)


def _matmul_tn(a, b, *, tm, tn, tk, name):
    t, m = a.shape
    n = b.shape[1]
    tm, tn, tk = _tile(m, tm), _tile(n, tn), _tile(t, tk)

    def body(a_ref, b_ref, o_ref):
        @pl.when(pl.program_id(2) == 0)
        def _():
            o_ref[...] = jnp.zeros_like(o_ref)

        o_ref[...] += _dot_tn(a_ref[...].astype(BF16), b_ref[...].astype(BF16))

    return pl.pallas_call(
        body, name=name, grid=(m // tm, n // tn, t // tk),
        in_specs=[pl.BlockSpec((tk, tm), lambda i, j, kk: (kk, i)), pl.BlockSpec((tk, tn), lambda i, j, kk: (kk, j))],
        out_specs=pl.BlockSpec((tm, tn), lambda i, j, kk: (i, j)),
        out_shape=jax.ShapeDtypeStruct((m, n), F32),
        compiler_params=_params("arbitrary", "arbitrary", "arbitrary"))(a, b)


def _ssd_tile_prologue(i_is_first, xbc_ref, halo_ref, dt_ref, cw_ref, cb_ref, dtb_ref, alog_ref, e_ref, buf, xc_scr,
                       xa_scr, a_scr, dte_scr, x_scr, ts):
    buf[0:SUBLANES, :] = jnp.where(i_is_first, 0.0, halo_ref[...])
    buf[SUBLANES:SUBLANES + ts, :] = xbc_ref[...]
    cw = cw_ref[...]
    xc = cb_ref[...]
    for k in range(SSD_CONV):
        off = SUBLANES - (SSD_CONV - 1) + k
        xc = xc + cw[k:k + 1, :] * buf[off:off + ts, :]
    if xc_scr is not None:
        xc_scr[...] = xc
    xa_scr[...] = xc * _sigmoid(xc)
    dt = _softplus(dt_ref[...] + dtb_ref[...])
    a_neg = -jnp.exp(alog_ref[...])
    a_scr[...] = dt * a_neg
    dte = _hdot(dt, e_ref[...])
    dte_scr[...] = dte
    x_scr[...] = xa_scr[:, 0:SSD_WIDTH] * dte
    return dt, a_neg


def _chunk_decays(a_c, tril, e):
    cs = _hdot(tril, a_c)
    cs_t = cs.T
    cs_e = _hdot(cs, e)
    last_e = cs_e[CHUNK - 1:CHUNK, :]
    return cs, cs_t, cs_e, last_e


def _ssd_fwd(proj, cw, cb, dtb, alog, dexp, ng, tril, e, *, ts, name, ex=None):
    t = proj.shape[0]
    ts = _tile(t, ts)
    nch = ts // CHUNK
    hb = ts // SUBLANES
    nt = t // ts
    ex_args, ex_in_specs, ex_out_shape, ex_out_specs, ex_scratch, ex_counts = _ex_parts(ex)

    def body(*refs):
        ((xbc_ref, halo_ref, z_ref, dt_ref, cw_ref, cb_ref, dtb_ref, alog_ref, dexp_ref, ng_ref, tril_ref, e_ref),
         ex_in, (y_ref, ypre_ref, st_ref), ex_out, (buf, xa_scr, a_scr, dte_scr, x_scr, ys_scr, hstate),
         ex_sems) = _split_refs(refs, (12, ex_counts[0], 3, ex_counts[1], 7, ex_counts[2]))
        i = pl.program_id(0)
        if ex is not None:
            ex_start, ex_wait = ex.ops(ex_in, ex_out, ex_sems)
            pl.when(i == 0)(ex_start)

        @pl.when(i == 0)
        def _():
            hstate[...] = jnp.zeros_like(hstate)

        _ssd_tile_prologue(i == 0, xbc_ref, halo_ref, dt_ref, cw_ref, cb_ref, dtb_ref, alog_ref, e_ref, buf, None,
                           xa_scr, a_scr, dte_scr, x_scr, ts)
        tril = tril_ref[...]
        e_mat = e_ref[...]
        causal = (lax.broadcasted_iota(jnp.int32, (CHUNK, CHUNK), 0)
                  >= lax.broadcasted_iota(jnp.int32, (CHUNK, CHUNK), 1))
        lane = lax.broadcasted_iota(jnp.int32, (CHUNK, LANES), 1)

        def chunk(c, carry):
            r0 = pl.multiple_of(c * CHUNK, CHUNK)
            rows = pl.ds(r0, CHUNK)
            cs, cs_t, cs_e, last_e = _chunk_decays(a_scr[rows, :], tril, e_mat)
            decay_e = jnp.exp(last_e - cs_e)
            ecs_e = jnp.exp(cs_e)
            xc = x_scr[rows, :]
            xb = xc.astype(BF16)
            xd = (xc * decay_e).astype(BF16)
            for g in range(SSD_GROUPS):
                bg = xa_scr[rows, SSD_WIDTH + g * SSD_STATE:SSD_WIDTH + (g + 1) * SSD_STATE].astype(BF16)
                cg = xa_scr[rows, SSD_WIDTH + (SSD_GROUPS + g) * SSD_STATE:
                            SSD_WIDTH + (SSD_GROUPS + g + 1) * SSD_STATE].astype(BF16)
                cbm = _dot_nt(cg, bg)
                for jj in range(2):
                    j = 2 * g + jj
                    cols = slice(j * LANES, (j + 1) * LANES)
                    xp = xb[:, cols]
                    ypair = jnp.zeros((CHUNK, LANES), F32)
                    for hh in range(2):
                        h = 2 * j + hh
                        seg = jnp.exp(jnp.where(causal, cs[:, h:h + 1] - cs_t[h:h + 1, :], -jnp.inf))
                        m = (cbm * seg).astype(BF16)
                        half = (lane < SSD_HEAD_DIM) if hh == 0 else (lane >= SSD_HEAD_DIM)
                        ypair = ypair + _dot(m, jnp.where(half, xp, jnp.zeros_like(xp)))
                    hp = hstate[j]
                    st_ref[c, j] = hp
                    ypair = ypair + _dot(cg, hp.astype(BF16)) * ecs_e[:, cols]
                    ys_scr[rows, cols] = ypair
                    hstate[j] = hp * jnp.exp(last_e[:, cols]) + _dot_tn(bg, xd[:, cols])
            return carry

        lax.fori_loop(0, nch, chunk, 0)
        ypre = ys_scr[...] + xa_scr[:, 0:SSD_WIDTH] * dexp_ref[...]
        ypre_ref[...] = ypre
        z = z_ref[...]
        yg = ypre * (z * _sigmoid(z))
        gw = SSD_WIDTH // SSD_GROUPS
        outs = []
        for g in range(SSD_GROUPS):
            v = yg[:, g * gw:(g + 1) * gw]
            outs.append(v * _rms_r(v))
        y_ref[...] = jnp.concatenate(outs, axis=1) * ng_ref[...]
        if ex is not None:
            pl.when(i == nt - 1)(ex_wait)

    full = lambda shape: pl.BlockSpec(shape, lambda i: tuple(0 for _ in shape))
    outs = pl.pallas_call(
        body, name=name, grid=(nt,),
        in_specs=[pl.BlockSpec((ts, SSD_XBC), lambda i: (i, COL_XBC // SSD_XBC)),
                  pl.BlockSpec((SUBLANES, SSD_XBC), lambda i: (jnp.maximum(i * hb - 1, 0), COL_XBC // SSD_XBC)),
                  pl.BlockSpec((ts, SSD_WIDTH), lambda i: (i, COL_Z // SSD_WIDTH)),
                  pl.BlockSpec((ts, LANES), lambda i: (i, COL_DT // LANES)),
                  full((SSD_CONV, SSD_XBC)), full((1, SSD_XBC)), full((1, LANES)), full((1, LANES)),
                  full((1, SSD_WIDTH)), full((1, SSD_WIDTH)), full((CHUNK, CHUNK)), full((LANES, SSD_WIDTH))]
        + ex_in_specs,
        out_specs=[pl.BlockSpec((ts, SSD_WIDTH), lambda i: (i, 0)), pl.BlockSpec((ts, SSD_WIDTH), lambda i: (i, 0)),
                   pl.BlockSpec((nch, N_PAIRS, SSD_STATE, LANES), lambda i: (i, 0, 0, 0))] + ex_out_specs,
        out_shape=[jax.ShapeDtypeStruct((t, SSD_WIDTH), F32), jax.ShapeDtypeStruct((t, SSD_WIDTH), F32),
                   jax.ShapeDtypeStruct((t // CHUNK, N_PAIRS, SSD_STATE, LANES), F32)] + ex_out_shape,
        scratch_shapes=[pltpu.VMEM((SUBLANES + ts, SSD_XBC), F32), pltpu.VMEM((ts, SSD_XBC), F32),
                        pltpu.VMEM((ts, LANES), F32), pltpu.VMEM((ts, SSD_WIDTH), F32),
                        pltpu.VMEM((ts, SSD_WIDTH), F32), pltpu.VMEM((ts, SSD_WIDTH), F32),
                        pltpu.VMEM((N_PAIRS, SSD_STATE, LANES), F32)] + ex_scratch,
        input_output_aliases={} if ex is None else ex.aliases(12, 3),
        compiler_params=_params("arbitrary"))(proj, proj, proj, proj, cw, cb, dtb, alog, dexp, ng, tril, e, *ex_args)
    return outs[0], outs[1], outs[2], outs[3:]


def _ssd_bwd(dymix, proj, ypre, states, du, cw, cb, dtb, alog, dexp, ng, tril, triu, e, *, ts, name, ex=None):
    t = proj.shape[0]
    ts = _tile(t, ts)
    nch = ts // CHUNK
    hb = ts // SUBLANES
    nt = t // ts
    ex_args, ex_in_specs, ex_out_shape, ex_out_specs, ex_scratch, ex_counts = _ex_parts(ex)

    def body(*refs):
        ((dy_ref, xbc_ref, halo_ref, z_ref, dt_ref, ypre_ref, st_ref, du_ref, cw_ref, cb_ref, dtb_ref, alog_ref,
          dexp_ref, ng_ref, tril_ref, triu_ref, e_ref), ex_in,
         (dproj_ref, dcw_ref, dcb_ref, ddtb_ref, dalog_ref, dd_ref, dng_ref), ex_out,
         (buf, xc_scr, xa_scr, a_scr, dte_scr, x_scr, dyp_scr, dxa_scr, dx_scr, dbuf, carry, gstate),
         ex_sems) = _split_refs(refs, (17, ex_counts[0], 7, ex_counts[1], 12, ex_counts[2]))
        i = pl.program_id(0)
        if ex is not None:
            ex_start, ex_wait = ex.ops(ex_in, ex_out, ex_sems)
            pl.when(i == 0)(ex_start)

        @pl.when(i == 0)
        def _():
            gstate[...] = jnp.zeros_like(gstate)
            carry[...] = jnp.zeros_like(carry)
            for ref in (dcw_ref, dcb_ref, ddtb_ref, dalog_ref, dd_ref, dng_ref):
                ref[...] = jnp.zeros_like(ref)

        dt, a_neg = _ssd_tile_prologue(i == nt - 1, xbc_ref, halo_ref, dt_ref, cw_ref, cb_ref, dtb_ref, alog_ref, e_ref,
                                       buf, xc_scr, xa_scr, a_scr, dte_scr, x_scr, ts)
        tril = tril_ref[...]
        triu = triu_ref[...]
        e_mat = e_ref[...]
        causal = (lax.broadcasted_iota(jnp.int32, (CHUNK, CHUNK), 0)
                  >= lax.broadcasted_iota(jnp.int32, (CHUNK, CHUNK), 1))
        lane = lax.broadcasted_iota(jnp.int32, (CHUNK, LANES), 1)
        sub = lax.broadcasted_iota(jnp.int32, (CHUNK, LANES), 0)

        z = z_ref[...]
        sig = _sigmoid(z)
        zs = z * sig
        ypre = ypre_ref[...]
        yg = ypre * zs
        dout = dy_ref[...]
        ngv = ng_ref[...]
        gw = SSD_WIDTH // SSD_GROUPS
        dyg_parts, dng_parts = [], []
        for g in range(SSD_GROUPS):
            cols = slice(g * gw, (g + 1) * gw)
            v = yg[:, cols]
            dx, dg = _rms_bwd(v, _rms_r(v), ngv[:, cols], dout[:, cols])
            dyg_parts.append(dx)
            dng_parts.append(dg)
        dyg = jnp.concatenate(dyg_parts, axis=1)
        dng_ref[...] += jnp.concatenate(dng_parts, axis=1)
        dyp = dyg * zs
        dyp_scr[...] = dyp
        dproj_ref[:, COL_Z:COL_Z + SSD_WIDTH] = dyg * ypre * (sig * (1.0 + z * (1.0 - sig)))
        dproj_ref[:, COL_U:COL_U + POOL_WIDTH] = du_ref[...]
        xs_all = xa_scr[:, 0:SSD_WIDTH]
        dd_ref[...] += _headsum(jnp.broadcast_to(_colsum(dyp * xs_all), (SUBLANES, SSD_WIDTH)), e_mat)[0:1, :]

        def chunk(k, carry_):
            c = nch - 1 - k
            r0 = pl.multiple_of(c * CHUNK, CHUNK)
            rows = pl.ds(r0, CHUNK)
            a_c = a_scr[rows, :]
            cs, cs_t, cs_e, last_e = _chunk_decays(a_c, tril, e_mat)
            decay_e = jnp.exp(last_e - cs_e)
            ecs_e = jnp.exp(cs_e)
            elast_e = jnp.exp(last_e)
            xc = x_scr[rows, :]
            xb = xc.astype(BF16)
            xd = (xc * decay_e).astype(BF16)
            dyc = dyp_scr[rows, :]
            dcs = jnp.zeros((CHUNK, LANES), F32)
            dcs_neg_t = jnp.zeros((LANES, CHUNK), F32)
            qoff, rin, ghrow = [], [], []
            for g in range(SSD_GROUPS):
                b_cols = slice(SSD_WIDTH + g * SSD_STATE, SSD_WIDTH + (g + 1) * SSD_STATE)
                c_cols = slice(SSD_WIDTH + (SSD_GROUPS + g) * SSD_STATE, SSD_WIDTH + (SSD_GROUPS + g + 1) * SSD_STATE)
                bg = xa_scr[rows, b_cols].astype(BF16)
                cg = xa_scr[rows, c_cols].astype(BF16)
                cbm = _dot_nt(cg, bg)
                dcb_m = jnp.zeros((CHUNK, CHUNK), F32)
                dbg = jnp.zeros((CHUNK, SSD_STATE), F32)
                dcg = jnp.zeros((CHUNK, SSD_STATE), F32)
                for jj in range(2):
                    j = 2 * g + jj
                    cols = slice(j * LANES, (j + 1) * LANES)
                    dyp_j = dyc[:, cols]
                    hp = st_ref[c, j]
                    hpb = hp.astype(BF16)
                    gt = gstate[j]
                    gtb = gt.astype(BF16)
                    ecs = ecs_e[:, cols]
                    yoff = _dot(cg, hpb) * ecs
                    dye = (dyp_j * ecs).astype(BF16)
                    dcg = dcg + _dot_nt(dye, hpb)
                    dht = _dot_tn(cg, dye)
                    qoff.append(dyp_j * yoff)
                    xg = _dot(bg, gtb)
                    dxp = xg * decay_e[:, cols]
                    rin.append(xg * xc[:, cols])
                    dbg = dbg + _dot_nt(xd[:, cols], gtb)
                    ghrow.append(_colsum(gt * hp) * elast_e[:, cols])
                    gstate[j] = dht + gt * elast_e[:, cols]
                    for hh in range(2):
                        h = 2 * j + hh
                        seg = jnp.exp(jnp.where(causal, cs[:, h:h + 1] - cs_t[h:h + 1, :], -jnp.inf))
                        m = cbm * seg
                        half = (lane < SSD_HEAD_DIM) if hh == 0 else (lane >= SSD_HEAD_DIM)
                        dym = jnp.where(half, dyp_j, 0.0).astype(BF16)
                        w = _dot_nt(dym, xb[:, cols])
                        pm = w * m
                        dcs = dcs + jnp.where(lane == h, jnp.sum(pm, axis=1, keepdims=True), 0.0)
                        dcs_neg_t = dcs_neg_t + jnp.where(sub == h, _colsum(pm), 0.0)
                        dcb_m = dcb_m + w * seg
                        dxp = dxp + _dot_tn(m.astype(BF16), dym)
                    dx_scr[:, cols] = dxp
                dcbb = dcb_m.astype(BF16)
                dxa_scr[rows, c_cols] = dcg + _dot(dcbb, bg)
                dxa_scr[rows, b_cols] = dbg + _dot_tn(dcbb, cg)
            decay_th = jnp.exp(cs[CHUNK - 1:CHUNK, :] - cs)
            rd = _headsum(jnp.concatenate(rin, axis=1), e_mat) * decay_th
            dcs = dcs - dcs_neg_t.T + _headsum(jnp.concatenate(qoff, axis=1), e_mat) - rd
            gh = _headsum(jnp.broadcast_to(jnp.concatenate(ghrow, axis=1), (SUBLANES, SSD_WIDTH)), e_mat)[0:1, :]
            dcs = dcs + jnp.where(sub == CHUNK - 1, _colsum(rd) + gh, 0.0)
            da = _hdot(triu, dcs)
            dx_all = dx_scr[...]
            xs = xa_scr[rows, 0:SSD_WIDTH]
            dt_c = _softplus(dt_ref[rows, :] + dtb_ref[...])
            ddt = da * a_neg + _headsum(dx_all * xs, e_mat)
            dalog_ref[...] += _colsum(da * dt_c) * a_neg
            ddtraw = ddt * _sigmoid(dt_ref[rows, :] + dtb_ref[...])
            dproj_ref[rows, COL_DT:COL_DT + LANES] = ddtraw
            ddtb_ref[...] += _colsum(ddtraw)
            dxa_scr[rows, 0:SSD_WIDTH] = dx_all * dte_scr[rows, :] + dyc * dexp_ref[...]
            return carry_

        lax.fori_loop(0, nch, chunk, 0)

        xcv = xc_scr[...]
        sgc = _sigmoid(xcv)
        dxc = dxa_scr[...] * (sgc * (1.0 + xcv * (1.0 - sgc)))
        dcb_ref[...] += _colsum(dxc)
        dbuf[0:ts, :] = dxc
        dbuf[ts:ts + SUBLANES, :] = carry[...]
        cwv = cw_ref[...]
        dxbc = jnp.zeros((ts, SSD_XBC), F32)
        dcw_rows = []
        for k in range(SSD_CONV):
            off = SUBLANES - (SSD_CONV - 1) + k
            dcw_rows.append(_colsum(dxc * buf[off:off + ts, :]))
            back = SSD_CONV - 1 - k
            dxbc = dxbc + cwv[k:k + 1, :] * dbuf[back:back + ts, :]
        dcw_ref[...] += jnp.concatenate(dcw_rows, axis=0)
        dproj_ref[:, COL_XBC:COL_XBC + SSD_XBC] = dxbc
        carry[...] = dxc[0:SUBLANES, :]
        if ex is not None:
            pl.when(i == nt - 1)(ex_wait)

    rev = lambda i: nt - 1 - i
    full = lambda shape: pl.BlockSpec(shape, lambda i: tuple(0 for _ in shape))
    outs = pl.pallas_call(
        body, name=name, grid=(nt,),
        in_specs=[pl.BlockSpec((ts, SSD_WIDTH), lambda i: (rev(i), 0)),
                  pl.BlockSpec((ts, SSD_XBC), lambda i: (rev(i), COL_XBC // SSD_XBC)),
                  pl.BlockSpec((SUBLANES, SSD_XBC), lambda i: (jnp.maximum(rev(i) * hb - 1, 0), COL_XBC // SSD_XBC)),
                  pl.BlockSpec((ts, SSD_WIDTH), lambda i: (rev(i), COL_Z // SSD_WIDTH)),
                  pl.BlockSpec((ts, LANES), lambda i: (rev(i), COL_DT // LANES)),
                  pl.BlockSpec((ts, SSD_WIDTH), lambda i: (rev(i), 0)),
                  pl.BlockSpec((nch, N_PAIRS, SSD_STATE, LANES), lambda i: (rev(i), 0, 0, 0)),
                  pl.BlockSpec((ts, POOL_WIDTH), lambda i: (rev(i), 0)),
                  full((SSD_CONV, SSD_XBC)), full((1, SSD_XBC)), full((1, LANES)), full((1, LANES)),
                  full((1, SSD_WIDTH)), full((1, SSD_WIDTH)), full((CHUNK, CHUNK)), full((CHUNK, CHUNK)),
                  full((LANES, SSD_WIDTH))] + ex_in_specs,
        out_specs=[pl.BlockSpec((ts, N_PROJ), lambda i: (rev(i), 0)),
                   full((SSD_CONV, SSD_XBC)), full((1, SSD_XBC)), full((1, LANES)), full((1, LANES)),
                   full((1, LANES)), full((1, SSD_WIDTH))] + ex_out_specs,
        out_shape=[jax.ShapeDtypeStruct((t, N_PROJ), F32),
                   jax.ShapeDtypeStruct((SSD_CONV, SSD_XBC), F32), jax.ShapeDtypeStruct((1, SSD_XBC), F32),
                   jax.ShapeDtypeStruct((1, LANES), F32), jax.ShapeDtypeStruct((1, LANES), F32),
                   jax.ShapeDtypeStruct((1, LANES), F32), jax.ShapeDtypeStruct((1, SSD_WIDTH), F32)] + ex_out_shape,
        scratch_shapes=[pltpu.VMEM((SUBLANES + ts, SSD_XBC), F32), pltpu.VMEM((ts, SSD_XBC), F32),
                        pltpu.VMEM((ts, SSD_XBC), F32), pltpu.VMEM((ts, LANES), F32),
                        pltpu.VMEM((ts, SSD_WIDTH), F32), pltpu.VMEM((ts, SSD_WIDTH), F32),
                        pltpu.VMEM((ts, SSD_WIDTH), F32), pltpu.VMEM((ts, SSD_XBC), F32),
                        pltpu.VMEM((CHUNK, SSD_WIDTH), F32), pltpu.VMEM((ts + SUBLANES, SSD_XBC), F32),
                        pltpu.VMEM((SUBLANES, SSD_XBC), F32), pltpu.VMEM((N_PAIRS, SSD_STATE, LANES), F32)]
        + ex_scratch,
        input_output_aliases={} if ex is None else ex.aliases(17, 7),
        compiler_params=_params("arbitrary"))(
            dymix, proj, proj, proj, proj, ypre, states, du, cw, cb, dtb, alog, dexp, ng, tril, triu, e, *ex_args)
    return outs[:7], outs[7:]


def _pooled(ubuf, u, pos, tt):
    out = []
    for gi, w in enumerate(POOL_WINDOWS):
        cols = slice(gi * POOL_GROUP, (gi + 1) * POOL_GROUP)
        acc = u[:, cols]
        for j in range(1, w):
            acc = acc + ubuf[POOL_HALO - j:POOL_HALO - j + tt, cols]
        out.append(acc / jnp.minimum(pos, float(w)) - u[:, cols])
    return out


def _mix_out(h, yssd, proj, pool_w, pool_scale, w_out, g_next, *, tt, name):
    t = h.shape[0]
    tt = _tile(t, tt)
    hb = tt // POOL_HALO

    def body(h_ref, ys_ref, u_ref, uh_ref, pw_ref, sc_ref, wo_ref, gn_ref, o_ref, ym_ref, n_ref, ubuf):
        i = pl.program_id(0)
        ubuf[0:POOL_HALO, :] = jnp.where(i == 0, 0.0, uh_ref[...])
        u = u_ref[...]
        ubuf[POOL_HALO:POOL_HALO + tt, :] = u
        pos = (i * tt + 1 + lax.broadcasted_iota(jnp.int32, (tt, 1), 0)).astype(F32)
        sc = sc_ref[...]
        parts = [ys_ref[...]]
        for gi, pooled in enumerate(_pooled(ubuf, u, pos, tt)):
            cols = slice(gi * POOL_GROUP, (gi + 1) * POOL_GROUP)
            parts.append(_dot(pooled.astype(BF16), pw_ref[gi]) * sc[:, cols])
        ymix = jnp.concatenate(parts, axis=1).astype(BF16)
        ym_ref[...] = ymix
        h2 = h_ref[...] + _dot(ymix, wo_ref[...])
        o_ref[...] = h2
        n_ref[...] = (h2 * _rms_r(h2) * gn_ref[...]).astype(BF16)

    full = lambda shape: pl.BlockSpec(shape, lambda i: tuple(0 for _ in shape))
    return pl.pallas_call(
        body, name=name, grid=(t // tt,),
        in_specs=[pl.BlockSpec((tt, D_MODEL), lambda i: (i, 0)), pl.BlockSpec((tt, SSD_WIDTH), lambda i: (i, 0)),
                  pl.BlockSpec((tt, POOL_WIDTH), lambda i: (i, COL_U // POOL_WIDTH)),
                  pl.BlockSpec((POOL_HALO, POOL_WIDTH), lambda i: (jnp.maximum(i * hb - 1, 0), COL_U // POOL_WIDTH)),
                  full((len(POOL_WINDOWS), POOL_GROUP, POOL_GROUP)), full((1, POOL_WIDTH)),
                  full((D_MODEL, D_MODEL)), full((1, D_MODEL))],
        out_specs=[pl.BlockSpec((tt, D_MODEL), lambda i: (i, 0))] * 3,
        out_shape=[jax.ShapeDtypeStruct((t, D_MODEL), F32), jax.ShapeDtypeStruct((t, D_MODEL), BF16),
                   jax.ShapeDtypeStruct((t, D_MODEL), BF16)],
        scratch_shapes=[pltpu.VMEM((POOL_HALO + tt, POOL_WIDTH), F32)],
        compiler_params=_params("arbitrary"))(h, yssd, proj, proj, pool_w, pool_scale, w_out, g_next)


def _out_bwd(dh, ymix, w_out_t, *, tt, name):
    t = dh.shape[0]
    tt = _tile(t, tt)

    def body(dh_ref, ym_ref, wt_ref, dym_ref, dw_ref):
        @pl.when(pl.program_id(0) == 0)
        def _():
            dw_ref[...] = jnp.zeros_like(dw_ref)

        dhb = dh_ref[...].astype(BF16)
        dym_ref[...] = _dot(dhb, wt_ref[...])
        dw_ref[...] += _dot_tn(ym_ref[...], dhb)

    return pl.pallas_call(
        body, name=name, grid=(t // tt,),
        in_specs=[pl.BlockSpec((tt, D_MODEL), lambda i: (i, 0)), pl.BlockSpec((tt, D_MODEL), lambda i: (i, 0)),
                  pl.BlockSpec((D_MODEL, D_MODEL), lambda i: (0, 0))],
        out_specs=[pl.BlockSpec((tt, D_MODEL), lambda i: (i, 0)), pl.BlockSpec((D_MODEL, D_MODEL), lambda i: (0, 0))],
        out_shape=[jax.ShapeDtypeStruct((t, D_MODEL), F32), jax.ShapeDtypeStruct((D_MODEL, D_MODEL), F32)],
        compiler_params=_params("arbitrary"))(dh, ymix, w_out_t)


def _pool_bwd(dymix, proj, pool_w, pool_w_t, pool_scale, *, tt, name):
    t = proj.shape[0]
    tt = _tile(t, tt)
    hb = tt // POOL_HALO
    nt = t // tt
    ng = len(POOL_WINDOWS)

    def body(dy_ref, dyh_ref, u_ref, uh_ref, pw_ref, pwt_ref, sc_ref, du_ref, dpw_ref, dsc_ref, ubuf, dbuf):
        i = pl.program_id(0)

        @pl.when(i == 0)
        def _():
            dpw_ref[...] = jnp.zeros_like(dpw_ref)
            dsc_ref[...] = jnp.zeros_like(dsc_ref)

        ubuf[0:POOL_HALO, :] = jnp.where(i == 0, 0.0, uh_ref[...])
        u = u_ref[...]
        ubuf[POOL_HALO:POOL_HALO + tt, :] = u
        pos = (i * tt + 1 + lax.broadcasted_iota(jnp.int32, (tt, 1), 0)).astype(F32)
        sc = sc_ref[...]
        dy = dy_ref[...]
        dyh = jnp.where(i == nt - 1, 0.0, dyh_ref[...])
        dsc_parts, du_parts = [], []
        for gi, pooled in enumerate(_pooled(ubuf, u, pos, tt)):
            w = POOL_WINDOWS[gi]
            cols = slice(gi * POOL_GROUP, (gi + 1) * POOL_GROUP)
            pb = pooled.astype(BF16)
            dsc_parts.append(_colsum(dy[:, cols] * _dot(pb, pw_ref[gi])))
            dmx = (dy[:, cols] * sc[:, cols]).astype(BF16)
            dpw_ref[gi] += _dot_tn(pb, dmx)
            dpool = _dot(dmx, pwt_ref[gi])
            dpool_h = _dot((dyh[:, cols] * sc[:, cols]).astype(BF16), pwt_ref[gi])
            dbuf[0:tt, cols] = dpool / jnp.minimum(pos, float(w))
            dbuf[tt:tt + POOL_HALO, cols] = dpool_h / float(w)
            acc = -dpool
            for j in range(w):
                acc = acc + dbuf[j:j + tt, cols]
            du_parts.append(acc)
        du_ref[...] = jnp.concatenate(du_parts, axis=1)
        dsc_ref[...] += jnp.concatenate(dsc_parts, axis=1)

    full = lambda shape: pl.BlockSpec(shape, lambda i: tuple(0 for _ in shape))
    ucol = COL_U // POOL_WIDTH
    return pl.pallas_call(
        body, name=name, grid=(nt,),
        in_specs=[pl.BlockSpec((tt, POOL_WIDTH), lambda i: (i, 1)),
                  pl.BlockSpec((POOL_HALO, POOL_WIDTH), lambda i: (jnp.minimum((i + 1) * hb, t // POOL_HALO - 1), 1)),
                  pl.BlockSpec((tt, POOL_WIDTH), lambda i: (i, ucol)),
                  pl.BlockSpec((POOL_HALO, POOL_WIDTH), lambda i: (jnp.maximum(i * hb - 1, 0), ucol)),
                  full((ng, POOL_GROUP, POOL_GROUP)), full((ng, POOL_GROUP, POOL_GROUP)), full((1, POOL_WIDTH))],
        out_specs=[pl.BlockSpec((tt, POOL_WIDTH), lambda i: (i, 0)), full((ng, POOL_GROUP, POOL_GROUP)),
                   full((1, POOL_WIDTH))],
        out_shape=[jax.ShapeDtypeStruct((t, POOL_WIDTH), F32), jax.ShapeDtypeStruct((ng, POOL_GROUP, POOL_GROUP), F32),
                   jax.ShapeDtypeStruct((1, POOL_WIDTH), F32)],
        scratch_shapes=[pltpu.VMEM((POOL_HALO + tt, POOL_WIDTH), F32), pltpu.VMEM((tt + POOL_HALO, POOL_WIDTH), F32)],
        compiler_params=_params("arbitrary"))(dymix, dymix, proj, proj, pool_w, pool_w_t, pool_scale)


def _in_bwd(dproj, h, g, w_in_t, dh, *, tt, name):
    t = h.shape[0]
    tt = _tile(t, tt)

    def body(dp_ref, h_ref, g_ref, wt_ref, dh_ref, o_ref, dw_ref, dg_ref):
        @pl.when(pl.program_id(0) == 0)
        def _():
            dw_ref[...] = jnp.zeros_like(dw_ref)
            dg_ref[...] = jnp.zeros_like(dg_ref)

        x = h_ref[...]
        r = _rms_r(x)
        gv = g_ref[...]
        dpb = dp_ref[...].astype(BF16)
        dw_ref[...] += _dot_tn((x * r * gv).astype(BF16), dpb)
        dx, dg = _rms_bwd(x, r, gv, _dot(dpb, wt_ref[...]))
        o_ref[...] = dh_ref[...] + dx
        dg_ref[...] += dg

    full = lambda shape: pl.BlockSpec(shape, lambda i: tuple(0 for _ in shape))
    row = lambda n: pl.BlockSpec((tt, n), lambda i: (i, 0))
    return pl.pallas_call(
        body, name=name, grid=(t // tt,),
        in_specs=[row(N_PROJ), row(D_MODEL), full((1, D_MODEL)), full((N_PROJ, D_MODEL)), row(D_MODEL)],
        out_specs=[row(D_MODEL), full((D_MODEL, N_PROJ)), full((1, D_MODEL))],
        out_shape=[jax.ShapeDtypeStruct((t, D_MODEL), F32), jax.ShapeDtypeStruct((D_MODEL, N_PROJ), F32),
                   jax.ShapeDtypeStruct((1, D_MODEL), F32)],
        compiler_params=_params("arbitrary"))(dproj, h, g, w_in_t, dh)


FFN_COLS = 256
N_SLABS = D_FF // FFN_COLS
N_SLAB_BUFS = 4


def _ffn_fwd(h, n2, w_up, cw, cb, w_down, *, tt, name, ex=None):
    t = h.shape[0]
    tt = _tile(t, tt)
    nt = t // tt
    ex_args, ex_in_specs, ex_out_shape, ex_out_specs, ex_scratch, ex_counts = _ex_parts(ex)

    def body(*refs):
        ((h_ref, n2_ref, wu_ref, cw_ref, cb_ref, wd_ref), ex_in, (o_ref, act_ref, pre_ref, up_ref), ex_out,
         (slab, halo), ex_sems) = _split_refs(refs, (6, ex_counts[0], 4, ex_counts[1], 2, ex_counts[2]))
        i = pl.program_id(0)
        if ex is not None:
            ex_start, ex_wait = ex.ops(ex_in, ex_out, ex_sems)
            pl.when(i == 0)(ex_start)

        @pl.when(i == 0)
        def _():
            halo[...] = jnp.zeros_like(halo)

        n2v = n2_ref[...]

        def slab_cols(s):
            return slice(s * FFN_COLS, (s + 1) * FFN_COLS), slice(D_FF + s * FFN_COLS, D_FF + (s + 1) * FFN_COLS)

        def project(s):
            return [_dot(n2v, wu_ref[:, cols]) for cols in slab_cols(s)]

        def conv(u, cols, buf_id):
            up_ref[:, cols] = u.astype(BF16)
            sb = slab.at[buf_id]
            sb[0:SUBLANES, :] = halo[:, cols]
            sb[SUBLANES:SUBLANES + tt, :] = u
            halo[:, cols] = u[tt - SUBLANES:tt, :]
            acc = cb_ref[:, cols] + cw_ref[FFN_CONV - 1:FFN_CONV, cols] * u
            for k in range(FFN_CONV - 1):
                off = SUBLANES - (FFN_CONV - 1) + k
                acc = acc + cw_ref[k:k + 1, cols] * sb[off:off + tt, :]
            pre_ref[:, cols] = acc
            return acc

        out = h_ref[...]
        ahead = project(0)
        for s in range(N_SLABS):
            (ug, uv), (gcols, vcols) = ahead, slab_cols(s)
            if s + 1 < N_SLABS:
                ahead = project(s + 1)
            gate = conv(ug, gcols, (2 * s) % N_SLAB_BUFS)
            val = conv(uv, vcols, (2 * s + 1) % N_SLAB_BUFS)
            act = (_gelu(gate) * val).astype(BF16)
            act_ref[:, s * FFN_COLS:(s + 1) * FFN_COLS] = act
            out = out + _dot(act, wd_ref[s * FFN_COLS:(s + 1) * FFN_COLS, :])
        o_ref[...] = out
        if ex is not None:
            pl.when(i == nt - 1)(ex_wait)

    full = lambda shape: pl.BlockSpec(shape, lambda i: tuple(0 for _ in shape))
    row = lambda n: pl.BlockSpec((tt, n), lambda i: (i, 0))
    outs = pl.pallas_call(
        body, name=name, grid=(nt,),
        in_specs=[row(D_MODEL), row(D_MODEL), full((D_MODEL, D_UP)), full((FFN_CONV, D_UP)), full((1, D_UP)),
                  full((D_FF, D_MODEL))] + ex_in_specs,
        out_specs=[row(D_MODEL), row(D_FF), row(D_UP), row(D_UP)] + ex_out_specs,
        out_shape=[jax.ShapeDtypeStruct((t, D_MODEL), F32), jax.ShapeDtypeStruct((t, D_FF), BF16),
                   jax.ShapeDtypeStruct((t, D_UP), F32), jax.ShapeDtypeStruct((t, D_UP), BF16)] + ex_out_shape,
        scratch_shapes=[pltpu.VMEM((N_SLAB_BUFS, SUBLANES + tt, FFN_COLS), F32), pltpu.VMEM((SUBLANES, D_UP), F32)]
        + ex_scratch,
        input_output_aliases={} if ex is None else ex.aliases(6, 4),
        compiler_params=_params("arbitrary"))(h, n2, w_up, cw, cb, w_down, *ex_args)
    return outs[0], outs[1], outs[2], outs[3], outs[4:]


def _ffn_bwd(dh, up, pre, h2, g2, w_down_t, w_up_t, cw, *, tt, name, ex=None):
    t = dh.shape[0]
    tt = _tile(t, tt)
    nt = t // tt
    ex_args, ex_in_specs, ex_out_shape, ex_out_specs, ex_scratch, ex_counts = _ex_parts(ex)

    def body(*refs):
        ((dh_ref, up_ref, pre_ref, h2_ref, g2_ref, wdt_ref, wut_ref, cw_ref), ex_in,
         (o_ref, dup_ref, dcw_ref, dcb_ref, dg_ref), ex_out, (slab, carry), ex_sems) = _split_refs(
            refs, (8, ex_counts[0], 5, ex_counts[1], 2, ex_counts[2]))
        i = pl.program_id(0)
        if ex is not None:
            ex_start, ex_wait = ex.ops(ex_in, ex_out, ex_sems)
            pl.when(i == 0)(ex_start)

        @pl.when(i == 0)
        def _():
            for ref in (dcw_ref, dcb_ref, dg_ref, carry):
                ref[...] = jnp.zeros_like(ref)

        dhv = dh_ref[...]
        dhb = dhv.astype(BF16)

        def slab_cols(s):
            return slice(s * FFN_COLS, (s + 1) * FFN_COLS), slice(D_FF + s * FFN_COLS, D_FF + (s + 1) * FFN_COLS)

        def d_act(s):
            return _dot(dhb, wdt_ref[:, slab_cols(s)[0]])

        def through_conv(dp, cols, buf_id):
            sb = slab.at[buf_id]
            sb[0:tt, :] = dp
            sb[tt:tt + SUBLANES, :] = carry[:, cols]
            carry[:, cols] = dp[0:SUBLANES, :]
            shifted = [sb[FFN_CONV - 1 - k:FFN_CONV - 1 - k + tt, :] for k in range(FFN_CONV - 1)] + [dp]
            x = up_ref[:, cols].astype(F32)
            dup = cw_ref[0:1, cols] * shifted[0]
            for k in range(1, FFN_CONV):
                dup = dup + cw_ref[k:k + 1, cols] * shifted[k]
            dcb_ref[:, cols] += _colsum(dp)
            dcw_ref[:, cols] += jnp.concatenate([_colsum(sh * x) for sh in shifted], axis=0)
            dupb = dup.astype(BF16)
            dup_ref[:, cols] = dupb
            return _dot(dupb, wut_ref[cols, :])

        dn = jnp.zeros((tt, D_MODEL), F32)
        ahead = d_act(0)
        for s in range(N_SLABS):
            da, (gcols, vcols) = ahead, slab_cols(s)
            if s + 1 < N_SLABS:
                ahead = d_act(s + 1)
            gate, val = pre_ref[:, gcols], pre_ref[:, vcols]
            dn = dn + through_conv(da * val * _gelu_grad(gate), gcols, (2 * s) % N_SLAB_BUFS)
            dn = dn + through_conv(da * _gelu(gate), vcols, (2 * s + 1) % N_SLAB_BUFS)
        xv = h2_ref[...]
        dx, dg = _rms_bwd(xv, _rms_r(xv), g2_ref[...], dn)
        o_ref[...] = dhv + dx
        dg_ref[...] += dg
        if ex is not None:
            pl.when(i == nt - 1)(ex_wait)

    rev = lambda i: nt - 1 - i
    full = lambda shape: pl.BlockSpec(shape, lambda i: tuple(0 for _ in shape))
    row = lambda n: pl.BlockSpec((tt, n), lambda i: (rev(i), 0))
    outs = pl.pallas_call(
        body, name=name, grid=(nt,),
        in_specs=[row(D_MODEL), row(D_UP), row(D_UP), row(D_MODEL), full((1, D_MODEL)), full((D_MODEL, D_FF)),
                  full((D_UP, D_MODEL)), full((FFN_CONV, D_UP))] + ex_in_specs,
        out_specs=[row(D_MODEL), row(D_UP), full((FFN_CONV, D_UP)), full((1, D_UP)), full((1, D_MODEL))]
        + ex_out_specs,
        out_shape=[jax.ShapeDtypeStruct((t, D_MODEL), F32), jax.ShapeDtypeStruct((t, D_UP), BF16),
                   jax.ShapeDtypeStruct((FFN_CONV, D_UP), F32), jax.ShapeDtypeStruct((1, D_UP), F32),
                   jax.ShapeDtypeStruct((1, D_MODEL), F32)] + ex_out_shape,
        scratch_shapes=[pltpu.VMEM((N_SLAB_BUFS, tt + SUBLANES, FFN_COLS), F32), pltpu.VMEM((SUBLANES, D_UP), F32)]
        + ex_scratch,
        input_output_aliases={} if ex is None else ex.aliases(8, 5),
        compiler_params=_params("arbitrary"))(dh, up, pre, h2, g2, w_down_t, w_up_t, cw, *ex_args)
    return outs[0], outs[1], outs[2], outs[3], outs[4], outs[5:]


def _ple_fwd(h, p, g, w_gate, w_proj, *, tt, name):
    t = h.shape[0]
    tt = _tile(t, tt)

    def body(h_ref, p_ref, g_ref, wg_ref, wp_ref, o_ref):
        x = h_ref[...]
        n = (x * _rms_r(x) * g_ref[...]).astype(BF16)
        gate = _sigmoid(_dot(n, wg_ref[...]))
        o_ref[...] = x + _dot(p_ref[...].astype(BF16), wp_ref[...]) * gate

    full = lambda shape: pl.BlockSpec(shape, lambda i: tuple(0 for _ in shape))
    return pl.pallas_call(
        body, name=name, grid=(t // tt,),
        in_specs=[pl.BlockSpec((tt, D_MODEL), lambda i: (i, 0)), pl.BlockSpec((tt, D_PLE), lambda i: (i, 0)),
                  full((1, D_MODEL)), full((D_MODEL, D_MODEL)), full((D_PLE, D_MODEL))],
        out_specs=pl.BlockSpec((tt, D_MODEL), lambda i: (i, 0)),
        out_shape=jax.ShapeDtypeStruct((t, D_MODEL), F32),
        compiler_params=_params("arbitrary"))(h, p, g, w_gate, w_proj)


def _ple_bwd(dh, h, p, g, w_gate, w_gate_t, w_proj, *, tt, name):
    t = h.shape[0]
    tt = _tile(t, tt)

    def body(dh_ref, h_ref, p_ref, g_ref, wg_ref, wgt_ref, wp_ref, o_ref, dwg_ref, dwp_ref, dg_ref):
        @pl.when(pl.program_id(0) == 0)
        def _():
            dwg_ref[...] = jnp.zeros_like(dwg_ref)
            dwp_ref[...] = jnp.zeros_like(dwp_ref)
            dg_ref[...] = jnp.zeros_like(dg_ref)

        x = h_ref[...]
        r = _rms_r(x)
        gv = g_ref[...]
        n = (x * r * gv).astype(BF16)
        gate = _sigmoid(_dot(n, wg_ref[...]))
        pb = p_ref[...].astype(BF16)
        pe = _dot(pb, wp_ref[...])
        dhv = dh_ref[...]
        dwp_ref[...] += _dot_tn(pb, (dhv * gate).astype(BF16))
        ds = (dhv * pe * gate * (1.0 - gate)).astype(BF16)
        dwg_ref[...] += _dot_tn(n, ds)
        dx, dg = _rms_bwd(x, r, gv, _dot(ds, wgt_ref[...]))
        o_ref[...] = dhv + dx
        dg_ref[...] += dg

    full = lambda shape: pl.BlockSpec(shape, lambda i: tuple(0 for _ in shape))
    row = lambda n: pl.BlockSpec((tt, n), lambda i: (i, 0))
    return pl.pallas_call(
        body, name=name, grid=(t // tt,),
        in_specs=[row(D_MODEL), row(D_MODEL), row(D_PLE), full((1, D_MODEL)), full((D_MODEL, D_MODEL)),
                  full((D_MODEL, D_MODEL)), full((D_PLE, D_MODEL))],
        out_specs=[row(D_MODEL), full((D_MODEL, D_MODEL)), full((D_PLE, D_MODEL)), full((1, D_MODEL))],
        out_shape=[jax.ShapeDtypeStruct((t, D_MODEL), F32), jax.ShapeDtypeStruct((D_MODEL, D_MODEL), F32),
                   jax.ShapeDtypeStruct((D_PLE, D_MODEL), F32), jax.ShapeDtypeStruct((1, D_MODEL), F32)],
        compiler_params=_params("arbitrary"))(dh, h, p, g, w_gate, w_gate_t, w_proj)


def _loss_head(h, g, target, *, tt, name):
    t = h.shape[0]
    tt = _tile(t, tt)

    def body(h_ref, g_ref, tg_ref, dh_ref, loss_ref, dg_ref):
        @pl.when(pl.program_id(0) == 0)
        def _():
            loss_ref[...] = jnp.zeros_like(loss_ref)
            dg_ref[...] = jnp.zeros_like(dg_ref)

        x = h_ref[...]
        r = _rms_r(x)
        gv = g_ref[...]
        diff = x * r * gv - tg_ref[...]
        loss_ref[...] += 0.5 * jnp.sum(jnp.mean(diff * diff, axis=-1, keepdims=True), axis=0, keepdims=True)
        dx, dg = _rms_bwd(x, r, gv, diff * (1.0 / D_MODEL))
        dh_ref[...] = dx
        dg_ref[...] += dg

    return pl.pallas_call(
        body, name=name, grid=(t // tt,),
        in_specs=[pl.BlockSpec((tt, D_MODEL), lambda i: (i, 0)), pl.BlockSpec((1, D_MODEL), lambda i: (0, 0)),
                  pl.BlockSpec((tt, D_MODEL), lambda i: (i, 0))],
        out_specs=[pl.BlockSpec((tt, D_MODEL), lambda i: (i, 0)), pl.BlockSpec((SUBLANES, LANES), lambda i: (0, 0)),
                   pl.BlockSpec((1, D_MODEL), lambda i: (0, 0))],
        out_shape=[jax.ShapeDtypeStruct((t, D_MODEL), F32), jax.ShapeDtypeStruct((SUBLANES, LANES), F32),
                   jax.ShapeDtypeStruct((1, D_MODEL), F32)],
        compiler_params=_params("arbitrary"))(h, g, target)


ADAM_BLOCK_BYTES = 4 * 1024 * 1024


def _adam_rows(rows, cols):
    lanes = -(-cols // LANES) * LANES
    for cand in (1024, 512, 256, 128, 64, 32, 16, 8):
        if rows % cand == 0 and N_DEV * cand * lanes * 4 <= ADAM_BLOCK_BYTES:
            return cand
    return rows


def _sum_adamw(parts, w, m, v, *, name):
    nl, rows, cols = w.shape
    tr = _adam_rows(rows, cols)

    def body(p_ref, w_ref, m_ref, v_ref, g_ref, d_ref, nm_ref, nv_ref):
        g = p_ref[0]
        for k in range(1, N_DEV):
            g = g + p_ref[k]
        g_ref[...] = g
        nm = ADAM_B1 * m_ref[...] + (1.0 - ADAM_B1) * g
        nv = ADAM_B2 * v_ref[...] + (1.0 - ADAM_B2) * (g * g)
        m_hat = nm / (1.0 - ADAM_B1 ** ADAM_STEP)
        v_hat = nv / (1.0 - ADAM_B2 ** ADAM_STEP)
        d_ref[...] = -ADAM_LR * (m_hat / (jnp.sqrt(v_hat) + ADAM_EPS) + ADAM_WD * w_ref[...])
        nm_ref[...] = nm
        nv_ref[...] = nv

    blk = pl.BlockSpec((None, tr, cols), lambda l, r: (l, r, 0))
    return pl.pallas_call(
        body, name=name, grid=(nl, rows // tr),
        in_specs=[pl.BlockSpec((None, N_DEV, tr, cols), lambda l, r: (l, 0, r, 0)), blk, blk, blk],
        out_specs=[blk, blk, blk, blk],
        out_shape=[jax.ShapeDtypeStruct((nl, rows, cols), F32)] * 4,
        compiler_params=_params("arbitrary", "arbitrary"))(parts, w, m, v)


PACK_ROWS = 512


def _pack(arrays):
    flat = jnp.concatenate([a.astype(F32).reshape(-1) for a in arrays])
    pad = (-flat.shape[0]) % (PACK_ROWS * LANES)
    return jnp.pad(flat, (0, pad)).reshape(-1, LANES)


def _unpack(buf, shapes):
    flat = buf.reshape(-1)
    out, off = [], 0
    for s in shapes:
        n = math.prod(s)
        out.append(flat[off:off + n].reshape(s))
        off += n
    return out


def _to_proj_cols(w):
    z, xbc, dtc, u = jnp.split(w, [SSD_WIDTH, SSD_WIDTH + SSD_XBC, SSD_WIDTH + SSD_XBC + SSD_HEADS], axis=-1)
    pad = jnp.zeros(w.shape[:-1] + (LANES - SSD_HEADS,), w.dtype)
    return jnp.concatenate([xbc, z, u, dtc, pad], axis=-1)


def _from_proj_cols(w):
    xbc, z, u, dtc = (w[..., COL_XBC:COL_Z], w[..., COL_Z:COL_U], w[..., COL_U:COL_DT],
                      w[..., COL_DT:COL_DT + SSD_HEADS])
    return jnp.concatenate([z, xbc, dtc, u], axis=-1)


def _pad_heads(v):
    return jnp.pad(v, (0, LANES - SSD_HEADS)).reshape(1, LANES)


def _cat_cols(g):
    return jnp.transpose(g, (1, 0, 2)).reshape(g.shape[1], N_DEV * g.shape[2])


def _split_cols(w):
    r, c = w.shape
    return jnp.transpose(w.reshape(r, N_DEV, c // N_DEV), (1, 0, 2))


def _cat_rows(g):
    return g.reshape(N_DEV * g.shape[1], g.shape[2])


def _split_rows(w):
    return w.reshape(N_DEV, w.shape[0] // N_DEV, w.shape[1])


SHARDED = ("w_in", "w_out", "ffn_w_up", "ffn_w_down", "ple_w_gate", "ple_w_proj", "ssd_conv_w", "ffn_conv_w")
COL_SHARDED = ("w_in", "ffn_w_up", "ple_w_proj", "ssd_conv_w", "ffn_conv_w")
MATMUL_W = SHARDED[:6]
REPLICATED = ("mix_norm_g", "ssd_conv_b", "ssd_dt_bias", "ssd_a_log", "ssd_d", "ssd_norm_g", "pool_w", "pool_scale",
              "ffn_norm_g", "ffn_conv_b", "ple_norm_g", "final_norm_g")
WEIGHTS = ("mix_norm_g", "w_in", "ssd_conv_w", "ssd_conv_b", "ssd_dt_bias", "ssd_a_log", "ssd_d", "ssd_norm_g",
           "pool_w", "pool_scale", "w_out", "ffn_norm_g", "ffn_w_up", "ffn_conv_w", "ffn_conv_b", "ffn_w_down",
           "ple_norm_g", "ple_w_gate", "ple_w_proj", "final_norm_g")


FIRST_USED = ("w_in", "ssd_conv_w")
LATER_USED = tuple(k for k in SHARDED if k not in FIRST_USED)
LAST_MADE = ("w_out", "ssd_conv_w", "w_in")
EARLY_MADE = tuple(k for k in SHARDED if k not in LAST_MADE)
TRANSPOSED = ("w_in", "w_out", "ffn_w_up", "ffn_w_down", "ple_w_gate")


def _pick(names, per_sharded):
    return [per_sharded[SHARDED.index(k)] for k in names]


def _put(names, per_sharded, values):
    out = list(per_sharded)
    for k, val in zip(names, values):
        out[SHARDED.index(k)] = val
    return out


def _assemble(names, gathered):
    full = {}
    for k, g in zip(names, gathered):
        full[k] = _cat_cols(g) if k in COL_SHARDED else _cat_rows(g)
        if k == "w_in":
            full[k] = _to_proj_cols(full[k])
        if k in TRANSPOSED:
            full[k + "_t"] = full[k].T
    return full


def _grad_shards(names, grads):
    out = []
    for k in names:
        g = _from_proj_cols(grads[k]) if k == "w_in" else grads[k]
        out.append(_split_cols(g) if k in COL_SHARDED else _split_rows(g))
    return out


def _layer_fwd(i, h1, p_i, lw, rep, consts, ex_own, ex):
    tril, e_mat = consts
    row = lambda v: v.reshape(1, -1)
    dtb, alog = _pad_heads(rep["ssd_dt_bias"]), _pad_heads(rep["ssd_a_log"])
    dexp = row(jnp.repeat(rep["ssd_d"], SSD_HEAD_DIM))
    pw = rep["pool_w"].astype(BF16)
    proj = _norm_matmul(h1, lw["w_in"], row(rep["mix_norm_g"]), tt=512, tn=N_PROJ, name=f"in_proj_{i}")
    yssd, ypre, states, own = _ssd_fwd(proj, lw["ssd_conv_w"], row(rep["ssd_conv_b"]), dtb, alog, dexp,
                                       row(rep["ssd_norm_g"]), tril, e_mat, ts=512, name=f"ssd_fwd_{i}", ex=ex_own)
    if ex_own is not None:
        lw = dict(lw, **_assemble(LATER_USED, own))
    h2, ymix, n2 = _mix_out(h1, yssd, proj, pw, row(rep["pool_scale"]), lw["w_out"], row(rep["ffn_norm_g"]), tt=512,
                            name=f"mix_out_{i}")
    h3, act, pre, up, gathered = _ffn_fwd(h2, n2, lw["ffn_w_up"], lw["ffn_conv_w"], row(rep["ffn_conv_b"]),
                                          lw["ffn_w_down"], tt=256, name=f"ffn_fwd_{i}", ex=ex)
    h4 = _ple_fwd(h3, p_i, row(rep["ple_norm_g"]), lw["ple_w_gate"], lw["ple_w_proj"], tt=512, name=f"ple_fwd_{i}")
    saved = dict(h1=h1, proj=proj, ypre=ypre, states=states, ymix=ymix, h2=h2, n2=n2, up=up, pre=pre, act=act, h3=h3,
                 dtb=dtb, alog=alog, dexp=dexp, pw=pw)
    return h4, saved, lw, gathered


def _layer_bwd(i, dh, p_i, lw, rep, s, consts, pending, parts, own_early):
    tril, triu, e_mat = consts
    ex = None if pending is None else _Exchange(pending, scatter=True, layer=i + 1, into=parts)
    row = lambda v: v.reshape(1, -1)
    g = {}
    dh, g["ple_w_gate"], g["ple_w_proj"], dg3 = _ple_bwd(dh, s["h3"], p_i, row(rep["ple_norm_g"]), lw["ple_w_gate"],
                                                         lw["ple_w_gate_t"], lw["ple_w_proj"], tt=512,
                                                         name=f"ple_bwd_{i}")
    g["ple_norm_g"] = dg3.reshape(-1)
    g["ffn_w_down"] = _matmul_tn(s["act"], dh, tm=D_FF // 2, tn=D_MODEL, tk=1024, name=f"dw_down_{i}")
    dh, dup, g["ffn_conv_w"], dcb, dg2, scattered = _ffn_bwd(
        dh, s["up"], s["pre"], s["h2"], row(rep["ffn_norm_g"]), lw["ffn_w_down_t"], lw["ffn_w_up_t"],
        lw["ffn_conv_w"], tt=256, name=f"ffn_bwd_{i}", ex=ex)
    if ex is not None:
        parts = scattered
    g["ffn_conv_b"], g["ffn_norm_g"] = dcb.reshape(-1), dg2.reshape(-1)
    g["ffn_w_up"] = _matmul_tn(s["n2"], dup, tm=D_MODEL, tn=D_UP // 4, tk=1024, name=f"dw_up_{i}")
    dymix, g["w_out"] = _out_bwd(dh, s["ymix"], lw["w_out_t"], tt=512, name=f"out_bwd_{i}")
    du, g["pool_w"], dsc = _pool_bwd(dymix, s["proj"], s["pw"], jnp.swapaxes(s["pw"], 1, 2), row(rep["pool_scale"]),
                                     tt=512, name=f"pool_bwd_{i}")
    g["pool_scale"] = dsc.reshape(-1)
    ex_own = None
    if own_early:
        ex_own = _Exchange(_grad_shards(EARLY_MADE, g), scatter=True, layer=i, into=_pick(EARLY_MADE, parts))
    (dproj, g["ssd_conv_w"], dcb, ddtb, dalog, dd, dng), own = _ssd_bwd(
        dymix, s["proj"], s["ypre"], s["states"], du, lw["ssd_conv_w"], row(rep["ssd_conv_b"]), s["dtb"], s["alog"],
        s["dexp"], row(rep["ssd_norm_g"]), tril, triu, e_mat, ts=512, name=f"ssd_bwd_{i}", ex=ex_own)
    if own_early:
        parts = _put(EARLY_MADE, parts, own)
    g["ssd_conv_b"], g["ssd_norm_g"] = dcb.reshape(-1), dng.reshape(-1)
    g["ssd_dt_bias"], g["ssd_a_log"], g["ssd_d"] = ddtb[0, :SSD_HEADS], dalog[0, :SSD_HEADS], dd[0, :SSD_HEADS]
    dh, g["w_in"], dg1 = _in_bwd(dproj, s["h1"], row(rep["mix_norm_g"]), lw["w_in_t"], dh, tt=256, name=f"in_bwd_{i}")
    g["mix_norm_g"] = dg1.reshape(-1)
    return dh, g, parts


def kernel(x, p, mix_norm_g, w_in, ssd_conv_w, ssd_conv_b, ssd_dt_bias, ssd_a_log, ssd_d, ssd_norm_g, pool_w, pool_scale, w_out, ffn_norm_g, ffn_w_up, ffn_conv_w, ffn_conv_b, ffn_w_down, ple_norm_g, ple_w_gate, ple_w_proj, final_norm_g, loss_target, m_mix_norm_g, m_w_in, m_ssd_conv_w, m_ssd_conv_b, m_ssd_dt_bias, m_ssd_a_log, m_ssd_d, m_ssd_norm_g, m_pool_w, m_pool_scale, m_w_out, m_ffn_norm_g, m_ffn_w_up, m_ffn_conv_w, m_ffn_conv_b, m_ffn_w_down, m_ple_norm_g, m_ple_w_gate, m_ple_w_proj, m_final_norm_g, v_mix_norm_g, v_w_in, v_ssd_conv_w, v_ssd_conv_b, v_ssd_dt_bias, v_ssd_a_log, v_ssd_d, v_ssd_norm_g, v_pool_w, v_pool_scale, v_w_out, v_ffn_norm_g, v_ffn_w_up, v_ffn_conv_w, v_ffn_conv_b, v_ffn_w_down, v_ple_norm_g, v_ple_w_gate, v_ple_w_proj, v_final_norm_g):
    w = dict(mix_norm_g=mix_norm_g, w_in=w_in, ssd_conv_w=ssd_conv_w, ssd_conv_b=ssd_conv_b, ssd_dt_bias=ssd_dt_bias,
             ssd_a_log=ssd_a_log, ssd_d=ssd_d, ssd_norm_g=ssd_norm_g, pool_w=pool_w, pool_scale=pool_scale, w_out=w_out,
             ffn_norm_g=ffn_norm_g, ffn_w_up=ffn_w_up, ffn_conv_w=ffn_conv_w, ffn_conv_b=ffn_conv_b,
             ffn_w_down=ffn_w_down, ple_norm_g=ple_norm_g, ple_w_gate=ple_w_gate, ple_w_proj=ple_w_proj,
             final_norm_g=final_norm_g)
    m = dict(mix_norm_g=m_mix_norm_g, w_in=m_w_in, ssd_conv_w=m_ssd_conv_w, ssd_conv_b=m_ssd_conv_b,
             ssd_dt_bias=m_ssd_dt_bias, ssd_a_log=m_ssd_a_log, ssd_d=m_ssd_d, ssd_norm_g=m_ssd_norm_g, pool_w=m_pool_w,
             pool_scale=m_pool_scale, w_out=m_w_out, ffn_norm_g=m_ffn_norm_g, ffn_w_up=m_ffn_w_up,
             ffn_conv_w=m_ffn_conv_w, ffn_conv_b=m_ffn_conv_b, ffn_w_down=m_ffn_w_down, ple_norm_g=m_ple_norm_g,
             ple_w_gate=m_ple_w_gate, ple_w_proj=m_ple_w_proj, final_norm_g=m_final_norm_g)
    v = dict(mix_norm_g=v_mix_norm_g, w_in=v_w_in, ssd_conv_w=v_ssd_conv_w, ssd_conv_b=v_ssd_conv_b,
             ssd_dt_bias=v_ssd_dt_bias, ssd_a_log=v_ssd_a_log, ssd_d=v_ssd_d, ssd_norm_g=v_ssd_norm_g, pool_w=v_pool_w,
             pool_scale=v_pool_scale, w_out=v_w_out, ffn_norm_g=v_ffn_norm_g, ffn_w_up=v_ffn_w_up,
             ffn_conv_w=v_ffn_conv_w, ffn_conv_b=v_ffn_conv_b, ffn_w_down=v_ffn_w_down, ple_norm_g=v_ple_norm_g,
             ple_w_gate=v_ple_w_gate, ple_w_proj=v_ple_w_proj, final_norm_g=v_final_norm_g)

    tril = jnp.tril(jnp.ones((CHUNK, CHUNK), BF16))
    triu = tril.T
    e_mat = (jnp.arange(SSD_WIDTH)[None, :] // SSD_HEAD_DIM == jnp.arange(LANES)[:, None]).astype(BF16)
    rep = [{k: w[k][i] for k in REPLICATED if k != "final_norm_g"} for i in range(DEPTH)]
    p_loc = p[:, 0]

    shards = [w[k].astype(BF16) if k in MATMUL_W else w[k] for k in SHARDED]
    lw = _assemble(FIRST_USED, _exchange_call(_Exchange(_pick(FIRST_USED, shards), scatter=False, layer=0),
                                              "gather_weights_0"))
    h, saved, layer_w = x[0], [], []
    for i in range(DEPTH):
        ex_own = _Exchange(_pick(LATER_USED, shards), scatter=False, layer=0) if i == 0 else None
        ex = _Exchange(shards, scatter=False, layer=i + 1) if i + 1 < DEPTH else None
        h, s, lw, gathered = _layer_fwd(i, h, p_loc[i], lw, rep[i], (tril, e_mat), ex_own, ex)
        saved.append(s)
        layer_w.append(lw)
        lw = _assemble(SHARDED, gathered)

    dh, loss_blk, dgf = _loss_head(h, final_norm_g.reshape(1, -1), loss_target[0], tt=512, name="loss_head")
    loss = lax.psum(loss_blk[0, 0], ("x", "y", "c"))

    rep_grads = [None] * DEPTH
    pending, parts = None, None
    for i in reversed(range(DEPTH)):
        dh, g, parts = _layer_bwd(i, dh, p_loc[i], layer_w[i], rep[i], saved[i], (tril, triu, e_mat), pending, parts,
                                  own_early=(i == 0))
        pending = _grad_shards(SHARDED, g) if i > 0 else _grad_shards(LAST_MADE, g)
        rep_grads[i] = g
    parts = _put(LAST_MADE, parts, _exchange_call(
        _Exchange(pending, scatter=True, layer=0, into=_pick(LAST_MADE, parts)), "scatter_grads_0"))

    out = {}
    for k, part in zip(SHARDED, parts):
        out[k] = _sum_adamw(part, w[k], m[k], v[k], name=f"adamw_{k}")

    rp_grads = [dgf.reshape(-1) if k == "final_norm_g" else jnp.stack([rep_grads[i][k] for i in range(DEPTH)])
                for k in REPLICATED]
    rp_shapes = [w[k].shape for k in REPLICATED]
    rp_parts = _exchange_call(_Exchange([_pack(rp_grads)[None]], scatter=False, layer=0), "gather_replicated_grads")[0]
    rp_out = _sum_adamw(rp_parts[None], *[_pack([d[k] for k in REPLICATED])[None] for d in (w, m, v)],
                        name="adamw_replicated")
    for j in range(4):
        for k, arr in zip(REPLICATED, _unpack(rp_out[j][0], rp_shapes)):
            out.setdefault(k, [None] * 4)[j] = arr
    results = [out[k][j] for j in range(4) for k in WEIGHTS]
    return (loss, dh[None], *results)
```

```python
import functools
import math

import jax
import jax.numpy as jnp
from jax import lax
from jax.experimental import pallas as pl
from jax.experimental.pallas import tpu as pltpu

F32 = jnp.float32
BF16 = jnp.bfloat16

N_DEV = 8
EPS = 1e-6
DEPTH = 4
D_MODEL = 1024
D_PLE = 256
SSD_WIDTH = 512
SSD_HEADS = 8
SSD_HEAD_DIM = 64
SSD_GROUPS = 2
SSD_STATE = 128
SSD_CONV = 4
CHUNK = 128
SSD_XBC = 1024
POOL_WINDOWS = (2, 4, 8, 16)
POOL_WIDTH = 512
POOL_GROUP = 128
POOL_HALO = 16
D_IN_PROJ = 2056
D_FF = 2816
D_UP = 2 * D_FF
FFN_CONV = 3
SUBLANES = 8
LANES = 128
N_PROJ = 2176
COL_XBC, COL_Z, COL_U, COL_DT = 0, 1024, 1536, 2048
N_PAIRS = SSD_HEADS // 2
ADAM_LR, ADAM_B1, ADAM_B2, ADAM_EPS, ADAM_WD, ADAM_STEP = 0.001, 0.9, 0.999, 1e-08, 0.01, 10
GELU_C = math.sqrt(2.0 / math.pi)
GELU_A = 0.044715
VMEM_LIMIT = 56 * 1024 * 1024

NT_DIMS = (((1,), (1,)), ((), ()))
TN_DIMS = (((0,), (0,)), ((), ()))


def _params(*sem):
    return pltpu.CompilerParams(dimension_semantics=sem, vmem_limit_bytes=VMEM_LIMIT)


def _dot(a, b):
    return jnp.dot(a, b, preferred_element_type=F32)


def _dot_nt(a, b):
    return lax.dot_general(a, b, NT_DIMS, preferred_element_type=F32)


def _dot_tn(a, b):
    return lax.dot_general(a, b, TN_DIMS, preferred_element_type=F32)


def _split3(a):
    hi = a.astype(BF16)
    r1 = a - hi.astype(F32)
    mid = r1.astype(BF16)
    return hi, mid, (r1 - mid.astype(F32)).astype(BF16)


def _hdot(a, b):
    if a.dtype == BF16:
        return sum(_dot(a, piece) for piece in _split3(b))
    return sum(_dot(piece, b) for piece in _split3(a))


def _headsum(q, e):
    return sum(_dot_nt(piece, e) for piece in _split3(q))


def _colsum(v):
    return jnp.sum(v, axis=0, keepdims=True)


def _sigmoid(v):
    return 1.0 / (1.0 + jnp.exp(-v))


def _softplus(v):
    e = jnp.exp(-jnp.abs(v))
    return jnp.maximum(v, 0.0) + jnp.where(e < 1e-4, e * (1.0 - 0.5 * e), jnp.log(1.0 + e))


def _rms_r(x):
    return lax.rsqrt(jnp.mean(x * x, axis=-1, keepdims=True) + EPS)


def _rms_bwd(x, r, g, dn):
    xhat = x * r
    gd = dn * g
    dx = r * (gd - xhat * jnp.mean(gd * xhat, axis=-1, keepdims=True))
    return dx, _colsum(dn * xhat)


def _gelu(v):
    return 0.5 * v * (1.0 + jnp.tanh(GELU_C * (v + GELU_A * v * v * v)))


def _gelu_grad(v):
    th = jnp.tanh(GELU_C * (v + GELU_A * v * v * v))
    return 0.5 * (1.0 + th) + 0.5 * v * (1.0 - th * th) * GELU_C * (1.0 + 3.0 * GELU_A * v * v)


def _tile(t, want):
    return min(t, want)


class _Exchange:
    def __init__(self, srcs, *, scatter, layer, into=None):
        self.srcs, self.scatter, self.layer = list(srcs), scatter, layer
        self.into = None if into is None else list(into)
        n = len(self.srcs)
        self.args = self.srcs + (self.into or [])
        self.in_specs = [pl.BlockSpec(memory_space=pl.ANY)] * len(self.args)
        if scatter:
            self.out_shape = [jax.ShapeDtypeStruct((DEPTH,) + s.shape, s.dtype) for s in self.srcs]
        else:
            self.out_shape = [jax.ShapeDtypeStruct((N_DEV,) + s.shape[1:], s.dtype) for s in self.srcs]
        self.out_specs = [pl.BlockSpec(memory_space=pl.ANY)] * n
        self.scratch = [pltpu.SemaphoreType.DMA((n, N_DEV - 1)), pltpu.SemaphoreType.DMA((n, N_DEV - 1)),
                        pltpu.SemaphoreType.DMA((n,))]

    def aliases(self, n_in_before, n_out_before):
        if self.into is None:
            return {}
        n = len(self.srcs)
        return {n_in_before + n + a: n_out_before + a for a in range(n)}

    def ops(self, in_refs, out_refs, sems):
        send_sems, recv_sems, local_sems = sems
        n = len(self.srcs)

        def copies():
            x, y, c = lax.axis_index("x"), lax.axis_index("y"), lax.axis_index("c")
            me = 4 * x + 2 * y + c

            def block(a, idx):
                return in_refs[a].at[idx] if self.scatter else in_refs[a].at[self.layer]

            def slot(a, idx):
                return out_refs[a].at[self.layer].at[idx] if self.scatter else out_refs[a].at[idx]

            local = [pltpu.make_async_copy(block(a, me), slot(a, me), local_sems.at[a]) for a in range(n)]
            sends, recvs = [], []
            for k in range(1, N_DEV):
                px = 1 - x if k & 4 else x
                py = 1 - y if k & 2 else y
                pc = 1 - c if k & 1 else c
                peer = 4 * px + 2 * py + pc
                for a in range(n):
                    kw = dict(send_sem=send_sems.at[a, k - 1], recv_sem=recv_sems.at[a, k - 1], device_id=(px, py, pc),
                              device_id_type=pl.DeviceIdType.MESH)
                    sends.append(pltpu.make_async_remote_copy(src_ref=block(a, peer), dst_ref=slot(a, me), **kw))
                    recvs.append(pltpu.make_async_remote_copy(src_ref=block(a, peer), dst_ref=slot(a, peer), **kw))
            return local, sends, recvs

        def start():
            local, sends, _ = copies()
            for cp in local + sends:
                cp.start()

        def wait():
            local, sends, recvs = copies()
            for send, recv in zip(sends, recvs):
                send.wait_send()
                recv.wait_recv()
            for cp in local:
                cp.wait()

        return start, wait


def _exchange_call(ex, name):
    n_in, n = len(ex.args), len(ex.srcs)

    def body(*refs):
        start, wait = ex.ops(refs[:n_in], refs[n_in:n_in + n], refs[n_in + n:])
        start()
        wait()

    return pl.pallas_call(
        body, name=name, in_specs=ex.in_specs, out_specs=ex.out_specs, out_shape=ex.out_shape,
        scratch_shapes=ex.scratch, input_output_aliases=ex.aliases(0, 0))(*ex.args)


def _split_refs(refs, counts):
    out, k = [], 0
    for cnt in counts:
        out.append(refs[k:k + cnt])
        k += cnt
    return out


def _ex_parts(ex):
    if ex is None:
        return [], [], [], [], [], (0, 0, 0)
    return ex.args, ex.in_specs, ex.out_shape, ex.out_specs, ex.scratch, (len(ex.args), len(ex.srcs), 3)


def _norm_matmul(h, w, g=None, *, tt, tn, name):
    t, k = h.shape
    n = w.shape[1]
    tt, tn = _tile(t, tt), _tile(n, tn)
    normed = g is not None

    def body(*refs):
        if normed:
            h_ref, g_ref, w_ref, o_ref = refs
            x = h_ref[...]
            xn = (x * _rms_r(x) * g_ref[...]).astype(BF16)
        else:
            h_ref, w_ref, o_ref = refs
            xn = h_ref[...].astype(BF16)
        o_ref[...] = _dot(xn, w_ref[...])

    in_specs = [pl.BlockSpec((tt, k), lambda j, i: (i, 0))]
    args = [h]
    if normed:
        in_specs.append(pl.BlockSpec((1, k), lambda j, i: (0, 0)))
        args.append(g)
    in_specs.append(pl.BlockSpec((k, tn), lambda j, i: (0, j)))
    args.append(w)
    return pl.pallas_call(
        body, name=name, grid=(n // tn, t // tt), in_specs=in_specs,
        out_specs=pl.BlockSpec((tt, tn), lambda j, i: (i, j)), out_shape=jax.ShapeDtypeStruct((t, n), F32),
        compiler_params=_params("arbitrary", "arbitrary"))(*args)


def _matmul_tn(a, b, *, tm, tn, tk, name):
    t, m = a.shape
    n = b.shape[1]
    tm, tn, tk = _tile(m, tm), _tile(n, tn), _tile(t, tk)

    def body(a_ref, b_ref, o_ref):
        @pl.when(pl.program_id(2) == 0)
        def _():
            o_ref[...] = jnp.zeros_like(o_ref)

        o_ref[...] += _dot_tn(a_ref[...].astype(BF16), b_ref[...].astype(BF16))

    return pl.pallas_call(
        body, name=name, grid=(m // tm, n // tn, t // tk),
        in_specs=[pl.BlockSpec((tk, tm), lambda i, j, kk: (kk, i)), pl.BlockSpec((tk, tn), lambda i, j, kk: (kk, j))],
        out_specs=pl.BlockSpec((tm, tn), lambda i, j, kk: (i, j)),
        out_shape=jax.ShapeDtypeStruct((m, n), F32),
        compiler_params=_params("arbitrary", "arbitrary", "arbitrary"))(a, b)


def _ssd_tile_prologue(i_is_first, xbc_ref, halo_ref, dt_ref, cw_ref, cb_ref, dtb_ref, alog_ref, e_ref, buf, xc_scr,
                       xa_scr, a_scr, dte_scr, x_scr, ts):
    buf[0:SUBLANES, :] = jnp.where(i_is_first, 0.0, halo_ref[...])
    buf[SUBLANES:SUBLANES + ts, :] = xbc_ref[...]
    cw = cw_ref[...]
    xc = cb_ref[...]
    for k in range(SSD_CONV):
        off = SUBLANES - (SSD_CONV - 1) + k
        xc = xc + cw[k:k + 1, :] * buf[off:off + ts, :]
    if xc_scr is not None:
        xc_scr[...] = xc
    xa_scr[...] = xc * _sigmoid(xc)
    dt = _softplus(dt_ref[...] + dtb_ref[...])
    a_neg = -jnp.exp(alog_ref[...])
    a_scr[...] = dt * a_neg
    dte = _hdot(dt, e_ref[...])
    dte_scr[...] = dte
    x_scr[...] = xa_scr[:, 0:SSD_WIDTH] * dte
    return dt, a_neg


def _chunk_decays(a_c, tril, e):
    cs = _hdot(tril, a_c)
    cs_t = cs.T
    cs_e = _hdot(cs, e)
    last_e = cs_e[CHUNK - 1:CHUNK, :]
    return cs, cs_t, cs_e, last_e


def _ssd_fwd(proj, cw, cb, dtb, alog, dexp, ng, tril, e, *, ts, name, ex=None):
    t = proj.shape[0]
    ts = _tile(t, ts)
    nch = ts // CHUNK
    hb = ts // SUBLANES
    nt = t // ts
    ex_args, ex_in_specs, ex_out_shape, ex_out_specs, ex_scratch, ex_counts = _ex_parts(ex)

    def body(*refs):
        ((xbc_ref, halo_ref, z_ref, dt_ref, cw_ref, cb_ref, dtb_ref, alog_ref, dexp_ref, ng_ref, tril_ref, e_ref),
         ex_in, (y_ref, ypre_ref, st_ref), ex_out, (buf, xa_scr, a_scr, dte_scr, x_scr, ys_scr, hstate),
         ex_sems) = _split_refs(refs, (12, ex_counts[0], 3, ex_counts[1], 7, ex_counts[2]))
        i = pl.program_id(0)
        if ex is not None:
            ex_start, ex_wait = ex.ops(ex_in, ex_out, ex_sems)
            pl.when(i == 0)(ex_start)

        @pl.when(i == 0)
        def _():
            hstate[...] = jnp.zeros_like(hstate)

        _ssd_tile_prologue(i == 0, xbc_ref, halo_ref, dt_ref, cw_ref, cb_ref, dtb_ref, alog_ref, e_ref, buf, None,
                           xa_scr, a_scr, dte_scr, x_scr, ts)
        tril = tril_ref[...]
        e_mat = e_ref[...]
        causal = (lax.broadcasted_iota(jnp.int32, (CHUNK, CHUNK), 0)
                  >= lax.broadcasted_iota(jnp.int32, (CHUNK, CHUNK), 1))
        lane = lax.broadcasted_iota(jnp.int32, (CHUNK, LANES), 1)

        def chunk(c, carry):
            r0 = pl.multiple_of(c * CHUNK, CHUNK)
            rows = pl.ds(r0, CHUNK)
            cs, cs_t, cs_e, last_e = _chunk_decays(a_scr[rows, :], tril, e_mat)
            decay_e = jnp.exp(last_e - cs_e)
            ecs_e = jnp.exp(cs_e)
            xc = x_scr[rows, :]
            xb = xc.astype(BF16)
            xd = (xc * decay_e).astype(BF16)
            for g in range(SSD_GROUPS):
                bg = xa_scr[rows, SSD_WIDTH + g * SSD_STATE:SSD_WIDTH + (g + 1) * SSD_STATE].astype(BF16)
                cg = xa_scr[rows, SSD_WIDTH + (SSD_GROUPS + g) * SSD_STATE:
                            SSD_WIDTH + (SSD_GROUPS + g + 1) * SSD_STATE].astype(BF16)
                cbm = _dot_nt(cg, bg)
                for jj in range(2):
                    j = 2 * g + jj
                    cols = slice(j * LANES, (j + 1) * LANES)
                    xp = xb[:, cols]
                    ypair = jnp.zeros((CHUNK, LANES), F32)
                    for hh in range(2):
                        h = 2 * j + hh
                        seg = jnp.exp(jnp.where(causal, cs[:, h:h + 1] - cs_t[h:h + 1, :], -jnp.inf))
                        m = (cbm * seg).astype(BF16)
                        half = (lane < SSD_HEAD_DIM) if hh == 0 else (lane >= SSD_HEAD_DIM)
                        ypair = ypair + _dot(m, jnp.where(half, xp, jnp.zeros_like(xp)))
                    hp = hstate[j]
                    st_ref[c, j] = hp
                    ypair = ypair + _dot(cg, hp.astype(BF16)) * ecs_e[:, cols]
                    ys_scr[rows, cols] = ypair
                    hstate[j] = hp * jnp.exp(last_e[:, cols]) + _dot_tn(bg, xd[:, cols])
            return carry

        lax.fori_loop(0, nch, chunk, 0, unroll=4)
        ypre = ys_scr[...] + xa_scr[:, 0:SSD_WIDTH] * dexp_ref[...]
        ypre_ref[...] = ypre
        z = z_ref[...]
        yg = ypre * (z * _sigmoid(z))
        gw = SSD_WIDTH // SSD_GROUPS
        outs = []
        for g in range(SSD_GROUPS):
            v = yg[:, g * gw:(g + 1) * gw]
            outs.append(v * _rms_r(v))
        y_ref[...] = jnp.concatenate(outs, axis=1) * ng_ref[...]
        if ex is not None:
            pl.when(i == nt - 1)(ex_wait)

    full = lambda shape: pl.BlockSpec(shape, lambda i: tuple(0 for _ in shape))
    outs = pl.pallas_call(
        body, name=name, grid=(nt,),
        in_specs=[pl.BlockSpec((ts, SSD_XBC), lambda i: (i, COL_XBC // SSD_XBC)),
                  pl.BlockSpec((SUBLANES, SSD_XBC), lambda i: (jnp.maximum(i * hb - 1, 0), COL_XBC // SSD_XBC)),
                  pl.BlockSpec((ts, SSD_WIDTH), lambda i: (i, COL_Z // SSD_WIDTH)),
                  pl.BlockSpec((ts, LANES), lambda i: (i, COL_DT // LANES)),
                  full((SSD_CONV, SSD_XBC)), full((1, SSD_XBC)), full((1, LANES)), full((1, LANES)),
                  full((1, SSD_WIDTH)), full((1, SSD_WIDTH)), full((CHUNK, CHUNK)), full((LANES, SSD_WIDTH))]
        + ex_in_specs,
        out_specs=[pl.BlockSpec((ts, SSD_WIDTH), lambda i: (i, 0)), pl.BlockSpec((ts, SSD_WIDTH), lambda i: (i, 0)),
                   pl.BlockSpec((nch, N_PAIRS, SSD_STATE, LANES), lambda i: (i, 0, 0, 0))] + ex_out_specs,
        out_shape=[jax.ShapeDtypeStruct((t, SSD_WIDTH), F32), jax.ShapeDtypeStruct((t, SSD_WIDTH), F32),
                   jax.ShapeDtypeStruct((t // CHUNK, N_PAIRS, SSD_STATE, LANES), F32)] + ex_out_shape,
        scratch_shapes=[pltpu.VMEM((SUBLANES + ts, SSD_XBC), F32), pltpu.VMEM((ts, SSD_XBC), F32),
                        pltpu.VMEM((ts, LANES), F32), pltpu.VMEM((ts, SSD_WIDTH), F32),
                        pltpu.VMEM((ts, SSD_WIDTH), F32), pltpu.VMEM((ts, SSD_WIDTH), F32),
                        pltpu.VMEM((N_PAIRS, SSD_STATE, LANES), F32)] + ex_scratch,
        input_output_aliases={} if ex is None else ex.aliases(12, 3),
        compiler_params=_params("arbitrary"))(proj, proj, proj, proj, cw, cb, dtb, alog, dexp, ng, tril, e, *ex_args)
    return outs[0], outs[1], outs[2], outs[3:]


def _ssd_bwd(dymix, proj, ypre, states, du, cw, cb, dtb, alog, dexp, ng, tril, triu, e, *, ts, name, ex=None):
    t = proj.shape[0]
    ts = _tile(t, ts)
    nch = ts // CHUNK
    hb = ts // SUBLANES
    nt = t // ts
    ex_args, ex_in_specs, ex_out_shape, ex_out_specs, ex_scratch, ex_counts = _ex_parts(ex)

    def body(*refs):
        ((dy_ref, xbc_ref, halo_ref, z_ref, dt_ref, ypre_ref, st_ref, du_ref, cw_ref, cb_ref, dtb_ref, alog_ref,
          dexp_ref, ng_ref, tril_ref, triu_ref, e_ref), ex_in,
         (dproj_ref, dcw_ref, dcb_ref, ddtb_ref, dalog_ref, dd_ref, dng_ref), ex_out,
         (buf, xc_scr, xa_scr, a_scr, dte_scr, x_scr, dyp_scr, dxa_scr, dx_scr, dbuf, carry, gstate),
         ex_sems) = _split_refs(refs, (17, ex_counts[0], 7, ex_counts[1], 12, ex_counts[2]))
        i = pl.program_id(0)
        if ex is not None:
            ex_start, ex_wait = ex.ops(ex_in, ex_out, ex_sems)
            pl.when(i == 0)(ex_start)

        @pl.when(i == 0)
        def _():
            gstate[...] = jnp.zeros_like(gstate)
            carry[...] = jnp.zeros_like(carry)
            for ref in (dcw_ref, dcb_ref, ddtb_ref, dalog_ref, dd_ref, dng_ref):
                ref[...] = jnp.zeros_like(ref)

        dt, a_neg = _ssd_tile_prologue(i == nt - 1, xbc_ref, halo_ref, dt_ref, cw_ref, cb_ref, dtb_ref, alog_ref, e_ref,
                                       buf, xc_scr, xa_scr, a_scr, dte_scr, x_scr, ts)
        tril = tril_ref[...]
        triu = triu_ref[...]
        e_mat = e_ref[...]
        causal = (lax.broadcasted_iota(jnp.int32, (CHUNK, CHUNK), 0)
                  >= lax.broadcasted_iota(jnp.int32, (CHUNK, CHUNK), 1))
        lane = lax.broadcasted_iota(jnp.int32, (CHUNK, LANES), 1)
        sub = lax.broadcasted_iota(jnp.int32, (CHUNK, LANES), 0)

        z = z_ref[...]
        sig = _sigmoid(z)
        zs = z * sig
        ypre = ypre_ref[...]
        yg = ypre * zs
        dout = dy_ref[...]
        ngv = ng_ref[...]
        gw = SSD_WIDTH // SSD_GROUPS
        dyg_parts, dng_parts = [], []
        for g in range(SSD_GROUPS):
            cols = slice(g * gw, (g + 1) * gw)
            v = yg[:, cols]
            dx, dg = _rms_bwd(v, _rms_r(v), ngv[:, cols], dout[:, cols])
            dyg_parts.append(dx)
            dng_parts.append(dg)
        dyg = jnp.concatenate(dyg_parts, axis=1)
        dng_ref[...] += jnp.concatenate(dng_parts, axis=1)
        dyp = dyg * zs
        dyp_scr[...] = dyp
        dproj_ref[:, COL_Z:COL_Z + SSD_WIDTH] = dyg * ypre * (sig * (1.0 + z * (1.0 - sig)))
        dproj_ref[:, COL_U:COL_U + POOL_WIDTH] = du_ref[...]
        xs_all = xa_scr[:, 0:SSD_WIDTH]
        dd_ref[...] += _headsum(jnp.broadcast_to(_colsum(dyp * xs_all), (SUBLANES, SSD_WIDTH)), e_mat)[0:1, :]

        def chunk(k, carry_):
            c = nch - 1 - k
            r0 = pl.multiple_of(c * CHUNK, CHUNK)
            rows = pl.ds(r0, CHUNK)
            a_c = a_scr[rows, :]
            cs, cs_t, cs_e, last_e = _chunk_decays(a_c, tril, e_mat)
            decay_e = jnp.exp(last_e - cs_e)
            ecs_e = jnp.exp(cs_e)
            elast_e = jnp.exp(last_e)
            xc = x_scr[rows, :]
            xb = xc.astype(BF16)
            xd = (xc * decay_e).astype(BF16)
            dyc = dyp_scr[rows, :]
            dcs = jnp.zeros((CHUNK, LANES), F32)
            dcs_neg_t = jnp.zeros((LANES, CHUNK), F32)
            qoff, rin, ghrow = [], [], []
            for g in range(SSD_GROUPS):
                b_cols = slice(SSD_WIDTH + g * SSD_STATE, SSD_WIDTH + (g + 1) * SSD_STATE)
                c_cols = slice(SSD_WIDTH + (SSD_GROUPS + g) * SSD_STATE, SSD_WIDTH + (SSD_GROUPS + g + 1) * SSD_STATE)
                bg = xa_scr[rows, b_cols].astype(BF16)
                cg = xa_scr[rows, c_cols].astype(BF16)
                cbm = _dot_nt(cg, bg)
                dcb_m = jnp.zeros((CHUNK, CHUNK), F32)
                dbg = jnp.zeros((CHUNK, SSD_STATE), F32)
                dcg = jnp.zeros((CHUNK, SSD_STATE), F32)
                for jj in range(2):
                    j = 2 * g + jj
                    cols = slice(j * LANES, (j + 1) * LANES)
                    dyp_j = dyc[:, cols]
                    hp = st_ref[c, j]
                    hpb = hp.astype(BF16)
                    gt = gstate[j]
                    gtb = gt.astype(BF16)
                    ecs = ecs_e[:, cols]
                    yoff = _dot(cg, hpb) * ecs
                    dye = (dyp_j * ecs).astype(BF16)
                    dcg = dcg + _dot_nt(dye, hpb)
                    dht = _dot_tn(cg, dye)
                    qoff.append(dyp_j * yoff)
                    xg = _dot(bg, gtb)
                    dxp = xg * decay_e[:, cols]
                    rin.append(xg * xc[:, cols])
                    dbg = dbg + _dot_nt(xd[:, cols], gtb)
                    ghrow.append(_colsum(gt * hp) * elast_e[:, cols])
                    gstate[j] = dht + gt * elast_e[:, cols]
                    for hh in range(2):
                        h = 2 * j + hh
                        seg = jnp.exp(jnp.where(causal, cs[:, h:h + 1] - cs_t[h:h + 1, :], -jnp.inf))
                        m = cbm * seg
                        half = (lane < SSD_HEAD_DIM) if hh == 0 else (lane >= SSD_HEAD_DIM)
                        dym = jnp.where(half, dyp_j, 0.0).astype(BF16)
                        w = _dot_nt(dym, xb[:, cols])
                        pm = w * m
                        dcs = dcs + jnp.where(lane == h, jnp.sum(pm, axis=1, keepdims=True), 0.0)
                        dcs_neg_t = dcs_neg_t + jnp.where(sub == h, _colsum(pm), 0.0)
                        dcb_m = dcb_m + w * seg
                        dxp = dxp + _dot_tn(m.astype(BF16), dym)
                    dx_scr[:, cols] = dxp
                dcbb = dcb_m.astype(BF16)
                dxa_scr[rows, c_cols] = dcg + _dot(dcbb, bg)
                dxa_scr[rows, b_cols] = dbg + _dot_tn(dcbb, cg)
            decay_th = jnp.exp(cs[CHUNK - 1:CHUNK, :] - cs)
            rd = _headsum(jnp.concatenate(rin, axis=1), e_mat) * decay_th
            dcs = dcs - dcs_neg_t.T + _headsum(jnp.concatenate(qoff, axis=1), e_mat) - rd
            gh = _headsum(jnp.broadcast_to(jnp.concatenate(ghrow, axis=1), (SUBLANES, SSD_WIDTH)), e_mat)[0:1, :]
            dcs = dcs + jnp.where(sub == CHUNK - 1, _colsum(rd) + gh, 0.0)
            da = _hdot(triu, dcs)
            dx_all = dx_scr[...]
            xs = xa_scr[rows, 0:SSD_WIDTH]
            dt_c = _softplus(dt_ref[rows, :] + dtb_ref[...])
            ddt = da * a_neg + _headsum(dx_all * xs, e_mat)
            dalog_ref[...] += _colsum(da * dt_c) * a_neg
            ddtraw = ddt * _sigmoid(dt_ref[rows, :] + dtb_ref[...])
            dproj_ref[rows, COL_DT:COL_DT + LANES] = ddtraw
            ddtb_ref[...] += _colsum(ddtraw)
            dxa_scr[rows, 0:SSD_WIDTH] = dx_all * dte_scr[rows, :] + dyc * dexp_ref[...]
            return carry_

        lax.fori_loop(0, nch, chunk, 0, unroll=4)

        xcv = xc_scr[...]
        sgc = _sigmoid(xcv)
        dxc = dxa_scr[...] * (sgc * (1.0 + xcv * (1.0 - sgc)))
        dcb_ref[...] += _colsum(dxc)
        dbuf[0:ts, :] = dxc
        dbuf[ts:ts + SUBLANES, :] = carry[...]
        cwv = cw_ref[...]
        dxbc = jnp.zeros((ts, SSD_XBC), F32)
        dcw_rows = []
        for k in range(SSD_CONV):
            off = SUBLANES - (SSD_CONV - 1) + k
            dcw_rows.append(_colsum(dxc * buf[off:off + ts, :]))
            back = SSD_CONV - 1 - k
            dxbc = dxbc + cwv[k:k + 1, :] * dbuf[back:back + ts, :]
        dcw_ref[...] += jnp.concatenate(dcw_rows, axis=0)
        dproj_ref[:, COL_XBC:COL_XBC + SSD_XBC] = dxbc
        carry[...] = dxc[0:SUBLANES, :]
        if ex is not None:
            pl.when(i == nt - 1)(ex_wait)

    rev = lambda i: nt - 1 - i
    full = lambda shape: pl.BlockSpec(shape, lambda i: tuple(0 for _ in shape))
    outs = pl.pallas_call(
        body, name=name, grid=(nt,),
        in_specs=[pl.BlockSpec((ts, SSD_WIDTH), lambda i: (rev(i), 0)),
                  pl.BlockSpec((ts, SSD_XBC), lambda i: (rev(i), COL_XBC // SSD_XBC)),
                  pl.BlockSpec((SUBLANES, SSD_XBC), lambda i: (jnp.maximum(rev(i) * hb - 1, 0), COL_XBC // SSD_XBC)),
                  pl.BlockSpec((ts, SSD_WIDTH), lambda i: (rev(i), COL_Z // SSD_WIDTH)),
                  pl.BlockSpec((ts, LANES), lambda i: (rev(i), COL_DT // LANES)),
                  pl.BlockSpec((ts, SSD_WIDTH), lambda i: (rev(i), 0)),
                  pl.BlockSpec((nch, N_PAIRS, SSD_STATE, LANES), lambda i: (rev(i), 0, 0, 0)),
                  pl.BlockSpec((ts, POOL_WIDTH), lambda i: (rev(i), 0)),
                  full((SSD_CONV, SSD_XBC)), full((1, SSD_XBC)), full((1, LANES)), full((1, LANES)),
                  full((1, SSD_WIDTH)), full((1, SSD_WIDTH)), full((CHUNK, CHUNK)), full((CHUNK, CHUNK)),
                  full((LANES, SSD_WIDTH))] + ex_in_specs,
        out_specs=[pl.BlockSpec((ts, N_PROJ), lambda i: (rev(i), 0)),
                   full((SSD_CONV, SSD_XBC)), full((1, SSD_XBC)), full((1, LANES)), full((1, LANES)),
                   full((1, LANES)), full((1, SSD_WIDTH))] + ex_out_specs,
        out_shape=[jax.ShapeDtypeStruct((t, N_PROJ), F32),
                   jax.ShapeDtypeStruct((SSD_CONV, SSD_XBC), F32), jax.ShapeDtypeStruct((1, SSD_XBC), F32),
                   jax.ShapeDtypeStruct((1, LANES), F32), jax.ShapeDtypeStruct((1, LANES), F32),
                   jax.ShapeDtypeStruct((1, LANES), F32), jax.ShapeDtypeStruct((1, SSD_WIDTH), F32)] + ex_out_shape,
        scratch_shapes=[pltpu.VMEM((SUBLANES + ts, SSD_XBC), F32), pltpu.VMEM((ts, SSD_XBC), F32),
                        pltpu.VMEM((ts, SSD_XBC), F32), pltpu.VMEM((ts, LANES), F32),
                        pltpu.VMEM((ts, SSD_WIDTH), F32), pltpu.VMEM((ts, SSD_WIDTH), F32),
                        pltpu.VMEM((ts, SSD_WIDTH), F32), pltpu.VMEM((ts, SSD_XBC), F32),
                        pltpu.VMEM((CHUNK, SSD_WIDTH), F32), pltpu.VMEM((ts + SUBLANES, SSD_XBC), F32),
                        pltpu.VMEM((SUBLANES, SSD_XBC), F32), pltpu.VMEM((N_PAIRS, SSD_STATE, LANES), F32)]
        + ex_scratch,
        input_output_aliases={} if ex is None else ex.aliases(17, 7),
        compiler_params=_params("arbitrary"))(
            dymix, proj, proj, proj, proj, ypre, states, du, cw, cb, dtb, alog, dexp, ng, tril, triu, e, *ex_args)
    return outs[:7], outs[7:]


def _pooled(ubuf, u, pos, tt):
    out = []
    for gi, w in enumerate(POOL_WINDOWS):
        cols = slice(gi * POOL_GROUP, (gi + 1) * POOL_GROUP)
        acc = u[:, cols]
        for j in range(1, w):
            acc = acc + ubuf[POOL_HALO - j:POOL_HALO - j + tt, cols]
        out.append(acc / jnp.minimum(pos, float(w)) - u[:, cols])
    return out


def _mix_out(h, yssd, proj, pool_w, pool_scale, w_out, g_next, *, tt, name):
    t = h.shape[0]
    tt = _tile(t, tt)
    hb = tt // POOL_HALO

    def body(h_ref, ys_ref, u_ref, uh_ref, pw_ref, sc_ref, wo_ref, gn_ref, o_ref, ym_ref, n_ref, ubuf):
        i = pl.program_id(0)
        ubuf[0:POOL_HALO, :] = jnp.where(i == 0, 0.0, uh_ref[...])
        u = u_ref[...]
        ubuf[POOL_HALO:POOL_HALO + tt, :] = u
        pos = (i * tt + 1 + lax.broadcasted_iota(jnp.int32, (tt, 1), 0)).astype(F32)
        sc = sc_ref[...]
        parts = [ys_ref[...]]
        for gi, pooled in enumerate(_pooled(ubuf, u, pos, tt)):
            cols = slice(gi * POOL_GROUP, (gi + 1) * POOL_GROUP)
            parts.append(_dot(pooled.astype(BF16), pw_ref[gi]) * sc[:, cols])
        ymix = jnp.concatenate(parts, axis=1).astype(BF16)
        ym_ref[...] = ymix
        h2 = h_ref[...] + _dot(ymix, wo_ref[...])
        o_ref[...] = h2
        n_ref[...] = (h2 * _rms_r(h2) * gn_ref[...]).astype(BF16)

    full = lambda shape: pl.BlockSpec(shape, lambda i: tuple(0 for _ in shape))
    return pl.pallas_call(
        body, name=name, grid=(t // tt,),
        in_specs=[pl.BlockSpec((tt, D_MODEL), lambda i: (i, 0)), pl.BlockSpec((tt, SSD_WIDTH), lambda i: (i, 0)),
                  pl.BlockSpec((tt, POOL_WIDTH), lambda i: (i, COL_U // POOL_WIDTH)),
                  pl.BlockSpec((POOL_HALO, POOL_WIDTH), lambda i: (jnp.maximum(i * hb - 1, 0), COL_U // POOL_WIDTH)),
                  full((len(POOL_WINDOWS), POOL_GROUP, POOL_GROUP)), full((1, POOL_WIDTH)),
                  full((D_MODEL, D_MODEL)), full((1, D_MODEL))],
        out_specs=[pl.BlockSpec((tt, D_MODEL), lambda i: (i, 0))] * 3,
        out_shape=[jax.ShapeDtypeStruct((t, D_MODEL), F32), jax.ShapeDtypeStruct((t, D_MODEL), BF16),
                   jax.ShapeDtypeStruct((t, D_MODEL), BF16)],
        scratch_shapes=[pltpu.VMEM((POOL_HALO + tt, POOL_WIDTH), F32)],
        compiler_params=_params("arbitrary"))(h, yssd, proj, proj, pool_w, pool_scale, w_out, g_next)


def _out_bwd(dh, ymix, w_out_t, *, tt, name):
    t = dh.shape[0]
    tt = _tile(t, tt)

    def body(dh_ref, ym_ref, wt_ref, dym_ref, dw_ref):
        @pl.when(pl.program_id(0) == 0)
        def _():
            dw_ref[...] = jnp.zeros_like(dw_ref)

        dhb = dh_ref[...].astype(BF16)
        dym_ref[...] = _dot(dhb, wt_ref[...])
        dw_ref[...] += _dot_tn(ym_ref[...], dhb)

    return pl.pallas_call(
        body, name=name, grid=(t // tt,),
        in_specs=[pl.BlockSpec((tt, D_MODEL), lambda i: (i, 0)), pl.BlockSpec((tt, D_MODEL), lambda i: (i, 0)),
                  pl.BlockSpec((D_MODEL, D_MODEL), lambda i: (0, 0))],
        out_specs=[pl.BlockSpec((tt, D_MODEL), lambda i: (i, 0)), pl.BlockSpec((D_MODEL, D_MODEL), lambda i: (0, 0))],
        out_shape=[jax.ShapeDtypeStruct((t, D_MODEL), F32), jax.ShapeDtypeStruct((D_MODEL, D_MODEL), F32)],
        compiler_params=_params("arbitrary"))(dh, ymix, w_out_t)


def _pool_bwd(dymix, proj, pool_w, pool_w_t, pool_scale, *, tt, name):
    t = proj.shape[0]
    tt = _tile(t, tt)
    hb = tt // POOL_HALO
    nt = t // tt
    ng = len(POOL_WINDOWS)

    def body(dy_ref, dyh_ref, u_ref, uh_ref, pw_ref, pwt_ref, sc_ref, du_ref, dpw_ref, dsc_ref, ubuf, dbuf):
        i = pl.program_id(0)

        @pl.when(i == 0)
        def _():
            dpw_ref[...] = jnp.zeros_like(dpw_ref)
            dsc_ref[...] = jnp.zeros_like(dsc_ref)

        ubuf[0:POOL_HALO, :] = jnp.where(i == 0, 0.0, uh_ref[...])
        u = u_ref[...]
        ubuf[POOL_HALO:POOL_HALO + tt, :] = u
        pos = (i * tt + 1 + lax.broadcasted_iota(jnp.int32, (tt, 1), 0)).astype(F32)
        sc = sc_ref[...]
        dy = dy_ref[...]
        dyh = jnp.where(i == nt - 1, 0.0, dyh_ref[...])
        dsc_parts, du_parts = [], []
        for gi, pooled in enumerate(_pooled(ubuf, u, pos, tt)):
            w = POOL_WINDOWS[gi]
            cols = slice(gi * POOL_GROUP, (gi + 1) * POOL_GROUP)
            pb = pooled.astype(BF16)
            dsc_parts.append(_colsum(dy[:, cols] * _dot(pb, pw_ref[gi])))
            dmx = (dy[:, cols] * sc[:, cols]).astype(BF16)
            dpw_ref[gi] += _dot_tn(pb, dmx)
            dpool = _dot(dmx, pwt_ref[gi])
            dpool_h = _dot((dyh[:, cols] * sc[:, cols]).astype(BF16), pwt_ref[gi])
            dbuf[0:tt, cols] = dpool / jnp.minimum(pos, float(w))
            dbuf[tt:tt + POOL_HALO, cols] = dpool_h / float(w)
            acc = -dpool
            for j in range(w):
                acc = acc + dbuf[j:j + tt, cols]
            du_parts.append(acc)
        du_ref[...] = jnp.concatenate(du_parts, axis=1)
        dsc_ref[...] += jnp.concatenate(dsc_parts, axis=1)

    full = lambda shape: pl.BlockSpec(shape, lambda i: tuple(0 for _ in shape))
    ucol = COL_U // POOL_WIDTH
    return pl.pallas_call(
        body, name=name, grid=(nt,),
        in_specs=[pl.BlockSpec((tt, POOL_WIDTH), lambda i: (i, 1)),
                  pl.BlockSpec((POOL_HALO, POOL_WIDTH), lambda i: (jnp.minimum((i + 1) * hb, t // POOL_HALO - 1), 1)),
                  pl.BlockSpec((tt, POOL_WIDTH), lambda i: (i, ucol)),
                  pl.BlockSpec((POOL_HALO, POOL_WIDTH), lambda i: (jnp.maximum(i * hb - 1, 0), ucol)),
                  full((ng, POOL_GROUP, POOL_GROUP)), full((ng, POOL_GROUP, POOL_GROUP)), full((1, POOL_WIDTH))],
        out_specs=[pl.BlockSpec((tt, POOL_WIDTH), lambda i: (i, 0)), full((ng, POOL_GROUP, POOL_GROUP)),
                   full((1, POOL_WIDTH))],
        out_shape=[jax.ShapeDtypeStruct((t, POOL_WIDTH), F32), jax.ShapeDtypeStruct((ng, POOL_GROUP, POOL_GROUP), F32),
                   jax.ShapeDtypeStruct((1, POOL_WIDTH), F32)],
        scratch_shapes=[pltpu.VMEM((POOL_HALO + tt, POOL_WIDTH), F32), pltpu.VMEM((tt + POOL_HALO, POOL_WIDTH), F32)],
        compiler_params=_params("arbitrary"))(dymix, dymix, proj, proj, pool_w, pool_w_t, pool_scale)


def _in_bwd(dproj, h, g, w_in_t, dh, *, tt, name):
    t = h.shape[0]
    tt = _tile(t, tt)

    def body(dp_ref, h_ref, g_ref, wt_ref, dh_ref, o_ref, dw_ref, dg_ref):
        @pl.when(pl.program_id(0) == 0)
        def _():
            dw_ref[...] = jnp.zeros_like(dw_ref)
            dg_ref[...] = jnp.zeros_like(dg_ref)

        x = h_ref[...]
        r = _rms_r(x)
        gv = g_ref[...]
        dpb = dp_ref[...].astype(BF16)
        dw_ref[...] += _dot_tn((x * r * gv).astype(BF16), dpb)
        dx, dg = _rms_bwd(x, r, gv, _dot(dpb, wt_ref[...]))
        o_ref[...] = dh_ref[...] + dx
        dg_ref[...] += dg

    full = lambda shape: pl.BlockSpec(shape, lambda i: tuple(0 for _ in shape))
    row = lambda n: pl.BlockSpec((tt, n), lambda i: (i, 0))
    return pl.pallas_call(
        body, name=name, grid=(t // tt,),
        in_specs=[row(N_PROJ), row(D_MODEL), full((1, D_MODEL)), full((N_PROJ, D_MODEL)), row(D_MODEL)],
        out_specs=[row(D_MODEL), full((D_MODEL, N_PROJ)), full((1, D_MODEL))],
        out_shape=[jax.ShapeDtypeStruct((t, D_MODEL), F32), jax.ShapeDtypeStruct((D_MODEL, N_PROJ), F32),
                   jax.ShapeDtypeStruct((1, D_MODEL), F32)],
        compiler_params=_params("arbitrary"))(dproj, h, g, w_in_t, dh)


FFN_COLS = 256
N_SLABS = D_FF // FFN_COLS
N_SLAB_BUFS = 4


def _ffn_fwd(h, n2, w_up, cw, cb, w_down, *, tt, name, ex=None):
    t = h.shape[0]
    tt = _tile(t, tt)
    nt = t // tt
    ex_args, ex_in_specs, ex_out_shape, ex_out_specs, ex_scratch, ex_counts = _ex_parts(ex)

    def body(*refs):
        ((h_ref, n2_ref, wu_ref, cw_ref, cb_ref, wd_ref), ex_in, (o_ref, act_ref, pre_ref, up_ref), ex_out,
         (slab, halo), ex_sems) = _split_refs(refs, (6, ex_counts[0], 4, ex_counts[1], 2, ex_counts[2]))
        i = pl.program_id(0)
        if ex is not None:
            ex_start, ex_wait = ex.ops(ex_in, ex_out, ex_sems)
            pl.when(i == 0)(ex_start)

        @pl.when(i == 0)
        def _():
            halo[...] = jnp.zeros_like(halo)

        n2v = n2_ref[...]

        def slab_cols(s):
            return slice(s * FFN_COLS, (s + 1) * FFN_COLS), slice(D_FF + s * FFN_COLS, D_FF + (s + 1) * FFN_COLS)

        def project(s):
            return [_dot(n2v, wu_ref[:, cols]) for cols in slab_cols(s)]

        def conv(u, cols, buf_id):
            up_ref[:, cols] = u.astype(BF16)
            sb = slab.at[buf_id]
            sb[0:SUBLANES, :] = halo[:, cols]
            sb[SUBLANES:SUBLANES + tt, :] = u
            halo[:, cols] = u[tt - SUBLANES:tt, :]
            acc = cb_ref[:, cols] + cw_ref[FFN_CONV - 1:FFN_CONV, cols] * u
            for k in range(FFN_CONV - 1):
                off = SUBLANES - (FFN_CONV - 1) + k
                acc = acc + cw_ref[k:k + 1, cols] * sb[off:off + tt, :]
            pre_ref[:, cols] = acc
            return acc

        out = h_ref[...]
        ahead = project(0)
        for s in range(N_SLABS):
            (ug, uv), (gcols, vcols) = ahead, slab_cols(s)
            if s + 1 < N_SLABS:
                ahead = project(s + 1)
            gate = conv(ug, gcols, (2 * s) % N_SLAB_BUFS)
            val = conv(uv, vcols, (2 * s + 1) % N_SLAB_BUFS)
            act = (_gelu(gate) * val).astype(BF16)
            act_ref[:, s * FFN_COLS:(s + 1) * FFN_COLS] = act
            out = out + _dot(act, wd_ref[s * FFN_COLS:(s + 1) * FFN_COLS, :])
        o_ref[...] = out
        if ex is not None:
            pl.when(i == nt - 1)(ex_wait)

    full = lambda shape: pl.BlockSpec(shape, lambda i: tuple(0 for _ in shape))
    row = lambda n: pl.BlockSpec((tt, n), lambda i: (i, 0))
    outs = pl.pallas_call(
        body, name=name, grid=(nt,),
        in_specs=[row(D_MODEL), row(D_MODEL), full((D_MODEL, D_UP)), full((FFN_CONV, D_UP)), full((1, D_UP)),
                  full((D_FF, D_MODEL))] + ex_in_specs,
        out_specs=[row(D_MODEL), row(D_FF), row(D_UP), row(D_UP)] + ex_out_specs,
        out_shape=[jax.ShapeDtypeStruct((t, D_MODEL), F32), jax.ShapeDtypeStruct((t, D_FF), BF16),
                   jax.ShapeDtypeStruct((t, D_UP), F32), jax.ShapeDtypeStruct((t, D_UP), BF16)] + ex_out_shape,
        scratch_shapes=[pltpu.VMEM((N_SLAB_BUFS, SUBLANES + tt, FFN_COLS), F32), pltpu.VMEM((SUBLANES, D_UP), F32)]
        + ex_scratch,
        input_output_aliases={} if ex is None else ex.aliases(6, 4),
        compiler_params=_params("arbitrary"))(h, n2, w_up, cw, cb, w_down, *ex_args)
    return outs[0], outs[1], outs[2], outs[3], outs[4:]


def _ffn_bwd(dh, up, pre, h2, g2, w_down_t, w_up_t, cw, *, tt, name, ex=None):
    t = dh.shape[0]
    tt = _tile(t, tt)
    nt = t // tt
    ex_args, ex_in_specs, ex_out_shape, ex_out_specs, ex_scratch, ex_counts = _ex_parts(ex)

    def body(*refs):
        ((dh_ref, up_ref, pre_ref, h2_ref, g2_ref, wdt_ref, wut_ref, cw_ref), ex_in,
         (o_ref, dup_ref, dcw_ref, dcb_ref, dg_ref), ex_out, (slab, carry), ex_sems) = _split_refs(
            refs, (8, ex_counts[0], 5, ex_counts[1], 2, ex_counts[2]))
        i = pl.program_id(0)
        if ex is not None:
            ex_start, ex_wait = ex.ops(ex_in, ex_out, ex_sems)
            pl.when(i == 0)(ex_start)

        @pl.when(i == 0)
        def _():
            for ref in (dcw_ref, dcb_ref, dg_ref, carry):
                ref[...] = jnp.zeros_like(ref)

        dhv = dh_ref[...]
        dhb = dhv.astype(BF16)

        def slab_cols(s):
            return slice(s * FFN_COLS, (s + 1) * FFN_COLS), slice(D_FF + s * FFN_COLS, D_FF + (s + 1) * FFN_COLS)

        def d_act(s):
            return _dot(dhb, wdt_ref[:, slab_cols(s)[0]])

        def through_conv(dp, cols, buf_id):
            sb = slab.at[buf_id]
            sb[0:tt, :] = dp
            sb[tt:tt + SUBLANES, :] = carry[:, cols]
            carry[:, cols] = dp[0:SUBLANES, :]
            shifted = [sb[FFN_CONV - 1 - k:FFN_CONV - 1 - k + tt, :] for k in range(FFN_CONV - 1)] + [dp]
            x = up_ref[:, cols].astype(F32)
            dup = cw_ref[0:1, cols] * shifted[0]
            for k in range(1, FFN_CONV):
                dup = dup + cw_ref[k:k + 1, cols] * shifted[k]
            dcb_ref[:, cols] += _colsum(dp)
            dcw_ref[:, cols] += jnp.concatenate([_colsum(sh * x) for sh in shifted], axis=0)
            dupb = dup.astype(BF16)
            dup_ref[:, cols] = dupb
            return _dot(dupb, wut_ref[cols, :])

        dn = jnp.zeros((tt, D_MODEL), F32)
        ahead = d_act(0)
        for s in range(N_SLABS):
            da, (gcols, vcols) = ahead, slab_cols(s)
            if s + 1 < N_SLABS:
                ahead = d_act(s + 1)
            gate, val = pre_ref[:, gcols], pre_ref[:, vcols]
            dn = dn + through_conv(da * val * _gelu_grad(gate), gcols, (2 * s) % N_SLAB_BUFS)
            dn = dn + through_conv(da * _gelu(gate), vcols, (2 * s + 1) % N_SLAB_BUFS)
        xv = h2_ref[...]
        dx, dg = _rms_bwd(xv, _rms_r(xv), g2_ref[...], dn)
        o_ref[...] = dhv + dx
        dg_ref[...] += dg
        if ex is not None:
            pl.when(i == nt - 1)(ex_wait)

    rev = lambda i: nt - 1 - i
    full = lambda shape: pl.BlockSpec(shape, lambda i: tuple(0 for _ in shape))
    row = lambda n: pl.BlockSpec((tt, n), lambda i: (rev(i), 0))
    outs = pl.pallas_call(
        body, name=name, grid=(nt,),
        in_specs=[row(D_MODEL), row(D_UP), row(D_UP), row(D_MODEL), full((1, D_MODEL)), full((D_MODEL, D_FF)),
                  full((D_UP, D_MODEL)), full((FFN_CONV, D_UP))] + ex_in_specs,
        out_specs=[row(D_MODEL), row(D_UP), full((FFN_CONV, D_UP)), full((1, D_UP)), full((1, D_MODEL))]
        + ex_out_specs,
        out_shape=[jax.ShapeDtypeStruct((t, D_MODEL), F32), jax.ShapeDtypeStruct((t, D_UP), BF16),
                   jax.ShapeDtypeStruct((FFN_CONV, D_UP), F32), jax.ShapeDtypeStruct((1, D_UP), F32),
                   jax.ShapeDtypeStruct((1, D_MODEL), F32)] + ex_out_shape,
        scratch_shapes=[pltpu.VMEM((N_SLAB_BUFS, tt + SUBLANES, FFN_COLS), F32), pltpu.VMEM((SUBLANES, D_UP), F32)]
        + ex_scratch,
        input_output_aliases={} if ex is None else ex.aliases(8, 5),
        compiler_params=_params("arbitrary"))(dh, up, pre, h2, g2, w_down_t, w_up_t, cw, *ex_args)
    return outs[0], outs[1], outs[2], outs[3], outs[4], outs[5:]


def _ple_fwd(h, p, g, w_gate, w_proj, *, tt, name):
    t = h.shape[0]
    tt = _tile(t, tt)

    def body(h_ref, p_ref, g_ref, wg_ref, wp_ref, o_ref):
        x = h_ref[...]
        n = (x * _rms_r(x) * g_ref[...]).astype(BF16)
        gate = _sigmoid(_dot(n, wg_ref[...]))
        o_ref[...] = x + _dot(p_ref[...].astype(BF16), wp_ref[...]) * gate

    full = lambda shape: pl.BlockSpec(shape, lambda i: tuple(0 for _ in shape))
    return pl.pallas_call(
        body, name=name, grid=(t // tt,),
        in_specs=[pl.BlockSpec((tt, D_MODEL), lambda i: (i, 0)), pl.BlockSpec((tt, D_PLE), lambda i: (i, 0)),
                  full((1, D_MODEL)), full((D_MODEL, D_MODEL)), full((D_PLE, D_MODEL))],
        out_specs=pl.BlockSpec((tt, D_MODEL), lambda i: (i, 0)),
        out_shape=jax.ShapeDtypeStruct((t, D_MODEL), F32),
        compiler_params=_params("arbitrary"))(h, p, g, w_gate, w_proj)


def _ple_bwd(dh, h, p, g, w_gate, w_gate_t, w_proj, *, tt, name):
    t = h.shape[0]
    tt = _tile(t, tt)

    def body(dh_ref, h_ref, p_ref, g_ref, wg_ref, wgt_ref, wp_ref, o_ref, dwg_ref, dwp_ref, dg_ref):
        @pl.when(pl.program_id(0) == 0)
        def _():
            dwg_ref[...] = jnp.zeros_like(dwg_ref)
            dwp_ref[...] = jnp.zeros_like(dwp_ref)
            dg_ref[...] = jnp.zeros_like(dg_ref)

        x = h_ref[...]
        r = _rms_r(x)
        gv = g_ref[...]
        n = (x * r * gv).astype(BF16)
        gate = _sigmoid(_dot(n, wg_ref[...]))
        pb = p_ref[...].astype(BF16)
        pe = _dot(pb, wp_ref[...])
        dhv = dh_ref[...]
        dwp_ref[...] += _dot_tn(pb, (dhv * gate).astype(BF16))
        ds = (dhv * pe * gate * (1.0 - gate)).astype(BF16)
        dwg_ref[...] += _dot_tn(n, ds)
        dx, dg = _rms_bwd(x, r, gv, _dot(ds, wgt_ref[...]))
        o_ref[...] = dhv + dx
        dg_ref[...] += dg

    full = lambda shape: pl.BlockSpec(shape, lambda i: tuple(0 for _ in shape))
    row = lambda n: pl.BlockSpec((tt, n), lambda i: (i, 0))
    return pl.pallas_call(
        body, name=name, grid=(t // tt,),
        in_specs=[row(D_MODEL), row(D_MODEL), row(D_PLE), full((1, D_MODEL)), full((D_MODEL, D_MODEL)),
                  full((D_MODEL, D_MODEL)), full((D_PLE, D_MODEL))],
        out_specs=[row(D_MODEL), full((D_MODEL, D_MODEL)), full((D_PLE, D_MODEL)), full((1, D_MODEL))],
        out_shape=[jax.ShapeDtypeStruct((t, D_MODEL), F32), jax.ShapeDtypeStruct((D_MODEL, D_MODEL), F32),
                   jax.ShapeDtypeStruct((D_PLE, D_MODEL), F32), jax.ShapeDtypeStruct((1, D_MODEL), F32)],
        compiler_params=_params("arbitrary"))(dh, h, p, g, w_gate, w_gate_t, w_proj)


def _loss_head(h, g, target, *, tt, name):
    t = h.shape[0]
    tt = _tile(t, tt)

    def body(h_ref, g_ref, tg_ref, dh_ref, loss_ref, dg_ref):
        @pl.when(pl.program_id(0) == 0)
        def _():
            loss_ref[...] = jnp.zeros_like(loss_ref)
            dg_ref[...] = jnp.zeros_like(dg_ref)

        x = h_ref[...]
        r = _rms_r(x)
        gv = g_ref[...]
        diff = x * r * gv - tg_ref[...]
        loss_ref[...] += 0.5 * jnp.sum(jnp.mean(diff * diff, axis=-1, keepdims=True), axis=0, keepdims=True)
        dx, dg = _rms_bwd(x, r, gv, diff * (1.0 / D_MODEL))
        dh_ref[...] = dx
        dg_ref[...] += dg

    return pl.pallas_call(
        body, name=name, grid=(t // tt,),
        in_specs=[pl.BlockSpec((tt, D_MODEL), lambda i: (i, 0)), pl.BlockSpec((1, D_MODEL), lambda i: (0, 0)),
                  pl.BlockSpec((tt, D_MODEL), lambda i: (i, 0))],
        out_specs=[pl.BlockSpec((tt, D_MODEL), lambda i: (i, 0)), pl.BlockSpec((SUBLANES, LANES), lambda i: (0, 0)),
                   pl.BlockSpec((1, D_MODEL), lambda i: (0, 0))],
        out_shape=[jax.ShapeDtypeStruct((t, D_MODEL), F32), jax.ShapeDtypeStruct((SUBLANES, LANES), F32),
                   jax.ShapeDtypeStruct((1, D_MODEL), F32)],
        compiler_params=_params("arbitrary"))(h, g, target)


ADAM_BLOCK_BYTES = 4 * 1024 * 1024


def _adam_rows(rows, cols):
    lanes = -(-cols // LANES) * LANES
    for cand in (1024, 512, 256, 128, 64, 32, 16, 8):
        if rows % cand == 0 and N_DEV * cand * lanes * 4 <= ADAM_BLOCK_BYTES:
            return cand
    return rows


def _sum_adamw(parts, w, m, v, *, name):
    nl, rows, cols = w.shape
    tr = _adam_rows(rows, cols)

    def body(p_ref, w_ref, m_ref, v_ref, g_ref, d_ref, nm_ref, nv_ref):
        g = p_ref[0]
        for k in range(1, N_DEV):
            g = g + p_ref[k]
        g_ref[...] = g
        nm = ADAM_B1 * m_ref[...] + (1.0 - ADAM_B1) * g
        nv = ADAM_B2 * v_ref[...] + (1.0 - ADAM_B2) * (g * g)
        m_hat = nm / (1.0 - ADAM_B1 ** ADAM_STEP)
        v_hat = nv / (1.0 - ADAM_B2 ** ADAM_STEP)
        d_ref[...] = -ADAM_LR * (m_hat / (jnp.sqrt(v_hat) + ADAM_EPS) + ADAM_WD * w_ref[...])
        nm_ref[...] = nm
        nv_ref[...] = nv

    blk = pl.BlockSpec((None, tr, cols), lambda l, r: (l, r, 0))
    return pl.pallas_call(
        body, name=name, grid=(nl, rows // tr),
        in_specs=[pl.BlockSpec((None, N_DEV, tr, cols), lambda l, r: (l, 0, r, 0)), blk, blk, blk],
        out_specs=[blk, blk, blk, blk],
        out_shape=[jax.ShapeDtypeStruct((nl, rows, cols), F32)] * 4,
        compiler_params=_params("arbitrary", "arbitrary"))(parts, w, m, v)


PACK_ROWS = 512


def _pack(arrays):
    flat = jnp.concatenate([a.astype(F32).reshape(-1) for a in arrays])
    pad = (-flat.shape[0]) % (PACK_ROWS * LANES)
    return jnp.pad(flat, (0, pad)).reshape(-1, LANES)


def _unpack(buf, shapes):
    flat = buf.reshape(-1)
    out, off = [], 0
    for s in shapes:
        n = math.prod(s)
        out.append(flat[off:off + n].reshape(s))
        off += n
    return out


def _to_proj_cols(w):
    z, xbc, dtc, u = jnp.split(w, [SSD_WIDTH, SSD_WIDTH + SSD_XBC, SSD_WIDTH + SSD_XBC + SSD_HEADS], axis=-1)
    pad = jnp.zeros(w.shape[:-1] + (LANES - SSD_HEADS,), w.dtype)
    return jnp.concatenate([xbc, z, u, dtc, pad], axis=-1)


def _from_proj_cols(w):
    xbc, z, u, dtc = (w[..., COL_XBC:COL_Z], w[..., COL_Z:COL_U], w[..., COL_U:COL_DT],
                      w[..., COL_DT:COL_DT + SSD_HEADS])
    return jnp.concatenate([z, xbc, dtc, u], axis=-1)


def _pad_heads(v):
    return jnp.pad(v, (0, LANES - SSD_HEADS)).reshape(1, LANES)


def _cat_cols(g):
    return jnp.transpose(g, (1, 0, 2)).reshape(g.shape[1], N_DEV * g.shape[2])


def _split_cols(w):
    r, c = w.shape
    return jnp.transpose(w.reshape(r, N_DEV, c // N_DEV), (1, 0, 2))


def _cat_rows(g):
    return g.reshape(N_DEV * g.shape[1], g.shape[2])


def _split_rows(w):
    return w.reshape(N_DEV, w.shape[0] // N_DEV, w.shape[1])


SHARDED = ("w_in", "w_out", "ffn_w_up", "ffn_w_down", "ple_w_gate", "ple_w_proj", "ssd_conv_w", "ffn_conv_w")
COL_SHARDED = ("w_in", "ffn_w_up", "ple_w_proj", "ssd_conv_w", "ffn_conv_w")
MATMUL_W = SHARDED[:6]
REPLICATED = ("mix_norm_g", "ssd_conv_b", "ssd_dt_bias", "ssd_a_log", "ssd_d", "ssd_norm_g", "pool_w", "pool_scale",
              "ffn_norm_g", "ffn_conv_b", "ple_norm_g", "final_norm_g")
WEIGHTS = ("mix_norm_g", "w_in", "ssd_conv_w", "ssd_conv_b", "ssd_dt_bias", "ssd_a_log", "ssd_d", "ssd_norm_g",
           "pool_w", "pool_scale", "w_out", "ffn_norm_g", "ffn_w_up", "ffn_conv_w", "ffn_conv_b", "ffn_w_down",
           "ple_norm_g", "ple_w_gate", "ple_w_proj", "final_norm_g")


FIRST_USED = ("w_in", "ssd_conv_w")
LATER_USED = tuple(k for k in SHARDED if k not in FIRST_USED)
LAST_MADE = ("w_out", "ssd_conv_w", "w_in")
EARLY_MADE = tuple(k for k in SHARDED if k not in LAST_MADE)
TRANSPOSED = ("w_in", "w_out", "ffn_w_up", "ffn_w_down", "ple_w_gate")


def _pick(names, per_sharded):
    return [per_sharded[SHARDED.index(k)] for k in names]


def _put(names, per_sharded, values):
    out = list(per_sharded)
    for k, val in zip(names, values):
        out[SHARDED.index(k)] = val
    return out


def _assemble(names, gathered):
    full = {}
    for k, g in zip(names, gathered):
        full[k] = _cat_cols(g) if k in COL_SHARDED else _cat_rows(g)
        if k == "w_in":
            full[k] = _to_proj_cols(full[k])
        if k in TRANSPOSED:
            full[k + "_t"] = full[k].T
    return full


def _grad_shards(names, grads):
    out = []
    for k in names:
        g = _from_proj_cols(grads[k]) if k == "w_in" else grads[k]
        out.append(_split_cols(g) if k in COL_SHARDED else _split_rows(g))
    return out


def _layer_fwd(i, h1, p_i, lw, rep, consts, ex_own, ex):
    tril, e_mat = consts
    row = lambda v: v.reshape(1, -1)
    dtb, alog = _pad_heads(rep["ssd_dt_bias"]), _pad_heads(rep["ssd_a_log"])
    dexp = row(jnp.repeat(rep["ssd_d"], SSD_HEAD_DIM))
    pw = rep["pool_w"].astype(BF16)
    proj = _norm_matmul(h1, lw["w_in"], row(rep["mix_norm_g"]), tt=512, tn=N_PROJ, name=f"in_proj_{i}")
    yssd, ypre, states, own = _ssd_fwd(proj, lw["ssd_conv_w"], row(rep["ssd_conv_b"]), dtb, alog, dexp,
                                       row(rep["ssd_norm_g"]), tril, e_mat, ts=512, name=f"ssd_fwd_{i}", ex=ex_own)
    if ex_own is not None:
        lw = dict(lw, **_assemble(LATER_USED, own))
    h2, ymix, n2 = _mix_out(h1, yssd, proj, pw, row(rep["pool_scale"]), lw["w_out"], row(rep["ffn_norm_g"]), tt=512,
                            name=f"mix_out_{i}")
    h3, act, pre, up, gathered = _ffn_fwd(h2, n2, lw["ffn_w_up"], lw["ffn_conv_w"], row(rep["ffn_conv_b"]),
                                          lw["ffn_w_down"], tt=256, name=f"ffn_fwd_{i}", ex=ex)
    h4 = _ple_fwd(h3, p_i, row(rep["ple_norm_g"]), lw["ple_w_gate"], lw["ple_w_proj"], tt=512, name=f"ple_fwd_{i}")
    saved = dict(h1=h1, proj=proj, ypre=ypre, states=states, ymix=ymix, h2=h2, n2=n2, up=up, pre=pre, act=act, h3=h3,
                 dtb=dtb, alog=alog, dexp=dexp, pw=pw)
    return h4, saved, lw, gathered


def _layer_bwd(i, dh, p_i, lw, rep, s, consts, pending, parts, own_early):
    tril, triu, e_mat = consts
    ex = None if pending is None else _Exchange(pending, scatter=True, layer=i + 1, into=parts)
    row = lambda v: v.reshape(1, -1)
    g = {}
    dh, g["ple_w_gate"], g["ple_w_proj"], dg3 = _ple_bwd(dh, s["h3"], p_i, row(rep["ple_norm_g"]), lw["ple_w_gate"],
                                                         lw["ple_w_gate_t"], lw["ple_w_proj"], tt=512,
                                                         name=f"ple_bwd_{i}")
    g["ple_norm_g"] = dg3.reshape(-1)
    g["ffn_w_down"] = _matmul_tn(s["act"], dh, tm=D_FF // 2, tn=D_MODEL, tk=1024, name=f"dw_down_{i}")
    dh, dup, g["ffn_conv_w"], dcb, dg2, scattered = _ffn_bwd(
        dh, s["up"], s["pre"], s["h2"], row(rep["ffn_norm_g"]), lw["ffn_w_down_t"], lw["ffn_w_up_t"],
        lw["ffn_conv_w"], tt=256, name=f"ffn_bwd_{i}", ex=ex)
    if ex is not None:
        parts = scattered
    g["ffn_conv_b"], g["ffn_norm_g"] = dcb.reshape(-1), dg2.reshape(-1)
    g["ffn_w_up"] = _matmul_tn(s["n2"], dup, tm=D_MODEL, tn=D_UP // 4, tk=1024, name=f"dw_up_{i}")
    dymix, g["w_out"] = _out_bwd(dh, s["ymix"], lw["w_out_t"], tt=512, name=f"out_bwd_{i}")
    du, g["pool_w"], dsc = _pool_bwd(dymix, s["proj"], s["pw"], jnp.swapaxes(s["pw"], 1, 2), row(rep["pool_scale"]),
                                     tt=512, name=f"pool_bwd_{i}")
    g["pool_scale"] = dsc.reshape(-1)
    ex_own = None
    if own_early:
        ex_own = _Exchange(_grad_shards(EARLY_MADE, g), scatter=True, layer=i, into=_pick(EARLY_MADE, parts))
    (dproj, g["ssd_conv_w"], dcb, ddtb, dalog, dd, dng), own = _ssd_bwd(
        dymix, s["proj"], s["ypre"], s["states"], du, lw["ssd_conv_w"], row(rep["ssd_conv_b"]), s["dtb"], s["alog"],
        s["dexp"], row(rep["ssd_norm_g"]), tril, triu, e_mat, ts=512, name=f"ssd_bwd_{i}", ex=ex_own)
    if own_early:
        parts = _put(EARLY_MADE, parts, own)
    g["ssd_conv_b"], g["ssd_norm_g"] = dcb.reshape(-1), dng.reshape(-1)
    g["ssd_dt_bias"], g["ssd_a_log"], g["ssd_d"] = ddtb[0, :SSD_HEADS], dalog[0, :SSD_HEADS], dd[0, :SSD_HEADS]
    dh, g["w_in"], dg1 = _in_bwd(dproj, s["h1"], row(rep["mix_norm_g"]), lw["w_in_t"], dh, tt=256, name=f"in_bwd_{i}")
    g["mix_norm_g"] = dg1.reshape(-1)
    return dh, g, parts


def kernel(x, p, mix_norm_g, w_in, ssd_conv_w, ssd_conv_b, ssd_dt_bias, ssd_a_log, ssd_d, ssd_norm_g, pool_w, pool_scale, w_out, ffn_norm_g, ffn_w_up, ffn_conv_w, ffn_conv_b, ffn_w_down, ple_norm_g, ple_w_gate, ple_w_proj, final_norm_g, loss_target, m_mix_norm_g, m_w_in, m_ssd_conv_w, m_ssd_conv_b, m_ssd_dt_bias, m_ssd_a_log, m_ssd_d, m_ssd_norm_g, m_pool_w, m_pool_scale, m_w_out, m_ffn_norm_g, m_ffn_w_up, m_ffn_conv_w, m_ffn_conv_b, m_ffn_w_down, m_ple_norm_g, m_ple_w_gate, m_ple_w_proj, m_final_norm_g, v_mix_norm_g, v_w_in, v_ssd_conv_w, v_ssd_conv_b, v_ssd_dt_bias, v_ssd_a_log, v_ssd_d, v_ssd_norm_g, v_pool_w, v_pool_scale, v_w_out, v_ffn_norm_g, v_ffn_w_up, v_ffn_conv_w, v_ffn_conv_b, v_ffn_w_down, v_ple_norm_g, v_ple_w_gate, v_ple_w_proj, v_final_norm_g):
    w = dict(mix_norm_g=mix_norm_g, w_in=w_in, ssd_conv_w=ssd_conv_w, ssd_conv_b=ssd_conv_b, ssd_dt_bias=ssd_dt_bias,
             ssd_a_log=ssd_a_log, ssd_d=ssd_d, ssd_norm_g=ssd_norm_g, pool_w=pool_w, pool_scale=pool_scale, w_out=w_out,
             ffn_norm_g=ffn_norm_g, ffn_w_up=ffn_w_up, ffn_conv_w=ffn_conv_w, ffn_conv_b=ffn_conv_b,
             ffn_w_down=ffn_w_down, ple_norm_g=ple_norm_g, ple_w_gate=ple_w_gate, ple_w_proj=ple_w_proj,
             final_norm_g=final_norm_g)
    m = dict(mix_norm_g=m_mix_norm_g, w_in=m_w_in, ssd_conv_w=m_ssd_conv_w, ssd_conv_b=m_ssd_conv_b,
             ssd_dt_bias=m_ssd_dt_bias, ssd_a_log=m_ssd_a_log, ssd_d=m_ssd_d, ssd_norm_g=m_ssd_norm_g, pool_w=m_pool_w,
             pool_scale=m_pool_scale, w_out=m_w_out, ffn_norm_g=m_ffn_norm_g, ffn_w_up=m_ffn_w_up,
             ffn_conv_w=m_ffn_conv_w, ffn_conv_b=m_ffn_conv_b, ffn_w_down=m_ffn_w_down, ple_norm_g=m_ple_norm_g,
             ple_w_gate=m_ple_w_gate, ple_w_proj=m_ple_w_proj, final_norm_g=m_final_norm_g)
    v = dict(mix_norm_g=v_mix_norm_g, w_in=v_w_in, ssd_conv_w=v_ssd_conv_w, ssd_conv_b=v_ssd_conv_b,
             ssd_dt_bias=v_ssd_dt_bias, ssd_a_log=v_ssd_a_log, ssd_d=v_ssd_d, ssd_norm_g=v_ssd_norm_g, pool_w=v_pool_w,
             pool_scale=v_pool_scale, w_out=v_w_out, ffn_norm_g=v_ffn_norm_g, ffn_w_up=v_ffn_w_up,
             ffn_conv_w=v_ffn_conv_w, ffn_conv_b=v_ffn_conv_b, ffn_w_down=v_ffn_w_down, ple_norm_g=v_ple_norm_g,
             ple_w_gate=v_ple_w_gate, ple_w_proj=v_ple_w_proj, final_norm_g=v_final_norm_g)

    tril = jnp.tril(jnp.ones((CHUNK, CHUNK), BF16))
    triu = tril.T
    e_mat = (jnp.arange(SSD_WIDTH)[None, :] // SSD_HEAD_DIM == jnp.arange(LANES)[:, None]).astype(BF16)
    rep = [{k: w[k][i] for k in REPLICATED if k != "final_norm_g"} for i in range(DEPTH)]
    p_loc = p[:, 0]

    shards = [w[k].astype(BF16) if k in MATMUL_W else w[k] for k in SHARDED]
    lw = _assemble(FIRST_USED, _exchange_call(_Exchange(_pick(FIRST_USED, shards), scatter=False, layer=0),
                                              "gather_weights_0"))
    h, saved, layer_w = x[0], [], []
    for i in range(DEPTH):
        ex_own = _Exchange(_pick(LATER_USED, shards), scatter=False, layer=0) if i == 0 else None
        ex = _Exchange(shards, scatter=False, layer=i + 1) if i + 1 < DEPTH else None
        h, s, lw, gathered = _layer_fwd(i, h, p_loc[i], lw, rep[i], (tril, e_mat), ex_own, ex)
        saved.append(s)
        layer_w.append(lw)
        lw = _assemble(SHARDED, gathered)

    dh, loss_blk, dgf = _loss_head(h, final_norm_g.reshape(1, -1), loss_target[0], tt=512, name="loss_head")
    loss = lax.psum(loss_blk[0, 0], ("x", "y", "c"))

    rep_grads = [None] * DEPTH
    pending, parts = None, None
    for i in reversed(range(DEPTH)):
        dh, g, parts = _layer_bwd(i, dh, p_loc[i], layer_w[i], rep[i], saved[i], (tril, triu, e_mat), pending, parts,
                                  own_early=(i == 0))
        pending = _grad_shards(SHARDED, g) if i > 0 else _grad_shards(LAST_MADE, g)
        rep_grads[i] = g
    parts = _put(LAST_MADE, parts, _exchange_call(
        _Exchange(pending, scatter=True, layer=0, into=_pick(LAST_MADE, parts)), "scatter_grads_0"))

    out = {}
    for k, part in zip(SHARDED, parts):
        out[k] = _sum_adamw(part, w[k], m[k], v[k], name=f"adamw_{k}")

    rp_grads = [dgf.reshape(-1) if k == "final_norm_g" else jnp.stack([rep_grads[i][k] for i in range(DEPTH)])
                for k in REPLICATED]
    rp_shapes = [w[k].shape for k in REPLICATED]
    rp_parts = _exchange_call(_Exchange([_pack(rp_grads)[None]], scatter=False, layer=0), "gather_replicated_grads")[0]
    rp_out = _sum_adamw(rp_parts[None], *[_pack([d[k] for k in REPLICATED])[None] for d in (w, m, v)],
                        name="adamw_replicated")
    for j in range(4):
        for k, arr in zip(REPLICATED, _unpack(rp_out[j][0], rp_shapes)):
            out.setdefault(k, [None] * 4)[j] = arr
    results = [out[k][j] for j in range(4) for k in WEIGHTS]
    return (loss, dh[None], *results)
```

```python
import functools
import math

import jax
import jax.numpy as jnp
from jax import lax
from jax.experimental import pallas as pl
from jax.experimental.pallas import tpu as pltpu

F32 = jnp.float32
BF16 = jnp.bfloat16

N_DEV = 8
EPS = 1e-6
DEPTH = 4
D_MODEL = 1024
D_PLE = 256
SSD_WIDTH = 512
SSD_HEADS = 8
SSD_HEAD_DIM = 64
SSD_GROUPS = 2
SSD_STATE = 128
SSD_CONV = 4
CHUNK = 128
SSD_XBC = 1024
POOL_WINDOWS = (2, 4, 8, 16)
POOL_WIDTH = 512
POOL_GROUP = 128
POOL_HALO = 16
D_IN_PROJ = 2056
D_FF = 2816
D_UP = 2 * D_FF
FFN_CONV = 3
SUBLANES = 8
LANES = 128
N_PROJ = 2176
COL_XBC, COL_Z, COL_U, COL_DT = 0, 1024, 1536, 2048
N_PAIRS = SSD_HEADS // 2
ADAM_LR, ADAM_B1, ADAM_B2, ADAM_EPS, ADAM_WD, ADAM_STEP = 0.001, 0.9, 0.999, 1e-08, 0.01, 10
GELU_C = math.sqrt(2.0 / math.pi)
GELU_A = 0.044715
VMEM_LIMIT = 56 * 1024 * 1024

NT_DIMS = (((1,), (1,)), ((), ()))
TN_DIMS = (((0,), (0,)), ((), ()))


def _params(*sem):
    return pltpu.CompilerParams(dimension_semantics=sem, vmem_limit_bytes=VMEM_LIMIT)


def _dot(a, b):
    return jnp.dot(a, b, preferred_element_type=F32)


def _dot_nt(a, b):
    return lax.dot_general(a, b, NT_DIMS, preferred_element_type=F32)


def _dot_tn(a, b):
    return lax.dot_general(a, b, TN_DIMS, preferred_element_type=F32)


def _split3(a):
    hi = a.astype(BF16)
    r1 = a - hi.astype(F32)
    mid = r1.astype(BF16)
    return hi, mid, (r1 - mid.astype(F32)).astype(BF16)


def _hdot(a, b):
    if a.dtype == BF16:
        return sum(_dot(a, piece) for piece in _split3(b))
    return sum(_dot(piece, b) for piece in _split3(a))


def _headsum(q, e):
    return sum(_dot_nt(piece, e) for piece in _split3(q))


def _colsum(v):
    return jnp.sum(v, axis=0, keepdims=True)


def _sigmoid(v):
    return 1.0 / (1.0 + jnp.exp(-v))


def _softplus(v):
    e = jnp.exp(-jnp.abs(v))
    return jnp.maximum(v, 0.0) + jnp.where(e < 1e-4, e * (1.0 - 0.5 * e), jnp.log(1.0 + e))


def _rms_r(x):
    return lax.rsqrt(jnp.mean(x * x, axis=-1, keepdims=True) + EPS)


def _rms_bwd(x, r, g, dn):
    xhat = x * r
    gd = dn * g
    dx = r * (gd - xhat * jnp.mean(gd * xhat, axis=-1, keepdims=True))
    return dx, _colsum(dn * xhat)


def _gelu(v):
    return 0.5 * v * (1.0 + jnp.tanh(GELU_C * (v + GELU_A * v * v * v)))


def _gelu_grad(v):
    th = jnp.tanh(GELU_C * (v + GELU_A * v * v * v))
    return 0.5 * (1.0 + th) + 0.5 * v * (1.0 - th * th) * GELU_C * (1.0 + 3.0 * GELU_A * v * v)


def _tile(t, want):
    return min(t, want)


class _Exchange:
    def __init__(self, srcs, *, scatter, layer, into=None):
        self.srcs, self.scatter, self.layer = list(srcs), scatter, layer
        self.into = None if into is None else list(into)
        n = len(self.srcs)
        self.args = self.srcs + (self.into or [])
        self.in_specs = [pl.BlockSpec(memory_space=pl.ANY)] * len(self.args)
        if scatter:
            self.out_shape = [jax.ShapeDtypeStruct((DEPTH,) + s.shape, s.dtype) for s in self.srcs]
        else:
            self.out_shape = [jax.ShapeDtypeStruct((N_DEV,) + s.shape[1:], s.dtype) for s in self.srcs]
        self.out_specs = [pl.BlockSpec(memory_space=pl.ANY)] * n
        self.scratch = [pltpu.SemaphoreType.DMA((n, N_DEV - 1)), pltpu.SemaphoreType.DMA((n, N_DEV - 1)),
                        pltpu.SemaphoreType.DMA((n,))]

    def aliases(self, n_in_before, n_out_before):
        if self.into is None:
            return {}
        n = len(self.srcs)
        return {n_in_before + n + a: n_out_before + a for a in range(n)}

    def ops(self, in_refs, out_refs, sems):
        send_sems, recv_sems, local_sems = sems
        n = len(self.srcs)

        def copies():
            x, y, c = lax.axis_index("x"), lax.axis_index("y"), lax.axis_index("c")
            me = 4 * x + 2 * y + c

            def block(a, idx):
                return in_refs[a].at[idx] if self.scatter else in_refs[a].at[self.layer]

            def slot(a, idx):
                return out_refs[a].at[self.layer].at[idx] if self.scatter else out_refs[a].at[idx]

            local = [pltpu.make_async_copy(block(a, me), slot(a, me), local_sems.at[a]) for a in range(n)]
            sends, recvs = [], []
            for k in range(1, N_DEV):
                px = 1 - x if k & 4 else x
                py = 1 - y if k & 2 else y
                pc = 1 - c if k & 1 else c
                peer = 4 * px + 2 * py + pc
                for a in range(n):
                    kw = dict(send_sem=send_sems.at[a, k - 1], recv_sem=recv_sems.at[a, k - 1], device_id=(px, py, pc),
                              device_id_type=pl.DeviceIdType.MESH)
                    sends.append(pltpu.make_async_remote_copy(src_ref=block(a, peer), dst_ref=slot(a, me), **kw))
                    recvs.append(pltpu.make_async_remote_copy(src_ref=block(a, peer), dst_ref=slot(a, peer), **kw))
            return local, sends, recvs

        def start():
            local, sends, _ = copies()
            for cp in local + sends:
                cp.start()

        def wait():
            local, sends, recvs = copies()
            for send, recv in zip(sends, recvs):
                send.wait_send()
                recv.wait_recv()
            for cp in local:
                cp.wait()

        return start, wait


def _exchange_call(ex, name):
    n_in, n = len(ex.args), len(ex.srcs)

    def body(*refs):
        start, wait = ex.ops(refs[:n_in], refs[n_in:n_in + n], refs[n_in + n:])
        start()
        wait()

    return pl.pallas_call(
        body, name=name, in_specs=ex.in_specs, out_specs=ex.out_specs, out_shape=ex.out_shape,
        scratch_shapes=ex.scratch, input_output_aliases=ex.aliases(0, 0))(*ex.args)


def _split_refs(refs, counts):
    out, k = [], 0
    for cnt in counts:
        out.append(refs[k:k + cnt])
        k += cnt
    return out


def _ex_parts(ex):
    if ex is None:
        return [], [], [], [], [], (0, 0, 0)
    return ex.args, ex.in_specs, ex.out_shape, ex.out_specs, ex.scratch, (len(ex.args), len(ex.srcs), 3)


def _norm_matmul(h, w, g=None, *, tt, tn, name):
    t, k = h.shape
    n = w.shape[1]
    tt, tn = _tile(t, tt), _tile(n, tn)
    normed = g is not None

    def body(*refs):
        if normed:
            h_ref, g_ref, w_ref, o_ref = refs
            x = h_ref[...]
            xn = (x * _rms_r(x) * g_ref[...]).astype(BF16)
        else:
            h_ref, w_ref, o_ref = refs
            xn = h_ref[...].astype(BF16)
        o_ref[...] = _dot(xn, w_ref[...])

    in_specs = [pl.BlockSpec((tt, k), lambda j, i: (i, 0))]
    args = [h]
    if normed:
        in_specs.append(pl.BlockSpec((1, k), lambda j, i: (0, 0)))
        args.append(g)
    in_specs.append(pl.BlockSpec((k, tn), lambda j, i: (0, j)))
    args.append(w)
    return pl.pallas_call(
        body, name=name, grid=(n // tn, t // tt), in_specs=in_specs,
        out_specs=pl.BlockSpec((tt, tn), lambda j, i: (i, j)), out_shape=jax.ShapeDtypeStruct((t, n), F32),
        compiler_params=_params("arbitrary", "arbitrary"))(*args)


def _matmul_tn(a, b, *, tm, tn, tk, name):
    t, m = a.shape
    n = b.shape[1]
    tm, tn, tk = _tile(m, tm), _tile(n, tn), _tile(t, tk)

    def body(a_ref, b_ref, o_ref):
        @pl.when(pl.program_id(2) == 0)
        def _():
            o_ref[...] = jnp.zeros_like(o_ref)

        o_ref[...] += _dot_tn(a_ref[...].astype(BF16), b_ref[...].astype(BF16))

    return pl.pallas_call(
        body, name=name, grid=(m // tm, n // tn, t // tk),
        in_specs=[pl.BlockSpec((tk, tm), lambda i, j, kk: (kk, i)), pl.BlockSpec((tk, tn), lambda i, j, kk: (kk, j))],
        out_specs=pl.BlockSpec((tm, tn), lambda i, j, kk: (i, j)),
        out_shape=jax.ShapeDtypeStruct((m, n), F32),
        compiler_params=_params("arbitrary", "arbitrary", "arbitrary"))(a, b)


def _ssd_conv(i_is_first, xbc_ref, halo_ref, cw_ref, cb_ref, buf, ts):
    buf[0:SUBLANES, :] = jnp.where(i_is_first, 0.0, halo_ref[...])
    buf[SUBLANES:SUBLANES + ts, :] = xbc_ref[...]
    cw = cw_ref[...]
    xc = cb_ref[...]
    for k in range(SSD_CONV):
        off = SUBLANES - (SSD_CONV - 1) + k
        xc = xc + cw[k:k + 1, :] * buf[off:off + ts, :]
    return xc


def _ssd_tile_prologue(xc, dt_ref, dtb_ref, alog_ref, e_ref, xa_scr, a_scr, dte_scr, x_scr):
    xa_scr[...] = xc * _sigmoid(xc)
    dt = _softplus(dt_ref[...] + dtb_ref[...])
    a_neg = -jnp.exp(alog_ref[...])
    a_scr[...] = dt * a_neg
    dte = _hdot(dt, e_ref[...])
    dte_scr[...] = dte
    x_scr[...] = xa_scr[:, 0:SSD_WIDTH] * dte
    return dt, a_neg


def _chunk_decays(a_c, tril, e):
    cs = _hdot(tril, a_c)
    cs_t = cs.T
    cs_e = _hdot(cs, e)
    last_e = cs_e[CHUNK - 1:CHUNK, :]
    return cs, cs_t, cs_e, last_e


def _ssd_fwd(proj, cw, cb, dtb, alog, dexp, ng, tril, e, *, ts, name, ex=None):
    t = proj.shape[0]
    ts = _tile(t, ts)
    nch = ts // CHUNK
    hb = ts // SUBLANES
    nt = t // ts
    ex_args, ex_in_specs, ex_out_shape, ex_out_specs, ex_scratch, ex_counts = _ex_parts(ex)

    def body(*refs):
        ((xbc_ref, halo_ref, z_ref, dt_ref, cw_ref, cb_ref, dtb_ref, alog_ref, dexp_ref, ng_ref, tril_ref, e_ref),
         ex_in, (y_ref, ypre_ref, st_ref, xc_ref), ex_out, (buf, xa_scr, a_scr, dte_scr, x_scr, ys_scr, hstate),
         ex_sems) = _split_refs(refs, (12, ex_counts[0], 4, ex_counts[1], 7, ex_counts[2]))
        i = pl.program_id(0)
        if ex is not None:
            ex_start, ex_wait = ex.ops(ex_in, ex_out, ex_sems)
            pl.when(i == 0)(ex_start)

        @pl.when(i == 0)
        def _():
            hstate[...] = jnp.zeros_like(hstate)

        xc = _ssd_conv(i == 0, xbc_ref, halo_ref, cw_ref, cb_ref, buf, ts)
        xc_ref[...] = xc
        _ssd_tile_prologue(xc, dt_ref, dtb_ref, alog_ref, e_ref, xa_scr, a_scr, dte_scr, x_scr)
        tril = tril_ref[...]
        e_mat = e_ref[...]
        causal = (lax.broadcasted_iota(jnp.int32, (CHUNK, CHUNK), 0)
                  >= lax.broadcasted_iota(jnp.int32, (CHUNK, CHUNK), 1))
        lane = lax.broadcasted_iota(jnp.int32, (CHUNK, LANES), 1)

        def chunk(c, carry):
            r0 = pl.multiple_of(c * CHUNK, CHUNK)
            rows = pl.ds(r0, CHUNK)
            cs, cs_t, cs_e, last_e = _chunk_decays(a_scr[rows, :], tril, e_mat)
            decay_e = jnp.exp(last_e - cs_e)
            ecs_e = jnp.exp(cs_e)
            xc = x_scr[rows, :]
            xb = xc.astype(BF16)
            xd = (xc * decay_e).astype(BF16)
            for g in range(SSD_GROUPS):
                bg = xa_scr[rows, SSD_WIDTH + g * SSD_STATE:SSD_WIDTH + (g + 1) * SSD_STATE].astype(BF16)
                cg = xa_scr[rows, SSD_WIDTH + (SSD_GROUPS + g) * SSD_STATE:
                            SSD_WIDTH + (SSD_GROUPS + g + 1) * SSD_STATE].astype(BF16)
                cbm = _dot_nt(cg, bg)
                for jj in range(2):
                    j = 2 * g + jj
                    cols = slice(j * LANES, (j + 1) * LANES)
                    xp = xb[:, cols]
                    ypair = jnp.zeros((CHUNK, LANES), F32)
                    for hh in range(2):
                        h = 2 * j + hh
                        seg = jnp.exp(jnp.where(causal, cs[:, h:h + 1] - cs_t[h:h + 1, :], -jnp.inf))
                        m = (cbm * seg).astype(BF16)
                        half = (lane < SSD_HEAD_DIM) if hh == 0 else (lane >= SSD_HEAD_DIM)
                        ypair = ypair + _dot(m, jnp.where(half, xp, jnp.zeros_like(xp)))
                    hp = hstate[j]
                    st_ref[c, j] = hp
                    ypair = ypair + _dot(cg, hp.astype(BF16)) * ecs_e[:, cols]
                    ys_scr[rows, cols] = ypair
                    hstate[j] = hp * jnp.exp(last_e[:, cols]) + _dot_tn(bg, xd[:, cols])
            return carry

        lax.fori_loop(0, nch, chunk, 0, unroll=4)
        ypre = ys_scr[...] + xa_scr[:, 0:SSD_WIDTH] * dexp_ref[...]
        ypre_ref[...] = ypre
        z = z_ref[...]
        yg = ypre * (z * _sigmoid(z))
        gw = SSD_WIDTH // SSD_GROUPS
        outs = []
        for g in range(SSD_GROUPS):
            v = yg[:, g * gw:(g + 1) * gw]
            outs.append(v * _rms_r(v))
        y_ref[...] = jnp.concatenate(outs, axis=1) * ng_ref[...]
        if ex is not None:
            pl.when(i == nt - 1)(ex_wait)

    full = lambda shape: pl.BlockSpec(shape, lambda i: tuple(0 for _ in shape))
    outs = pl.pallas_call(
        body, name=name, grid=(nt,),
        in_specs=[pl.BlockSpec((ts, SSD_XBC), lambda i: (i, COL_XBC // SSD_XBC)),
                  pl.BlockSpec((SUBLANES, SSD_XBC), lambda i: (jnp.maximum(i * hb - 1, 0), COL_XBC // SSD_XBC)),
                  pl.BlockSpec((ts, SSD_WIDTH), lambda i: (i, COL_Z // SSD_WIDTH)),
                  pl.BlockSpec((ts, LANES), lambda i: (i, COL_DT // LANES)),
                  full((SSD_CONV, SSD_XBC)), full((1, SSD_XBC)), full((1, LANES)), full((1, LANES)),
                  full((1, SSD_WIDTH)), full((1, SSD_WIDTH)), full((CHUNK, CHUNK)), full((LANES, SSD_WIDTH))]
        + ex_in_specs,
        out_specs=[pl.BlockSpec((ts, SSD_WIDTH), lambda i: (i, 0)), pl.BlockSpec((ts, SSD_WIDTH), lambda i: (i, 0)),
                   pl.BlockSpec((nch, N_PAIRS, SSD_STATE, LANES), lambda i: (i, 0, 0, 0)),
                   pl.BlockSpec((ts, SSD_XBC), lambda i: (i, 0))] + ex_out_specs,
        out_shape=[jax.ShapeDtypeStruct((t, SSD_WIDTH), F32), jax.ShapeDtypeStruct((t, SSD_WIDTH), F32),
                   jax.ShapeDtypeStruct((t // CHUNK, N_PAIRS, SSD_STATE, LANES), F32),
                   jax.ShapeDtypeStruct((t, SSD_XBC), F32)] + ex_out_shape,
        scratch_shapes=[pltpu.VMEM((SUBLANES + ts, SSD_XBC), F32), pltpu.VMEM((ts, SSD_XBC), F32),
                        pltpu.VMEM((ts, LANES), F32), pltpu.VMEM((ts, SSD_WIDTH), F32),
                        pltpu.VMEM((ts, SSD_WIDTH), F32), pltpu.VMEM((ts, SSD_WIDTH), F32),
                        pltpu.VMEM((N_PAIRS, SSD_STATE, LANES), F32)] + ex_scratch,
        input_output_aliases={} if ex is None else ex.aliases(12, 4),
        compiler_params=_params("arbitrary"))(proj, proj, proj, proj, cw, cb, dtb, alog, dexp, ng, tril, e, *ex_args)
    return outs[0], outs[1], outs[2], outs[3], outs[4:]


def _ssd_bwd(dymix, proj, xc, ypre, states, du, cw, dtb, alog, dexp, ng, tril, triu, e, *, ts, name, ex=None):
    t = proj.shape[0]
    ts = _tile(t, ts)
    nch = ts // CHUNK
    nt = t // ts
    ex_args, ex_in_specs, ex_out_shape, ex_out_specs, ex_scratch, ex_counts = _ex_parts(ex)

    def body(*refs):
        ((dy_ref, xbc_ref, xc_ref, z_ref, dt_ref, ypre_ref, st_ref, du_ref, cw_ref, dtb_ref, alog_ref,
          dexp_ref, ng_ref, tril_ref, triu_ref, e_ref), ex_in,
         (dproj_ref, dcw_ref, dcb_ref, ddtb_ref, dalog_ref, dd_ref, dng_ref), ex_out,
         (xa_scr, a_scr, dte_scr, x_scr, dyp_scr, dxa_scr, dx_scr, dbuf, carry, gstate),
         ex_sems) = _split_refs(refs, (16, ex_counts[0], 7, ex_counts[1], 10, ex_counts[2]))
        i = pl.program_id(0)
        if ex is not None:
            ex_start, ex_wait = ex.ops(ex_in, ex_out, ex_sems)
            pl.when(i == 0)(ex_start)

        @pl.when(i == 0)
        def _():
            gstate[...] = jnp.zeros_like(gstate)
            carry[...] = jnp.zeros_like(carry)
            for ref in (dcw_ref, dcb_ref, ddtb_ref, dalog_ref, dd_ref, dng_ref):
                ref[...] = jnp.zeros_like(ref)

        dt, a_neg = _ssd_tile_prologue(xc_ref[...], dt_ref, dtb_ref, alog_ref, e_ref, xa_scr, a_scr, dte_scr, x_scr)
        tril = tril_ref[...]
        triu = triu_ref[...]
        e_mat = e_ref[...]
        causal = (lax.broadcasted_iota(jnp.int32, (CHUNK, CHUNK), 0)
                  >= lax.broadcasted_iota(jnp.int32, (CHUNK, CHUNK), 1))
        lane = lax.broadcasted_iota(jnp.int32, (CHUNK, LANES), 1)
        sub = lax.broadcasted_iota(jnp.int32, (CHUNK, LANES), 0)

        z = z_ref[...]
        sig = _sigmoid(z)
        zs = z * sig
        ypre = ypre_ref[...]
        yg = ypre * zs
        dout = dy_ref[...]
        ngv = ng_ref[...]
        gw = SSD_WIDTH // SSD_GROUPS
        dyg_parts, dng_parts = [], []
        for g in range(SSD_GROUPS):
            cols = slice(g * gw, (g + 1) * gw)
            v = yg[:, cols]
            dx, dg = _rms_bwd(v, _rms_r(v), ngv[:, cols], dout[:, cols])
            dyg_parts.append(dx)
            dng_parts.append(dg)
        dyg = jnp.concatenate(dyg_parts, axis=1)
        dng_ref[...] += jnp.concatenate(dng_parts, axis=1)
        dyp = dyg * zs
        dyp_scr[...] = dyp
        dproj_ref[:, COL_Z:COL_Z + SSD_WIDTH] = dyg * ypre * (sig * (1.0 + z * (1.0 - sig)))
        dproj_ref[:, COL_U:COL_U + POOL_WIDTH] = du_ref[...]
        xs_all = xa_scr[:, 0:SSD_WIDTH]
        dd_ref[...] += _headsum(jnp.broadcast_to(_colsum(dyp * xs_all), (SUBLANES, SSD_WIDTH)), e_mat)[0:1, :]

        def chunk(k, carry_):
            c = nch - 1 - k
            r0 = pl.multiple_of(c * CHUNK, CHUNK)
            rows = pl.ds(r0, CHUNK)
            a_c = a_scr[rows, :]
            cs, cs_t, cs_e, last_e = _chunk_decays(a_c, tril, e_mat)
            decay_e = jnp.exp(last_e - cs_e)
            ecs_e = jnp.exp(cs_e)
            elast_e = jnp.exp(last_e)
            xc = x_scr[rows, :]
            xb = xc.astype(BF16)
            xd = (xc * decay_e).astype(BF16)
            dyc = dyp_scr[rows, :]
            dcs = jnp.zeros((CHUNK, LANES), F32)
            dcs_neg_t = jnp.zeros((LANES, CHUNK), F32)
            qoff, rin, ghrow = [], [], []
            for g in range(SSD_GROUPS):
                b_cols = slice(SSD_WIDTH + g * SSD_STATE, SSD_WIDTH + (g + 1) * SSD_STATE)
                c_cols = slice(SSD_WIDTH + (SSD_GROUPS + g) * SSD_STATE, SSD_WIDTH + (SSD_GROUPS + g + 1) * SSD_STATE)
                bg = xa_scr[rows, b_cols].astype(BF16)
                cg = xa_scr[rows, c_cols].astype(BF16)
                cbm = _dot_nt(cg, bg)
                dcb_m = jnp.zeros((CHUNK, CHUNK), F32)
                dbg = jnp.zeros((CHUNK, SSD_STATE), F32)
                dcg = jnp.zeros((CHUNK, SSD_STATE), F32)
                for jj in range(2):
                    j = 2 * g + jj
                    cols = slice(j * LANES, (j + 1) * LANES)
                    dyp_j = dyc[:, cols]
                    hp = st_ref[c, j]
                    hpb = hp.astype(BF16)
                    gt = gstate[j]
                    gtb = gt.astype(BF16)
                    ecs = ecs_e[:, cols]
                    yoff = _dot(cg, hpb) * ecs
                    dye = (dyp_j * ecs).astype(BF16)
                    dcg = dcg + _dot_nt(dye, hpb)
                    dht = _dot_tn(cg, dye)
                    qoff.append(dyp_j * yoff)
                    xg = _dot(bg, gtb)
                    dxp = xg * decay_e[:, cols]
                    rin.append(xg * xc[:, cols])
                    dbg = dbg + _dot_nt(xd[:, cols], gtb)
                    ghrow.append(_colsum(gt * hp) * elast_e[:, cols])
                    gstate[j] = dht + gt * elast_e[:, cols]
                    for hh in range(2):
                        h = 2 * j + hh
                        seg = jnp.exp(jnp.where(causal, cs[:, h:h + 1] - cs_t[h:h + 1, :], -jnp.inf))
                        m = cbm * seg
                        half = (lane < SSD_HEAD_DIM) if hh == 0 else (lane >= SSD_HEAD_DIM)
                        dym = jnp.where(half, dyp_j, 0.0).astype(BF16)
                        w = _dot_nt(dym, xb[:, cols])
                        pm = w * m
                        dcs = dcs + jnp.where(lane == h, jnp.sum(pm, axis=1, keepdims=True), 0.0)
                        dcs_neg_t = dcs_neg_t + jnp.where(sub == h, _colsum(pm), 0.0)
                        dcb_m = dcb_m + w * seg
                        dxp = dxp + _dot_tn(m.astype(BF16), dym)
                    dx_scr[:, cols] = dxp
                dcbb = dcb_m.astype(BF16)
                dxa_scr[rows, c_cols] = dcg + _dot(dcbb, bg)
                dxa_scr[rows, b_cols] = dbg + _dot_tn(dcbb, cg)
            decay_th = jnp.exp(cs[CHUNK - 1:CHUNK, :] - cs)
            rd = _headsum(jnp.concatenate(rin, axis=1), e_mat) * decay_th
            dcs = dcs - dcs_neg_t.T + _headsum(jnp.concatenate(qoff, axis=1), e_mat) - rd
            gh = _headsum(jnp.broadcast_to(jnp.concatenate(ghrow, axis=1), (SUBLANES, SSD_WIDTH)), e_mat)[0:1, :]
            dcs = dcs + jnp.where(sub == CHUNK - 1, _colsum(rd) + gh, 0.0)
            da = _hdot(triu, dcs)
            dx_all = dx_scr[...]
            xs = xa_scr[rows, 0:SSD_WIDTH]
            dt_c = _softplus(dt_ref[rows, :] + dtb_ref[...])
            ddt = da * a_neg + _headsum(dx_all * xs, e_mat)
            dalog_ref[...] += _colsum(da * dt_c) * a_neg
            ddtraw = ddt * _sigmoid(dt_ref[rows, :] + dtb_ref[...])
            dproj_ref[rows, COL_DT:COL_DT + LANES] = ddtraw
            ddtb_ref[...] += _colsum(ddtraw)
            dxa_scr[rows, 0:SSD_WIDTH] = dx_all * dte_scr[rows, :] + dyc * dexp_ref[...]
            return carry_

        lax.fori_loop(0, nch, chunk, 0, unroll=4)

        xcv = xc_ref[...]
        sgc = _sigmoid(xcv)
        dxc = dxa_scr[...] * (sgc * (1.0 + xcv * (1.0 - sgc)))
        dcb_ref[...] += _colsum(dxc)
        dbuf[0:ts, :] = dxc
        dbuf[ts:ts + SUBLANES, :] = carry[...]
        cwv = cw_ref[...]
        xbc = xbc_ref[...]
        dxbc = jnp.zeros((ts, SSD_XBC), F32)
        dcw_rows = []
        for k in range(SSD_CONV):
            back = SSD_CONV - 1 - k
            shifted = dbuf[back:back + ts, :] if back else dxc
            dcw_rows.append(_colsum(shifted * xbc))
            dxbc = dxbc + cwv[k:k + 1, :] * shifted
        dcw_ref[...] += jnp.concatenate(dcw_rows, axis=0)
        dproj_ref[:, COL_XBC:COL_XBC + SSD_XBC] = dxbc
        carry[...] = dxc[0:SUBLANES, :]
        if ex is not None:
            pl.when(i == nt - 1)(ex_wait)

    rev = lambda i: nt - 1 - i
    full = lambda shape: pl.BlockSpec(shape, lambda i: tuple(0 for _ in shape))
    outs = pl.pallas_call(
        body, name=name, grid=(nt,),
        in_specs=[pl.BlockSpec((ts, SSD_WIDTH), lambda i: (rev(i), 0)),
                  pl.BlockSpec((ts, SSD_XBC), lambda i: (rev(i), COL_XBC // SSD_XBC)),
                  pl.BlockSpec((ts, SSD_XBC), lambda i: (rev(i), 0)),
                  pl.BlockSpec((ts, SSD_WIDTH), lambda i: (rev(i), COL_Z // SSD_WIDTH)),
                  pl.BlockSpec((ts, LANES), lambda i: (rev(i), COL_DT // LANES)),
                  pl.BlockSpec((ts, SSD_WIDTH), lambda i: (rev(i), 0)),
                  pl.BlockSpec((nch, N_PAIRS, SSD_STATE, LANES), lambda i: (rev(i), 0, 0, 0)),
                  pl.BlockSpec((ts, POOL_WIDTH), lambda i: (rev(i), 0)),
                  full((SSD_CONV, SSD_XBC)), full((1, LANES)), full((1, LANES)),
                  full((1, SSD_WIDTH)), full((1, SSD_WIDTH)), full((CHUNK, CHUNK)), full((CHUNK, CHUNK)),
                  full((LANES, SSD_WIDTH))] + ex_in_specs,
        out_specs=[pl.BlockSpec((ts, N_PROJ), lambda i: (rev(i), 0)),
                   full((SSD_CONV, SSD_XBC)), full((1, SSD_XBC)), full((1, LANES)), full((1, LANES)),
                   full((1, LANES)), full((1, SSD_WIDTH))] + ex_out_specs,
        out_shape=[jax.ShapeDtypeStruct((t, N_PROJ), F32),
                   jax.ShapeDtypeStruct((SSD_CONV, SSD_XBC), F32), jax.ShapeDtypeStruct((1, SSD_XBC), F32),
                   jax.ShapeDtypeStruct((1, LANES), F32), jax.ShapeDtypeStruct((1, LANES), F32),
                   jax.ShapeDtypeStruct((1, LANES), F32), jax.ShapeDtypeStruct((1, SSD_WIDTH), F32)] + ex_out_shape,
        scratch_shapes=[pltpu.VMEM((ts, SSD_XBC), F32), pltpu.VMEM((ts, LANES), F32),
                        pltpu.VMEM((ts, SSD_WIDTH), F32), pltpu.VMEM((ts, SSD_WIDTH), F32),
                        pltpu.VMEM((ts, SSD_WIDTH), F32), pltpu.VMEM((ts, SSD_XBC), F32),
                        pltpu.VMEM((CHUNK, SSD_WIDTH), F32), pltpu.VMEM((ts + SUBLANES, SSD_XBC), F32),
                        pltpu.VMEM((SUBLANES, SSD_XBC), F32), pltpu.VMEM((N_PAIRS, SSD_STATE, LANES), F32)]
        + ex_scratch,
        input_output_aliases={} if ex is None else ex.aliases(16, 7),
        compiler_params=_params("arbitrary"))(
            dymix, proj, xc, proj, proj, ypre, states, du, cw, dtb, alog, dexp, ng, tril, triu, e, *ex_args)
    return outs[:7], outs[7:]


def _pooled(ubuf, u, pos, tt):
    out = []
    for gi, w in enumerate(POOL_WINDOWS):
        cols = slice(gi * POOL_GROUP, (gi + 1) * POOL_GROUP)
        acc = u[:, cols]
        for j in range(1, w):
            acc = acc + ubuf[POOL_HALO - j:POOL_HALO - j + tt, cols]
        out.append(acc / jnp.minimum(pos, float(w)) - u[:, cols])
    return out


def _mix_out(h, yssd, proj, pool_w, pool_scale, w_out, g_next, *, tt, name):
    t = h.shape[0]
    tt = _tile(t, tt)
    hb = tt // POOL_HALO

    def body(h_ref, ys_ref, u_ref, uh_ref, pw_ref, sc_ref, wo_ref, gn_ref, o_ref, ym_ref, n_ref, ubuf):
        i = pl.program_id(0)
        ubuf[0:POOL_HALO, :] = jnp.where(i == 0, 0.0, uh_ref[...])
        u = u_ref[...]
        ubuf[POOL_HALO:POOL_HALO + tt, :] = u
        pos = (i * tt + 1 + lax.broadcasted_iota(jnp.int32, (tt, 1), 0)).astype(F32)
        sc = sc_ref[...]
        parts = [ys_ref[...]]
        for gi, pooled in enumerate(_pooled(ubuf, u, pos, tt)):
            cols = slice(gi * POOL_GROUP, (gi + 1) * POOL_GROUP)
            parts.append(_dot(pooled.astype(BF16), pw_ref[gi]) * sc[:, cols])
        ymix = jnp.concatenate(parts, axis=1).astype(BF16)
        ym_ref[...] = ymix
        h2 = h_ref[...] + _dot(ymix, wo_ref[...])
        o_ref[...] = h2
        n_ref[...] = (h2 * _rms_r(h2) * gn_ref[...]).astype(BF16)

    full = lambda shape: pl.BlockSpec(shape, lambda i: tuple(0 for _ in shape))
    return pl.pallas_call(
        body, name=name, grid=(t // tt,),
        in_specs=[pl.BlockSpec((tt, D_MODEL), lambda i: (i, 0)), pl.BlockSpec((tt, SSD_WIDTH), lambda i: (i, 0)),
                  pl.BlockSpec((tt, POOL_WIDTH), lambda i: (i, COL_U // POOL_WIDTH)),
                  pl.BlockSpec((POOL_HALO, POOL_WIDTH), lambda i: (jnp.maximum(i * hb - 1, 0), COL_U // POOL_WIDTH)),
                  full((len(POOL_WINDOWS), POOL_GROUP, POOL_GROUP)), full((1, POOL_WIDTH)),
                  full((D_MODEL, D_MODEL)), full((1, D_MODEL))],
        out_specs=[pl.BlockSpec((tt, D_MODEL), lambda i: (i, 0))] * 3,
        out_shape=[jax.ShapeDtypeStruct((t, D_MODEL), F32), jax.ShapeDtypeStruct((t, D_MODEL), BF16),
                   jax.ShapeDtypeStruct((t, D_MODEL), BF16)],
        scratch_shapes=[pltpu.VMEM((POOL_HALO + tt, POOL_WIDTH), F32)],
        compiler_params=_params("arbitrary"))(h, yssd, proj, proj, pool_w, pool_scale, w_out, g_next)


def _out_bwd(dh, ymix, w_out_t, *, tt, name):
    t = dh.shape[0]
    tt = _tile(t, tt)

    def body(dh_ref, ym_ref, wt_ref, dym_ref, dw_ref):
        @pl.when(pl.program_id(0) == 0)
        def _():
            dw_ref[...] = jnp.zeros_like(dw_ref)

        dhb = dh_ref[...].astype(BF16)
        dym_ref[...] = _dot(dhb, wt_ref[...])
        dw_ref[...] += _dot_tn(ym_ref[...], dhb)

    return pl.pallas_call(
        body, name=name, grid=(t // tt,),
        in_specs=[pl.BlockSpec((tt, D_MODEL), lambda i: (i, 0)), pl.BlockSpec((tt, D_MODEL), lambda i: (i, 0)),
                  pl.BlockSpec((D_MODEL, D_MODEL), lambda i: (0, 0))],
        out_specs=[pl.BlockSpec((tt, D_MODEL), lambda i: (i, 0)), pl.BlockSpec((D_MODEL, D_MODEL), lambda i: (0, 0))],
        out_shape=[jax.ShapeDtypeStruct((t, D_MODEL), F32), jax.ShapeDtypeStruct((D_MODEL, D_MODEL), F32)],
        compiler_params=_params("arbitrary"))(dh, ymix, w_out_t)


def _pool_bwd(dymix, proj, pool_w, pool_w_t, pool_scale, *, tt, name):
    t = proj.shape[0]
    tt = _tile(t, tt)
    hb = tt // POOL_HALO
    nt = t // tt
    ng = len(POOL_WINDOWS)

    def body(dy_ref, dyh_ref, u_ref, uh_ref, pw_ref, pwt_ref, sc_ref, du_ref, dpw_ref, dsc_ref, ubuf, dbuf):
        i = pl.program_id(0)

        @pl.when(i == 0)
        def _():
            dpw_ref[...] = jnp.zeros_like(dpw_ref)
            dsc_ref[...] = jnp.zeros_like(dsc_ref)

        ubuf[0:POOL_HALO, :] = jnp.where(i == 0, 0.0, uh_ref[...])
        u = u_ref[...]
        ubuf[POOL_HALO:POOL_HALO + tt, :] = u
        pos = (i * tt + 1 + lax.broadcasted_iota(jnp.int32, (tt, 1), 0)).astype(F32)
        sc = sc_ref[...]
        dy = dy_ref[...]
        dyh = jnp.where(i == nt - 1, 0.0, dyh_ref[...])
        dsc_parts, du_parts = [], []
        for gi, pooled in enumerate(_pooled(ubuf, u, pos, tt)):
            w = POOL_WINDOWS[gi]
            cols = slice(gi * POOL_GROUP, (gi + 1) * POOL_GROUP)
            pb = pooled.astype(BF16)
            dsc_parts.append(_colsum(dy[:, cols] * _dot(pb, pw_ref[gi])))
            dmx = (dy[:, cols] * sc[:, cols]).astype(BF16)
            dpw_ref[gi] += _dot_tn(pb, dmx)
            dpool = _dot(dmx, pwt_ref[gi])
            dpool_h = _dot((dyh[:, cols] * sc[:, cols]).astype(BF16), pwt_ref[gi])
            dbuf[0:tt, cols] = dpool / jnp.minimum(pos, float(w))
            dbuf[tt:tt + POOL_HALO, cols] = dpool_h / float(w)
            acc = -dpool
            for j in range(w):
                acc = acc + dbuf[j:j + tt, cols]
            du_parts.append(acc)
        du_ref[...] = jnp.concatenate(du_parts, axis=1)
        dsc_ref[...] += jnp.concatenate(dsc_parts, axis=1)

    full = lambda shape: pl.BlockSpec(shape, lambda i: tuple(0 for _ in shape))
    ucol = COL_U // POOL_WIDTH
    return pl.pallas_call(
        body, name=name, grid=(nt,),
        in_specs=[pl.BlockSpec((tt, POOL_WIDTH), lambda i: (i, 1)),
                  pl.BlockSpec((POOL_HALO, POOL_WIDTH), lambda i: (jnp.minimum((i + 1) * hb, t // POOL_HALO - 1), 1)),
                  pl.BlockSpec((tt, POOL_WIDTH), lambda i: (i, ucol)),
                  pl.BlockSpec((POOL_HALO, POOL_WIDTH), lambda i: (jnp.maximum(i * hb - 1, 0), ucol)),
                  full((ng, POOL_GROUP, POOL_GROUP)), full((ng, POOL_GROUP, POOL_GROUP)), full((1, POOL_WIDTH))],
        out_specs=[pl.BlockSpec((tt, POOL_WIDTH), lambda i: (i, 0)), full((ng, POOL_GROUP, POOL_GROUP)),
                   full((1, POOL_WIDTH))],
        out_shape=[jax.ShapeDtypeStruct((t, POOL_WIDTH), F32), jax.ShapeDtypeStruct((ng, POOL_GROUP, POOL_GROUP), F32),
                   jax.ShapeDtypeStruct((1, POOL_WIDTH), F32)],
        scratch_shapes=[pltpu.VMEM((POOL_HALO + tt, POOL_WIDTH), F32), pltpu.VMEM((tt + POOL_HALO, POOL_WIDTH), F32)],
        compiler_params=_params("arbitrary"))(dymix, dymix, proj, proj, pool_w, pool_w_t, pool_scale)


def _in_bwd(dproj, h, g, w_in_t, dh, *, tt, name):
    t = h.shape[0]
    tt = _tile(t, tt)

    def body(dp_ref, h_ref, g_ref, wt_ref, dh_ref, o_ref, dw_ref, dg_ref):
        @pl.when(pl.program_id(0) == 0)
        def _():
            dw_ref[...] = jnp.zeros_like(dw_ref)
            dg_ref[...] = jnp.zeros_like(dg_ref)

        x = h_ref[...]
        r = _rms_r(x)
        gv = g_ref[...]
        dpb = dp_ref[...].astype(BF16)
        dw_ref[...] += _dot_tn((x * r * gv).astype(BF16), dpb)
        dx, dg = _rms_bwd(x, r, gv, _dot(dpb, wt_ref[...]))
        o_ref[...] = dh_ref[...] + dx
        dg_ref[...] += dg

    full = lambda shape: pl.BlockSpec(shape, lambda i: tuple(0 for _ in shape))
    row = lambda n: pl.BlockSpec((tt, n), lambda i: (i, 0))
    return pl.pallas_call(
        body, name=name, grid=(t // tt,),
        in_specs=[row(N_PROJ), row(D_MODEL), full((1, D_MODEL)), full((N_PROJ, D_MODEL)), row(D_MODEL)],
        out_specs=[row(D_MODEL), full((D_MODEL, N_PROJ)), full((1, D_MODEL))],
        out_shape=[jax.ShapeDtypeStruct((t, D_MODEL), F32), jax.ShapeDtypeStruct((D_MODEL, N_PROJ), F32),
                   jax.ShapeDtypeStruct((1, D_MODEL), F32)],
        compiler_params=_params("arbitrary"))(dproj, h, g, w_in_t, dh)


FFN_COLS = 256
N_SLABS = D_FF // FFN_COLS
N_SLAB_BUFS = 4


def _ffn_fwd(h, n2, w_up, cw, cb, w_down, *, tt, name, ex=None):
    t = h.shape[0]
    tt = _tile(t, tt)
    nt = t // tt
    ex_args, ex_in_specs, ex_out_shape, ex_out_specs, ex_scratch, ex_counts = _ex_parts(ex)

    def body(*refs):
        ((h_ref, n2_ref, wu_ref, cw_ref, cb_ref, wd_ref), ex_in, (o_ref, act_ref, pre_ref, up_ref), ex_out,
         (slab, halo), ex_sems) = _split_refs(refs, (6, ex_counts[0], 4, ex_counts[1], 2, ex_counts[2]))
        i = pl.program_id(0)
        if ex is not None:
            ex_start, ex_wait = ex.ops(ex_in, ex_out, ex_sems)
            pl.when(i == 0)(ex_start)

        @pl.when(i == 0)
        def _():
            halo[...] = jnp.zeros_like(halo)

        n2v = n2_ref[...]

        def slab_cols(s):
            return slice(s * FFN_COLS, (s + 1) * FFN_COLS), slice(D_FF + s * FFN_COLS, D_FF + (s + 1) * FFN_COLS)

        def project(s):
            return [_dot(n2v, wu_ref[:, cols]) for cols in slab_cols(s)]

        def conv(u, cols, buf_id):
            up_ref[:, cols] = u.astype(BF16)
            sb = slab.at[buf_id]
            sb[0:SUBLANES, :] = halo[:, cols]
            sb[SUBLANES:SUBLANES + tt, :] = u
            halo[:, cols] = u[tt - SUBLANES:tt, :]
            acc = cb_ref[:, cols] + cw_ref[FFN_CONV - 1:FFN_CONV, cols] * u
            for k in range(FFN_CONV - 1):
                off = SUBLANES - (FFN_CONV - 1) + k
                acc = acc + cw_ref[k:k + 1, cols] * sb[off:off + tt, :]
            pre_ref[:, cols] = acc
            return acc

        out = h_ref[...]
        ahead = project(0)
        for s in range(N_SLABS):
            (ug, uv), (gcols, vcols) = ahead, slab_cols(s)
            if s + 1 < N_SLABS:
                ahead = project(s + 1)
            gate = conv(ug, gcols, (2 * s) % N_SLAB_BUFS)
            val = conv(uv, vcols, (2 * s + 1) % N_SLAB_BUFS)
            act = (_gelu(gate) * val).astype(BF16)
            act_ref[:, s * FFN_COLS:(s + 1) * FFN_COLS] = act
            out = out + _dot(act, wd_ref[s * FFN_COLS:(s + 1) * FFN_COLS, :])
        o_ref[...] = out
        if ex is not None:
            pl.when(i == nt - 1)(ex_wait)

    full = lambda shape: pl.BlockSpec(shape, lambda i: tuple(0 for _ in shape))
    row = lambda n: pl.BlockSpec((tt, n), lambda i: (i, 0))
    outs = pl.pallas_call(
        body, name=name, grid=(nt,),
        in_specs=[row(D_MODEL), row(D_MODEL), full((D_MODEL, D_UP)), full((FFN_CONV, D_UP)), full((1, D_UP)),
                  full((D_FF, D_MODEL))] + ex_in_specs,
        out_specs=[row(D_MODEL), row(D_FF), row(D_UP), row(D_UP)] + ex_out_specs,
        out_shape=[jax.ShapeDtypeStruct((t, D_MODEL), F32), jax.ShapeDtypeStruct((t, D_FF), BF16),
                   jax.ShapeDtypeStruct((t, D_UP), F32), jax.ShapeDtypeStruct((t, D_UP), BF16)] + ex_out_shape,
        scratch_shapes=[pltpu.VMEM((N_SLAB_BUFS, SUBLANES + tt, FFN_COLS), F32), pltpu.VMEM((SUBLANES, D_UP), F32)]
        + ex_scratch,
        input_output_aliases={} if ex is None else ex.aliases(6, 4),
        compiler_params=_params("arbitrary"))(h, n2, w_up, cw, cb, w_down, *ex_args)
    return outs[0], outs[1], outs[2], outs[3], outs[4:]


def _ffn_bwd(dh, up, pre, h2, g2, w_down_t, w_up_t, cw, *, tt, name, ex=None):
    t = dh.shape[0]
    tt = _tile(t, tt)
    nt = t // tt
    ex_args, ex_in_specs, ex_out_shape, ex_out_specs, ex_scratch, ex_counts = _ex_parts(ex)

    def body(*refs):
        ((dh_ref, up_ref, pre_ref, h2_ref, g2_ref, wdt_ref, wut_ref, cw_ref), ex_in,
         (o_ref, dup_ref, dcw_ref, dcb_ref, dg_ref), ex_out, (slab, carry), ex_sems) = _split_refs(
            refs, (8, ex_counts[0], 5, ex_counts[1], 2, ex_counts[2]))
        i = pl.program_id(0)
        if ex is not None:
            ex_start, ex_wait = ex.ops(ex_in, ex_out, ex_sems)
            pl.when(i == 0)(ex_start)

        @pl.when(i == 0)
        def _():
            for ref in (dcw_ref, dcb_ref, dg_ref, carry):
                ref[...] = jnp.zeros_like(ref)

        dhv = dh_ref[...]
        dhb = dhv.astype(BF16)

        def slab_cols(s):
            return slice(s * FFN_COLS, (s + 1) * FFN_COLS), slice(D_FF + s * FFN_COLS, D_FF + (s + 1) * FFN_COLS)

        def d_act(s):
            return _dot(dhb, wdt_ref[:, slab_cols(s)[0]])

        def through_conv(dp, cols, buf_id):
            sb = slab.at[buf_id]
            sb[0:tt, :] = dp
            sb[tt:tt + SUBLANES, :] = carry[:, cols]
            carry[:, cols] = dp[0:SUBLANES, :]
            shifted = [sb[FFN_CONV - 1 - k:FFN_CONV - 1 - k + tt, :] for k in range(FFN_CONV - 1)] + [dp]
            x = up_ref[:, cols].astype(F32)
            dup = cw_ref[0:1, cols] * shifted[0]
            for k in range(1, FFN_CONV):
                dup = dup + cw_ref[k:k + 1, cols] * shifted[k]
            dcb_ref[:, cols] += _colsum(dp)
            dcw_ref[:, cols] += jnp.concatenate([_colsum(sh * x) for sh in shifted], axis=0)
            dupb = dup.astype(BF16)
            dup_ref[:, cols] = dupb
            return _dot(dupb, wut_ref[cols, :])

        dn = jnp.zeros((tt, D_MODEL), F32)
        ahead = d_act(0)
        for s in range(N_SLABS):
            da, (gcols, vcols) = ahead, slab_cols(s)
            if s + 1 < N_SLABS:
                ahead = d_act(s + 1)
            gate, val = pre_ref[:, gcols], pre_ref[:, vcols]
            dn = dn + through_conv(da * val * _gelu_grad(gate), gcols, (2 * s) % N_SLAB_BUFS)
            dn = dn + through_conv(da * _gelu(gate), vcols, (2 * s + 1) % N_SLAB_BUFS)
        xv = h2_ref[...]
        dx, dg = _rms_bwd(xv, _rms_r(xv), g2_ref[...], dn)
        o_ref[...] = dhv + dx
        dg_ref[...] += dg
        if ex is not None:
            pl.when(i == nt - 1)(ex_wait)

    rev = lambda i: nt - 1 - i
    full = lambda shape: pl.BlockSpec(shape, lambda i: tuple(0 for _ in shape))
    row = lambda n: pl.BlockSpec((tt, n), lambda i: (rev(i), 0))
    outs = pl.pallas_call(
        body, name=name, grid=(nt,),
        in_specs=[row(D_MODEL), row(D_UP), row(D_UP), row(D_MODEL), full((1, D_MODEL)), full((D_MODEL, D_FF)),
                  full((D_UP, D_MODEL)), full((FFN_CONV, D_UP))] + ex_in_specs,
        out_specs=[row(D_MODEL), row(D_UP), full((FFN_CONV, D_UP)), full((1, D_UP)), full((1, D_MODEL))]
        + ex_out_specs,
        out_shape=[jax.ShapeDtypeStruct((t, D_MODEL), F32), jax.ShapeDtypeStruct((t, D_UP), BF16),
                   jax.ShapeDtypeStruct((FFN_CONV, D_UP), F32), jax.ShapeDtypeStruct((1, D_UP), F32),
                   jax.ShapeDtypeStruct((1, D_MODEL), F32)] + ex_out_shape,
        scratch_shapes=[pltpu.VMEM((N_SLAB_BUFS, tt + SUBLANES, FFN_COLS), F32), pltpu.VMEM((SUBLANES, D_UP), F32)]
        + ex_scratch,
        input_output_aliases={} if ex is None else ex.aliases(8, 5),
        compiler_params=_params("arbitrary"))(dh, up, pre, h2, g2, w_down_t, w_up_t, cw, *ex_args)
    return outs[0], outs[1], outs[2], outs[3], outs[4], outs[5:]


def _ple_fwd(h, p, g, w_gate, w_proj, *, tt, name):
    t = h.shape[0]
    tt = _tile(t, tt)

    def body(h_ref, p_ref, g_ref, wg_ref, wp_ref, o_ref):
        x = h_ref[...]
        n = (x * _rms_r(x) * g_ref[...]).astype(BF16)
        gate = _sigmoid(_dot(n, wg_ref[...]))
        o_ref[...] = x + _dot(p_ref[...].astype(BF16), wp_ref[...]) * gate

    full = lambda shape: pl.BlockSpec(shape, lambda i: tuple(0 for _ in shape))
    return pl.pallas_call(
        body, name=name, grid=(t // tt,),
        in_specs=[pl.BlockSpec((tt, D_MODEL), lambda i: (i, 0)), pl.BlockSpec((tt, D_PLE), lambda i: (i, 0)),
                  full((1, D_MODEL)), full((D_MODEL, D_MODEL)), full((D_PLE, D_MODEL))],
        out_specs=pl.BlockSpec((tt, D_MODEL), lambda i: (i, 0)),
        out_shape=jax.ShapeDtypeStruct((t, D_MODEL), F32),
        compiler_params=_params("arbitrary"))(h, p, g, w_gate, w_proj)


def _ple_bwd(dh, h, p, g, w_gate, w_gate_t, w_proj, *, tt, name):
    t = h.shape[0]
    tt = _tile(t, tt)

    def body(dh_ref, h_ref, p_ref, g_ref, wg_ref, wgt_ref, wp_ref, o_ref, dwg_ref, dwp_ref, dg_ref):
        @pl.when(pl.program_id(0) == 0)
        def _():
            dwg_ref[...] = jnp.zeros_like(dwg_ref)
            dwp_ref[...] = jnp.zeros_like(dwp_ref)
            dg_ref[...] = jnp.zeros_like(dg_ref)

        x = h_ref[...]
        r = _rms_r(x)
        gv = g_ref[...]
        n = (x * r * gv).astype(BF16)
        gate = _sigmoid(_dot(n, wg_ref[...]))
        pb = p_ref[...].astype(BF16)
        pe = _dot(pb, wp_ref[...])
        dhv = dh_ref[...]
        dwp_ref[...] += _dot_tn(pb, (dhv * gate).astype(BF16))
        ds = (dhv * pe * gate * (1.0 - gate)).astype(BF16)
        dwg_ref[...] += _dot_tn(n, ds)
        dx, dg = _rms_bwd(x, r, gv, _dot(ds, wgt_ref[...]))
        o_ref[...] = dhv + dx
        dg_ref[...] += dg

    full = lambda shape: pl.BlockSpec(shape, lambda i: tuple(0 for _ in shape))
    row = lambda n: pl.BlockSpec((tt, n), lambda i: (i, 0))
    return pl.pallas_call(
        body, name=name, grid=(t // tt,),
        in_specs=[row(D_MODEL), row(D_MODEL), row(D_PLE), full((1, D_MODEL)), full((D_MODEL, D_MODEL)),
                  full((D_MODEL, D_MODEL)), full((D_PLE, D_MODEL))],
        out_specs=[row(D_MODEL), full((D_MODEL, D_MODEL)), full((D_PLE, D_MODEL)), full((1, D_MODEL))],
        out_shape=[jax.ShapeDtypeStruct((t, D_MODEL), F32), jax.ShapeDtypeStruct((D_MODEL, D_MODEL), F32),
                   jax.ShapeDtypeStruct((D_PLE, D_MODEL), F32), jax.ShapeDtypeStruct((1, D_MODEL), F32)],
        compiler_params=_params("arbitrary"))(dh, h, p, g, w_gate, w_gate_t, w_proj)


def _loss_head(h, g, target, *, tt, name):
    t = h.shape[0]
    tt = _tile(t, tt)

    def body(h_ref, g_ref, tg_ref, dh_ref, loss_ref, dg_ref):
        @pl.when(pl.program_id(0) == 0)
        def _():
            loss_ref[...] = jnp.zeros_like(loss_ref)
            dg_ref[...] = jnp.zeros_like(dg_ref)

        x = h_ref[...]
        r = _rms_r(x)
        gv = g_ref[...]
        diff = x * r * gv - tg_ref[...]
        loss_ref[...] += 0.5 * jnp.sum(jnp.mean(diff * diff, axis=-1, keepdims=True), axis=0, keepdims=True)
        dx, dg = _rms_bwd(x, r, gv, diff * (1.0 / D_MODEL))
        dh_ref[...] = dx
        dg_ref[...] += dg

    return pl.pallas_call(
        body, name=name, grid=(t // tt,),
        in_specs=[pl.BlockSpec((tt, D_MODEL), lambda i: (i, 0)), pl.BlockSpec((1, D_MODEL), lambda i: (0, 0)),
                  pl.BlockSpec((tt, D_MODEL), lambda i: (i, 0))],
        out_specs=[pl.BlockSpec((tt, D_MODEL), lambda i: (i, 0)), pl.BlockSpec((SUBLANES, LANES), lambda i: (0, 0)),
                   pl.BlockSpec((1, D_MODEL), lambda i: (0, 0))],
        out_shape=[jax.ShapeDtypeStruct((t, D_MODEL), F32), jax.ShapeDtypeStruct((SUBLANES, LANES), F32),
                   jax.ShapeDtypeStruct((1, D_MODEL), F32)],
        compiler_params=_params("arbitrary"))(h, g, target)


ADAM_BLOCK_BYTES = 4 * 1024 * 1024


def _adam_rows(rows, cols):
    lanes = -(-cols // LANES) * LANES
    for cand in (1024, 512, 256, 128, 64, 32, 16, 8):
        if rows % cand == 0 and N_DEV * cand * lanes * 4 <= ADAM_BLOCK_BYTES:
            return cand
    return rows


def _sum_adamw(parts, w, m, v, *, name):
    nl, rows, cols = w.shape
    tr = _adam_rows(rows, cols)

    def body(p_ref, w_ref, m_ref, v_ref, g_ref, d_ref, nm_ref, nv_ref):
        g = p_ref[0]
        for k in range(1, N_DEV):
            g = g + p_ref[k]
        g_ref[...] = g
        nm = ADAM_B1 * m_ref[...] + (1.0 - ADAM_B1) * g
        nv = ADAM_B2 * v_ref[...] + (1.0 - ADAM_B2) * (g * g)
        m_hat = nm / (1.0 - ADAM_B1 ** ADAM_STEP)
        v_hat = nv / (1.0 - ADAM_B2 ** ADAM_STEP)
        d_ref[...] = -ADAM_LR * (m_hat / (jnp.sqrt(v_hat) + ADAM_EPS) + ADAM_WD * w_ref[...])
        nm_ref[...] = nm
        nv_ref[...] = nv

    blk = pl.BlockSpec((None, tr, cols), lambda l, r: (l, r, 0))
    return pl.pallas_call(
        body, name=name, grid=(nl, rows // tr),
        in_specs=[pl.BlockSpec((None, N_DEV, tr, cols), lambda l, r: (l, 0, r, 0)), blk, blk, blk],
        out_specs=[blk, blk, blk, blk],
        out_shape=[jax.ShapeDtypeStruct((nl, rows, cols), F32)] * 4,
        compiler_params=_params("arbitrary", "arbitrary"))(parts, w, m, v)


PACK_ROWS = 512


def _pack(arrays):
    flat = jnp.concatenate([a.astype(F32).reshape(-1) for a in arrays])
    pad = (-flat.shape[0]) % (PACK_ROWS * LANES)
    return jnp.pad(flat, (0, pad)).reshape(-1, LANES)


def _unpack(buf, shapes):
    flat = buf.reshape(-1)
    out, off = [], 0
    for s in shapes:
        n = math.prod(s)
        out.append(flat[off:off + n].reshape(s))
        off += n
    return out


def _to_proj_cols(w):
    z, xbc, dtc, u = jnp.split(w, [SSD_WIDTH, SSD_WIDTH + SSD_XBC, SSD_WIDTH + SSD_XBC + SSD_HEADS], axis=-1)
    pad = jnp.zeros(w.shape[:-1] + (LANES - SSD_HEADS,), w.dtype)
    return jnp.concatenate([xbc, z, u, dtc, pad], axis=-1)


def _from_proj_cols(w):
    xbc, z, u, dtc = (w[..., COL_XBC:COL_Z], w[..., COL_Z:COL_U], w[..., COL_U:COL_DT],
                      w[..., COL_DT:COL_DT + SSD_HEADS])
    return jnp.concatenate([z, xbc, dtc, u], axis=-1)


def _pad_heads(v):
    return jnp.pad(v, (0, LANES - SSD_HEADS)).reshape(1, LANES)


def _cat_cols(g):
    return jnp.transpose(g, (1, 0, 2)).reshape(g.shape[1], N_DEV * g.shape[2])


def _split_cols(w):
    r, c = w.shape
    return jnp.transpose(w.reshape(r, N_DEV, c // N_DEV), (1, 0, 2))


def _cat_rows(g):
    return g.reshape(N_DEV * g.shape[1], g.shape[2])


def _split_rows(w):
    return w.reshape(N_DEV, w.shape[0] // N_DEV, w.shape[1])


SHARDED = ("w_in", "w_out", "ffn_w_up", "ffn_w_down", "ple_w_gate", "ple_w_proj", "ssd_conv_w", "ffn_conv_w")
COL_SHARDED = ("w_in", "ffn_w_up", "ple_w_proj", "ssd_conv_w", "ffn_conv_w")
MATMUL_W = SHARDED[:6]
REPLICATED = ("mix_norm_g", "ssd_conv_b", "ssd_dt_bias", "ssd_a_log", "ssd_d", "ssd_norm_g", "pool_w", "pool_scale",
              "ffn_norm_g", "ffn_conv_b", "ple_norm_g", "final_norm_g")
WEIGHTS = ("mix_norm_g", "w_in", "ssd_conv_w", "ssd_conv_b", "ssd_dt_bias", "ssd_a_log", "ssd_d", "ssd_norm_g",
           "pool_w", "pool_scale", "w_out", "ffn_norm_g", "ffn_w_up", "ffn_conv_w", "ffn_conv_b", "ffn_w_down",
           "ple_norm_g", "ple_w_gate", "ple_w_proj", "final_norm_g")


FIRST_USED = ("w_in", "ssd_conv_w")
LATER_USED = tuple(k for k in SHARDED if k not in FIRST_USED)
LAST_MADE = ("w_out", "ssd_conv_w", "w_in")
EARLY_MADE = tuple(k for k in SHARDED if k not in LAST_MADE)
TRANSPOSED = ("w_in", "w_out", "ffn_w_up", "ffn_w_down", "ple_w_gate")


def _pick(names, per_sharded):
    return [per_sharded[SHARDED.index(k)] for k in names]


def _put(names, per_sharded, values):
    out = list(per_sharded)
    for k, val in zip(names, values):
        out[SHARDED.index(k)] = val
    return out


def _assemble(names, gathered):
    full = {}
    for k, g in zip(names, gathered):
        full[k] = _cat_cols(g) if k in COL_SHARDED else _cat_rows(g)
        if k == "w_in":
            full[k] = _to_proj_cols(full[k])
        if k in TRANSPOSED:
            full[k + "_t"] = full[k].T
    return full


def _grad_shards(names, grads):
    out = []
    for k in names:
        g = _from_proj_cols(grads[k]) if k == "w_in" else grads[k]
        out.append(_split_cols(g) if k in COL_SHARDED else _split_rows(g))
    return out


def _layer_fwd(i, h1, p_i, lw, rep, consts, ex_own, ex):
    tril, e_mat = consts
    row = lambda v: v.reshape(1, -1)
    dtb, alog = _pad_heads(rep["ssd_dt_bias"]), _pad_heads(rep["ssd_a_log"])
    dexp = row(jnp.repeat(rep["ssd_d"], SSD_HEAD_DIM))
    pw = rep["pool_w"].astype(BF16)
    proj = _norm_matmul(h1, lw["w_in"], row(rep["mix_norm_g"]), tt=512, tn=N_PROJ, name=f"in_proj_{i}")
    yssd, ypre, states, xc, own = _ssd_fwd(proj, lw["ssd_conv_w"], row(rep["ssd_conv_b"]), dtb, alog, dexp,
                                       row(rep["ssd_norm_g"]), tril, e_mat, ts=512, name=f"ssd_fwd_{i}", ex=ex_own)
    if ex_own is not None:
        lw = dict(lw, **_assemble(LATER_USED, own))
    h2, ymix, n2 = _mix_out(h1, yssd, proj, pw, row(rep["pool_scale"]), lw["w_out"], row(rep["ffn_norm_g"]), tt=512,
                            name=f"mix_out_{i}")
    h3, act, pre, up, gathered = _ffn_fwd(h2, n2, lw["ffn_w_up"], lw["ffn_conv_w"], row(rep["ffn_conv_b"]),
                                          lw["ffn_w_down"], tt=256, name=f"ffn_fwd_{i}", ex=ex)
    h4 = _ple_fwd(h3, p_i, row(rep["ple_norm_g"]), lw["ple_w_gate"], lw["ple_w_proj"], tt=512, name=f"ple_fwd_{i}")
    saved = dict(h1=h1, proj=proj, xc=xc, ypre=ypre, states=states, ymix=ymix, h2=h2, n2=n2, up=up, pre=pre, act=act, h3=h3,
                 dtb=dtb, alog=alog, dexp=dexp, pw=pw)
    return h4, saved, lw, gathered


def _layer_bwd(i, dh, p_i, lw, rep, s, consts, pending, parts, own_early):
    tril, triu, e_mat = consts
    ex = None if pending is None else _Exchange(pending, scatter=True, layer=i + 1, into=parts)
    row = lambda v: v.reshape(1, -1)
    g = {}
    dh, g["ple_w_gate"], g["ple_w_proj"], dg3 = _ple_bwd(dh, s["h3"], p_i, row(rep["ple_norm_g"]), lw["ple_w_gate"],
                                                         lw["ple_w_gate_t"], lw["ple_w_proj"], tt=512,
                                                         name=f"ple_bwd_{i}")
    g["ple_norm_g"] = dg3.reshape(-1)
    g["ffn_w_down"] = _matmul_tn(s["act"], dh, tm=D_FF // 2, tn=D_MODEL, tk=1024, name=f"dw_down_{i}")
    dh, dup, g["ffn_conv_w"], dcb, dg2, scattered = _ffn_bwd(
        dh, s["up"], s["pre"], s["h2"], row(rep["ffn_norm_g"]), lw["ffn_w_down_t"], lw["ffn_w_up_t"],
        lw["ffn_conv_w"], tt=256, name=f"ffn_bwd_{i}", ex=ex)
    if ex is not None:
        parts = scattered
    g["ffn_conv_b"], g["ffn_norm_g"] = dcb.reshape(-1), dg2.reshape(-1)
    g["ffn_w_up"] = _matmul_tn(s["n2"], dup, tm=D_MODEL, tn=D_UP // 4, tk=1024, name=f"dw_up_{i}")
    dymix, g["w_out"] = _out_bwd(dh, s["ymix"], lw["w_out_t"], tt=512, name=f"out_bwd_{i}")
    du, g["pool_w"], dsc = _pool_bwd(dymix, s["proj"], s["pw"], jnp.swapaxes(s["pw"], 1, 2), row(rep["pool_scale"]),
                                     tt=512, name=f"pool_bwd_{i}")
    g["pool_scale"] = dsc.reshape(-1)
    ex_own = None
    if own_early:
        ex_own = _Exchange(_grad_shards(EARLY_MADE, g), scatter=True, layer=i, into=_pick(EARLY_MADE, parts))
    (dproj, g["ssd_conv_w"], dcb, ddtb, dalog, dd, dng), own = _ssd_bwd(
        dymix, s["proj"], s["xc"], s["ypre"], s["states"], du, lw["ssd_conv_w"], s["dtb"], s["alog"], s["dexp"],
        row(rep["ssd_norm_g"]), tril, triu, e_mat, ts=512, name=f"ssd_bwd_{i}", ex=ex_own)
    if own_early:
        parts = _put(EARLY_MADE, parts, own)
    g["ssd_conv_b"], g["ssd_norm_g"] = dcb.reshape(-1), dng.reshape(-1)
    g["ssd_dt_bias"], g["ssd_a_log"], g["ssd_d"] = ddtb[0, :SSD_HEADS], dalog[0, :SSD_HEADS], dd[0, :SSD_HEADS]
    dh, g["w_in"], dg1 = _in_bwd(dproj, s["h1"], row(rep["mix_norm_g"]), lw["w_in_t"], dh, tt=256, name=f"in_bwd_{i}")
    g["mix_norm_g"] = dg1.reshape(-1)
    return dh, g, parts


def kernel(x, p, mix_norm_g, w_in, ssd_conv_w, ssd_conv_b, ssd_dt_bias, ssd_a_log, ssd_d, ssd_norm_g, pool_w, pool_scale, w_out, ffn_norm_g, ffn_w_up, ffn_conv_w, ffn_conv_b, ffn_w_down, ple_norm_g, ple_w_gate, ple_w_proj, final_norm_g, loss_target, m_mix_norm_g, m_w_in, m_ssd_conv_w, m_ssd_conv_b, m_ssd_dt_bias, m_ssd_a_log, m_ssd_d, m_ssd_norm_g, m_pool_w, m_pool_scale, m_w_out, m_ffn_norm_g, m_ffn_w_up, m_ffn_conv_w, m_ffn_conv_b, m_ffn_w_down, m_ple_norm_g, m_ple_w_gate, m_ple_w_proj, m_final_norm_g, v_mix_norm_g, v_w_in, v_ssd_conv_w, v_ssd_conv_b, v_ssd_dt_bias, v_ssd_a_log, v_ssd_d, v_ssd_norm_g, v_pool_w, v_pool_scale, v_w_out, v_ffn_norm_g, v_ffn_w_up, v_ffn_conv_w, v_ffn_conv_b, v_ffn_w_down, v_ple_norm_g, v_ple_w_gate, v_ple_w_proj, v_final_norm_g):
    w = dict(mix_norm_g=mix_norm_g, w_in=w_in, ssd_conv_w=ssd_conv_w, ssd_conv_b=ssd_conv_b, ssd_dt_bias=ssd_dt_bias,
             ssd_a_log=ssd_a_log, ssd_d=ssd_d, ssd_norm_g=ssd_norm_g, pool_w=pool_w, pool_scale=pool_scale, w_out=w_out,
             ffn_norm_g=ffn_norm_g, ffn_w_up=ffn_w_up, ffn_conv_w=ffn_conv_w, ffn_conv_b=ffn_conv_b,
             ffn_w_down=ffn_w_down, ple_norm_g=ple_norm_g, ple_w_gate=ple_w_gate, ple_w_proj=ple_w_proj,
             final_norm_g=final_norm_g)
    m = dict(mix_norm_g=m_mix_norm_g, w_in=m_w_in, ssd_conv_w=m_ssd_conv_w, ssd_conv_b=m_ssd_conv_b,
             ssd_dt_bias=m_ssd_dt_bias, ssd_a_log=m_ssd_a_log, ssd_d=m_ssd_d, ssd_norm_g=m_ssd_norm_g, pool_w=m_pool_w,
             pool_scale=m_pool_scale, w_out=m_w_out, ffn_norm_g=m_ffn_norm_g, ffn_w_up=m_ffn_w_up,
             ffn_conv_w=m_ffn_conv_w, ffn_conv_b=m_ffn_conv_b, ffn_w_down=m_ffn_w_down, ple_norm_g=m_ple_norm_g,
             ple_w_gate=m_ple_w_gate, ple_w_proj=m_ple_w_proj, final_norm_g=m_final_norm_g)
    v = dict(mix_norm_g=v_mix_norm_g, w_in=v_w_in, ssd_conv_w=v_ssd_conv_w, ssd_conv_b=v_ssd_conv_b,
             ssd_dt_bias=v_ssd_dt_bias, ssd_a_log=v_ssd_a_log, ssd_d=v_ssd_d, ssd_norm_g=v_ssd_norm_g, pool_w=v_pool_w,
             pool_scale=v_pool_scale, w_out=v_w_out, ffn_norm_g=v_ffn_norm_g, ffn_w_up=v_ffn_w_up,
             ffn_conv_w=v_ffn_conv_w, ffn_conv_b=v_ffn_conv_b, ffn_w_down=v_ffn_w_down, ple_norm_g=v_ple_norm_g,
             ple_w_gate=v_ple_w_gate, ple_w_proj=v_ple_w_proj, final_norm_g=v_final_norm_g)

    tril = jnp.tril(jnp.ones((CHUNK, CHUNK), BF16))
    triu = tril.T
    e_mat = (jnp.arange(SSD_WIDTH)[None, :] // SSD_HEAD_DIM == jnp.arange(LANES)[:, None]).astype(BF16)
    rep = [{k: w[k][i] for k in REPLICATED if k != "final_norm_g"} for i in range(DEPTH)]
    p_loc = p[:, 0]

    shards = [w[k].astype(BF16) if k in MATMUL_W else w[k] for k in SHARDED]
    lw = _assemble(FIRST_USED, _exchange_call(_Exchange(_pick(FIRST_USED, shards), scatter=False, layer=0),
                                              "gather_weights_0"))
    h, saved, layer_w = x[0], [], []
    for i in range(DEPTH):
        ex_own = _Exchange(_pick(LATER_USED, shards), scatter=False, layer=0) if i == 0 else None
        ex = _Exchange(shards, scatter=False, layer=i + 1) if i + 1 < DEPTH else None
        h, s, lw, gathered = _layer_fwd(i, h, p_loc[i], lw, rep[i], (tril, e_mat), ex_own, ex)
        saved.append(s)
        layer_w.append(lw)
        lw = _assemble(SHARDED, gathered)

    dh, loss_blk, dgf = _loss_head(h, final_norm_g.reshape(1, -1), loss_target[0], tt=512, name="loss_head")
    loss = lax.psum(loss_blk[0, 0], ("x", "y", "c"))

    rep_grads = [None] * DEPTH
    pending, parts = None, None
    for i in reversed(range(DEPTH)):
        dh, g, parts = _layer_bwd(i, dh, p_loc[i], layer_w[i], rep[i], saved[i], (tril, triu, e_mat), pending, parts,
                                  own_early=(i == 0))
        pending = _grad_shards(SHARDED, g) if i > 0 else _grad_shards(LAST_MADE, g)
        rep_grads[i] = g
    parts = _put(LAST_MADE, parts, _exchange_call(
        _Exchange(pending, scatter=True, layer=0, into=_pick(LAST_MADE, parts)), "scatter_grads_0"))

    out = {}
    for k, part in zip(SHARDED, parts):
        out[k] = _sum_adamw(part, w[k], m[k], v[k], name=f"adamw_{k}")

    rp_grads = [dgf.reshape(-1) if k == "final_norm_g" else jnp.stack([rep_grads[i][k] for i in range(DEPTH)])
                for k in REPLICATED]
    rp_shapes = [w[k].shape for k in REPLICATED]
    rp_parts = _exchange_call(_Exchange([_pack(rp_grads)[None]], scatter=False, layer=0), "gather_replicated_grads")[0]
    rp_out = _sum_adamw(rp_parts[None], *[_pack([d[k] for k in REPLICATED])[None] for d in (w, m, v)],
                        name="adamw_replicated")
    for j in range(4):
        for k, arr in zip(REPLICATED, _unpack(rp_out[j][0], rp_shapes)):
            out.setdefault(k, [None] * 4)[j] = arr
    results = [out[k][j] for j in range(4) for k in WEIGHTS]
    return (loss, dh[None], *results)
```

```python
import math

import jax
import jax.numpy as jnp
from jax import lax
from jax.experimental import pallas as pl
from jax.experimental.pallas import tpu as pltpu

F32 = jnp.float32
BF16 = jnp.bfloat16

N_DEV = 8
EPS = 1e-6
DEPTH = 4
D_MODEL = 1024
D_PLE = 256
SSD_WIDTH = 512
SSD_HEADS = 8
SSD_HEAD_DIM = 64
SSD_GROUPS = 2
SSD_STATE = 128
SSD_CONV = 4
CHUNK = 128
SSD_XBC = 1024
POOL_WINDOWS = (2, 4, 8, 16)
POOL_WIDTH = 512
POOL_GROUP = 128
POOL_HALO = 16
D_IN_PROJ = 2056
D_FF = 2816
D_UP = 2 * D_FF
FFN_CONV = 3
SUBLANES = 8
LANES = 128
N_PROJ = 2176
COL_XBC, COL_Z, COL_U, COL_DT = 0, 1024, 1536, 2048
N_PAIRS = SSD_HEADS // 2
ADAM_LR, ADAM_B1, ADAM_B2, ADAM_EPS, ADAM_WD, ADAM_STEP = 0.001, 0.9, 0.999, 1e-08, 0.01, 10
GELU_C = math.sqrt(2.0 / math.pi)
GELU_A = 0.044715
VMEM_LIMIT = 56 * 1024 * 1024

NT_DIMS = (((1,), (1,)), ((), ()))
TN_DIMS = (((0,), (0,)), ((), ()))


def _params(*sem):
    return pltpu.CompilerParams(dimension_semantics=sem, vmem_limit_bytes=VMEM_LIMIT)


def _dot(a, b):
    return jnp.dot(a, b, preferred_element_type=F32)


def _dot_nt(a, b):
    return lax.dot_general(a, b, NT_DIMS, preferred_element_type=F32)


def _dot_tn(a, b):
    return lax.dot_general(a, b, TN_DIMS, preferred_element_type=F32)


def _split3(a):
    hi = a.astype(BF16)
    r1 = a - hi.astype(F32)
    mid = r1.astype(BF16)
    return hi, mid, (r1 - mid.astype(F32)).astype(BF16)


def _hdot(a, b):
    if a.dtype == BF16:
        return sum(_dot(a, piece) for piece in _split3(b))
    return sum(_dot(piece, b) for piece in _split3(a))


def _headsum(q, e):
    return sum(_dot_nt(piece, e) for piece in _split3(q))


def _colsum(v):
    return jnp.sum(v, axis=0, keepdims=True)


def _sigmoid(v):
    return 1.0 / (1.0 + jnp.exp(-v))


def _softplus(v):
    e = jnp.exp(-jnp.abs(v))
    return jnp.maximum(v, 0.0) + jnp.where(e < 1e-4, e * (1.0 - 0.5 * e), jnp.log(1.0 + e))


def _rms_r(x):
    return lax.rsqrt(jnp.mean(x * x, axis=-1, keepdims=True) + EPS)


def _rms_bwd(x, r, g, dn):
    xhat = x * r
    gd = dn * g
    dx = r * (gd - xhat * jnp.mean(gd * xhat, axis=-1, keepdims=True))
    return dx, _colsum(dn * xhat)


def _gelu(v):
    return 0.5 * v * (1.0 + jnp.tanh(GELU_C * (v + GELU_A * v * v * v)))


def _gelu_grad(v):
    th = jnp.tanh(GELU_C * (v + GELU_A * v * v * v))
    return 0.5 * (1.0 + th) + 0.5 * v * (1.0 - th * th) * GELU_C * (1.0 + 3.0 * GELU_A * v * v)


def _tile(t, want):
    return min(t, want)


class _Exchange:
    def __init__(self, srcs, *, scatter, layer, into=None):
        self.srcs, self.scatter, self.layer = list(srcs), scatter, layer
        self.into = None if into is None else list(into)
        n = len(self.srcs)
        self.args = self.srcs + (self.into or [])
        self.in_specs = [pl.BlockSpec(memory_space=pl.ANY)] * len(self.args)
        if scatter:
            self.out_shape = [jax.ShapeDtypeStruct((DEPTH,) + s.shape, s.dtype) for s in self.srcs]
        else:
            self.out_shape = [jax.ShapeDtypeStruct((N_DEV,) + s.shape[1:], s.dtype) for s in self.srcs]
        self.out_specs = [pl.BlockSpec(memory_space=pl.ANY)] * n
        self.scratch = [pltpu.SemaphoreType.DMA((n, N_DEV - 1)), pltpu.SemaphoreType.DMA((n, N_DEV - 1)),
                        pltpu.SemaphoreType.DMA((n,))]

    def aliases(self, n_in_before, n_out_before):
        if self.into is None:
            return {}
        n = len(self.srcs)
        return {n_in_before + n + a: n_out_before + a for a in range(n)}

    def ops(self, in_refs, out_refs, sems):
        send_sems, recv_sems, local_sems = sems
        n = len(self.srcs)

        def copies():
            x, y, c = lax.axis_index("x"), lax.axis_index("y"), lax.axis_index("c")
            me = 4 * x + 2 * y + c

            def block(a, idx):
                return in_refs[a].at[idx] if self.scatter else in_refs[a].at[self.layer]

            def slot(a, idx):
                return out_refs[a].at[self.layer].at[idx] if self.scatter else out_refs[a].at[idx]

            local = [pltpu.make_async_copy(block(a, me), slot(a, me), local_sems.at[a]) for a in range(n)]
            sends, recvs = [], []
            for k in range(1, N_DEV):
                px = 1 - x if k & 4 else x
                py = 1 - y if k & 2 else y
                pc = 1 - c if k & 1 else c
                peer = 4 * px + 2 * py + pc
                for a in range(n):
                    kw = dict(send_sem=send_sems.at[a, k - 1], recv_sem=recv_sems.at[a, k - 1], device_id=(px, py, pc),
                              device_id_type=pl.DeviceIdType.MESH)
                    sends.append(pltpu.make_async_remote_copy(src_ref=block(a, peer), dst_ref=slot(a, me), **kw))
                    recvs.append(pltpu.make_async_remote_copy(src_ref=block(a, peer), dst_ref=slot(a, peer), **kw))
            return local, sends, recvs

        def start():
            local, sends, _ = copies()
            for cp in local + sends:
                cp.start()

        def wait():
            local, sends, recvs = copies()
            for send, recv in zip(sends, recvs):
                send.wait_send()
                recv.wait_recv()
            for cp in local:
                cp.wait()

        return start, wait


def _exchange_call(ex, name):
    n_in, n = len(ex.args), len(ex.srcs)

    def body(*refs):
        start, wait = ex.ops(refs[:n_in], refs[n_in:n_in + n], refs[n_in + n:])
        start()
        wait()

    return pl.pallas_call(
        body, name=name, in_specs=ex.in_specs, out_specs=ex.out_specs, out_shape=ex.out_shape,
        scratch_shapes=ex.scratch, input_output_aliases=ex.aliases(0, 0))(*ex.args)


def _split_refs(refs, counts):
    out, k = [], 0
    for cnt in counts:
        out.append(refs[k:k + cnt])
        k += cnt
    return out


def _ex_parts(ex):
    if ex is None:
        return [], [], [], [], [], (0, 0, 0)
    return ex.args, ex.in_specs, ex.out_shape, ex.out_specs, ex.scratch, (len(ex.args), len(ex.srcs), 3)


def _norm_matmul(h, w, g, *, tt, tn, name):
    t, k = h.shape
    n = w.shape[1]
    tt, tn = _tile(t, tt), _tile(n, tn)

    def body(h_ref, g_ref, w_ref, o_ref):
        x = h_ref[...]
        o_ref[...] = _dot((x * _rms_r(x) * g_ref[...]).astype(BF16), w_ref[...])

    return pl.pallas_call(
        body, name=name, grid=(n // tn, t // tt),
        in_specs=[pl.BlockSpec((tt, k), lambda j, i: (i, 0)), pl.BlockSpec((1, k), lambda j, i: (0, 0)),
                  pl.BlockSpec((k, tn), lambda j, i: (0, j))],
        out_specs=pl.BlockSpec((tt, tn), lambda j, i: (i, j)), out_shape=jax.ShapeDtypeStruct((t, n), F32),
        compiler_params=_params("arbitrary", "arbitrary"))(h, g, w)


def _matmul_tn(a, b, *, tm, tn, tk, name):
    t, m = a.shape
    n = b.shape[1]
    tm, tn, tk = _tile(m, tm), _tile(n, tn), _tile(t, tk)

    def body(a_ref, b_ref, o_ref):
        @pl.when(pl.program_id(2) == 0)
        def _():
            o_ref[...] = jnp.zeros_like(o_ref)

        o_ref[...] += _dot_tn(a_ref[...].astype(BF16), b_ref[...].astype(BF16))

    return pl.pallas_call(
        body, name=name, grid=(m // tm, n // tn, t // tk),
        in_specs=[pl.BlockSpec((tk, tm), lambda i, j, kk: (kk, i)), pl.BlockSpec((tk, tn), lambda i, j, kk: (kk, j))],
        out_specs=pl.BlockSpec((tm, tn), lambda i, j, kk: (i, j)),
        out_shape=jax.ShapeDtypeStruct((m, n), F32),
        compiler_params=_params("arbitrary", "arbitrary", "arbitrary"))(a, b)


def _ssd_conv(i_is_first, xbc_ref, halo_ref, cw_ref, cb_ref, buf, ts):
    buf[0:SUBLANES, :] = jnp.where(i_is_first, 0.0, halo_ref[...])
    buf[SUBLANES:SUBLANES + ts, :] = xbc_ref[...]
    cw = cw_ref[...]
    xc = cb_ref[...]
    for k in range(SSD_CONV):
        off = SUBLANES - (SSD_CONV - 1) + k
        xc = xc + cw[k:k + 1, :] * buf[off:off + ts, :]
    return xc


def _ssd_tile_prologue(xc, dt_ref, dtb_ref, alog_ref, e_ref, xa_scr, a_scr, dte_scr, x_scr):
    xa_scr[...] = xc * _sigmoid(xc)
    dt = _softplus(dt_ref[...] + dtb_ref[...])
    a_neg = -jnp.exp(alog_ref[...])
    a_scr[...] = dt * a_neg
    dte = _hdot(dt, e_ref[...])
    dte_scr[...] = dte
    x_scr[...] = xa_scr[:, 0:SSD_WIDTH] * dte
    return dt, a_neg


def _chunk_decays(a_c, tril, e):
    cs = _hdot(tril, a_c)
    cs_t = cs.T
    cs_e = _hdot(cs, e)
    last_e = cs_e[CHUNK - 1:CHUNK, :]
    return cs, cs_t, cs_e, last_e


def _ssd_fwd(proj, cw, cb, dtb, alog, dexp, ng, tril, e, *, ts, name, ex=None):
    t = proj.shape[0]
    ts = _tile(t, ts)
    nch = ts // CHUNK
    hb = ts // SUBLANES
    nt = t // ts
    ex_args, ex_in_specs, ex_out_shape, ex_out_specs, ex_scratch, ex_counts = _ex_parts(ex)

    def body(*refs):
        ((xbc_ref, halo_ref, z_ref, dt_ref, cw_ref, cb_ref, dtb_ref, alog_ref, dexp_ref, ng_ref, tril_ref, e_ref),
         ex_in, (y_ref, ypre_ref, st_ref, xc_ref), ex_out, (buf, xa_scr, a_scr, dte_scr, x_scr, ys_scr, hstate),
         ex_sems) = _split_refs(refs, (12, ex_counts[0], 4, ex_counts[1], 7, ex_counts[2]))
        i = pl.program_id(0)
        if ex is not None:
            ex_start, ex_wait = ex.ops(ex_in, ex_out, ex_sems)
            pl.when(i == 0)(ex_start)

        @pl.when(i == 0)
        def _():
            hstate[...] = jnp.zeros_like(hstate)

        xc = _ssd_conv(i == 0, xbc_ref, halo_ref, cw_ref, cb_ref, buf, ts)
        xc_ref[...] = xc
        _ssd_tile_prologue(xc, dt_ref, dtb_ref, alog_ref, e_ref, xa_scr, a_scr, dte_scr, x_scr)
        tril = tril_ref[...]
        e_mat = e_ref[...]
        causal = (lax.broadcasted_iota(jnp.int32, (CHUNK, CHUNK), 0)
                  >= lax.broadcasted_iota(jnp.int32, (CHUNK, CHUNK), 1))
        lane = lax.broadcasted_iota(jnp.int32, (CHUNK, LANES), 1)

        def chunk(c, carry):
            r0 = pl.multiple_of(c * CHUNK, CHUNK)
            rows = pl.ds(r0, CHUNK)
            cs, cs_t, cs_e, last_e = _chunk_decays(a_scr[rows, :], tril, e_mat)
            decay_e = jnp.exp(last_e - cs_e)
            ecs_e = jnp.exp(cs_e)
            xc = x_scr[rows, :]
            xb = xc.astype(BF16)
            xd = (xc * decay_e).astype(BF16)
            for g in range(SSD_GROUPS):
                bg = xa_scr[rows, SSD_WIDTH + g * SSD_STATE:SSD_WIDTH + (g + 1) * SSD_STATE].astype(BF16)
                cg = xa_scr[rows, SSD_WIDTH + (SSD_GROUPS + g) * SSD_STATE:
                            SSD_WIDTH + (SSD_GROUPS + g + 1) * SSD_STATE].astype(BF16)
                cbm = _dot_nt(cg, bg)
                for jj in range(2):
                    j = 2 * g + jj
                    cols = slice(j * LANES, (j + 1) * LANES)
                    xp = xb[:, cols]
                    ypair = jnp.zeros((CHUNK, LANES), F32)
                    for hh in range(2):
                        h = 2 * j + hh
                        seg = jnp.exp(jnp.where(causal, cs[:, h:h + 1] - cs_t[h:h + 1, :], -jnp.inf))
                        m = (cbm * seg).astype(BF16)
                        half = (lane < SSD_HEAD_DIM) if hh == 0 else (lane >= SSD_HEAD_DIM)
                        ypair = ypair + _dot(m, jnp.where(half, xp, jnp.zeros_like(xp)))
                    hp = hstate[j]
                    st_ref[c, j] = hp
                    ypair = ypair + _dot(cg, hp.astype(BF16)) * ecs_e[:, cols]
                    ys_scr[rows, cols] = ypair
                    hstate[j] = hp * jnp.exp(last_e[:, cols]) + _dot_tn(bg, xd[:, cols])
            return carry

        lax.fori_loop(0, nch, chunk, 0, unroll=4)
        ypre = ys_scr[...] + xa_scr[:, 0:SSD_WIDTH] * dexp_ref[...]
        ypre_ref[...] = ypre
        z = z_ref[...]
        yg = ypre * (z * _sigmoid(z))
        gw = SSD_WIDTH // SSD_GROUPS
        outs = []
        for g in range(SSD_GROUPS):
            v = yg[:, g * gw:(g + 1) * gw]
            outs.append(v * _rms_r(v))
        y_ref[...] = jnp.concatenate(outs, axis=1) * ng_ref[...]
        if ex is not None:
            pl.when(i == nt - 1)(ex_wait)

    full = lambda shape: pl.BlockSpec(shape, lambda i: tuple(0 for _ in shape))
    outs = pl.pallas_call(
        body, name=name, grid=(nt,),
        in_specs=[pl.BlockSpec((ts, SSD_XBC), lambda i: (i, COL_XBC // SSD_XBC)),
                  pl.BlockSpec((SUBLANES, SSD_XBC), lambda i: (jnp.maximum(i * hb - 1, 0), COL_XBC // SSD_XBC)),
                  pl.BlockSpec((ts, SSD_WIDTH), lambda i: (i, COL_Z // SSD_WIDTH)),
                  pl.BlockSpec((ts, LANES), lambda i: (i, COL_DT // LANES)),
                  full((SSD_CONV, SSD_XBC)), full((1, SSD_XBC)), full((1, LANES)), full((1, LANES)),
                  full((1, SSD_WIDTH)), full((1, SSD_WIDTH)), full((CHUNK, CHUNK)), full((LANES, SSD_WIDTH))]
        + ex_in_specs,
        out_specs=[pl.BlockSpec((ts, SSD_WIDTH), lambda i: (i, 0)), pl.BlockSpec((ts, SSD_WIDTH), lambda i: (i, 0)),
                   pl.BlockSpec((nch, N_PAIRS, SSD_STATE, LANES), lambda i: (i, 0, 0, 0)),
                   pl.BlockSpec((ts, SSD_XBC), lambda i: (i, 0))] + ex_out_specs,
        out_shape=[jax.ShapeDtypeStruct((t, SSD_WIDTH), F32), jax.ShapeDtypeStruct((t, SSD_WIDTH), F32),
                   jax.ShapeDtypeStruct((t // CHUNK, N_PAIRS, SSD_STATE, LANES), F32),
                   jax.ShapeDtypeStruct((t, SSD_XBC), F32)] + ex_out_shape,
        scratch_shapes=[pltpu.VMEM((SUBLANES + ts, SSD_XBC), F32), pltpu.VMEM((ts, SSD_XBC), F32),
                        pltpu.VMEM((ts, LANES), F32), pltpu.VMEM((ts, SSD_WIDTH), F32),
                        pltpu.VMEM((ts, SSD_WIDTH), F32), pltpu.VMEM((ts, SSD_WIDTH), F32),
                        pltpu.VMEM((N_PAIRS, SSD_STATE, LANES), F32)] + ex_scratch,
        input_output_aliases={} if ex is None else ex.aliases(12, 4),
        compiler_params=_params("arbitrary"))(proj, proj, proj, proj, cw, cb, dtb, alog, dexp, ng, tril, e, *ex_args)
    return outs[0], outs[1], outs[2], outs[3], outs[4:]


def _ssd_bwd(dymix, proj, xc, ypre, states, du, cw, dtb, alog, dexp, ng, tril, triu, e, *, ts, name, ex=None):
    t = proj.shape[0]
    ts = _tile(t, ts)
    nch = ts // CHUNK
    nt = t // ts
    ex_args, ex_in_specs, ex_out_shape, ex_out_specs, ex_scratch, ex_counts = _ex_parts(ex)

    def body(*refs):
        ((dy_ref, xbc_ref, xc_ref, z_ref, dt_ref, ypre_ref, st_ref, du_ref, cw_ref, dtb_ref, alog_ref,
          dexp_ref, ng_ref, tril_ref, triu_ref, e_ref), ex_in,
         (dproj_ref, dcw_ref, dcb_ref, ddtb_ref, dalog_ref, dd_ref, dng_ref), ex_out,
         (xa_scr, a_scr, dte_scr, x_scr, dyp_scr, dxa_scr, dx_scr, dbuf, carry, gstate),
         ex_sems) = _split_refs(refs, (16, ex_counts[0], 7, ex_counts[1], 10, ex_counts[2]))
        i = pl.program_id(0)
        if ex is not None:
            ex_start, ex_wait = ex.ops(ex_in, ex_out, ex_sems)
            pl.when(i == 0)(ex_start)

        @pl.when(i == 0)
        def _():
            gstate[...] = jnp.zeros_like(gstate)
            carry[...] = jnp.zeros_like(carry)
            for ref in (dcw_ref, dcb_ref, ddtb_ref, dalog_ref, dd_ref, dng_ref):
                ref[...] = jnp.zeros_like(ref)

        dt, a_neg = _ssd_tile_prologue(xc_ref[...], dt_ref, dtb_ref, alog_ref, e_ref, xa_scr, a_scr, dte_scr, x_scr)
        tril = tril_ref[...]
        triu = triu_ref[...]
        e_mat = e_ref[...]
        causal = (lax.broadcasted_iota(jnp.int32, (CHUNK, CHUNK), 0)
                  >= lax.broadcasted_iota(jnp.int32, (CHUNK, CHUNK), 1))
        lane = lax.broadcasted_iota(jnp.int32, (CHUNK, LANES), 1)
        sub = lax.broadcasted_iota(jnp.int32, (CHUNK, LANES), 0)

        z = z_ref[...]
        sig = _sigmoid(z)
        zs = z * sig
        ypre = ypre_ref[...]
        yg = ypre * zs
        dout = dy_ref[...]
        ngv = ng_ref[...]
        gw = SSD_WIDTH // SSD_GROUPS
        dyg_parts, dng_parts = [], []
        for g in range(SSD_GROUPS):
            cols = slice(g * gw, (g + 1) * gw)
            v = yg[:, cols]
            dx, dg = _rms_bwd(v, _rms_r(v), ngv[:, cols], dout[:, cols])
            dyg_parts.append(dx)
            dng_parts.append(dg)
        dyg = jnp.concatenate(dyg_parts, axis=1)
        dng_ref[...] += jnp.concatenate(dng_parts, axis=1)
        dyp = dyg * zs
        dyp_scr[...] = dyp
        dproj_ref[:, COL_Z:COL_Z + SSD_WIDTH] = dyg * ypre * (sig * (1.0 + z * (1.0 - sig)))
        dproj_ref[:, COL_U:COL_U + POOL_WIDTH] = du_ref[...]
        xs_all = xa_scr[:, 0:SSD_WIDTH]
        dd_ref[...] += _headsum(jnp.broadcast_to(_colsum(dyp * xs_all), (SUBLANES, SSD_WIDTH)), e_mat)[0:1, :]

        def chunk(k, carry_):
            c = nch - 1 - k
            r0 = pl.multiple_of(c * CHUNK, CHUNK)
            rows = pl.ds(r0, CHUNK)
            a_c = a_scr[rows, :]
            cs, cs_t, cs_e, last_e = _chunk_decays(a_c, tril, e_mat)
            decay_e = jnp.exp(last_e - cs_e)
            ecs_e = jnp.exp(cs_e)
            elast_e = jnp.exp(last_e)
            xc = x_scr[rows, :]
            xb = xc.astype(BF16)
            xd = (xc * decay_e).astype(BF16)
            dyc = dyp_scr[rows, :]
            dcs = jnp.zeros((CHUNK, LANES), F32)
            dcs_neg_t = jnp.zeros((LANES, CHUNK), F32)
            qoff, rin, ghrow = [], [], []
            for g in range(SSD_GROUPS):
                b_cols = slice(SSD_WIDTH + g * SSD_STATE, SSD_WIDTH + (g + 1) * SSD_STATE)
                c_cols = slice(SSD_WIDTH + (SSD_GROUPS + g) * SSD_STATE, SSD_WIDTH + (SSD_GROUPS + g + 1) * SSD_STATE)
                bg = xa_scr[rows, b_cols].astype(BF16)
                cg = xa_scr[rows, c_cols].astype(BF16)
                cbm = _dot_nt(cg, bg)
                dcb_m = jnp.zeros((CHUNK, CHUNK), F32)
                dbg = jnp.zeros((CHUNK, SSD_STATE), F32)
                dcg = jnp.zeros((CHUNK, SSD_STATE), F32)
                for jj in range(2):
                    j = 2 * g + jj
                    cols = slice(j * LANES, (j + 1) * LANES)
                    dyp_j = dyc[:, cols]
                    hp = st_ref[c, j]
                    hpb = hp.astype(BF16)
                    gt = gstate[j]
                    gtb = gt.astype(BF16)
                    ecs = ecs_e[:, cols]
                    yoff = _dot(cg, hpb) * ecs
                    dye = (dyp_j * ecs).astype(BF16)
                    dcg = dcg + _dot_nt(dye, hpb)
                    dht = _dot_tn(cg, dye)
                    qoff.append(dyp_j * yoff)
                    xg = _dot(bg, gtb)
                    dxp = xg * decay_e[:, cols]
                    rin.append(xg * xc[:, cols])
                    dbg = dbg + _dot_nt(xd[:, cols], gtb)
                    ghrow.append(_colsum(gt * hp) * elast_e[:, cols])
                    gstate[j] = dht + gt * elast_e[:, cols]
                    for hh in range(2):
                        h = 2 * j + hh
                        seg = jnp.exp(jnp.where(causal, cs[:, h:h + 1] - cs_t[h:h + 1, :], -jnp.inf))
                        m = cbm * seg
                        half = (lane < SSD_HEAD_DIM) if hh == 0 else (lane >= SSD_HEAD_DIM)
                        dym = jnp.where(half, dyp_j, 0.0).astype(BF16)
                        w = _dot_nt(dym, xb[:, cols])
                        pm = w * m
                        dcs = dcs + jnp.where(lane == h, jnp.sum(pm, axis=1, keepdims=True), 0.0)
                        dcs_neg_t = dcs_neg_t + jnp.where(sub == h, _colsum(pm), 0.0)
                        dcb_m = dcb_m + w * seg
                        dxp = dxp + _dot_tn(m.astype(BF16), dym)
                    dx_scr[:, cols] = dxp
                dcbb = dcb_m.astype(BF16)
                dxa_scr[rows, c_cols] = dcg + _dot(dcbb, bg)
                dxa_scr[rows, b_cols] = dbg + _dot_tn(dcbb, cg)
            decay_th = jnp.exp(cs[CHUNK - 1:CHUNK, :] - cs)
            rd = _headsum(jnp.concatenate(rin, axis=1), e_mat) * decay_th
            dcs = dcs - dcs_neg_t.T + _headsum(jnp.concatenate(qoff, axis=1), e_mat) - rd
            gh = _headsum(jnp.broadcast_to(jnp.concatenate(ghrow, axis=1), (SUBLANES, SSD_WIDTH)), e_mat)[0:1, :]
            dcs = dcs + jnp.where(sub == CHUNK - 1, _colsum(rd) + gh, 0.0)
            da = _hdot(triu, dcs)
            dx_all = dx_scr[...]
            xs = xa_scr[rows, 0:SSD_WIDTH]
            dt_c = _softplus(dt_ref[rows, :] + dtb_ref[...])
            ddt = da * a_neg + _headsum(dx_all * xs, e_mat)
            dalog_ref[...] += _colsum(da * dt_c) * a_neg
            ddtraw = ddt * _sigmoid(dt_ref[rows, :] + dtb_ref[...])
            dproj_ref[rows, COL_DT:COL_DT + LANES] = ddtraw
            ddtb_ref[...] += _colsum(ddtraw)
            dxa_scr[rows, 0:SSD_WIDTH] = dx_all * dte_scr[rows, :] + dyc * dexp_ref[...]
            return carry_

        lax.fori_loop(0, nch, chunk, 0, unroll=4)

        xcv = xc_ref[...]
        sgc = _sigmoid(xcv)
        dxc = dxa_scr[...] * (sgc * (1.0 + xcv * (1.0 - sgc)))
        dcb_ref[...] += _colsum(dxc)
        dbuf[0:ts, :] = dxc
        dbuf[ts:ts + SUBLANES, :] = carry[...]
        cwv = cw_ref[...]
        xbc = xbc_ref[...]
        dxbc = jnp.zeros((ts, SSD_XBC), F32)
        dcw_rows = []
        for k in range(SSD_CONV):
            back = SSD_CONV - 1 - k
            shifted = dbuf[back:back + ts, :] if back else dxc
            dcw_rows.append(_colsum(shifted * xbc))
            dxbc = dxbc + cwv[k:k + 1, :] * shifted
        dcw_ref[...] += jnp.concatenate(dcw_rows, axis=0)
        dproj_ref[:, COL_XBC:COL_XBC + SSD_XBC] = dxbc
        carry[...] = dxc[0:SUBLANES, :]
        if ex is not None:
            pl.when(i == nt - 1)(ex_wait)

    rev = lambda i: nt - 1 - i
    full = lambda shape: pl.BlockSpec(shape, lambda i: tuple(0 for _ in shape))
    outs = pl.pallas_call(
        body, name=name, grid=(nt,),
        in_specs=[pl.BlockSpec((ts, SSD_WIDTH), lambda i: (rev(i), 0)),
                  pl.BlockSpec((ts, SSD_XBC), lambda i: (rev(i), COL_XBC // SSD_XBC)),
                  pl.BlockSpec((ts, SSD_XBC), lambda i: (rev(i), 0)),
                  pl.BlockSpec((ts, SSD_WIDTH), lambda i: (rev(i), COL_Z // SSD_WIDTH)),
                  pl.BlockSpec((ts, LANES), lambda i: (rev(i), COL_DT // LANES)),
                  pl.BlockSpec((ts, SSD_WIDTH), lambda i: (rev(i), 0)),
                  pl.BlockSpec((nch, N_PAIRS, SSD_STATE, LANES), lambda i: (rev(i), 0, 0, 0)),
                  pl.BlockSpec((ts, POOL_WIDTH), lambda i: (rev(i), 0)),
                  full((SSD_CONV, SSD_XBC)), full((1, LANES)), full((1, LANES)),
                  full((1, SSD_WIDTH)), full((1, SSD_WIDTH)), full((CHUNK, CHUNK)), full((CHUNK, CHUNK)),
                  full((LANES, SSD_WIDTH))] + ex_in_specs,
        out_specs=[pl.BlockSpec((ts, N_PROJ), lambda i: (rev(i), 0)),
                   full((SSD_CONV, SSD_XBC)), full((1, SSD_XBC)), full((1, LANES)), full((1, LANES)),
                   full((1, LANES)), full((1, SSD_WIDTH))] + ex_out_specs,
        out_shape=[jax.ShapeDtypeStruct((t, N_PROJ), F32),
                   jax.ShapeDtypeStruct((SSD_CONV, SSD_XBC), F32), jax.ShapeDtypeStruct((1, SSD_XBC), F32),
                   jax.ShapeDtypeStruct((1, LANES), F32), jax.ShapeDtypeStruct((1, LANES), F32),
                   jax.ShapeDtypeStruct((1, LANES), F32), jax.ShapeDtypeStruct((1, SSD_WIDTH), F32)] + ex_out_shape,
        scratch_shapes=[pltpu.VMEM((ts, SSD_XBC), F32), pltpu.VMEM((ts, LANES), F32),
                        pltpu.VMEM((ts, SSD_WIDTH), F32), pltpu.VMEM((ts, SSD_WIDTH), F32),
                        pltpu.VMEM((ts, SSD_WIDTH), F32), pltpu.VMEM((ts, SSD_XBC), F32),
                        pltpu.VMEM((CHUNK, SSD_WIDTH), F32), pltpu.VMEM((ts + SUBLANES, SSD_XBC), F32),
                        pltpu.VMEM((SUBLANES, SSD_XBC), F32), pltpu.VMEM((N_PAIRS, SSD_STATE, LANES), F32)]
        + ex_scratch,
        input_output_aliases={} if ex is None else ex.aliases(16, 7),
        compiler_params=_params("arbitrary"))(
            dymix, proj, xc, proj, proj, ypre, states, du, cw, dtb, alog, dexp, ng, tril, triu, e, *ex_args)
    return outs[:7], outs[7:]


def _pooled(ubuf, u, pos, tt):
    out = []
    for gi, w in enumerate(POOL_WINDOWS):
        cols = slice(gi * POOL_GROUP, (gi + 1) * POOL_GROUP)
        acc = u[:, cols]
        for j in range(1, w):
            acc = acc + ubuf[POOL_HALO - j:POOL_HALO - j + tt, cols]
        out.append(acc / jnp.minimum(pos, float(w)) - u[:, cols])
    return out


def _mix_out(h, yssd, proj, pool_w, pool_scale, w_out, g_next, *, tt, name):
    t = h.shape[0]
    tt = _tile(t, tt)
    hb = tt // POOL_HALO

    def body(h_ref, ys_ref, u_ref, uh_ref, pw_ref, sc_ref, wo_ref, gn_ref, o_ref, ym_ref, n_ref, ubuf):
        i = pl.program_id(0)
        ubuf[0:POOL_HALO, :] = jnp.where(i == 0, 0.0, uh_ref[...])
        u = u_ref[...]
        ubuf[POOL_HALO:POOL_HALO + tt, :] = u
        pos = (i * tt + 1 + lax.broadcasted_iota(jnp.int32, (tt, 1), 0)).astype(F32)
        sc = sc_ref[...]
        parts = [ys_ref[...]]
        for gi, pooled in enumerate(_pooled(ubuf, u, pos, tt)):
            cols = slice(gi * POOL_GROUP, (gi + 1) * POOL_GROUP)
            parts.append(_dot(pooled.astype(BF16), pw_ref[gi]) * sc[:, cols])
        ymix = jnp.concatenate(parts, axis=1).astype(BF16)
        ym_ref[...] = ymix
        h2 = h_ref[...] + _dot(ymix, wo_ref[...])
        o_ref[...] = h2
        n_ref[...] = (h2 * _rms_r(h2) * gn_ref[...]).astype(BF16)

    full = lambda shape: pl.BlockSpec(shape, lambda i: tuple(0 for _ in shape))
    return pl.pallas_call(
        body, name=name, grid=(t // tt,),
        in_specs=[pl.BlockSpec((tt, D_MODEL), lambda i: (i, 0)), pl.BlockSpec((tt, SSD_WIDTH), lambda i: (i, 0)),
                  pl.BlockSpec((tt, POOL_WIDTH), lambda i: (i, COL_U // POOL_WIDTH)),
                  pl.BlockSpec((POOL_HALO, POOL_WIDTH), lambda i: (jnp.maximum(i * hb - 1, 0), COL_U // POOL_WIDTH)),
                  full((len(POOL_WINDOWS), POOL_GROUP, POOL_GROUP)), full((1, POOL_WIDTH)),
                  full((D_MODEL, D_MODEL)), full((1, D_MODEL))],
        out_specs=[pl.BlockSpec((tt, D_MODEL), lambda i: (i, 0))] * 3,
        out_shape=[jax.ShapeDtypeStruct((t, D_MODEL), F32), jax.ShapeDtypeStruct((t, D_MODEL), BF16),
                   jax.ShapeDtypeStruct((t, D_MODEL), BF16)],
        scratch_shapes=[pltpu.VMEM((POOL_HALO + tt, POOL_WIDTH), F32)],
        compiler_params=_params("arbitrary"))(h, yssd, proj, proj, pool_w, pool_scale, w_out, g_next)


def _out_bwd(dh, ymix, w_out_t, *, tt, name):
    t = dh.shape[0]
    tt = _tile(t, tt)

    def body(dh_ref, ym_ref, wt_ref, dym_ref, dw_ref):
        @pl.when(pl.program_id(0) == 0)
        def _():
            dw_ref[...] = jnp.zeros_like(dw_ref)

        dhb = dh_ref[...].astype(BF16)
        dym_ref[...] = _dot(dhb, wt_ref[...])
        dw_ref[...] += _dot_tn(ym_ref[...], dhb)

    return pl.pallas_call(
        body, name=name, grid=(t // tt,),
        in_specs=[pl.BlockSpec((tt, D_MODEL), lambda i: (i, 0)), pl.BlockSpec((tt, D_MODEL), lambda i: (i, 0)),
                  pl.BlockSpec((D_MODEL, D_MODEL), lambda i: (0, 0))],
        out_specs=[pl.BlockSpec((tt, D_MODEL), lambda i: (i, 0)), pl.BlockSpec((D_MODEL, D_MODEL), lambda i: (0, 0))],
        out_shape=[jax.ShapeDtypeStruct((t, D_MODEL), F32), jax.ShapeDtypeStruct((D_MODEL, D_MODEL), F32)],
        compiler_params=_params("arbitrary"))(dh, ymix, w_out_t)


def _pool_bwd(dymix, proj, pool_w, pool_w_t, pool_scale, *, tt, name):
    t = proj.shape[0]
    tt = _tile(t, tt)
    hb = tt // POOL_HALO
    nt = t // tt
    ng = len(POOL_WINDOWS)

    def body(dy_ref, dyh_ref, u_ref, uh_ref, pw_ref, pwt_ref, sc_ref, du_ref, dpw_ref, dsc_ref, ubuf, dbuf):
        i = pl.program_id(0)

        @pl.when(i == 0)
        def _():
            dpw_ref[...] = jnp.zeros_like(dpw_ref)
            dsc_ref[...] = jnp.zeros_like(dsc_ref)

        ubuf[0:POOL_HALO, :] = jnp.where(i == 0, 0.0, uh_ref[...])
        u = u_ref[...]
        ubuf[POOL_HALO:POOL_HALO + tt, :] = u
        pos = (i * tt + 1 + lax.broadcasted_iota(jnp.int32, (tt, 1), 0)).astype(F32)
        sc = sc_ref[...]
        dy = dy_ref[...]
        dyh = jnp.where(i == nt - 1, 0.0, dyh_ref[...])
        dsc_parts, du_parts = [], []
        for gi, pooled in enumerate(_pooled(ubuf, u, pos, tt)):
            w = POOL_WINDOWS[gi]
            cols = slice(gi * POOL_GROUP, (gi + 1) * POOL_GROUP)
            pb = pooled.astype(BF16)
            dsc_parts.append(_colsum(dy[:, cols] * _dot(pb, pw_ref[gi])))
            dmx = (dy[:, cols] * sc[:, cols]).astype(BF16)
            dpw_ref[gi] += _dot_tn(pb, dmx)
            dpool = _dot(dmx, pwt_ref[gi])
            dpool_h = _dot((dyh[:, cols] * sc[:, cols]).astype(BF16), pwt_ref[gi])
            dbuf[0:tt, cols] = dpool / jnp.minimum(pos, float(w))
            dbuf[tt:tt + POOL_HALO, cols] = dpool_h / float(w)
            acc = -dpool
            for j in range(w):
                acc = acc + dbuf[j:j + tt, cols]
            du_parts.append(acc)
        du_ref[...] = jnp.concatenate(du_parts, axis=1)
        dsc_ref[...] += jnp.concatenate(dsc_parts, axis=1)

    full = lambda shape: pl.BlockSpec(shape, lambda i: tuple(0 for _ in shape))
    ucol = COL_U // POOL_WIDTH
    return pl.pallas_call(
        body, name=name, grid=(nt,),
        in_specs=[pl.BlockSpec((tt, POOL_WIDTH), lambda i: (i, 1)),
                  pl.BlockSpec((POOL_HALO, POOL_WIDTH), lambda i: (jnp.minimum((i + 1) * hb, t // POOL_HALO - 1), 1)),
                  pl.BlockSpec((tt, POOL_WIDTH), lambda i: (i, ucol)),
                  pl.BlockSpec((POOL_HALO, POOL_WIDTH), lambda i: (jnp.maximum(i * hb - 1, 0), ucol)),
                  full((ng, POOL_GROUP, POOL_GROUP)), full((ng, POOL_GROUP, POOL_GROUP)), full((1, POOL_WIDTH))],
        out_specs=[pl.BlockSpec((tt, POOL_WIDTH), lambda i: (i, 0)), full((ng, POOL_GROUP, POOL_GROUP)),
                   full((1, POOL_WIDTH))],
        out_shape=[jax.ShapeDtypeStruct((t, POOL_WIDTH), F32), jax.ShapeDtypeStruct((ng, POOL_GROUP, POOL_GROUP), F32),
                   jax.ShapeDtypeStruct((1, POOL_WIDTH), F32)],
        scratch_shapes=[pltpu.VMEM((POOL_HALO + tt, POOL_WIDTH), F32), pltpu.VMEM((tt + POOL_HALO, POOL_WIDTH), F32)],
        compiler_params=_params("arbitrary"))(dymix, dymix, proj, proj, pool_w, pool_w_t, pool_scale)


def _in_bwd(dproj, h, g, w_in_t, dh, *, tt, name):
    t = h.shape[0]
    tt = _tile(t, tt)

    def body(dp_ref, h_ref, g_ref, wt_ref, dh_ref, o_ref, dw_ref, dg_ref):
        @pl.when(pl.program_id(0) == 0)
        def _():
            dw_ref[...] = jnp.zeros_like(dw_ref)
            dg_ref[...] = jnp.zeros_like(dg_ref)

        x = h_ref[...]
        r = _rms_r(x)
        gv = g_ref[...]
        dpb = dp_ref[...].astype(BF16)
        dw_ref[...] += _dot_tn((x * r * gv).astype(BF16), dpb)
        dx, dg = _rms_bwd(x, r, gv, _dot(dpb, wt_ref[...]))
        o_ref[...] = dh_ref[...] + dx
        dg_ref[...] += dg

    full = lambda shape: pl.BlockSpec(shape, lambda i: tuple(0 for _ in shape))
    row = lambda n: pl.BlockSpec((tt, n), lambda i: (i, 0))
    return pl.pallas_call(
        body, name=name, grid=(t // tt,),
        in_specs=[row(N_PROJ), row(D_MODEL), full((1, D_MODEL)), full((N_PROJ, D_MODEL)), row(D_MODEL)],
        out_specs=[row(D_MODEL), full((D_MODEL, N_PROJ)), full((1, D_MODEL))],
        out_shape=[jax.ShapeDtypeStruct((t, D_MODEL), F32), jax.ShapeDtypeStruct((D_MODEL, N_PROJ), F32),
                   jax.ShapeDtypeStruct((1, D_MODEL), F32)],
        compiler_params=_params("arbitrary"))(dproj, h, g, w_in_t, dh)


FFN_COLS = 256
N_SLABS = D_FF // FFN_COLS
N_SLAB_BUFS = 4


def _ffn_fwd(h, n2, w_up, cw, cb, w_down, *, tt, name, ex=None):
    t = h.shape[0]
    tt = _tile(t, tt)
    nt = t // tt
    ex_args, ex_in_specs, ex_out_shape, ex_out_specs, ex_scratch, ex_counts = _ex_parts(ex)

    def body(*refs):
        ((h_ref, n2_ref, wu_ref, cw_ref, cb_ref, wd_ref), ex_in, (o_ref, act_ref, pre_ref, up_ref), ex_out,
         (slab, halo), ex_sems) = _split_refs(refs, (6, ex_counts[0], 4, ex_counts[1], 2, ex_counts[2]))
        i = pl.program_id(0)
        if ex is not None:
            ex_start, ex_wait = ex.ops(ex_in, ex_out, ex_sems)
            pl.when(i == 0)(ex_start)

        @pl.when(i == 0)
        def _():
            halo[...] = jnp.zeros_like(halo)

        n2v = n2_ref[...]

        def slab_cols(s):
            return slice(s * FFN_COLS, (s + 1) * FFN_COLS), slice(D_FF + s * FFN_COLS, D_FF + (s + 1) * FFN_COLS)

        def project(s):
            return [_dot(n2v, wu_ref[:, cols]) for cols in slab_cols(s)]

        def conv(u, cols, buf_id):
            up_ref[:, cols] = u.astype(BF16)
            sb = slab.at[buf_id]
            sb[0:SUBLANES, :] = halo[:, cols]
            sb[SUBLANES:SUBLANES + tt, :] = u
            halo[:, cols] = u[tt - SUBLANES:tt, :]
            acc = cb_ref[:, cols] + cw_ref[FFN_CONV - 1:FFN_CONV, cols] * u
            for k in range(FFN_CONV - 1):
                off = SUBLANES - (FFN_CONV - 1) + k
                acc = acc + cw_ref[k:k + 1, cols] * sb[off:off + tt, :]
            pre_ref[:, cols] = acc
            return acc

        out = h_ref[...]
        ahead = project(0)
        for s in range(N_SLABS):
            (ug, uv), (gcols, vcols) = ahead, slab_cols(s)
            if s + 1 < N_SLABS:
                ahead = project(s + 1)
            gate = conv(ug, gcols, (2 * s) % N_SLAB_BUFS)
            val = conv(uv, vcols, (2 * s + 1) % N_SLAB_BUFS)
            act = (_gelu(gate) * val).astype(BF16)
            act_ref[:, s * FFN_COLS:(s + 1) * FFN_COLS] = act
            out = out + _dot(act, wd_ref[s * FFN_COLS:(s + 1) * FFN_COLS, :])
        o_ref[...] = out
        if ex is not None:
            pl.when(i == nt - 1)(ex_wait)

    full = lambda shape: pl.BlockSpec(shape, lambda i: tuple(0 for _ in shape))
    row = lambda n: pl.BlockSpec((tt, n), lambda i: (i, 0))
    outs = pl.pallas_call(
        body, name=name, grid=(nt,),
        in_specs=[row(D_MODEL), row(D_MODEL), full((D_MODEL, D_UP)), full((FFN_CONV, D_UP)), full((1, D_UP)),
                  full((D_FF, D_MODEL))] + ex_in_specs,
        out_specs=[row(D_MODEL), row(D_FF), row(D_UP), row(D_UP)] + ex_out_specs,
        out_shape=[jax.ShapeDtypeStruct((t, D_MODEL), F32), jax.ShapeDtypeStruct((t, D_FF), BF16),
                   jax.ShapeDtypeStruct((t, D_UP), F32), jax.ShapeDtypeStruct((t, D_UP), BF16)] + ex_out_shape,
        scratch_shapes=[pltpu.VMEM((N_SLAB_BUFS, SUBLANES + tt, FFN_COLS), F32), pltpu.VMEM((SUBLANES, D_UP), F32)]
        + ex_scratch,
        input_output_aliases={} if ex is None else ex.aliases(6, 4),
        compiler_params=_params("arbitrary"))(h, n2, w_up, cw, cb, w_down, *ex_args)
    return outs[0], outs[1], outs[2], outs[3], outs[4:]


def _ffn_bwd(dh, up, pre, h2, g2, w_down_t, w_up_t, cw, *, tt, name, ex=None):
    t = dh.shape[0]
    tt = _tile(t, tt)
    nt = t // tt
    ex_args, ex_in_specs, ex_out_shape, ex_out_specs, ex_scratch, ex_counts = _ex_parts(ex)

    def body(*refs):
        ((dh_ref, up_ref, pre_ref, h2_ref, g2_ref, wdt_ref, wut_ref, cw_ref), ex_in,
         (o_ref, dup_ref, dcw_ref, dcb_ref, dg_ref), ex_out, (slab, carry), ex_sems) = _split_refs(
            refs, (8, ex_counts[0], 5, ex_counts[1], 2, ex_counts[2]))
        i = pl.program_id(0)
        if ex is not None:
            ex_start, ex_wait = ex.ops(ex_in, ex_out, ex_sems)
            pl.when(i == 0)(ex_start)

        @pl.when(i == 0)
        def _():
            for ref in (dcw_ref, dcb_ref, dg_ref, carry):
                ref[...] = jnp.zeros_like(ref)

        dhv = dh_ref[...]
        dhb = dhv.astype(BF16)

        def slab_cols(s):
            return slice(s * FFN_COLS, (s + 1) * FFN_COLS), slice(D_FF + s * FFN_COLS, D_FF + (s + 1) * FFN_COLS)

        def d_act(s):
            return _dot(dhb, wdt_ref[:, slab_cols(s)[0]])

        def through_conv(dp, cols, buf_id):
            sb = slab.at[buf_id]
            sb[0:tt, :] = dp
            sb[tt:tt + SUBLANES, :] = carry[:, cols]
            carry[:, cols] = dp[0:SUBLANES, :]
            shifted = [sb[FFN_CONV - 1 - k:FFN_CONV - 1 - k + tt, :] for k in range(FFN_CONV - 1)] + [dp]
            x = up_ref[:, cols].astype(F32)
            dup = cw_ref[0:1, cols] * shifted[0]
            for k in range(1, FFN_CONV):
                dup = dup + cw_ref[k:k + 1, cols] * shifted[k]
            dcb_ref[:, cols] += _colsum(dp)
            dcw_ref[:, cols] += jnp.concatenate([_colsum(sh * x) for sh in shifted], axis=0)
            dupb = dup.astype(BF16)
            dup_ref[:, cols] = dupb
            return _dot(dupb, wut_ref[cols, :])

        dn = jnp.zeros((tt, D_MODEL), F32)
        ahead = d_act(0)
        for s in range(N_SLABS):
            da, (gcols, vcols) = ahead, slab_cols(s)
            if s + 1 < N_SLABS:
                ahead = d_act(s + 1)
            gate, val = pre_ref[:, gcols], pre_ref[:, vcols]
            dn = dn + through_conv(da * val * _gelu_grad(gate), gcols, (2 * s) % N_SLAB_BUFS)
            dn = dn + through_conv(da * _gelu(gate), vcols, (2 * s + 1) % N_SLAB_BUFS)
        xv = h2_ref[...]
        dx, dg = _rms_bwd(xv, _rms_r(xv), g2_ref[...], dn)
        o_ref[...] = dhv + dx
        dg_ref[...] += dg
        if ex is not None:
            pl.when(i == nt - 1)(ex_wait)

    rev = lambda i: nt - 1 - i
    full = lambda shape: pl.BlockSpec(shape, lambda i: tuple(0 for _ in shape))
    row = lambda n: pl.BlockSpec((tt, n), lambda i: (rev(i), 0))
    outs = pl.pallas_call(
        body, name=name, grid=(nt,),
        in_specs=[row(D_MODEL), row(D_UP), row(D_UP), row(D_MODEL), full((1, D_MODEL)), full((D_MODEL, D_FF)),
                  full((D_UP, D_MODEL)), full((FFN_CONV, D_UP))] + ex_in_specs,
        out_specs=[row(D_MODEL), row(D_UP), full((FFN_CONV, D_UP)), full((1, D_UP)), full((1, D_MODEL))]
        + ex_out_specs,
        out_shape=[jax.ShapeDtypeStruct((t, D_MODEL), F32), jax.ShapeDtypeStruct((t, D_UP), BF16),
                   jax.ShapeDtypeStruct((FFN_CONV, D_UP), F32), jax.ShapeDtypeStruct((1, D_UP), F32),
                   jax.ShapeDtypeStruct((1, D_MODEL), F32)] + ex_out_shape,
        scratch_shapes=[pltpu.VMEM((N_SLAB_BUFS, tt + SUBLANES, FFN_COLS), F32), pltpu.VMEM((SUBLANES, D_UP), F32)]
        + ex_scratch,
        input_output_aliases={} if ex is None else ex.aliases(8, 5),
        compiler_params=_params("arbitrary"))(dh, up, pre, h2, g2, w_down_t, w_up_t, cw, *ex_args)
    return outs[0], outs[1], outs[2], outs[3], outs[4], outs[5:]


def _ple_fwd(h, p, g, w_gate, w_proj, *, tt, name):
    t = h.shape[0]
    tt = _tile(t, tt)

    def body(h_ref, p_ref, g_ref, wg_ref, wp_ref, o_ref):
        x = h_ref[...]
        n = (x * _rms_r(x) * g_ref[...]).astype(BF16)
        gate = _sigmoid(_dot(n, wg_ref[...]))
        o_ref[...] = x + _dot(p_ref[...].astype(BF16), wp_ref[...]) * gate

    full = lambda shape: pl.BlockSpec(shape, lambda i: tuple(0 for _ in shape))
    return pl.pallas_call(
        body, name=name, grid=(t // tt,),
        in_specs=[pl.BlockSpec((tt, D_MODEL), lambda i: (i, 0)), pl.BlockSpec((tt, D_PLE), lambda i: (i, 0)),
                  full((1, D_MODEL)), full((D_MODEL, D_MODEL)), full((D_PLE, D_MODEL))],
        out_specs=pl.BlockSpec((tt, D_MODEL), lambda i: (i, 0)),
        out_shape=jax.ShapeDtypeStruct((t, D_MODEL), F32),
        compiler_params=_params("arbitrary"))(h, p, g, w_gate, w_proj)


def _ple_bwd(dh, h, p, g, w_gate, w_gate_t, w_proj, *, tt, name):
    t = h.shape[0]
    tt = _tile(t, tt)

    def body(dh_ref, h_ref, p_ref, g_ref, wg_ref, wgt_ref, wp_ref, o_ref, dwg_ref, dwp_ref, dg_ref):
        @pl.when(pl.program_id(0) == 0)
        def _():
            dwg_ref[...] = jnp.zeros_like(dwg_ref)
            dwp_ref[...] = jnp.zeros_like(dwp_ref)
            dg_ref[...] = jnp.zeros_like(dg_ref)

        x = h_ref[...]
        r = _rms_r(x)
        gv = g_ref[...]
        n = (x * r * gv).astype(BF16)
        gate = _sigmoid(_dot(n, wg_ref[...]))
        pb = p_ref[...].astype(BF16)
        pe = _dot(pb, wp_ref[...])
        dhv = dh_ref[...]
        dwp_ref[...] += _dot_tn(pb, (dhv * gate).astype(BF16))
        ds = (dhv * pe * gate * (1.0 - gate)).astype(BF16)
        dwg_ref[...] += _dot_tn(n, ds)
        dx, dg = _rms_bwd(x, r, gv, _dot(ds, wgt_ref[...]))
        o_ref[...] = dhv + dx
        dg_ref[...] += dg

    full = lambda shape: pl.BlockSpec(shape, lambda i: tuple(0 for _ in shape))
    row = lambda n: pl.BlockSpec((tt, n), lambda i: (i, 0))
    return pl.pallas_call(
        body, name=name, grid=(t // tt,),
        in_specs=[row(D_MODEL), row(D_MODEL), row(D_PLE), full((1, D_MODEL)), full((D_MODEL, D_MODEL)),
                  full((D_MODEL, D_MODEL)), full((D_PLE, D_MODEL))],
        out_specs=[row(D_MODEL), full((D_MODEL, D_MODEL)), full((D_PLE, D_MODEL)), full((1, D_MODEL))],
        out_shape=[jax.ShapeDtypeStruct((t, D_MODEL), F32), jax.ShapeDtypeStruct((D_MODEL, D_MODEL), F32),
                   jax.ShapeDtypeStruct((D_PLE, D_MODEL), F32), jax.ShapeDtypeStruct((1, D_MODEL), F32)],
        compiler_params=_params("arbitrary"))(dh, h, p, g, w_gate, w_gate_t, w_proj)


def _loss_head(h, g, target, *, tt, name):
    t = h.shape[0]
    tt = _tile(t, tt)

    def body(h_ref, g_ref, tg_ref, dh_ref, loss_ref, dg_ref):
        @pl.when(pl.program_id(0) == 0)
        def _():
            loss_ref[...] = jnp.zeros_like(loss_ref)
            dg_ref[...] = jnp.zeros_like(dg_ref)

        x = h_ref[...]
        r = _rms_r(x)
        gv = g_ref[...]
        diff = x * r * gv - tg_ref[...]
        loss_ref[...] += 0.5 * jnp.sum(jnp.mean(diff * diff, axis=-1, keepdims=True), axis=0, keepdims=True)
        dx, dg = _rms_bwd(x, r, gv, diff * (1.0 / D_MODEL))
        dh_ref[...] = dx
        dg_ref[...] += dg

    return pl.pallas_call(
        body, name=name, grid=(t // tt,),
        in_specs=[pl.BlockSpec((tt, D_MODEL), lambda i: (i, 0)), pl.BlockSpec((1, D_MODEL), lambda i: (0, 0)),
                  pl.BlockSpec((tt, D_MODEL), lambda i: (i, 0))],
        out_specs=[pl.BlockSpec((tt, D_MODEL), lambda i: (i, 0)), pl.BlockSpec((SUBLANES, LANES), lambda i: (0, 0)),
                   pl.BlockSpec((1, D_MODEL), lambda i: (0, 0))],
        out_shape=[jax.ShapeDtypeStruct((t, D_MODEL), F32), jax.ShapeDtypeStruct((SUBLANES, LANES), F32),
                   jax.ShapeDtypeStruct((1, D_MODEL), F32)],
        compiler_params=_params("arbitrary"))(h, g, target)


ADAM_BLOCK_BYTES = 4 * 1024 * 1024


def _adam_rows(rows, cols):
    lanes = -(-cols // LANES) * LANES
    for cand in (1024, 512, 256, 128, 64, 32, 16, 8):
        if rows % cand == 0 and N_DEV * cand * lanes * 4 <= ADAM_BLOCK_BYTES:
            return cand
    return rows


def _sum_adamw(parts, w, m, v, *, name):
    nl, rows, cols = w.shape
    tr = _adam_rows(rows, cols)

    def body(p_ref, w_ref, m_ref, v_ref, g_ref, d_ref, nm_ref, nv_ref):
        g = p_ref[0]
        for k in range(1, N_DEV):
            g = g + p_ref[k]
        g_ref[...] = g
        nm = ADAM_B1 * m_ref[...] + (1.0 - ADAM_B1) * g
        nv = ADAM_B2 * v_ref[...] + (1.0 - ADAM_B2) * (g * g)
        m_hat = nm / (1.0 - ADAM_B1 ** ADAM_STEP)
        v_hat = nv / (1.0 - ADAM_B2 ** ADAM_STEP)
        d_ref[...] = -ADAM_LR * (m_hat / (jnp.sqrt(v_hat) + ADAM_EPS) + ADAM_WD * w_ref[...])
        nm_ref[...] = nm
        nv_ref[...] = nv

    blk = pl.BlockSpec((None, tr, cols), lambda l, r: (l, r, 0))
    return pl.pallas_call(
        body, name=name, grid=(nl, rows // tr),
        in_specs=[pl.BlockSpec((None, N_DEV, tr, cols), lambda l, r: (l, 0, r, 0)), blk, blk, blk],
        out_specs=[blk, blk, blk, blk],
        out_shape=[jax.ShapeDtypeStruct((nl, rows, cols), F32)] * 4,
        compiler_params=_params("arbitrary", "arbitrary"))(parts, w, m, v)


PACK_ROWS = 512


def _pack(arrays):
    flat = jnp.concatenate([a.astype(F32).reshape(-1) for a in arrays])
    pad = (-flat.shape[0]) % (PACK_ROWS * LANES)
    return jnp.pad(flat, (0, pad)).reshape(-1, LANES)


def _unpack(buf, shapes):
    flat = buf.reshape(-1)
    out, off = [], 0
    for s in shapes:
        n = math.prod(s)
        out.append(flat[off:off + n].reshape(s))
        off += n
    return out


def _to_proj_cols(w):
    z, xbc, dtc, u = jnp.split(w, [SSD_WIDTH, SSD_WIDTH + SSD_XBC, SSD_WIDTH + SSD_XBC + SSD_HEADS], axis=-1)
    pad = jnp.zeros(w.shape[:-1] + (LANES - SSD_HEADS,), w.dtype)
    return jnp.concatenate([xbc, z, u, dtc, pad], axis=-1)


def _from_proj_cols(w):
    xbc, z, u, dtc = (w[..., COL_XBC:COL_Z], w[..., COL_Z:COL_U], w[..., COL_U:COL_DT],
                      w[..., COL_DT:COL_DT + SSD_HEADS])
    return jnp.concatenate([z, xbc, dtc, u], axis=-1)


def _pad_heads(v):
    return jnp.pad(v, (0, LANES - SSD_HEADS)).reshape(1, LANES)


def _cat_cols(g):
    return jnp.transpose(g, (1, 0, 2)).reshape(g.shape[1], N_DEV * g.shape[2])


def _split_cols(w):
    r, c = w.shape
    return jnp.transpose(w.reshape(r, N_DEV, c // N_DEV), (1, 0, 2))


def _cat_rows(g):
    return g.reshape(N_DEV * g.shape[1], g.shape[2])


def _split_rows(w):
    return w.reshape(N_DEV, w.shape[0] // N_DEV, w.shape[1])


SHARDED = ("w_in", "w_out", "ffn_w_up", "ffn_w_down", "ple_w_gate", "ple_w_proj", "ssd_conv_w", "ffn_conv_w")
COL_SHARDED = ("w_in", "ffn_w_up", "ple_w_proj", "ssd_conv_w", "ffn_conv_w")
MATMUL_W = SHARDED[:6]
REPLICATED = ("mix_norm_g", "ssd_conv_b", "ssd_dt_bias", "ssd_a_log", "ssd_d", "ssd_norm_g", "pool_w", "pool_scale",
              "ffn_norm_g", "ffn_conv_b", "ple_norm_g", "final_norm_g")
WEIGHTS = ("mix_norm_g", "w_in", "ssd_conv_w", "ssd_conv_b", "ssd_dt_bias", "ssd_a_log", "ssd_d", "ssd_norm_g",
           "pool_w", "pool_scale", "w_out", "ffn_norm_g", "ffn_w_up", "ffn_conv_w", "ffn_conv_b", "ffn_w_down",
           "ple_norm_g", "ple_w_gate", "ple_w_proj", "final_norm_g")


FIRST_USED = ("w_in", "ssd_conv_w")
LATER_USED = tuple(k for k in SHARDED if k not in FIRST_USED)
LAST_MADE = ("w_out", "ssd_conv_w", "w_in")
EARLY_MADE = tuple(k for k in SHARDED if k not in LAST_MADE)
TRANSPOSED = ("w_in", "w_out", "ffn_w_up", "ffn_w_down", "ple_w_gate")


def _pick(names, per_sharded):
    return [per_sharded[SHARDED.index(k)] for k in names]


def _put(names, per_sharded, values):
    out = list(per_sharded)
    for k, val in zip(names, values):
        out[SHARDED.index(k)] = val
    return out


def _assemble(names, gathered):
    full = {}
    for k, g in zip(names, gathered):
        full[k] = _cat_cols(g) if k in COL_SHARDED else _cat_rows(g)
        if k == "w_in":
            full[k] = _to_proj_cols(full[k])
        if k in TRANSPOSED:
            full[k + "_t"] = full[k].T
    return full


def _grad_shards(names, grads):
    out = []
    for k in names:
        g = _from_proj_cols(grads[k]) if k == "w_in" else grads[k]
        out.append(_split_cols(g) if k in COL_SHARDED else _split_rows(g))
    return out


def _layer_fwd(i, h1, p_i, lw, rep, consts, ex_own, ex):
    tril, e_mat = consts
    row = lambda v: v.reshape(1, -1)
    dtb, alog = _pad_heads(rep["ssd_dt_bias"]), _pad_heads(rep["ssd_a_log"])
    dexp = row(jnp.repeat(rep["ssd_d"], SSD_HEAD_DIM))
    pw = rep["pool_w"].astype(BF16)
    proj = _norm_matmul(h1, lw["w_in"], row(rep["mix_norm_g"]), tt=1024, tn=N_PROJ, name=f"in_proj_{i}")
    yssd, ypre, states, xc, own = _ssd_fwd(proj, lw["ssd_conv_w"], row(rep["ssd_conv_b"]), dtb, alog, dexp,
                                       row(rep["ssd_norm_g"]), tril, e_mat, ts=512, name=f"ssd_fwd_{i}", ex=ex_own)
    if ex_own is not None:
        lw = dict(lw, **_assemble(LATER_USED, own))
    h2, ymix, n2 = _mix_out(h1, yssd, proj, pw, row(rep["pool_scale"]), lw["w_out"], row(rep["ffn_norm_g"]), tt=512,
                            name=f"mix_out_{i}")
    h3, act, pre, up, gathered = _ffn_fwd(h2, n2, lw["ffn_w_up"], lw["ffn_conv_w"], row(rep["ffn_conv_b"]),
                                          lw["ffn_w_down"], tt=256, name=f"ffn_fwd_{i}", ex=ex)
    h4 = _ple_fwd(h3, p_i, row(rep["ple_norm_g"]), lw["ple_w_gate"], lw["ple_w_proj"], tt=1024, name=f"ple_fwd_{i}")
    saved = dict(h1=h1, proj=proj, xc=xc, ypre=ypre, states=states, ymix=ymix, h2=h2, n2=n2, up=up, pre=pre, act=act, h3=h3,
                 dtb=dtb, alog=alog, dexp=dexp, pw=pw)
    return h4, saved, lw, gathered


def _layer_bwd(i, dh, p_i, lw, rep, s, consts, pending, parts, own_early):
    tril, triu, e_mat = consts
    ex = None if pending is None else _Exchange(pending, scatter=True, layer=i + 1, into=parts)
    row = lambda v: v.reshape(1, -1)
    g = {}
    dh, g["ple_w_gate"], g["ple_w_proj"], dg3 = _ple_bwd(dh, s["h3"], p_i, row(rep["ple_norm_g"]), lw["ple_w_gate"],
                                                         lw["ple_w_gate_t"], lw["ple_w_proj"], tt=512,
                                                         name=f"ple_bwd_{i}")
    g["ple_norm_g"] = dg3.reshape(-1)
    g["ffn_w_down"] = _matmul_tn(s["act"], dh, tm=D_FF // 2, tn=D_MODEL, tk=1024, name=f"dw_down_{i}")
    dh, dup, g["ffn_conv_w"], dcb, dg2, scattered = _ffn_bwd(
        dh, s["up"], s["pre"], s["h2"], row(rep["ffn_norm_g"]), lw["ffn_w_down_t"], lw["ffn_w_up_t"],
        lw["ffn_conv_w"], tt=256, name=f"ffn_bwd_{i}", ex=ex)
    if ex is not None:
        parts = scattered
    g["ffn_conv_b"], g["ffn_norm_g"] = dcb.reshape(-1), dg2.reshape(-1)
    g["ffn_w_up"] = _matmul_tn(s["n2"], dup, tm=D_MODEL, tn=D_UP // 4, tk=1024, name=f"dw_up_{i}")
    dymix, g["w_out"] = _out_bwd(dh, s["ymix"], lw["w_out_t"], tt=1024, name=f"out_bwd_{i}")
    du, g["pool_w"], dsc = _pool_bwd(dymix, s["proj"], s["pw"], jnp.swapaxes(s["pw"], 1, 2), row(rep["pool_scale"]),
                                     tt=512, name=f"pool_bwd_{i}")
    g["pool_scale"] = dsc.reshape(-1)
    ex_own = None
    if own_early:
        ex_own = _Exchange(_grad_shards(EARLY_MADE, g), scatter=True, layer=i, into=_pick(EARLY_MADE, parts))
    (dproj, g["ssd_conv_w"], dcb, ddtb, dalog, dd, dng), own = _ssd_bwd(
        dymix, s["proj"], s["xc"], s["ypre"], s["states"], du, lw["ssd_conv_w"], s["dtb"], s["alog"], s["dexp"],
        row(rep["ssd_norm_g"]), tril, triu, e_mat, ts=512, name=f"ssd_bwd_{i}", ex=ex_own)
    if own_early:
        parts = _put(EARLY_MADE, parts, own)
    g["ssd_conv_b"], g["ssd_norm_g"] = dcb.reshape(-1), dng.reshape(-1)
    g["ssd_dt_bias"], g["ssd_a_log"], g["ssd_d"] = ddtb[0, :SSD_HEADS], dalog[0, :SSD_HEADS], dd[0, :SSD_HEADS]
    dh, g["w_in"], dg1 = _in_bwd(dproj, s["h1"], row(rep["mix_norm_g"]), lw["w_in_t"], dh, tt=512, name=f"in_bwd_{i}")
    g["mix_norm_g"] = dg1.reshape(-1)
    return dh, g, parts


def kernel(x, p, mix_norm_g, w_in, ssd_conv_w, ssd_conv_b, ssd_dt_bias, ssd_a_log, ssd_d, ssd_norm_g, pool_w, pool_scale, w_out, ffn_norm_g, ffn_w_up, ffn_conv_w, ffn_conv_b, ffn_w_down, ple_norm_g, ple_w_gate, ple_w_proj, final_norm_g, loss_target, m_mix_norm_g, m_w_in, m_ssd_conv_w, m_ssd_conv_b, m_ssd_dt_bias, m_ssd_a_log, m_ssd_d, m_ssd_norm_g, m_pool_w, m_pool_scale, m_w_out, m_ffn_norm_g, m_ffn_w_up, m_ffn_conv_w, m_ffn_conv_b, m_ffn_w_down, m_ple_norm_g, m_ple_w_gate, m_ple_w_proj, m_final_norm_g, v_mix_norm_g, v_w_in, v_ssd_conv_w, v_ssd_conv_b, v_ssd_dt_bias, v_ssd_a_log, v_ssd_d, v_ssd_norm_g, v_pool_w, v_pool_scale, v_w_out, v_ffn_norm_g, v_ffn_w_up, v_ffn_conv_w, v_ffn_conv_b, v_ffn_w_down, v_ple_norm_g, v_ple_w_gate, v_ple_w_proj, v_final_norm_g):
    w = dict(mix_norm_g=mix_norm_g, w_in=w_in, ssd_conv_w=ssd_conv_w, ssd_conv_b=ssd_conv_b, ssd_dt_bias=ssd_dt_bias,
             ssd_a_log=ssd_a_log, ssd_d=ssd_d, ssd_norm_g=ssd_norm_g, pool_w=pool_w, pool_scale=pool_scale, w_out=w_out,
             ffn_norm_g=ffn_norm_g, ffn_w_up=ffn_w_up, ffn_conv_w=ffn_conv_w, ffn_conv_b=ffn_conv_b,
             ffn_w_down=ffn_w_down, ple_norm_g=ple_norm_g, ple_w_gate=ple_w_gate, ple_w_proj=ple_w_proj,
             final_norm_g=final_norm_g)
    m = dict(mix_norm_g=m_mix_norm_g, w_in=m_w_in, ssd_conv_w=m_ssd_conv_w, ssd_conv_b=m_ssd_conv_b,
             ssd_dt_bias=m_ssd_dt_bias, ssd_a_log=m_ssd_a_log, ssd_d=m_ssd_d, ssd_norm_g=m_ssd_norm_g, pool_w=m_pool_w,
             pool_scale=m_pool_scale, w_out=m_w_out, ffn_norm_g=m_ffn_norm_g, ffn_w_up=m_ffn_w_up,
             ffn_conv_w=m_ffn_conv_w, ffn_conv_b=m_ffn_conv_b, ffn_w_down=m_ffn_w_down, ple_norm_g=m_ple_norm_g,
             ple_w_gate=m_ple_w_gate, ple_w_proj=m_ple_w_proj, final_norm_g=m_final_norm_g)
    v = dict(mix_norm_g=v_mix_norm_g, w_in=v_w_in, ssd_conv_w=v_ssd_conv_w, ssd_conv_b=v_ssd_conv_b,
             ssd_dt_bias=v_ssd_dt_bias, ssd_a_log=v_ssd_a_log, ssd_d=v_ssd_d, ssd_norm_g=v_ssd_norm_g, pool_w=v_pool_w,
             pool_scale=v_pool_scale, w_out=v_w_out, ffn_norm_g=v_ffn_norm_g, ffn_w_up=v_ffn_w_up,
             ffn_conv_w=v_ffn_conv_w, ffn_conv_b=v_ffn_conv_b, ffn_w_down=v_ffn_w_down, ple_norm_g=v_ple_norm_g,
             ple_w_gate=v_ple_w_gate, ple_w_proj=v_ple_w_proj, final_norm_g=v_final_norm_g)

    tril = jnp.tril(jnp.ones((CHUNK, CHUNK), BF16))
    triu = tril.T
    e_mat = (jnp.arange(SSD_WIDTH)[None, :] // SSD_HEAD_DIM == jnp.arange(LANES)[:, None]).astype(BF16)
    rep = [{k: w[k][i] for k in REPLICATED if k != "final_norm_g"} for i in range(DEPTH)]
    p_loc = p[:, 0]

    shards = [w[k].astype(BF16) if k in MATMUL_W else w[k] for k in SHARDED]
    lw = _assemble(FIRST_USED, _exchange_call(_Exchange(_pick(FIRST_USED, shards), scatter=False, layer=0),
                                              "gather_weights_0"))
    h, saved, layer_w = x[0], [], []
    for i in range(DEPTH):
        ex_own = _Exchange(_pick(LATER_USED, shards), scatter=False, layer=0) if i == 0 else None
        ex = _Exchange(shards, scatter=False, layer=i + 1) if i + 1 < DEPTH else None
        h, s, lw, gathered = _layer_fwd(i, h, p_loc[i], lw, rep[i], (tril, e_mat), ex_own, ex)
        saved.append(s)
        layer_w.append(lw)
        lw = _assemble(SHARDED, gathered)

    dh, loss_blk, dgf = _loss_head(h, final_norm_g.reshape(1, -1), loss_target[0], tt=1024, name="loss_head")
    loss = lax.psum(loss_blk[0, 0], ("x", "y", "c"))

    rep_grads = [None] * DEPTH
    pending, parts = None, None
    for i in reversed(range(DEPTH)):
        dh, g, parts = _layer_bwd(i, dh, p_loc[i], layer_w[i], rep[i], saved[i], (tril, triu, e_mat), pending, parts,
                                  own_early=(i == 0))
        pending = _grad_shards(SHARDED, g) if i > 0 else _grad_shards(LAST_MADE, g)
        rep_grads[i] = g
    parts = _put(LAST_MADE, parts, _exchange_call(
        _Exchange(pending, scatter=True, layer=0, into=_pick(LAST_MADE, parts)), "scatter_grads_0"))

    out = {}
    for k, part in zip(SHARDED, parts):
        out[k] = _sum_adamw(part, w[k], m[k], v[k], name=f"adamw_{k}")

    rp_grads = [dgf.reshape(-1) if k == "final_norm_g" else jnp.stack([rep_grads[i][k] for i in range(DEPTH)])
                for k in REPLICATED]
    rp_shapes = [w[k].shape for k in REPLICATED]
    rp_parts = _exchange_call(_Exchange([_pack(rp_grads)[None]], scatter=False, layer=0), "gather_replicated_grads")[0]
    rp_out = _sum_adamw(rp_parts[None], *[_pack([d[k] for k in REPLICATED])[None] for d in (w, m, v)],
                        name="adamw_replicated")
    for j in range(4):
        for k, arr in zip(REPLICATED, _unpack(rp_out[j][0], rp_shapes)):
            out.setdefault(k, [None] * 4)[j] = arr
    results = [out[k][j] for j in range(4) for k in WEIGHTS]
    return (loss, dh[None], *results)
```

```python
import math

import jax
import jax.numpy as jnp
from jax import lax
from jax.experimental import pallas as pl
from jax.experimental.pallas import tpu as pltpu

F32 = jnp.float32
BF16 = jnp.bfloat16

N_DEV = 8
EPS = 1e-6
DEPTH = 4
D_MODEL = 1024
D_PLE = 256
SSD_WIDTH = 512
SSD_HEADS = 8
SSD_HEAD_DIM = 64
SSD_GROUPS = 2
SSD_STATE = 128
SSD_CONV = 4
CHUNK = 128
SSD_XBC = 1024
POOL_WINDOWS = (2, 4, 8, 16)
POOL_WIDTH = 512
POOL_GROUP = 128
POOL_HALO = 16
D_IN_PROJ = 2056
D_FF = 2816
D_UP = 2 * D_FF
FFN_CONV = 3
SUBLANES = 8
LANES = 128
N_PROJ = 2176
COL_XBC, COL_Z, COL_U, COL_DT = 0, 1024, 1536, 2048
N_PAIRS = SSD_HEADS // 2
ADAM_LR, ADAM_B1, ADAM_B2, ADAM_EPS, ADAM_WD, ADAM_STEP = 0.001, 0.9, 0.999, 1e-08, 0.01, 10
GELU_C = math.sqrt(2.0 / math.pi)
GELU_A = 0.044715
VMEM_LIMIT = 56 * 1024 * 1024

NT_DIMS = (((1,), (1,)), ((), ()))
TN_DIMS = (((0,), (0,)), ((), ()))


def _params(*sem):
    return pltpu.CompilerParams(dimension_semantics=sem, vmem_limit_bytes=VMEM_LIMIT)


def _dot(a, b):
    return jnp.dot(a, b, preferred_element_type=F32)


def _dot_nt(a, b):
    return lax.dot_general(a, b, NT_DIMS, preferred_element_type=F32)


def _dot_tn(a, b):
    return lax.dot_general(a, b, TN_DIMS, preferred_element_type=F32)


def _split3(a):
    hi = a.astype(BF16)
    r1 = a - hi.astype(F32)
    mid = r1.astype(BF16)
    return hi, mid, (r1 - mid.astype(F32)).astype(BF16)


def _hdot(a, b):
    if a.dtype == BF16:
        return sum(_dot(a, piece) for piece in _split3(b))
    return sum(_dot(piece, b) for piece in _split3(a))


def _headsum(q, e):
    return sum(_dot_nt(piece, e) for piece in _split3(q))


def _colsum(v):
    return jnp.sum(v, axis=0, keepdims=True)


def _sigmoid(v):
    return 1.0 / (1.0 + jnp.exp(-v))


def _softplus(v):
    e = jnp.exp(-jnp.abs(v))
    return jnp.maximum(v, 0.0) + jnp.where(e < 1e-4, e * (1.0 - 0.5 * e), jnp.log(1.0 + e))


def _rms_r(x):
    return lax.rsqrt(jnp.mean(x * x, axis=-1, keepdims=True) + EPS)


def _rms_bwd(x, r, g, dn):
    xhat = x * r
    gd = dn * g
    dx = r * (gd - xhat * jnp.mean(gd * xhat, axis=-1, keepdims=True))
    return dx, _colsum(dn * xhat)


def _gelu(v):
    return 0.5 * v * (1.0 + jnp.tanh(GELU_C * (v + GELU_A * v * v * v)))


def _gelu_grad(v):
    th = jnp.tanh(GELU_C * (v + GELU_A * v * v * v))
    return 0.5 * (1.0 + th) + 0.5 * v * (1.0 - th * th) * GELU_C * (1.0 + 3.0 * GELU_A * v * v)


def _tile(t, want):
    return min(t, want)


class _Exchange:
    def __init__(self, srcs, *, scatter, layer, into=None):
        self.srcs, self.scatter, self.layer = list(srcs), scatter, layer
        self.into = None if into is None else list(into)
        n = len(self.srcs)
        self.args = self.srcs + (self.into or [])
        self.in_specs = [pl.BlockSpec(memory_space=pl.ANY)] * len(self.args)
        if scatter:
            self.out_shape = [jax.ShapeDtypeStruct((DEPTH,) + s.shape, s.dtype) for s in self.srcs]
        else:
            self.out_shape = [jax.ShapeDtypeStruct((N_DEV,) + s.shape[1:], s.dtype) for s in self.srcs]
        self.out_specs = [pl.BlockSpec(memory_space=pl.ANY)] * n
        self.scratch = [pltpu.SemaphoreType.DMA((n, N_DEV - 1)), pltpu.SemaphoreType.DMA((n, N_DEV - 1)),
                        pltpu.SemaphoreType.DMA((n,))]

    def aliases(self, n_in_before, n_out_before):
        if self.into is None:
            return {}
        n = len(self.srcs)
        return {n_in_before + n + a: n_out_before + a for a in range(n)}

    def ops(self, in_refs, out_refs, sems):
        send_sems, recv_sems, local_sems = sems
        n = len(self.srcs)

        def copies():
            x, y, c = lax.axis_index("x"), lax.axis_index("y"), lax.axis_index("c")
            me = 4 * x + 2 * y + c

            def block(a, idx):
                return in_refs[a].at[idx] if self.scatter else in_refs[a].at[self.layer]

            def slot(a, idx):
                return out_refs[a].at[self.layer].at[idx] if self.scatter else out_refs[a].at[idx]

            local = [pltpu.make_async_copy(block(a, me), slot(a, me), local_sems.at[a]) for a in range(n)]
            sends, recvs = [], []
            for k in range(1, N_DEV):
                px = 1 - x if k & 4 else x
                py = 1 - y if k & 2 else y
                pc = 1 - c if k & 1 else c
                peer = 4 * px + 2 * py + pc
                for a in range(n):
                    kw = dict(send_sem=send_sems.at[a, k - 1], recv_sem=recv_sems.at[a, k - 1], device_id=(px, py, pc),
                              device_id_type=pl.DeviceIdType.MESH)
                    sends.append(pltpu.make_async_remote_copy(src_ref=block(a, peer), dst_ref=slot(a, me), **kw))
                    recvs.append(pltpu.make_async_remote_copy(src_ref=block(a, peer), dst_ref=slot(a, peer), **kw))
            return local, sends, recvs

        def start():
            local, sends, _ = copies()
            for cp in local + sends:
                cp.start()

        def wait():
            local, sends, recvs = copies()
            for send, recv in zip(sends, recvs):
                send.wait_send()
                recv.wait_recv()
            for cp in local:
                cp.wait()

        return start, wait


def _exchange_call(ex, name):
    n_in, n = len(ex.args), len(ex.srcs)

    def body(*refs):
        start, wait = ex.ops(refs[:n_in], refs[n_in:n_in + n], refs[n_in + n:])
        start()
        wait()

    return pl.pallas_call(
        body, name=name, in_specs=ex.in_specs, out_specs=ex.out_specs, out_shape=ex.out_shape,
        scratch_shapes=ex.scratch, input_output_aliases=ex.aliases(0, 0))(*ex.args)


def _split_refs(refs, counts):
    out, k = [], 0
    for cnt in counts:
        out.append(refs[k:k + cnt])
        k += cnt
    return out


def _ex_parts(ex):
    if ex is None:
        return [], [], [], [], [], (0, 0, 0)
    return ex.args, ex.in_specs, ex.out_shape, ex.out_specs, ex.scratch, (len(ex.args), len(ex.srcs), 3)


def _norm_matmul(h, w, g, *, tt, tn, name):
    t, k = h.shape
    n = w.shape[1]
    tt, tn = _tile(t, tt), _tile(n, tn)

    def body(h_ref, g_ref, w_ref, o_ref):
        x = h_ref[...]
        o_ref[...] = _dot((x * _rms_r(x) * g_ref[...]).astype(BF16), w_ref[...])

    return pl.pallas_call(
        body, name=name, grid=(n // tn, t // tt),
        in_specs=[pl.BlockSpec((tt, k), lambda j, i: (i, 0)), pl.BlockSpec((1, k), lambda j, i: (0, 0)),
                  pl.BlockSpec((k, tn), lambda j, i: (0, j))],
        out_specs=pl.BlockSpec((tt, tn), lambda j, i: (i, j)), out_shape=jax.ShapeDtypeStruct((t, n), F32),
        compiler_params=_params("arbitrary", "arbitrary"))(h, g, w)


def _matmul_tn(a, b, *, tm, tn, tk, name):
    t, m = a.shape
    n = b.shape[1]
    tm, tn, tk = _tile(m, tm), _tile(n, tn), _tile(t, tk)

    def body(a_ref, b_ref, o_ref):
        @pl.when(pl.program_id(2) == 0)
        def _():
            o_ref[...] = jnp.zeros_like(o_ref)

        o_ref[...] += _dot_tn(a_ref[...].astype(BF16), b_ref[...].astype(BF16))

    return pl.pallas_call(
        body, name=name, grid=(m // tm, n // tn, t // tk),
        in_specs=[pl.BlockSpec((tk, tm), lambda i, j, kk: (kk, i)), pl.BlockSpec((tk, tn), lambda i, j, kk: (kk, j))],
        out_specs=pl.BlockSpec((tm, tn), lambda i, j, kk: (i, j)),
        out_shape=jax.ShapeDtypeStruct((m, n), F32),
        compiler_params=_params("arbitrary", "arbitrary", "arbitrary"))(a, b)


def _ssd_conv(i_is_first, xbc_ref, halo_ref, cw_ref, cb_ref, buf, ts):
    buf[0:SUBLANES, :] = jnp.where(i_is_first, 0.0, halo_ref[...])
    buf[SUBLANES:SUBLANES + ts, :] = xbc_ref[...]
    cw = cw_ref[...]
    xc = cb_ref[...]
    for k in range(SSD_CONV):
        off = SUBLANES - (SSD_CONV - 1) + k
        xc = xc + cw[k:k + 1, :] * buf[off:off + ts, :]
    return xc


def _ssd_tile_prologue(xc, dt_ref, dtb_ref, alog_ref, e_ref, xa_scr, a_scr, dte_scr, x_scr):
    xa_scr[...] = xc * _sigmoid(xc)
    dt = _softplus(dt_ref[...] + dtb_ref[...])
    a_neg = -jnp.exp(alog_ref[...])
    a_scr[...] = dt * a_neg
    dte = _hdot(dt, e_ref[...])
    dte_scr[...] = dte
    x_scr[...] = xa_scr[:, 0:SSD_WIDTH] * dte
    return dt, a_neg


def _chunk_decays(a_c, tril, e):
    cs = _hdot(tril, a_c)
    cs_t = cs.T
    cs_e = _hdot(cs, e)
    last_e = cs_e[CHUNK - 1:CHUNK, :]
    return cs, cs_t, cs_e, last_e


def _ssd_fwd(proj, cw, cb, dtb, alog, dexp, ng, tril, e, *, ts, name, ex=None):
    t = proj.shape[0]
    ts = _tile(t, ts)
    nch = ts // CHUNK
    hb = ts // SUBLANES
    nt = t // ts
    ex_args, ex_in_specs, ex_out_shape, ex_out_specs, ex_scratch, ex_counts = _ex_parts(ex)

    def body(*refs):
        ((xbc_ref, halo_ref, z_ref, dt_ref, cw_ref, cb_ref, dtb_ref, alog_ref, dexp_ref, ng_ref, tril_ref, e_ref),
         ex_in, (y_ref, ypre_ref, st_ref, xc_ref), ex_out, (buf, xa_scr, a_scr, dte_scr, x_scr, ys_scr, hstate),
         ex_sems) = _split_refs(refs, (12, ex_counts[0], 4, ex_counts[1], 7, ex_counts[2]))
        i = pl.program_id(0)
        if ex is not None:
            ex_start, ex_wait = ex.ops(ex_in, ex_out, ex_sems)
            pl.when(i == 0)(ex_start)

        @pl.when(i == 0)
        def _():
            hstate[...] = jnp.zeros_like(hstate)

        xc = _ssd_conv(i == 0, xbc_ref, halo_ref, cw_ref, cb_ref, buf, ts)
        xc_ref[...] = xc
        _ssd_tile_prologue(xc, dt_ref, dtb_ref, alog_ref, e_ref, xa_scr, a_scr, dte_scr, x_scr)
        tril = tril_ref[...]
        e_mat = e_ref[...]
        causal = (lax.broadcasted_iota(jnp.int32, (CHUNK, CHUNK), 0)
                  >= lax.broadcasted_iota(jnp.int32, (CHUNK, CHUNK), 1))
        lane = lax.broadcasted_iota(jnp.int32, (CHUNK, LANES), 1)

        def chunk(c, carry):
            r0 = pl.multiple_of(c * CHUNK, CHUNK)
            rows = pl.ds(r0, CHUNK)
            cs, cs_t, cs_e, last_e = _chunk_decays(a_scr[rows, :], tril, e_mat)
            decay_e = jnp.exp(last_e - cs_e)
            ecs_e = jnp.exp(cs_e)
            xc = x_scr[rows, :]
            xb = xc.astype(BF16)
            xd = (xc * decay_e).astype(BF16)
            for g in range(SSD_GROUPS):
                bg = xa_scr[rows, SSD_WIDTH + g * SSD_STATE:SSD_WIDTH + (g + 1) * SSD_STATE].astype(BF16)
                cg = xa_scr[rows, SSD_WIDTH + (SSD_GROUPS + g) * SSD_STATE:
                            SSD_WIDTH + (SSD_GROUPS + g + 1) * SSD_STATE].astype(BF16)
                cbm = _dot_nt(cg, bg)
                for jj in range(2):
                    j = 2 * g + jj
                    cols = slice(j * LANES, (j + 1) * LANES)
                    xp = xb[:, cols]
                    ypair = jnp.zeros((CHUNK, LANES), F32)
                    for hh in range(2):
                        h = 2 * j + hh
                        seg = jnp.exp(jnp.where(causal, cs[:, h:h + 1] - cs_t[h:h + 1, :], -jnp.inf))
                        m = (cbm * seg).astype(BF16)
                        half = (lane < SSD_HEAD_DIM) if hh == 0 else (lane >= SSD_HEAD_DIM)
                        ypair = ypair + _dot(m, jnp.where(half, xp, jnp.zeros_like(xp)))
                    hp = hstate[j]
                    st_ref[c, j] = hp
                    ypair = ypair + _dot(cg, hp.astype(BF16)) * ecs_e[:, cols]
                    ys_scr[rows, cols] = ypair
                    hstate[j] = hp * jnp.exp(last_e[:, cols]) + _dot_tn(bg, xd[:, cols])
            return carry

        lax.fori_loop(0, nch, chunk, 0, unroll=4)
        ypre = ys_scr[...] + xa_scr[:, 0:SSD_WIDTH] * dexp_ref[...]
        ypre_ref[...] = ypre
        z = z_ref[...]
        yg = ypre * (z * _sigmoid(z))
        gw = SSD_WIDTH // SSD_GROUPS
        outs = []
        for g in range(SSD_GROUPS):
            v = yg[:, g * gw:(g + 1) * gw]
            outs.append(v * _rms_r(v))
        y_ref[...] = jnp.concatenate(outs, axis=1) * ng_ref[...]
        if ex is not None:
            pl.when(i == nt - 1)(ex_wait)

    full = lambda shape: pl.BlockSpec(shape, lambda i: tuple(0 for _ in shape))
    outs = pl.pallas_call(
        body, name=name, grid=(nt,),
        in_specs=[pl.BlockSpec((ts, SSD_XBC), lambda i: (i, COL_XBC // SSD_XBC)),
                  pl.BlockSpec((SUBLANES, SSD_XBC), lambda i: (jnp.maximum(i * hb - 1, 0), COL_XBC // SSD_XBC)),
                  pl.BlockSpec((ts, SSD_WIDTH), lambda i: (i, COL_Z // SSD_WIDTH)),
                  pl.BlockSpec((ts, LANES), lambda i: (i, COL_DT // LANES)),
                  full((SSD_CONV, SSD_XBC)), full((1, SSD_XBC)), full((1, LANES)), full((1, LANES)),
                  full((1, SSD_WIDTH)), full((1, SSD_WIDTH)), full((CHUNK, CHUNK)), full((LANES, SSD_WIDTH))]
        + ex_in_specs,
        out_specs=[pl.BlockSpec((ts, SSD_WIDTH), lambda i: (i, 0)), pl.BlockSpec((ts, SSD_WIDTH), lambda i: (i, 0)),
                   pl.BlockSpec((nch, N_PAIRS, SSD_STATE, LANES), lambda i: (i, 0, 0, 0)),
                   pl.BlockSpec((ts, SSD_XBC), lambda i: (i, 0))] + ex_out_specs,
        out_shape=[jax.ShapeDtypeStruct((t, SSD_WIDTH), F32), jax.ShapeDtypeStruct((t, SSD_WIDTH), F32),
                   jax.ShapeDtypeStruct((t // CHUNK, N_PAIRS, SSD_STATE, LANES), F32),
                   jax.ShapeDtypeStruct((t, SSD_XBC), F32)] + ex_out_shape,
        scratch_shapes=[pltpu.VMEM((SUBLANES + ts, SSD_XBC), F32), pltpu.VMEM((ts, SSD_XBC), F32),
                        pltpu.VMEM((ts, LANES), F32), pltpu.VMEM((ts, SSD_WIDTH), F32),
                        pltpu.VMEM((ts, SSD_WIDTH), F32), pltpu.VMEM((ts, SSD_WIDTH), F32),
                        pltpu.VMEM((N_PAIRS, SSD_STATE, LANES), F32)] + ex_scratch,
        input_output_aliases={} if ex is None else ex.aliases(12, 4),
        compiler_params=_params("arbitrary"))(proj, proj, proj, proj, cw, cb, dtb, alog, dexp, ng, tril, e, *ex_args)
    return outs[0], outs[1], outs[2], outs[3], outs[4:]


def _ssd_bwd(dymix, proj, xc, ypre, states, du, cw, dtb, alog, dexp, ng, tril, triu, e, *, ts, name, ex=None):
    t = proj.shape[0]
    ts = _tile(t, ts)
    nch = ts // CHUNK
    nt = t // ts
    ex_args, ex_in_specs, ex_out_shape, ex_out_specs, ex_scratch, ex_counts = _ex_parts(ex)

    def body(*refs):
        ((dy_ref, xbc_ref, xc_ref, z_ref, dt_ref, ypre_ref, st_ref, du_ref, cw_ref, dtb_ref, alog_ref,
          dexp_ref, ng_ref, tril_ref, triu_ref, e_ref), ex_in,
         (dproj_ref, dcw_ref, dcb_ref, ddtb_ref, dalog_ref, dd_ref, dng_ref), ex_out,
         (xa_scr, a_scr, dte_scr, x_scr, dyp_scr, dxa_scr, dx_scr, dbuf, carry, gstate),
         ex_sems) = _split_refs(refs, (16, ex_counts[0], 7, ex_counts[1], 10, ex_counts[2]))
        i = pl.program_id(0)
        if ex is not None:
            ex_start, ex_wait = ex.ops(ex_in, ex_out, ex_sems)
            pl.when(i == 0)(ex_start)

        @pl.when(i == 0)
        def _():
            gstate[...] = jnp.zeros_like(gstate)
            carry[...] = jnp.zeros_like(carry)
            for ref in (dcw_ref, dcb_ref, ddtb_ref, dalog_ref, dd_ref, dng_ref):
                ref[...] = jnp.zeros_like(ref)

        dt, a_neg = _ssd_tile_prologue(xc_ref[...], dt_ref, dtb_ref, alog_ref, e_ref, xa_scr, a_scr, dte_scr, x_scr)
        tril = tril_ref[...]
        triu = triu_ref[...]
        e_mat = e_ref[...]
        causal = (lax.broadcasted_iota(jnp.int32, (CHUNK, CHUNK), 0)
                  >= lax.broadcasted_iota(jnp.int32, (CHUNK, CHUNK), 1))
        lane = lax.broadcasted_iota(jnp.int32, (CHUNK, LANES), 1)
        sub = lax.broadcasted_iota(jnp.int32, (CHUNK, LANES), 0)

        z = z_ref[...]
        sig = _sigmoid(z)
        zs = z * sig
        ypre = ypre_ref[...]
        yg = ypre * zs
        dout = dy_ref[...]
        ngv = ng_ref[...]
        gw = SSD_WIDTH // SSD_GROUPS
        dyg_parts, dng_parts = [], []
        for g in range(SSD_GROUPS):
            cols = slice(g * gw, (g + 1) * gw)
            v = yg[:, cols]
            dx, dg = _rms_bwd(v, _rms_r(v), ngv[:, cols], dout[:, cols])
            dyg_parts.append(dx)
            dng_parts.append(dg)
        dyg = jnp.concatenate(dyg_parts, axis=1)
        dng_ref[...] += jnp.concatenate(dng_parts, axis=1)
        dyp = dyg * zs
        dyp_scr[...] = dyp
        dproj_ref[:, COL_Z:COL_Z + SSD_WIDTH] = dyg * ypre * (sig * (1.0 + z * (1.0 - sig)))
        dproj_ref[:, COL_U:COL_U + POOL_WIDTH] = du_ref[...]
        xs_all = xa_scr[:, 0:SSD_WIDTH]
        dd_ref[...] += _headsum(jnp.broadcast_to(_colsum(dyp * xs_all), (SUBLANES, SSD_WIDTH)), e_mat)[0:1, :]

        def chunk(k, carry_):
            c = nch - 1 - k
            r0 = pl.multiple_of(c * CHUNK, CHUNK)
            rows = pl.ds(r0, CHUNK)
            a_c = a_scr[rows, :]
            cs, cs_t, cs_e, last_e = _chunk_decays(a_c, tril, e_mat)
            decay_e = jnp.exp(last_e - cs_e)
            ecs_e = jnp.exp(cs_e)
            elast_e = jnp.exp(last_e)
            xc = x_scr[rows, :]
            xb = xc.astype(BF16)
            xd = (xc * decay_e).astype(BF16)
            dyc = dyp_scr[rows, :]
            dcs = jnp.zeros((CHUNK, LANES), F32)
            dcs_neg_t = jnp.zeros((LANES, CHUNK), F32)
            qoff, rin, ghrow = [], [], []
            for g in range(SSD_GROUPS):
                b_cols = slice(SSD_WIDTH + g * SSD_STATE, SSD_WIDTH + (g + 1) * SSD_STATE)
                c_cols = slice(SSD_WIDTH + (SSD_GROUPS + g) * SSD_STATE, SSD_WIDTH + (SSD_GROUPS + g + 1) * SSD_STATE)
                bg = xa_scr[rows, b_cols].astype(BF16)
                cg = xa_scr[rows, c_cols].astype(BF16)
                cbm = _dot_nt(cg, bg)
                dcb_m = jnp.zeros((CHUNK, CHUNK), F32)
                dbg = jnp.zeros((CHUNK, SSD_STATE), F32)
                dcg = jnp.zeros((CHUNK, SSD_STATE), F32)
                for jj in range(2):
                    j = 2 * g + jj
                    cols = slice(j * LANES, (j + 1) * LANES)
                    dyp_j = dyc[:, cols]
                    hp = st_ref[c, j]
                    hpb = hp.astype(BF16)
                    gt = gstate[j]
                    gtb = gt.astype(BF16)
                    ecs = ecs_e[:, cols]
                    yoff = _dot(cg, hpb) * ecs
                    dye = (dyp_j * ecs).astype(BF16)
                    dcg = dcg + _dot_nt(dye, hpb)
                    dht = _dot_tn(cg, dye)
                    qoff.append(dyp_j * yoff)
                    xg = _dot(bg, gtb)
                    dxp = xg * decay_e[:, cols]
                    rin.append(xg * xc[:, cols])
                    dbg = dbg + _dot_nt(xd[:, cols], gtb)
                    ghrow.append(_colsum(gt * hp) * elast_e[:, cols])
                    gstate[j] = dht + gt * elast_e[:, cols]
                    for hh in range(2):
                        h = 2 * j + hh
                        seg = jnp.exp(jnp.where(causal, cs[:, h:h + 1] - cs_t[h:h + 1, :], -jnp.inf))
                        m = cbm * seg
                        half = (lane < SSD_HEAD_DIM) if hh == 0 else (lane >= SSD_HEAD_DIM)
                        dym = jnp.where(half, dyp_j, 0.0).astype(BF16)
                        w = _dot_nt(dym, xb[:, cols])
                        pm = w * m
                        dcs = dcs + jnp.where(lane == h, jnp.sum(pm, axis=1, keepdims=True), 0.0)
                        dcs_neg_t = dcs_neg_t + jnp.where(sub == h, _colsum(pm), 0.0)
                        dcb_m = dcb_m + w * seg
                        dxp = dxp + _dot_tn(m.astype(BF16), dym)
                    dx_scr[:, cols] = dxp
                dcbb = dcb_m.astype(BF16)
                dxa_scr[rows, c_cols] = dcg + _dot(dcbb, bg)
                dxa_scr[rows, b_cols] = dbg + _dot_tn(dcbb, cg)
            decay_th = jnp.exp(cs[CHUNK - 1:CHUNK, :] - cs)
            rd = _headsum(jnp.concatenate(rin, axis=1), e_mat) * decay_th
            dcs = dcs - dcs_neg_t.T + _headsum(jnp.concatenate(qoff, axis=1), e_mat) - rd
            gh = _headsum(jnp.broadcast_to(jnp.concatenate(ghrow, axis=1), (SUBLANES, SSD_WIDTH)), e_mat)[0:1, :]
            dcs = dcs + jnp.where(sub == CHUNK - 1, _colsum(rd) + gh, 0.0)
            da = _hdot(triu, dcs)
            dx_all = dx_scr[...]
            xs = xa_scr[rows, 0:SSD_WIDTH]
            dt_c = _softplus(dt_ref[rows, :] + dtb_ref[...])
            ddt = da * a_neg + _headsum(dx_all * xs, e_mat)
            dalog_ref[...] += _colsum(da * dt_c) * a_neg
            ddtraw = ddt * _sigmoid(dt_ref[rows, :] + dtb_ref[...])
            dproj_ref[rows, COL_DT:COL_DT + LANES] = ddtraw
            ddtb_ref[...] += _colsum(ddtraw)
            dxa_scr[rows, 0:SSD_WIDTH] = dx_all * dte_scr[rows, :] + dyc * dexp_ref[...]
            return carry_

        lax.fori_loop(0, nch, chunk, 0, unroll=4)

        xcv = xc_ref[...]
        sgc = _sigmoid(xcv)
        dxc = dxa_scr[...] * (sgc * (1.0 + xcv * (1.0 - sgc)))
        dcb_ref[...] += _colsum(dxc)
        dbuf[0:ts, :] = dxc
        dbuf[ts:ts + SUBLANES, :] = carry[...]
        cwv = cw_ref[...]
        xbc = xbc_ref[...]
        dxbc = jnp.zeros((ts, SSD_XBC), F32)
        dcw_rows = []
        for k in range(SSD_CONV):
            back = SSD_CONV - 1 - k
            shifted = dbuf[back:back + ts, :] if back else dxc
            dcw_rows.append(_colsum(shifted * xbc))
            dxbc = dxbc + cwv[k:k + 1, :] * shifted
        dcw_ref[...] += jnp.concatenate(dcw_rows, axis=0)
        dproj_ref[:, COL_XBC:COL_XBC + SSD_XBC] = dxbc
        carry[...] = dxc[0:SUBLANES, :]
        if ex is not None:
            pl.when(i == nt - 1)(ex_wait)

    rev = lambda i: nt - 1 - i
    full = lambda shape: pl.BlockSpec(shape, lambda i: tuple(0 for _ in shape))
    outs = pl.pallas_call(
        body, name=name, grid=(nt,),
        in_specs=[pl.BlockSpec((ts, SSD_WIDTH), lambda i: (rev(i), 0)),
                  pl.BlockSpec((ts, SSD_XBC), lambda i: (rev(i), COL_XBC // SSD_XBC)),
                  pl.BlockSpec((ts, SSD_XBC), lambda i: (rev(i), 0)),
                  pl.BlockSpec((ts, SSD_WIDTH), lambda i: (rev(i), COL_Z // SSD_WIDTH)),
                  pl.BlockSpec((ts, LANES), lambda i: (rev(i), COL_DT // LANES)),
                  pl.BlockSpec((ts, SSD_WIDTH), lambda i: (rev(i), 0)),
                  pl.BlockSpec((nch, N_PAIRS, SSD_STATE, LANES), lambda i: (rev(i), 0, 0, 0)),
                  pl.BlockSpec((ts, POOL_WIDTH), lambda i: (rev(i), 0)),
                  full((SSD_CONV, SSD_XBC)), full((1, LANES)), full((1, LANES)),
                  full((1, SSD_WIDTH)), full((1, SSD_WIDTH)), full((CHUNK, CHUNK)), full((CHUNK, CHUNK)),
                  full((LANES, SSD_WIDTH))] + ex_in_specs,
        out_specs=[pl.BlockSpec((ts, N_PROJ), lambda i: (rev(i), 0)),
                   full((SSD_CONV, SSD_XBC)), full((1, SSD_XBC)), full((1, LANES)), full((1, LANES)),
                   full((1, LANES)), full((1, SSD_WIDTH))] + ex_out_specs,
        out_shape=[jax.ShapeDtypeStruct((t, N_PROJ), F32),
                   jax.ShapeDtypeStruct((SSD_CONV, SSD_XBC), F32), jax.ShapeDtypeStruct((1, SSD_XBC), F32),
                   jax.ShapeDtypeStruct((1, LANES), F32), jax.ShapeDtypeStruct((1, LANES), F32),
                   jax.ShapeDtypeStruct((1, LANES), F32), jax.ShapeDtypeStruct((1, SSD_WIDTH), F32)] + ex_out_shape,
        scratch_shapes=[pltpu.VMEM((ts, SSD_XBC), F32), pltpu.VMEM((ts, LANES), F32),
                        pltpu.VMEM((ts, SSD_WIDTH), F32), pltpu.VMEM((ts, SSD_WIDTH), F32),
                        pltpu.VMEM((ts, SSD_WIDTH), F32), pltpu.VMEM((ts, SSD_XBC), F32),
                        pltpu.VMEM((CHUNK, SSD_WIDTH), F32), pltpu.VMEM((ts + SUBLANES, SSD_XBC), F32),
                        pltpu.VMEM((SUBLANES, SSD_XBC), F32), pltpu.VMEM((N_PAIRS, SSD_STATE, LANES), F32)]
        + ex_scratch,
        input_output_aliases={} if ex is None else ex.aliases(16, 7),
        compiler_params=_params("arbitrary"))(
            dymix, proj, xc, proj, proj, ypre, states, du, cw, dtb, alog, dexp, ng, tril, triu, e, *ex_args)
    return outs[:7], outs[7:]


def _pooled(ubuf, u, pos, tt):
    out = []
    for gi, w in enumerate(POOL_WINDOWS):
        cols = slice(gi * POOL_GROUP, (gi + 1) * POOL_GROUP)
        acc = u[:, cols]
        for j in range(1, w):
            acc = acc + ubuf[POOL_HALO - j:POOL_HALO - j + tt, cols]
        out.append(acc / jnp.minimum(pos, float(w)) - u[:, cols])
    return out


def _mix_out(h, yssd, proj, pool_w, pool_scale, w_out, g_next, *, tt, name):
    t = h.shape[0]
    tt = _tile(t, tt)
    hb = tt // POOL_HALO

    def body(h_ref, ys_ref, u_ref, uh_ref, pw_ref, sc_ref, wo_ref, gn_ref, o_ref, ym_ref, n_ref, ubuf):
        i = pl.program_id(0)
        ubuf[0:POOL_HALO, :] = jnp.where(i == 0, 0.0, uh_ref[...])
        u = u_ref[...]
        ubuf[POOL_HALO:POOL_HALO + tt, :] = u
        pos = (i * tt + 1 + lax.broadcasted_iota(jnp.int32, (tt, 1), 0)).astype(F32)
        sc = sc_ref[...]
        parts = [ys_ref[...]]
        for gi, pooled in enumerate(_pooled(ubuf, u, pos, tt)):
            cols = slice(gi * POOL_GROUP, (gi + 1) * POOL_GROUP)
            parts.append(_dot(pooled.astype(BF16), pw_ref[gi]) * sc[:, cols])
        ymix = jnp.concatenate(parts, axis=1).astype(BF16)
        ym_ref[...] = ymix
        h2 = h_ref[...] + _dot(ymix, wo_ref[...])
        o_ref[...] = h2
        n_ref[...] = (h2 * _rms_r(h2) * gn_ref[...]).astype(BF16)

    full = lambda shape: pl.BlockSpec(shape, lambda i: tuple(0 for _ in shape))
    return pl.pallas_call(
        body, name=name, grid=(t // tt,),
        in_specs=[pl.BlockSpec((tt, D_MODEL), lambda i: (i, 0)), pl.BlockSpec((tt, SSD_WIDTH), lambda i: (i, 0)),
                  pl.BlockSpec((tt, POOL_WIDTH), lambda i: (i, COL_U // POOL_WIDTH)),
                  pl.BlockSpec((POOL_HALO, POOL_WIDTH), lambda i: (jnp.maximum(i * hb - 1, 0), COL_U // POOL_WIDTH)),
                  full((len(POOL_WINDOWS), POOL_GROUP, POOL_GROUP)), full((1, POOL_WIDTH)),
                  full((D_MODEL, D_MODEL)), full((1, D_MODEL))],
        out_specs=[pl.BlockSpec((tt, D_MODEL), lambda i: (i, 0))] * 3,
        out_shape=[jax.ShapeDtypeStruct((t, D_MODEL), F32), jax.ShapeDtypeStruct((t, D_MODEL), BF16),
                   jax.ShapeDtypeStruct((t, D_MODEL), BF16)],
        scratch_shapes=[pltpu.VMEM((POOL_HALO + tt, POOL_WIDTH), F32)],
        compiler_params=_params("arbitrary"))(h, yssd, proj, proj, pool_w, pool_scale, w_out, g_next)


def _out_bwd(dh, ymix, w_out_t, *, tt, name):
    t = dh.shape[0]
    tt = _tile(t, tt)

    def body(dh_ref, ym_ref, wt_ref, dym_ref, dw_ref):
        @pl.when(pl.program_id(0) == 0)
        def _():
            dw_ref[...] = jnp.zeros_like(dw_ref)

        dhb = dh_ref[...].astype(BF16)
        dym_ref[...] = _dot(dhb, wt_ref[...])
        dw_ref[...] += _dot_tn(ym_ref[...], dhb)

    return pl.pallas_call(
        body, name=name, grid=(t // tt,),
        in_specs=[pl.BlockSpec((tt, D_MODEL), lambda i: (i, 0)), pl.BlockSpec((tt, D_MODEL), lambda i: (i, 0)),
                  pl.BlockSpec((D_MODEL, D_MODEL), lambda i: (0, 0))],
        out_specs=[pl.BlockSpec((tt, D_MODEL), lambda i: (i, 0)), pl.BlockSpec((D_MODEL, D_MODEL), lambda i: (0, 0))],
        out_shape=[jax.ShapeDtypeStruct((t, D_MODEL), F32), jax.ShapeDtypeStruct((D_MODEL, D_MODEL), F32)],
        compiler_params=_params("arbitrary"))(dh, ymix, w_out_t)


def _pool_bwd(dymix, proj, pool_w, pool_w_t, pool_scale, *, tt, name):
    t = proj.shape[0]
    tt = _tile(t, tt)
    hb = tt // POOL_HALO
    nt = t // tt
    ng = len(POOL_WINDOWS)

    def body(dy_ref, dyh_ref, u_ref, uh_ref, pw_ref, pwt_ref, sc_ref, du_ref, dpw_ref, dsc_ref, ubuf, dbuf):
        i = pl.program_id(0)

        @pl.when(i == 0)
        def _():
            dpw_ref[...] = jnp.zeros_like(dpw_ref)
            dsc_ref[...] = jnp.zeros_like(dsc_ref)

        ubuf[0:POOL_HALO, :] = jnp.where(i == 0, 0.0, uh_ref[...])
        u = u_ref[...]
        ubuf[POOL_HALO:POOL_HALO + tt, :] = u
        pos = (i * tt + 1 + lax.broadcasted_iota(jnp.int32, (tt, 1), 0)).astype(F32)
        sc = sc_ref[...]
        dy = dy_ref[...]
        dyh = jnp.where(i == nt - 1, 0.0, dyh_ref[...])
        dsc_parts, du_parts = [], []
        for gi, pooled in enumerate(_pooled(ubuf, u, pos, tt)):
            w = POOL_WINDOWS[gi]
            cols = slice(gi * POOL_GROUP, (gi + 1) * POOL_GROUP)
            pb = pooled.astype(BF16)
            dsc_parts.append(_colsum(dy[:, cols] * _dot(pb, pw_ref[gi])))
            dmx = (dy[:, cols] * sc[:, cols]).astype(BF16)
            dpw_ref[gi] += _dot_tn(pb, dmx)
            dpool = _dot(dmx, pwt_ref[gi])
            dpool_h = _dot((dyh[:, cols] * sc[:, cols]).astype(BF16), pwt_ref[gi])
            dbuf[0:tt, cols] = dpool / jnp.minimum(pos, float(w))
            dbuf[tt:tt + POOL_HALO, cols] = dpool_h / float(w)
            acc = -dpool
            for j in range(w):
                acc = acc + dbuf[j:j + tt, cols]
            du_parts.append(acc)
        du_ref[...] = jnp.concatenate(du_parts, axis=1)
        dsc_ref[...] += jnp.concatenate(dsc_parts, axis=1)

    full = lambda shape: pl.BlockSpec(shape, lambda i: tuple(0 for _ in shape))
    ucol = COL_U // POOL_WIDTH
    return pl.pallas_call(
        body, name=name, grid=(nt,),
        in_specs=[pl.BlockSpec((tt, POOL_WIDTH), lambda i: (i, 1)),
                  pl.BlockSpec((POOL_HALO, POOL_WIDTH), lambda i: (jnp.minimum((i + 1) * hb, t // POOL_HALO - 1), 1)),
                  pl.BlockSpec((tt, POOL_WIDTH), lambda i: (i, ucol)),
                  pl.BlockSpec((POOL_HALO, POOL_WIDTH), lambda i: (jnp.maximum(i * hb - 1, 0), ucol)),
                  full((ng, POOL_GROUP, POOL_GROUP)), full((ng, POOL_GROUP, POOL_GROUP)), full((1, POOL_WIDTH))],
        out_specs=[pl.BlockSpec((tt, POOL_WIDTH), lambda i: (i, 0)), full((ng, POOL_GROUP, POOL_GROUP)),
                   full((1, POOL_WIDTH))],
        out_shape=[jax.ShapeDtypeStruct((t, POOL_WIDTH), F32), jax.ShapeDtypeStruct((ng, POOL_GROUP, POOL_GROUP), F32),
                   jax.ShapeDtypeStruct((1, POOL_WIDTH), F32)],
        scratch_shapes=[pltpu.VMEM((POOL_HALO + tt, POOL_WIDTH), F32), pltpu.VMEM((tt + POOL_HALO, POOL_WIDTH), F32)],
        compiler_params=_params("arbitrary"))(dymix, dymix, proj, proj, pool_w, pool_w_t, pool_scale)


def _in_bwd(dproj, h, g, w_in_t, dh, *, tt, name):
    t = h.shape[0]
    tt = _tile(t, tt)

    def body(dp_ref, h_ref, g_ref, wt_ref, dh_ref, o_ref, dw_ref, dg_ref):
        @pl.when(pl.program_id(0) == 0)
        def _():
            dw_ref[...] = jnp.zeros_like(dw_ref)
            dg_ref[...] = jnp.zeros_like(dg_ref)

        x = h_ref[...]
        r = _rms_r(x)
        gv = g_ref[...]
        dpb = dp_ref[...].astype(BF16)
        dw_ref[...] += _dot_tn((x * r * gv).astype(BF16), dpb)
        dx, dg = _rms_bwd(x, r, gv, _dot(dpb, wt_ref[...]))
        o_ref[...] = dh_ref[...] + dx
        dg_ref[...] += dg

    full = lambda shape: pl.BlockSpec(shape, lambda i: tuple(0 for _ in shape))
    row = lambda n: pl.BlockSpec((tt, n), lambda i: (i, 0))
    return pl.pallas_call(
        body, name=name, grid=(t // tt,),
        in_specs=[row(N_PROJ), row(D_MODEL), full((1, D_MODEL)), full((N_PROJ, D_MODEL)), row(D_MODEL)],
        out_specs=[row(D_MODEL), full((D_MODEL, N_PROJ)), full((1, D_MODEL))],
        out_shape=[jax.ShapeDtypeStruct((t, D_MODEL), F32), jax.ShapeDtypeStruct((D_MODEL, N_PROJ), F32),
                   jax.ShapeDtypeStruct((1, D_MODEL), F32)],
        compiler_params=_params("arbitrary"))(dproj, h, g, w_in_t, dh)


FFN_COLS = 256
N_SLABS = D_FF // FFN_COLS
N_SLAB_BUFS = 4


def _ffn_fwd(h, n2, w_up, cw, cb, w_down, *, tt, name, ex=None):
    t = h.shape[0]
    tt = _tile(t, tt)
    nt = t // tt
    ex_args, ex_in_specs, ex_out_shape, ex_out_specs, ex_scratch, ex_counts = _ex_parts(ex)

    def body(*refs):
        ((h_ref, n2_ref, wu_ref, cw_ref, cb_ref, wd_ref), ex_in, (o_ref, act_ref, pre_ref, up_ref), ex_out,
         (slab, halo), ex_sems) = _split_refs(refs, (6, ex_counts[0], 4, ex_counts[1], 2, ex_counts[2]))
        i = pl.program_id(0)
        if ex is not None:
            ex_start, ex_wait = ex.ops(ex_in, ex_out, ex_sems)
            pl.when(i == 0)(ex_start)

        @pl.when(i == 0)
        def _():
            halo[...] = jnp.zeros_like(halo)

        n2v = n2_ref[...]

        def slab_cols(s):
            return slice(s * FFN_COLS, (s + 1) * FFN_COLS), slice(D_FF + s * FFN_COLS, D_FF + (s + 1) * FFN_COLS)

        def project(s):
            return [_dot(n2v, wu_ref[:, cols]) for cols in slab_cols(s)]

        def conv(u, cols, buf_id):
            up_ref[:, cols] = u.astype(BF16)
            sb = slab.at[buf_id]
            sb[0:SUBLANES, :] = halo[:, cols]
            sb[SUBLANES:SUBLANES + tt, :] = u
            halo[:, cols] = u[tt - SUBLANES:tt, :]
            acc = cb_ref[:, cols] + cw_ref[FFN_CONV - 1:FFN_CONV, cols] * u
            for k in range(FFN_CONV - 1):
                off = SUBLANES - (FFN_CONV - 1) + k
                acc = acc + cw_ref[k:k + 1, cols] * sb[off:off + tt, :]
            pre_ref[:, cols] = acc
            return acc

        out = h_ref[...]
        ahead = project(0)
        for s in range(N_SLABS):
            (ug, uv), (gcols, vcols) = ahead, slab_cols(s)
            if s + 1 < N_SLABS:
                ahead = project(s + 1)
            gate = conv(ug, gcols, (2 * s) % N_SLAB_BUFS)
            val = conv(uv, vcols, (2 * s + 1) % N_SLAB_BUFS)
            act = (_gelu(gate) * val).astype(BF16)
            act_ref[:, s * FFN_COLS:(s + 1) * FFN_COLS] = act
            out = out + _dot(act, wd_ref[s * FFN_COLS:(s + 1) * FFN_COLS, :])
        o_ref[...] = out
        if ex is not None:
            pl.when(i == nt - 1)(ex_wait)

    full = lambda shape: pl.BlockSpec(shape, lambda i: tuple(0 for _ in shape))
    row = lambda n: pl.BlockSpec((tt, n), lambda i: (i, 0))
    outs = pl.pallas_call(
        body, name=name, grid=(nt,),
        in_specs=[row(D_MODEL), row(D_MODEL), full((D_MODEL, D_UP)), full((FFN_CONV, D_UP)), full((1, D_UP)),
                  full((D_FF, D_MODEL))] + ex_in_specs,
        out_specs=[row(D_MODEL), row(D_FF), row(D_UP), row(D_UP)] + ex_out_specs,
        out_shape=[jax.ShapeDtypeStruct((t, D_MODEL), F32), jax.ShapeDtypeStruct((t, D_FF), BF16),
                   jax.ShapeDtypeStruct((t, D_UP), F32), jax.ShapeDtypeStruct((t, D_UP), BF16)] + ex_out_shape,
        scratch_shapes=[pltpu.VMEM((N_SLAB_BUFS, SUBLANES + tt, FFN_COLS), F32), pltpu.VMEM((SUBLANES, D_UP), F32)]
        + ex_scratch,
        input_output_aliases={} if ex is None else ex.aliases(6, 4),
        compiler_params=_params("arbitrary"))(h, n2, w_up, cw, cb, w_down, *ex_args)
    return outs[0], outs[1], outs[2], outs[3], outs[4:]


def _ffn_bwd(dh, up, pre, h2, g2, w_down_t, w_up_t, cw, *, tt, name, ex=None):
    t = dh.shape[0]
    tt = _tile(t, tt)
    nt = t // tt
    ex_args, ex_in_specs, ex_out_shape, ex_out_specs, ex_scratch, ex_counts = _ex_parts(ex)

    def body(*refs):
        ((dh_ref, up_ref, pre_ref, h2_ref, g2_ref, wdt_ref, wut_ref, cw_ref), ex_in,
         (o_ref, dup_ref, dcw_ref, dcb_ref, dg_ref), ex_out, (slab, carry), ex_sems) = _split_refs(
            refs, (8, ex_counts[0], 5, ex_counts[1], 2, ex_counts[2]))
        i = pl.program_id(0)
        if ex is not None:
            ex_start, ex_wait = ex.ops(ex_in, ex_out, ex_sems)
            pl.when(i == 0)(ex_start)

        @pl.when(i == 0)
        def _():
            for ref in (dcw_ref, dcb_ref, dg_ref, carry):
                ref[...] = jnp.zeros_like(ref)

        dhv = dh_ref[...]
        dhb = dhv.astype(BF16)

        def slab_cols(s):
            return slice(s * FFN_COLS, (s + 1) * FFN_COLS), slice(D_FF + s * FFN_COLS, D_FF + (s + 1) * FFN_COLS)

        def d_act(s):
            return _dot(dhb, wdt_ref[:, slab_cols(s)[0]])

        def through_conv(dp, cols, buf_id):
            sb = slab.at[buf_id]
            sb[0:tt, :] = dp
            sb[tt:tt + SUBLANES, :] = carry[:, cols]
            carry[:, cols] = dp[0:SUBLANES, :]
            shifted = [sb[FFN_CONV - 1 - k:FFN_CONV - 1 - k + tt, :] for k in range(FFN_CONV - 1)] + [dp]
            x = up_ref[:, cols].astype(F32)
            dup = cw_ref[0:1, cols] * shifted[0]
            for k in range(1, FFN_CONV):
                dup = dup + cw_ref[k:k + 1, cols] * shifted[k]
            dcb_ref[:, cols] += _colsum(dp)
            dcw_ref[:, cols] += jnp.concatenate([_colsum(sh * x) for sh in shifted], axis=0)
            dupb = dup.astype(BF16)
            dup_ref[:, cols] = dupb
            return dupb

        def project(dups, first, last, base):
            lhs = dups[0] if len(dups) == 1 else jnp.concatenate(dups, axis=1)
            return _dot(lhs, wut_ref[base + first * FFN_COLS:base + (last + 1) * FFN_COLS, :])

        dn = jnp.zeros((tt, D_MODEL), F32)
        ahead = d_act(0)
        held = None
        for s in range(N_SLABS):
            da, (gcols, vcols) = ahead, slab_cols(s)
            if s + 1 < N_SLABS:
                ahead = d_act(s + 1)
            gate, val = pre_ref[:, gcols], pre_ref[:, vcols]
            dup_g = through_conv(da * val * _gelu_grad(gate), gcols, (2 * s) % N_SLAB_BUFS)
            dup_v = through_conv(da * _gelu(gate), vcols, (2 * s + 1) % N_SLAB_BUFS)
            if held is None and s + 1 < N_SLABS:
                held = (dup_g, dup_v)
                continue
            first = s if held is None else s - 1
            dn = dn + project(([] if held is None else [held[0]]) + [dup_g], first, s, 0)
            dn = dn + project(([] if held is None else [held[1]]) + [dup_v], first, s, D_FF)
            held = None
        xv = h2_ref[...]
        dx, dg = _rms_bwd(xv, _rms_r(xv), g2_ref[...], dn)
        o_ref[...] = dhv + dx
        dg_ref[...] += dg
        if ex is not None:
            pl.when(i == nt - 1)(ex_wait)

    rev = lambda i: nt - 1 - i
    full = lambda shape: pl.BlockSpec(shape, lambda i: tuple(0 for _ in shape))
    row = lambda n: pl.BlockSpec((tt, n), lambda i: (rev(i), 0))
    outs = pl.pallas_call(
        body, name=name, grid=(nt,),
        in_specs=[row(D_MODEL), row(D_UP), row(D_UP), row(D_MODEL), full((1, D_MODEL)), full((D_MODEL, D_FF)),
                  full((D_UP, D_MODEL)), full((FFN_CONV, D_UP))] + ex_in_specs,
        out_specs=[row(D_MODEL), row(D_UP), full((FFN_CONV, D_UP)), full((1, D_UP)), full((1, D_MODEL))]
        + ex_out_specs,
        out_shape=[jax.ShapeDtypeStruct((t, D_MODEL), F32), jax.ShapeDtypeStruct((t, D_UP), BF16),
                   jax.ShapeDtypeStruct((FFN_CONV, D_UP), F32), jax.ShapeDtypeStruct((1, D_UP), F32),
                   jax.ShapeDtypeStruct((1, D_MODEL), F32)] + ex_out_shape,
        scratch_shapes=[pltpu.VMEM((N_SLAB_BUFS, tt + SUBLANES, FFN_COLS), F32), pltpu.VMEM((SUBLANES, D_UP), F32)]
        + ex_scratch,
        input_output_aliases={} if ex is None else ex.aliases(8, 5),
        compiler_params=_params("arbitrary"))(dh, up, pre, h2, g2, w_down_t, w_up_t, cw, *ex_args)
    return outs[0], outs[1], outs[2], outs[3], outs[4], outs[5:]


def _ple_fwd(h, p, g, w_gate, w_proj, *, tt, name):
    t = h.shape[0]
    tt = _tile(t, tt)

    def body(h_ref, p_ref, g_ref, wg_ref, wp_ref, o_ref):
        x = h_ref[...]
        n = (x * _rms_r(x) * g_ref[...]).astype(BF16)
        gate = _sigmoid(_dot(n, wg_ref[...]))
        o_ref[...] = x + _dot(p_ref[...].astype(BF16), wp_ref[...]) * gate

    full = lambda shape: pl.BlockSpec(shape, lambda i: tuple(0 for _ in shape))
    return pl.pallas_call(
        body, name=name, grid=(t // tt,),
        in_specs=[pl.BlockSpec((tt, D_MODEL), lambda i: (i, 0)), pl.BlockSpec((tt, D_PLE), lambda i: (i, 0)),
                  full((1, D_MODEL)), full((D_MODEL, D_MODEL)), full((D_PLE, D_MODEL))],
        out_specs=pl.BlockSpec((tt, D_MODEL), lambda i: (i, 0)),
        out_shape=jax.ShapeDtypeStruct((t, D_MODEL), F32),
        compiler_params=_params("arbitrary"))(h, p, g, w_gate, w_proj)


def _ple_bwd(dh, h, p, g, w_gate, w_gate_t, w_proj, *, tt, name):
    t = h.shape[0]
    tt = _tile(t, tt)

    def body(dh_ref, h_ref, p_ref, g_ref, wg_ref, wgt_ref, wp_ref, o_ref, dwg_ref, dwp_ref, dg_ref):
        @pl.when(pl.program_id(0) == 0)
        def _():
            dwg_ref[...] = jnp.zeros_like(dwg_ref)
            dwp_ref[...] = jnp.zeros_like(dwp_ref)
            dg_ref[...] = jnp.zeros_like(dg_ref)

        x = h_ref[...]
        r = _rms_r(x)
        gv = g_ref[...]
        n = (x * r * gv).astype(BF16)
        gate = _sigmoid(_dot(n, wg_ref[...]))
        pb = p_ref[...].astype(BF16)
        pe = _dot(pb, wp_ref[...])
        dhv = dh_ref[...]
        dwp_ref[...] += _dot_tn(pb, (dhv * gate).astype(BF16))
        ds = (dhv * pe * gate * (1.0 - gate)).astype(BF16)
        dwg_ref[...] += _dot_tn(n, ds)
        dx, dg = _rms_bwd(x, r, gv, _dot(ds, wgt_ref[...]))
        o_ref[...] = dhv + dx
        dg_ref[...] += dg

    full = lambda shape: pl.BlockSpec(shape, lambda i: tuple(0 for _ in shape))
    row = lambda n: pl.BlockSpec((tt, n), lambda i: (i, 0))
    return pl.pallas_call(
        body, name=name, grid=(t // tt,),
        in_specs=[row(D_MODEL), row(D_MODEL), row(D_PLE), full((1, D_MODEL)), full((D_MODEL, D_MODEL)),
                  full((D_MODEL, D_MODEL)), full((D_PLE, D_MODEL))],
        out_specs=[row(D_MODEL), full((D_MODEL, D_MODEL)), full((D_PLE, D_MODEL)), full((1, D_MODEL))],
        out_shape=[jax.ShapeDtypeStruct((t, D_MODEL), F32), jax.ShapeDtypeStruct((D_MODEL, D_MODEL), F32),
                   jax.ShapeDtypeStruct((D_PLE, D_MODEL), F32), jax.ShapeDtypeStruct((1, D_MODEL), F32)],
        compiler_params=_params("arbitrary"))(dh, h, p, g, w_gate, w_gate_t, w_proj)


def _loss_head(h, g, target, *, tt, name):
    t = h.shape[0]
    tt = _tile(t, tt)

    def body(h_ref, g_ref, tg_ref, dh_ref, loss_ref, dg_ref):
        @pl.when(pl.program_id(0) == 0)
        def _():
            loss_ref[...] = jnp.zeros_like(loss_ref)
            dg_ref[...] = jnp.zeros_like(dg_ref)

        x = h_ref[...]
        r = _rms_r(x)
        gv = g_ref[...]
        diff = x * r * gv - tg_ref[...]
        loss_ref[...] += 0.5 * jnp.sum(jnp.mean(diff * diff, axis=-1, keepdims=True), axis=0, keepdims=True)
        dx, dg = _rms_bwd(x, r, gv, diff * (1.0 / D_MODEL))
        dh_ref[...] = dx
        dg_ref[...] += dg

    return pl.pallas_call(
        body, name=name, grid=(t // tt,),
        in_specs=[pl.BlockSpec((tt, D_MODEL), lambda i: (i, 0)), pl.BlockSpec((1, D_MODEL), lambda i: (0, 0)),
                  pl.BlockSpec((tt, D_MODEL), lambda i: (i, 0))],
        out_specs=[pl.BlockSpec((tt, D_MODEL), lambda i: (i, 0)), pl.BlockSpec((SUBLANES, LANES), lambda i: (0, 0)),
                   pl.BlockSpec((1, D_MODEL), lambda i: (0, 0))],
        out_shape=[jax.ShapeDtypeStruct((t, D_MODEL), F32), jax.ShapeDtypeStruct((SUBLANES, LANES), F32),
                   jax.ShapeDtypeStruct((1, D_MODEL), F32)],
        compiler_params=_params("arbitrary"))(h, g, target)


ADAM_BLOCK_BYTES = 4 * 1024 * 1024


def _adam_rows(rows, cols):
    lanes = -(-cols // LANES) * LANES
    for cand in (1024, 512, 256, 128, 64, 32, 16, 8):
        if rows % cand == 0 and N_DEV * cand * lanes * 4 <= ADAM_BLOCK_BYTES:
            return cand
    return rows


def _sum_adamw(parts, w, m, v, *, name):
    nl, rows, cols = w.shape
    tr = _adam_rows(rows, cols)

    def body(p_ref, w_ref, m_ref, v_ref, g_ref, d_ref, nm_ref, nv_ref):
        g = p_ref[0]
        for k in range(1, N_DEV):
            g = g + p_ref[k]
        g_ref[...] = g
        nm = ADAM_B1 * m_ref[...] + (1.0 - ADAM_B1) * g
        nv = ADAM_B2 * v_ref[...] + (1.0 - ADAM_B2) * (g * g)
        m_hat = nm / (1.0 - ADAM_B1 ** ADAM_STEP)
        v_hat = nv / (1.0 - ADAM_B2 ** ADAM_STEP)
        d_ref[...] = -ADAM_LR * (m_hat / (jnp.sqrt(v_hat) + ADAM_EPS) + ADAM_WD * w_ref[...])
        nm_ref[...] = nm
        nv_ref[...] = nv

    blk = pl.BlockSpec((None, tr, cols), lambda l, r: (l, r, 0))
    return pl.pallas_call(
        body, name=name, grid=(nl, rows // tr),
        in_specs=[pl.BlockSpec((None, N_DEV, tr, cols), lambda l, r: (l, 0, r, 0)), blk, blk, blk],
        out_specs=[blk, blk, blk, blk],
        out_shape=[jax.ShapeDtypeStruct((nl, rows, cols), F32)] * 4,
        compiler_params=_params("arbitrary", "arbitrary"))(parts, w, m, v)


PACK_ROWS = 512


def _pack(arrays):
    flat = jnp.concatenate([a.astype(F32).reshape(-1) for a in arrays])
    pad = (-flat.shape[0]) % (PACK_ROWS * LANES)
    return jnp.pad(flat, (0, pad)).reshape(-1, LANES)


def _unpack(buf, shapes):
    flat = buf.reshape(-1)
    out, off = [], 0
    for s in shapes:
        n = math.prod(s)
        out.append(flat[off:off + n].reshape(s))
        off += n
    return out


def _to_proj_cols(w):
    z, xbc, dtc, u = jnp.split(w, [SSD_WIDTH, SSD_WIDTH + SSD_XBC, SSD_WIDTH + SSD_XBC + SSD_HEADS], axis=-1)
    pad = jnp.zeros(w.shape[:-1] + (LANES - SSD_HEADS,), w.dtype)
    return jnp.concatenate([xbc, z, u, dtc, pad], axis=-1)


def _from_proj_cols(w):
    xbc, z, u, dtc = (w[..., COL_XBC:COL_Z], w[..., COL_Z:COL_U], w[..., COL_U:COL_DT],
                      w[..., COL_DT:COL_DT + SSD_HEADS])
    return jnp.concatenate([z, xbc, dtc, u], axis=-1)


def _pad_heads(v):
    return jnp.pad(v, (0, LANES - SSD_HEADS)).reshape(1, LANES)


def _cat_cols(g):
    return jnp.transpose(g, (1, 0, 2)).reshape(g.shape[1], N_DEV * g.shape[2])


def _split_cols(w):
    r, c = w.shape
    return jnp.transpose(w.reshape(r, N_DEV, c // N_DEV), (1, 0, 2))


def _cat_rows(g):
    return g.reshape(N_DEV * g.shape[1], g.shape[2])


def _split_rows(w):
    return w.reshape(N_DEV, w.shape[0] // N_DEV, w.shape[1])


SHARDED = ("w_in", "w_out", "ffn_w_up", "ffn_w_down", "ple_w_gate", "ple_w_proj", "ssd_conv_w", "ffn_conv_w")
COL_SHARDED = ("w_in", "ffn_w_up", "ple_w_proj", "ssd_conv_w", "ffn_conv_w")
MATMUL_W = SHARDED[:6]
REPLICATED = ("mix_norm_g", "ssd_conv_b", "ssd_dt_bias", "ssd_a_log", "ssd_d", "ssd_norm_g", "pool_w", "pool_scale",
              "ffn_norm_g", "ffn_conv_b", "ple_norm_g", "final_norm_g")
WEIGHTS = ("mix_norm_g", "w_in", "ssd_conv_w", "ssd_conv_b", "ssd_dt_bias", "ssd_a_log", "ssd_d", "ssd_norm_g",
           "pool_w", "pool_scale", "w_out", "ffn_norm_g", "ffn_w_up", "ffn_conv_w", "ffn_conv_b", "ffn_w_down",
           "ple_norm_g", "ple_w_gate", "ple_w_proj", "final_norm_g")


FIRST_USED = ("w_in", "ssd_conv_w")
LATER_USED = tuple(k for k in SHARDED if k not in FIRST_USED)
LAST_MADE = ("w_out", "ssd_conv_w", "w_in")
EARLY_MADE = tuple(k for k in SHARDED if k not in LAST_MADE)
TRANSPOSED = ("w_in", "w_out", "ffn_w_up", "ffn_w_down", "ple_w_gate")


def _pick(names, per_sharded):
    return [per_sharded[SHARDED.index(k)] for k in names]


def _put(names, per_sharded, values):
    out = list(per_sharded)
    for k, val in zip(names, values):
        out[SHARDED.index(k)] = val
    return out


def _assemble(names, gathered):
    full = {}
    for k, g in zip(names, gathered):
        full[k] = _cat_cols(g) if k in COL_SHARDED else _cat_rows(g)
        if k == "w_in":
            full[k] = _to_proj_cols(full[k])
        if k in TRANSPOSED:
            full[k + "_t"] = full[k].T
    return full


def _grad_shards(names, grads):
    out = []
    for k in names:
        g = _from_proj_cols(grads[k]) if k == "w_in" else grads[k]
        out.append(_split_cols(g) if k in COL_SHARDED else _split_rows(g))
    return out


def _layer_fwd(i, h1, p_i, lw, rep, consts, ex_own, ex):
    tril, e_mat = consts
    row = lambda v: v.reshape(1, -1)
    dtb, alog = _pad_heads(rep["ssd_dt_bias"]), _pad_heads(rep["ssd_a_log"])
    dexp = row(jnp.repeat(rep["ssd_d"], SSD_HEAD_DIM))
    pw = rep["pool_w"].astype(BF16)
    proj = _norm_matmul(h1, lw["w_in"], row(rep["mix_norm_g"]), tt=1024, tn=N_PROJ, name=f"in_proj_{i}")
    yssd, ypre, states, xc, own = _ssd_fwd(proj, lw["ssd_conv_w"], row(rep["ssd_conv_b"]), dtb, alog, dexp,
                                       row(rep["ssd_norm_g"]), tril, e_mat, ts=512, name=f"ssd_fwd_{i}", ex=ex_own)
    if ex_own is not None:
        lw = dict(lw, **_assemble(LATER_USED, own))
    h2, ymix, n2 = _mix_out(h1, yssd, proj, pw, row(rep["pool_scale"]), lw["w_out"], row(rep["ffn_norm_g"]), tt=512,
                            name=f"mix_out_{i}")
    h3, act, pre, up, gathered = _ffn_fwd(h2, n2, lw["ffn_w_up"], lw["ffn_conv_w"], row(rep["ffn_conv_b"]),
                                          lw["ffn_w_down"], tt=256, name=f"ffn_fwd_{i}", ex=ex)
    h4 = _ple_fwd(h3, p_i, row(rep["ple_norm_g"]), lw["ple_w_gate"], lw["ple_w_proj"], tt=1024, name=f"ple_fwd_{i}")
    saved = dict(h1=h1, proj=proj, xc=xc, ypre=ypre, states=states, ymix=ymix, h2=h2, n2=n2, up=up, pre=pre, act=act, h3=h3,
                 dtb=dtb, alog=alog, dexp=dexp, pw=pw)
    return h4, saved, lw, gathered


def _layer_bwd(i, dh, p_i, lw, rep, s, consts, pending, parts, own_early):
    tril, triu, e_mat = consts
    ex = None if pending is None else _Exchange(pending, scatter=True, layer=i + 1, into=parts)
    row = lambda v: v.reshape(1, -1)
    g = {}
    dh, g["ple_w_gate"], g["ple_w_proj"], dg3 = _ple_bwd(dh, s["h3"], p_i, row(rep["ple_norm_g"]), lw["ple_w_gate"],
                                                         lw["ple_w_gate_t"], lw["ple_w_proj"], tt=512,
                                                         name=f"ple_bwd_{i}")
    g["ple_norm_g"] = dg3.reshape(-1)
    g["ffn_w_down"] = _matmul_tn(s["act"], dh, tm=D_FF // 2, tn=D_MODEL, tk=1024, name=f"dw_down_{i}")
    dh, dup, g["ffn_conv_w"], dcb, dg2, scattered = _ffn_bwd(
        dh, s["up"], s["pre"], s["h2"], row(rep["ffn_norm_g"]), lw["ffn_w_down_t"], lw["ffn_w_up_t"],
        lw["ffn_conv_w"], tt=256, name=f"ffn_bwd_{i}", ex=ex)
    if ex is not None:
        parts = scattered
    g["ffn_conv_b"], g["ffn_norm_g"] = dcb.reshape(-1), dg2.reshape(-1)
    g["ffn_w_up"] = _matmul_tn(s["n2"], dup, tm=D_MODEL, tn=D_UP // 4, tk=1024, name=f"dw_up_{i}")
    dymix, g["w_out"] = _out_bwd(dh, s["ymix"], lw["w_out_t"], tt=1024, name=f"out_bwd_{i}")
    du, g["pool_w"], dsc = _pool_bwd(dymix, s["proj"], s["pw"], jnp.swapaxes(s["pw"], 1, 2), row(rep["pool_scale"]),
                                     tt=512, name=f"pool_bwd_{i}")
    g["pool_scale"] = dsc.reshape(-1)
    ex_own = None
    if own_early:
        ex_own = _Exchange(_grad_shards(EARLY_MADE, g), scatter=True, layer=i, into=_pick(EARLY_MADE, parts))
    (dproj, g["ssd_conv_w"], dcb, ddtb, dalog, dd, dng), own = _ssd_bwd(
        dymix, s["proj"], s["xc"], s["ypre"], s["states"], du, lw["ssd_conv_w"], s["dtb"], s["alog"], s["dexp"],
        row(rep["ssd_norm_g"]), tril, triu, e_mat, ts=512, name=f"ssd_bwd_{i}", ex=ex_own)
    if own_early:
        parts = _put(EARLY_MADE, parts, own)
    g["ssd_conv_b"], g["ssd_norm_g"] = dcb.reshape(-1), dng.reshape(-1)
    g["ssd_dt_bias"], g["ssd_a_log"], g["ssd_d"] = ddtb[0, :SSD_HEADS], dalog[0, :SSD_HEADS], dd[0, :SSD_HEADS]
    dh, g["w_in"], dg1 = _in_bwd(dproj, s["h1"], row(rep["mix_norm_g"]), lw["w_in_t"], dh, tt=512, name=f"in_bwd_{i}")
    g["mix_norm_g"] = dg1.reshape(-1)
    return dh, g, parts


def kernel(x, p, mix_norm_g, w_in, ssd_conv_w, ssd_conv_b, ssd_dt_bias, ssd_a_log, ssd_d, ssd_norm_g, pool_w, pool_scale, w_out, ffn_norm_g, ffn_w_up, ffn_conv_w, ffn_conv_b, ffn_w_down, ple_norm_g, ple_w_gate, ple_w_proj, final_norm_g, loss_target, m_mix_norm_g, m_w_in, m_ssd_conv_w, m_ssd_conv_b, m_ssd_dt_bias, m_ssd_a_log, m_ssd_d, m_ssd_norm_g, m_pool_w, m_pool_scale, m_w_out, m_ffn_norm_g, m_ffn_w_up, m_ffn_conv_w, m_ffn_conv_b, m_ffn_w_down, m_ple_norm_g, m_ple_w_gate, m_ple_w_proj, m_final_norm_g, v_mix_norm_g, v_w_in, v_ssd_conv_w, v_ssd_conv_b, v_ssd_dt_bias, v_ssd_a_log, v_ssd_d, v_ssd_norm_g, v_pool_w, v_pool_scale, v_w_out, v_ffn_norm_g, v_ffn_w_up, v_ffn_conv_w, v_ffn_conv_b, v_ffn_w_down, v_ple_norm_g, v_ple_w_gate, v_ple_w_proj, v_final_norm_g):
    w = dict(mix_norm_g=mix_norm_g, w_in=w_in, ssd_conv_w=ssd_conv_w, ssd_conv_b=ssd_conv_b, ssd_dt_bias=ssd_dt_bias,
             ssd_a_log=ssd_a_log, ssd_d=ssd_d, ssd_norm_g=ssd_norm_g, pool_w=pool_w, pool_scale=pool_scale, w_out=w_out,
             ffn_norm_g=ffn_norm_g, ffn_w_up=ffn_w_up, ffn_conv_w=ffn_conv_w, ffn_conv_b=ffn_conv_b,
             ffn_w_down=ffn_w_down, ple_norm_g=ple_norm_g, ple_w_gate=ple_w_gate, ple_w_proj=ple_w_proj,
             final_norm_g=final_norm_g)
    m = dict(mix_norm_g=m_mix_norm_g, w_in=m_w_in, ssd_conv_w=m_ssd_conv_w, ssd_conv_b=m_ssd_conv_b,
             ssd_dt_bias=m_ssd_dt_bias, ssd_a_log=m_ssd_a_log, ssd_d=m_ssd_d, ssd_norm_g=m_ssd_norm_g, pool_w=m_pool_w,
             pool_scale=m_pool_scale, w_out=m_w_out, ffn_norm_g=m_ffn_norm_g, ffn_w_up=m_ffn_w_up,
             ffn_conv_w=m_ffn_conv_w, ffn_conv_b=m_ffn_conv_b, ffn_w_down=m_ffn_w_down, ple_norm_g=m_ple_norm_g,
             ple_w_gate=m_ple_w_gate, ple_w_proj=m_ple_w_proj, final_norm_g=m_final_norm_g)
    v = dict(mix_norm_g=v_mix_norm_g, w_in=v_w_in, ssd_conv_w=v_ssd_conv_w, ssd_conv_b=v_ssd_conv_b,
             ssd_dt_bias=v_ssd_dt_bias, ssd_a_log=v_ssd_a_log, ssd_d=v_ssd_d, ssd_norm_g=v_ssd_norm_g, pool_w=v_pool_w,
             pool_scale=v_pool_scale, w_out=v_w_out, ffn_norm_g=v_ffn_norm_g, ffn_w_up=v_ffn_w_up,
             ffn_conv_w=v_ffn_conv_w, ffn_conv_b=v_ffn_conv_b, ffn_w_down=v_ffn_w_down, ple_norm_g=v_ple_norm_g,
             ple_w_gate=v_ple_w_gate, ple_w_proj=v_ple_w_proj, final_norm_g=v_final_norm_g)

    tril = jnp.tril(jnp.ones((CHUNK, CHUNK), BF16))
    triu = tril.T
    e_mat = (jnp.arange(SSD_WIDTH)[None, :] // SSD_HEAD_DIM == jnp.arange(LANES)[:, None]).astype(BF16)
    rep = [{k: w[k][i] for k in REPLICATED if k != "final_norm_g"} for i in range(DEPTH)]
    p_loc = p[:, 0]

    shards = [w[k].astype(BF16) if k in MATMUL_W else w[k] for k in SHARDED]
    lw = _assemble(FIRST_USED, _exchange_call(_Exchange(_pick(FIRST_USED, shards), scatter=False, layer=0),
                                              "gather_weights_0"))
    h, saved, layer_w = x[0], [], []
    for i in range(DEPTH):
        ex_own = _Exchange(_pick(LATER_USED, shards), scatter=False, layer=0) if i == 0 else None
        ex = _Exchange(shards, scatter=False, layer=i + 1) if i + 1 < DEPTH else None
        h, s, lw, gathered = _layer_fwd(i, h, p_loc[i], lw, rep[i], (tril, e_mat), ex_own, ex)
        saved.append(s)
        layer_w.append(lw)
        lw = _assemble(SHARDED, gathered)

    dh, loss_blk, dgf = _loss_head(h, final_norm_g.reshape(1, -1), loss_target[0], tt=1024, name="loss_head")
    loss = lax.psum(loss_blk[0, 0], ("x", "y", "c"))

    rep_grads = [None] * DEPTH
    pending, parts = None, None
    for i in reversed(range(DEPTH)):
        dh, g, parts = _layer_bwd(i, dh, p_loc[i], layer_w[i], rep[i], saved[i], (tril, triu, e_mat), pending, parts,
                                  own_early=(i == 0))
        pending = _grad_shards(SHARDED, g) if i > 0 else _grad_shards(LAST_MADE, g)
        rep_grads[i] = g
    parts = _put(LAST_MADE, parts, _exchange_call(
        _Exchange(pending, scatter=True, layer=0, into=_pick(LAST_MADE, parts)), "scatter_grads_0"))

    out = {}
    for k, part in zip(SHARDED, parts):
        out[k] = _sum_adamw(part, w[k], m[k], v[k], name=f"adamw_{k}")

    rp_grads = [dgf.reshape(-1) if k == "final_norm_g" else jnp.stack([rep_grads[i][k] for i in range(DEPTH)])
                for k in REPLICATED]
    rp_shapes = [w[k].shape for k in REPLICATED]
    rp_parts = _exchange_call(_Exchange([_pack(rp_grads)[None]], scatter=False, layer=0), "gather_replicated_grads")[0]
    rp_out = _sum_adamw(rp_parts[None], *[_pack([d[k] for k in REPLICATED])[None] for d in (w, m, v)],
                        name="adamw_replicated")
    for j in range(4):
        for k, arr in zip(REPLICATED, _unpack(rp_out[j][0], rp_shapes)):
            out.setdefault(k, [None] * 4)[j] = arr
    results = [out[k][j] for j in range(4) for k in WEIGHTS]
    return (loss, dh[None], *results)
```
